```python
import math
import jax
import jax.numpy as jnp
from jax import lax
import numpy as np

D_MODEL = 1024
BATCH = 16
SEQ = 2048
DEPTH = 2

N_EVEN = (DEPTH + 1) // 2
N_ODD = DEPTH // 2
BRANCH = D_MODEL // 2
S5_GROUP = 16
S5_GROUPS = BRANCH // S5_GROUP
S5_STATE = 64
S5_DT_MIN = 1e-3
S5_DT_MAX = 1e-1
ML_HEADS = 4
ML_HEAD_DIM = BRANCH // ML_HEADS
ML_CONV = 4
ML_CHUNK = 64
DIL_HEADS = 8
DIL_HEAD_DIM = BRANCH // DIL_HEADS
DIL_PAIRS = ((128, 1), (512, 4), (2048, 16))
DIFF_HEADS = 4
DIFF_HEAD_DIM = BRANCH // (2 * DIFF_HEADS)
Q_BLOCK = 128
ROPE_THETA = 10000.0
NORM_EPS = 1e-6
HEAD_NORM_EPS = 1e-5
F32 = jnp.float32

kernel_name = 'hybrid_s5_mlstm_dilated_diffattn'


def _rmsnorm(x, g, eps=NORM_EPS):
    xf = x.astype(F32)
    y = xf * lax.rsqrt(jnp.mean(xf * xf, axis=-1, keepdims=True) + eps)
    return (y * g.astype(F32)).astype(x.dtype)


def _rotary(x, pos):
    dh = x.shape[-1]
    L = x.shape[1]
    inv = ROPE_THETA ** (-jnp.arange(0, dh, 2, dtype=F32) / dh)
    ang = pos.astype(F32)[:, None] * inv[None, :]
    shape = (1, L) + (1,) * (x.ndim - 3) + (dh // 2,)
    cos = jnp.cos(ang).reshape(shape)
    sin = jnp.sin(ang).reshape(shape)
    xf = x.astype(F32)
    x1, x2 = xf[..., : dh // 2], xf[..., dh // 2:]
    return jnp.concatenate([x1 * cos - x2 * sin, x2 * cos + x1 * sin], axis=-1).astype(x.dtype)


def _complex_diag_scan(a_r, a_i, b_r, b_i):
    def combine(e1, e2):
        ar1, ai1, br1, bi1 = e1
        ar2, ai2, br2, bi2 = e2
        return (ar2 * ar1 - ai2 * ai1, ar2 * ai1 + ai2 * ar1,
                ar2 * br1 - ai2 * bi1 + br2, ar2 * bi1 + ai2 * br1 + bi2)
    _, _, x_r, x_i = lax.associative_scan(combine, (a_r, a_i, b_r, b_i), axis=0)
    return x_r, x_i


def _s5_branch(u, lam_re, lam_im, log_dt, b_re, b_im, c_re, c_im, d_skip, glu_w, glu_b):
    Bsz, L, _ = u.shape
    uf = u.astype(F32)
    ug = uf.reshape(Bsz, L, S5_GROUPS, S5_GROUP)
    lr, li = lam_re.astype(F32), lam_im.astype(F32)
    dt = jnp.exp(log_dt.astype(F32))[:, None]
    mag = jnp.exp(lr * dt)
    ab_r, ab_i = mag * jnp.cos(li * dt), mag * jnp.sin(li * dt)
    den = lr * lr + li * li
    f_r = ((ab_r - 1.0) * lr + ab_i * li) / den
    f_i = (ab_i * lr - (ab_r - 1.0) * li) / den
    br, bi = b_re.astype(F32), b_im.astype(F32)
    bb_r = f_r[..., None] * br - f_i[..., None] * bi
    bb_i = f_r[..., None] * bi + f_i[..., None] * br
    bu_r = jnp.einsum('blgc,gpc->blgp', ug, bb_r)
    bu_i = jnp.einsum('blgc,gpc->blgp', ug, bb_i)
    a_r = jnp.broadcast_to(ab_r, (L,) + ab_r.shape)
    a_i = jnp.broadcast_to(ab_i, (L,) + ab_i.shape)
    x_r, x_i = jax.vmap(_complex_diag_scan, in_axes=(None, None, 0, 0))(a_r, a_i, bu_r, bu_i)
    y = (jnp.einsum('blgp,gcp->blgc', x_r, c_re.astype(F32))
         - jnp.einsum('blgp,gcp->blgc', x_i, c_im.astype(F32)))
    y = y.reshape(Bsz, L, BRANCH) + d_skip.astype(F32) * uf
    g = jax.nn.gelu(y)
    out = g * jax.nn.sigmoid(g @ glu_w.astype(F32) + glu_b.astype(F32))
    return out.astype(u.dtype)


def _mlstm_chunkwise(q, k, v, log_i, log_f):
    Bsz, H, L, dh = q.shape
    nc = L // ML_CHUNK

    def chunks(t):
        t = t.reshape((Bsz, H, nc, ML_CHUNK) + t.shape[3:])
        return jnp.moveaxis(t, 2, 0)

    causal = jnp.tril(jnp.ones((ML_CHUNK, ML_CHUNK), dtype=bool))

    def step(carry, inp):
        C, n, m = carry
        qc, kc, vc, li, lf = inp
        b = jnp.cumsum(lf, axis=-1)
        logD = jnp.where(causal, b[..., :, None] - b[..., None, :] + li[..., None, :], -jnp.inf)
        inter = b + m[..., None]
        m_t = jnp.maximum(inter, jnp.max(logD, axis=-1))
        dmat = jnp.exp(logD - m_t[..., None])
        sc = jnp.exp(inter - m_t)
        s = jnp.einsum('bhtd,bhsd->bhts', qc, kc) * dmat
        num = sc[..., None] * jnp.einsum('bhtd,bhed->bhte', qc, C) + jnp.einsum('bhts,bhse->bhte', s, vc)
        dnm = sc * jnp.einsum('bhtd,bhd->bht', qc, n) + jnp.sum(s, axis=-1)
        h = num / jnp.maximum(jnp.abs(dnm), jnp.exp(-m_t))[..., None]
        b_last = b[..., -1]
        w_log = b_last[..., None] - b + li
        m_new = jnp.maximum(b_last + m, jnp.max(w_log, axis=-1))
        w = jnp.exp(w_log - m_new[..., None])
        decay = jnp.exp(b_last + m - m_new)
        C_new = decay[..., None, None] * C + jnp.einsum('bhs,bhse,bhsd->bhed', w, vc, kc)
        n_new = decay[..., None] * n + jnp.einsum('bhs,bhsd->bhd', w, kc)
        return (C_new, n_new, m_new), h

    init = (jnp.zeros((Bsz, H, dh, dh), F32), jnp.zeros((Bsz, H, dh), F32), jnp.zeros((Bsz, H), F32))
    _, h = lax.scan(step, init, tuple(map(chunks, (q, k, v, log_i, log_f))))
    return jnp.moveaxis(h, 0, 2).reshape(Bsz, H, L, dh)


def _mlstm_branch(xm, conv_w, conv_b, wq, wk, wv, gate_w, gate_b, norm_g, skip):
    Bsz, L, W = xm.shape
    xp = jnp.pad(xm, ((0, 0), (ML_CONV - 1, 0), (0, 0)))
    xc = conv_b + sum(xp[:, j:j + L] * conv_w[j] for j in range(ML_CONV))
    xc = jax.nn.silu(xc)
    xc_h = xc.reshape(Bsz, L, ML_HEADS, ML_HEAD_DIM)
    xm_h = xm.reshape(Bsz, L, ML_HEADS, ML_HEAD_DIM)
    q = jnp.einsum('blhd,hde->blhe', xc_h, wq)
    k = jnp.einsum('blhd,hde->blhe', xc_h, wk)
    v = jnp.einsum('blhd,hde->blhe', xm_h, wv)
    qkv = jnp.concatenate([q.reshape(Bsz, L, W), k.reshape(Bsz, L, W), v.reshape(Bsz, L, W)], axis=-1)
    gates = (qkv @ gate_w + gate_b).astype(F32)
    log_i = jnp.transpose(gates[..., :ML_HEADS], (0, 2, 1))
    log_f = jnp.transpose(jax.nn.log_sigmoid(gates[..., ML_HEADS:]), (0, 2, 1))

    def to_bhl(t):
        return jnp.transpose(t.astype(F32), (0, 2, 1, 3))

    h = _mlstm_chunkwise(to_bhl(q), to_bhl(k) * (ML_HEAD_DIM ** -0.5), to_bhl(v), log_i, log_f)
    mu = jnp.mean(h, axis=-1, keepdims=True)
    var = jnp.mean(jnp.square(h - mu), axis=-1, keepdims=True)
    h = (h - mu) * lax.rsqrt(var + HEAD_NORM_EPS)
    h = jnp.transpose(h, (0, 2, 1, 3)).reshape(Bsz, L, W) * norm_g.astype(F32)
    return (h + skip.astype(F32) * xc.astype(F32)).astype(xm.dtype)


def _strided_window_attention(q, k, v, window, dilation):
    Bsz, L, H, dh = q.shape
    n_back = window // dilation
    bs = n_back
    n_sub = -(-L // dilation)
    n_blk = -(-n_sub // bs)
    pad = dilation * n_blk * bs - L

    def to_blocks(t):
        t = jnp.pad(t, ((0, 0), (0, pad), (0, 0), (0, 0)))
        return t.reshape(Bsz, n_blk, bs, dilation, H, dh)

    def with_prev(t):
        prev = jnp.concatenate([jnp.zeros_like(t[:, :1]), t[:, :-1]], axis=1)
        return jnp.concatenate([prev, t], axis=2)

    qb = to_blocks(q)
    kw = with_prev(to_blocks(k))
    vw = with_prev(to_blocks(v)).astype(F32)
    s = jnp.einsum('bnirhd,bnjrhd->bnrhij', qb, kw).astype(F32) * (dh ** -0.5)
    i_rel = jnp.arange(bs)[:, None]
    j_rel = jnp.arange(2 * bs)[None, :]
    delta = bs + i_rel - j_rel
    band = (delta >= 0) & (delta <= n_back)
    valid = (jnp.arange(n_blk)[:, None, None] > 0) | (j_rel[None] >= bs)
    mask = band[None] & valid
    s = jnp.where(mask[None, :, None, None], s, -jnp.inf)
    m = jnp.max(s, axis=-1, keepdims=True)
    p = jnp.exp(s - m)
    den = jnp.sum(p, axis=-1)
    o = jnp.einsum('bnrhij,bnjrhd->bnirhd', p, vw) / jnp.transpose(den, (0, 1, 4, 2, 3))[..., None]
    lse = jnp.transpose(m[..., 0] + jnp.log(den), (0, 1, 4, 2, 3))
    o = o.reshape(Bsz, -1, H, dh)[:, :L]
    lse = lse.reshape(Bsz, -1, H)[:, :L]
    return o, lse


def _dilated_mixture(q, k, v):
    outs, lses = [], []
    for window, dilation in DIL_PAIRS:
        o, lse = _strided_window_attention(q, k, v, window, dilation)
        outs.append(o)
        lses.append(lse)
    wts = jax.nn.softmax(jnp.stack(lses, axis=0), axis=0)
    return jnp.sum(wts[..., None] * jnp.stack(outs, axis=0), axis=0)


def _diff_attention(q, k, v, lam):
    Bsz, L, H, _, dh = q.shape
    n_qb = L // Q_BLOCK
    k_pos = jnp.arange(L)
    vf = v.astype(F32)
    scale = dh ** -0.5

    def block(i):
        start = i * Q_BLOCK
        qb = lax.dynamic_slice_in_dim(q, start, Q_BLOCK, axis=1)
        s = jnp.einsum('bqhcd,bkhcd->bhcqk', qb, k).astype(F32) * scale
        q_pos = start + jnp.arange(Q_BLOCK)
        s = jnp.where(k_pos[None, :] <= q_pos[:, None], s, -jnp.inf)
        p = jax.nn.softmax(s, axis=-1)
        attn = p[:, :, 0] - lam * p[:, :, 1]
        return jnp.einsum('bhqk,bkhe->bqhe', attn, vf)

    o = lax.map(block, jnp.arange(n_qb))
    return jnp.moveaxis(o, 0, 1).reshape(Bsz, L, H, -1)


def _even_layer(z, w_in, lam_re, lam_im, log_dt, b_re, b_im, c_re, c_im, d_skip, glu_w, glu_b,
                conv_w, conv_b, wq, wk, wv, gate_w, gate_b, ml_norm, ml_skip, w_out):
    proj = z @ w_in
    s5_u, s5_z, ml_x, ml_z = jnp.split(proj, 4, axis=-1)
    a = _s5_branch(s5_u, lam_re, lam_im, log_dt, b_re, b_im, c_re, c_im, d_skip, glu_w, glu_b) * jax.nn.silu(s5_z)
    b = _mlstm_branch(ml_x, conv_w, conv_b, wq, wk, wv, gate_w, gate_b, ml_norm, ml_skip) * jax.nn.silu(ml_z)
    return jnp.concatenate([a, b], axis=-1) @ w_out


def _odd_layer(z, w_in, lq1, lk1, lq2, lk2, diff_norm, w_out, layer_idx):
    Bsz, L, _ = z.shape
    proj = z @ w_in
    cq, ck, cv, cz, dq, dk, dv, dz = jnp.split(proj, 8, axis=-1)
    pos = jnp.arange(L)

    def heads_c(t):
        return t.reshape(Bsz, L, DIL_HEADS, DIL_HEAD_DIM)

    def heads_d(t):
        return t.reshape(Bsz, L, DIFF_HEADS, 2, DIFF_HEAD_DIM)

    c_o = _dilated_mixture(_rotary(heads_c(cq), pos), _rotary(heads_c(ck), pos), heads_c(cv))
    c_out = c_o.reshape(Bsz, L, BRANCH).astype(z.dtype) * jax.nn.silu(cz)
    lam_init = 0.8 - 0.6 * math.exp(-0.3 * layer_idx)
    lam = (jnp.exp(jnp.sum(lq1.astype(F32) * lk1.astype(F32)))
           - jnp.exp(jnp.sum(lq2.astype(F32) * lk2.astype(F32))) + lam_init)
    d_o = _diff_attention(_rotary(heads_d(dq), pos), _rotary(heads_d(dk), pos),
                          dv.reshape(Bsz, L, DIFF_HEADS, 2 * DIFF_HEAD_DIM), lam)
    d_o = _rmsnorm(d_o, diff_norm, HEAD_NORM_EPS) * (1.0 - lam_init)
    d_out = d_o.reshape(Bsz, L, BRANCH).astype(z.dtype) * jax.nn.silu(dz)
    return jnp.concatenate([c_out, d_out], axis=-1) @ w_out


def setup_inputs(seed: int = 0) -> dict:
    key = jax.random.key(seed)
    ks = iter(jax.random.split(key, 40))

    def nrm(shape, std):
        return std * jax.random.normal(next(ks), shape, F32)

    G, P = S5_GROUPS, S5_STATE
    NE, NO = N_EVEN, N_ODD
    gate_b = jnp.concatenate([
        nrm((NE, ML_HEADS), 0.1),
        jnp.linspace(3.0, 6.0, ML_HEADS, dtype=F32)[None, :] + nrm((NE, ML_HEADS), 0.1)], axis=-1)
    return {
        'x': nrm((BATCH, SEQ, D_MODEL), 1.0),
        'pre_norm': 1.0 + nrm((DEPTH, D_MODEL), 0.1),
        'post_norm': 1.0 + nrm((DEPTH, D_MODEL), 0.1),
        'w_in_ab': nrm((NE, D_MODEL, 4 * BRANCH), D_MODEL ** -0.5),
        's5_lambda_re': -0.5 + nrm((NE, G, P), 0.02),
        's5_lambda_im': jnp.pi * jnp.arange(P, dtype=F32) + nrm((NE, G, P), 0.02),
        's5_log_dt': jax.random.uniform(next(ks), (NE, G), F32, math.log(S5_DT_MIN), math.log(S5_DT_MAX)),
        's5_b_re': nrm((NE, G, P, S5_GROUP), (2 * S5_GROUP) ** -0.5),
        's5_b_im': nrm((NE, G, P, S5_GROUP), (2 * S5_GROUP) ** -0.5),
        's5_c_re': nrm((NE, G, S5_GROUP, P), P ** -0.5),
        's5_c_im': nrm((NE, G, S5_GROUP, P), P ** -0.5),
        's5_d': nrm((NE, BRANCH), 1.0),
        's5_glu_w': nrm((NE, BRANCH, BRANCH), BRANCH ** -0.5),
        's5_glu_b': nrm((NE, BRANCH), 0.02),
        'ml_conv_w': nrm((NE, ML_CONV, BRANCH), ML_CONV ** -0.5),
        'ml_conv_b': nrm((NE, BRANCH), 0.02),
        'ml_wq': nrm((NE, ML_HEADS, ML_HEAD_DIM, ML_HEAD_DIM), ML_HEAD_DIM ** -0.5),
        'ml_wk': nrm((NE, ML_HEADS, ML_HEAD_DIM, ML_HEAD_DIM), ML_HEAD_DIM ** -0.5),
        'ml_wv': nrm((NE, ML_HEADS, ML_HEAD_DIM, ML_HEAD_DIM), ML_HEAD_DIM ** -0.5),
        'ml_gate_w': nrm((NE, 3 * BRANCH, 2 * ML_HEADS), 0.1 * (3 * BRANCH) ** -0.5),
        'ml_gate_b': gate_b,
        'ml_norm': 1.0 + nrm((NE, BRANCH), 0.1),
        'ml_skip': 1.0 + nrm((NE, BRANCH), 0.1),
        'w_out_ab': nrm((NE, 2 * BRANCH, D_MODEL), (2 * BRANCH) ** -0.5),
        'w_in_cd': nrm((NO, D_MODEL, 8 * BRANCH), D_MODEL ** -0.5),
        'diff_lq1': nrm((NO, DIFF_HEAD_DIM), 0.1),
        'diff_lk1': nrm((NO, DIFF_HEAD_DIM), 0.1),
        'diff_lq2': nrm((NO, DIFF_HEAD_DIM), 0.1),
        'diff_lk2': nrm((NO, DIFF_HEAD_DIM), 0.1),
        'diff_norm': 1.0 + nrm((NO, 2 * DIFF_HEAD_DIM), 0.1),
        'w_out_cd': nrm((NO, 2 * BRANCH, D_MODEL), (2 * BRANCH) ** -0.5),
    }


def reference(x, pre_norm, post_norm, w_in_ab, s5_lambda_re, s5_lambda_im, s5_log_dt, s5_b_re, s5_b_im,
              s5_c_re, s5_c_im, s5_d, s5_glu_w, s5_glu_b, ml_conv_w, ml_conv_b, ml_wq, ml_wk, ml_wv,
              ml_gate_w, ml_gate_b, ml_norm, ml_skip, w_out_ab, w_in_cd, diff_lq1, diff_lk1, diff_lq2,
              diff_lk2, diff_norm, w_out_cd):
    h = x
    for l in range(DEPTH):
        z = _rmsnorm(h, pre_norm[l])
        i = l // 2
        if l % 2 == 0:
            y = _even_layer(z, w_in_ab[i], s5_lambda_re[i], s5_lambda_im[i], s5_log_dt[i], s5_b_re[i],
                            s5_b_im[i], s5_c_re[i], s5_c_im[i], s5_d[i], s5_glu_w[i], s5_glu_b[i],
                            ml_conv_w[i], ml_conv_b[i], ml_wq[i], ml_wk[i], ml_wv[i], ml_gate_w[i],
                            ml_gate_b[i], ml_norm[i], ml_skip[i], w_out_ab[i])
        else:
            y = _odd_layer(z, w_in_cd[i], diff_lq1[i], diff_lk1[i], diff_lq2[i], diff_lk2[i],
                           diff_norm[i], w_out_cd[i], l)
        h = h + _rmsnorm(y, post_norm[l])
    return h
```

```python
import functools
import math

import numpy as np
import jax
import jax.numpy as jnp
from jax import lax
from jax.experimental import pallas as pl
from jax.experimental.pallas import tpu as pltpu

F32 = jnp.float32
BF16 = jnp.bfloat16

S5_GROUP = 16
S5_STATE = 64
ML_HEADS = 4
ML_CONV = 4
DIL_PAIRS = ((128, 1), (512, 4), (2048, 16))
ROPE_THETA = 10000.0
NORM_EPS = 1e-6
HEAD_NORM_EPS = 1e-5
ATTN_HEAD_DIM = 64

LANES = 128
SUBLANES = 8
VMEM_LIMIT_BYTES = 56 * 1024 * 1024

NEG_BIG = -1e30

PROJ_ROWS = 512
PROJ_COLS = 512
S5_BATCH = SUBLANES
S5_TIME = 128
ML_CHUNK = 256
ATTN_BLOCK = 256


def _params(*semantics):
    return pltpu.CompilerParams(dimension_semantics=semantics, vmem_limit_bytes=VMEM_LIMIT_BYTES)


def _silu(x):
    return x * jax.nn.sigmoid(x)


def _norm_proj_kernel(x_ref, g_ref, w_ref, o_ref, z_scr):
    x = x_ref[0]
    ms = jnp.mean(x * x, axis=-1, keepdims=True)
    z_scr[...] = (x * lax.rsqrt(ms + NORM_EPS) * g_ref[...]).astype(BF16)
    n_out = w_ref.shape[1]
    for c in range(n_out // PROJ_COLS):
        cols = slice(c * PROJ_COLS, (c + 1) * PROJ_COLS)
        o_ref[0, :, cols] = jnp.dot(z_scr[...], w_ref[:, cols], preferred_element_type=F32).astype(o_ref.dtype)


def _norm_proj(x, g, w):
    B, L, D = x.shape
    N = w.shape[1]
    tm = min(PROJ_ROWS, L)
    return pl.pallas_call(
        _norm_proj_kernel,
        grid=(B, L // tm),
        in_specs=[
            pl.BlockSpec((1, tm, D), lambda b, i: (b, i, 0)),
            pl.BlockSpec((1, D), lambda b, i: (0, 0)),
            pl.BlockSpec((D, N), lambda b, i: (0, 0)),
        ],
        out_specs=pl.BlockSpec((1, tm, N), lambda b, i: (b, i, 0)),
        out_shape=jax.ShapeDtypeStruct((B, L, N), BF16),
        scratch_shapes=[pltpu.VMEM((tm, D), BF16)],
        compiler_params=_params("parallel", "parallel"),
        name="norm_proj_even",
    )(x, g.reshape(1, D), w)


def _swap_halves(x, first_half):
    return jnp.where(first_half, pltpu.roll(x, LANES - ATTN_HEAD_DIM // 2, 1), pltpu.roll(x, ATTN_HEAD_DIM // 2, 1))


def _norm_proj_rope_kernel(x_ref, g_ref, w_ref, cos_ref, sin_ref, o_ref, z_scr, *, rope_chunks, scaled_chunks):
    x = x_ref[0]
    ms = jnp.mean(x * x, axis=-1, keepdims=True)
    z_scr[...] = (x * lax.rsqrt(ms + NORM_EPS) * g_ref[...]).astype(BF16)
    n_out = w_ref.shape[1]
    slabs = PROJ_COLS // LANES
    lane = lax.broadcasted_iota(jnp.int32, (1, LANES), 1)
    first_half = (lane % ATTN_HEAD_DIM) < (ATTN_HEAD_DIM // 2)
    for c in range(n_out // PROJ_COLS):
        y = jnp.dot(z_scr[...], w_ref[:, c * PROJ_COLS:(c + 1) * PROJ_COLS], preferred_element_type=F32)
        for s in range(slabs):
            t = y[:, s * LANES:(s + 1) * LANES]
            if c in rope_chunks:
                t = t * cos_ref[...] + _swap_halves(t, first_half) * sin_ref[...]
            if c in scaled_chunks:
                t = t * (ATTN_HEAD_DIM ** -0.5)
            o_ref[0, c * slabs + s] = t.astype(o_ref.dtype)


def _norm_proj_rope(x, g, w, cos_t, sin_t):
    B, L, D = x.shape
    N = w.shape[1]
    tm = min(PROJ_ROWS, L)
    kern = functools.partial(_norm_proj_rope_kernel, rope_chunks=(0, 1, 4, 5), scaled_chunks=(0, 4))
    return pl.pallas_call(
        kern,
        grid=(B, L // tm),
        in_specs=[
            pl.BlockSpec((1, tm, D), lambda b, i: (b, i, 0)),
            pl.BlockSpec((1, D), lambda b, i: (0, 0)),
            pl.BlockSpec((D, N), lambda b, i: (0, 0)),
            pl.BlockSpec((tm, LANES), lambda b, i: (i, 0)),
            pl.BlockSpec((tm, LANES), lambda b, i: (i, 0)),
        ],
        out_specs=pl.BlockSpec((1, N // LANES, tm, LANES), lambda b, i: (b, 0, i, 0)),
        out_shape=jax.ShapeDtypeStruct((B, N // LANES, L, LANES), BF16),
        scratch_shapes=[pltpu.VMEM((tm, D), BF16)],
        compiler_params=_params("parallel", "parallel"),
        name="norm_proj_odd",
    )(x, g.reshape(1, D), w, cos_t, sin_t)


def _out_proj_kernel(a_ref, b_ref, wa_ref, wb_ref, g_ref, h_ref, o_ref):
    y = jnp.dot(a_ref[0], wa_ref[...], preferred_element_type=F32)
    y = y + jnp.dot(b_ref[0], wb_ref[...], preferred_element_type=F32)
    ms = jnp.mean(y * y, axis=-1, keepdims=True)
    o_ref[0] = h_ref[0] + y * lax.rsqrt(ms + NORM_EPS) * g_ref[...]


def _out_proj(a, b, w_out, g, h):
    B, L, W = a.shape
    D = w_out.shape[1]
    tm = min(PROJ_ROWS, L)
    return pl.pallas_call(
        _out_proj_kernel,
        grid=(B, L // tm),
        in_specs=[
            pl.BlockSpec((1, tm, W), lambda b_, i: (b_, i, 0)),
            pl.BlockSpec((1, tm, W), lambda b_, i: (b_, i, 0)),
            pl.BlockSpec((W, D), lambda b_, i: (0, 0)),
            pl.BlockSpec((W, D), lambda b_, i: (0, 0)),
            pl.BlockSpec((1, D), lambda b_, i: (0, 0)),
            pl.BlockSpec((1, tm, D), lambda b_, i: (b_, i, 0)),
        ],
        out_specs=pl.BlockSpec((1, tm, D), lambda b_, i: (b_, i, 0)),
        out_shape=jax.ShapeDtypeStruct((B, L, D), F32),
        compiler_params=_params("parallel", "parallel"),
        name="out_proj",
    )(a, b, w_out[:W].astype(BF16), w_out[W:].astype(BF16), g.reshape(1, D), h)


def _s5_kernel(u_ref, z_ref, bmat_ref, cmat_ref, ar_ref, ai_ref, d_ref, gw_ref, gb_ref, o_ref,
               us_scr, xs_scr, ys_scr, yn_scr, st_scr):
    nb, T, W = u_ref.shape
    n_blk = W // LANES
    half = xs_scr.shape[1] // 2
    rows = nb * T

    @pl.when(pl.program_id(1) == 0)
    def _():
        st_scr[...] = jnp.zeros_like(st_scr)

    for b in range(nb):
        ub = u_ref[b].astype(F32)
        for k in range(n_blk):
            us_scr[k, pl.ds(b, T, stride=nb), :] = ub[:, k * LANES:(k + 1) * LANES]

    for k in range(n_blk):
        xs_scr[...] = jnp.dot(us_scr[k].astype(BF16), bmat_ref[k], preferred_element_type=F32)
        ar = ar_ref[k]
        ai = ai_ref[k]

        def step(t, carry):
            xr, xi = carry
            r = pl.multiple_of(t * nb, nb)
            nr = ar * xr - ai * xi + xs_scr[pl.ds(r, nb), 0:half]
            ni = ar * xi + ai * xr + xs_scr[pl.ds(r, nb), half:2 * half]
            xs_scr[pl.ds(r, nb), 0:half] = nr
            xs_scr[pl.ds(r, nb), half:2 * half] = ni
            return nr, ni

        xr, xi = lax.fori_loop(0, T, step, (st_scr[k, 0], st_scr[k, 1]), unroll=2)
        st_scr[k, 0] = xr
        st_scr[k, 1] = xi
        ys_scr[k] = jnp.dot(xs_scr[...].astype(BF16), cmat_ref[k], preferred_element_type=F32)

    for b in range(nb):
        for k in range(n_blk):
            yn_scr[b * T:(b + 1) * T, k * LANES:(k + 1) * LANES] = ys_scr[k, pl.ds(b, T, stride=nb), :]

    u = u_ref[...].reshape(rows, W).astype(F32)
    y = yn_scr[...] + d_ref[...] * u
    g = jax.nn.gelu(y)
    gate = jnp.dot(g.astype(BF16), gw_ref[...], preferred_element_type=F32) + gb_ref[...]
    out = g * jax.nn.sigmoid(gate) * _silu(z_ref[...].reshape(rows, W).astype(F32))
    o_ref[...] = out.reshape(nb, T, W).astype(o_ref.dtype)


def _s5_weights(lam_re, lam_im, log_dt, b_re, b_im, c_re, c_im):
    G, P = lam_re.shape
    gpb = LANES // S5_GROUP
    n_blk = G // gpb
    lr, li = lam_re.astype(F32), lam_im.astype(F32)
    dt = jnp.exp(log_dt.astype(F32))[:, None]
    mag = jnp.exp(lr * dt)
    ab_r, ab_i = mag * jnp.cos(li * dt), mag * jnp.sin(li * dt)
    den = lr * lr + li * li
    f_r = ((ab_r - 1.0) * lr + ab_i * li) / den
    f_i = (ab_i * lr - (ab_r - 1.0) * li) / den
    br, bi = b_re.astype(F32), b_im.astype(F32)
    bb_r = f_r[..., None] * br - f_i[..., None] * bi
    bb_i = f_r[..., None] * bi + f_i[..., None] * br
    eye = jnp.eye(gpb, dtype=F32)
    bb = jnp.stack([bb_r, bb_i]).reshape(2, n_blk, gpb, P, S5_GROUP)
    bb = jnp.transpose(bb, (1, 2, 4, 0, 3))
    bmat = bb[:, :, :, :, None, :] * eye[None, :, None, None, :, None]
    bmat = bmat.reshape(n_blk, LANES, 2 * gpb * P).astype(BF16)
    cc = jnp.stack([c_re.astype(F32), -c_im.astype(F32)]).reshape(2, n_blk, gpb, S5_GROUP, P)
    cc = jnp.transpose(cc, (1, 0, 2, 4, 3))
    cmat = cc[:, :, :, :, None, :] * eye[None, None, :, None, :, None]
    cmat = cmat.reshape(n_blk, 2 * gpb * P, LANES).astype(BF16)
    ar = jnp.broadcast_to(ab_r.reshape(n_blk, 1, gpb * P), (n_blk, S5_BATCH, gpb * P))
    ai = jnp.broadcast_to(ab_i.reshape(n_blk, 1, gpb * P), (n_blk, S5_BATCH, gpb * P))
    return bmat, cmat, ar, ai


def _s5_branch(proj, lam_re, lam_im, log_dt, b_re, b_im, c_re, c_im, d_skip, glu_w, glu_b):
    B, L, N = proj.shape
    W = N // 4
    nb = S5_BATCH
    T = min(S5_TIME, L)
    n_blk = W // LANES
    bmat, cmat, ar, ai = _s5_weights(lam_re, lam_im, log_dt, b_re, b_im, c_re, c_im)
    S2 = bmat.shape[2]
    const3 = lambda g, c: (0, 0, 0)
    const2 = lambda g, c: (0, 0)
    return pl.pallas_call(
        _s5_kernel,
        grid=(B // nb, L // T),
        in_specs=[
            pl.BlockSpec((nb, T, W), lambda g, c: (g, c, 0)),
            pl.BlockSpec((nb, T, W), lambda g, c: (g, c, 1)),
            pl.BlockSpec(bmat.shape, const3),
            pl.BlockSpec(cmat.shape, const3),
            pl.BlockSpec(ar.shape, const3),
            pl.BlockSpec(ai.shape, const3),
            pl.BlockSpec((1, W), const2),
            pl.BlockSpec((W, W), const2),
            pl.BlockSpec((1, W), const2),
        ],
        out_specs=pl.BlockSpec((nb, T, W), lambda g, c: (g, c, 0)),
        out_shape=jax.ShapeDtypeStruct((B, L, W), BF16),
        scratch_shapes=[
            pltpu.VMEM((n_blk, nb * T, LANES), F32),
            pltpu.VMEM((nb * T, S2), F32),
            pltpu.VMEM((n_blk, nb * T, LANES), F32),
            pltpu.VMEM((nb * T, W), F32),
            pltpu.VMEM((n_blk, 2, nb, S2 // 2), F32),
        ],
        compiler_params=_params("parallel", "arbitrary"),
        name="s5_scan",
    )(proj, proj, bmat, cmat, ar, ai, d_skip.reshape(1, W).astype(F32), glu_w.astype(BF16),
      glu_b.reshape(1, W).astype(F32))


def _log_sigmoid(x):
    return jnp.minimum(x, 0.0) - jnp.log1p(jnp.exp(-jnp.abs(x)))


def _split3(x):
    hi = x.astype(BF16)
    r1 = x - hi.astype(F32)
    mid = r1.astype(BF16)
    lo = (r1 - mid.astype(F32)).astype(BF16)
    return hi, mid, lo


def _mlstm_kernel(x_ref, z_ref, cw_ref, cb_ref, wq_ref, wk_ref, wv_ref, gw_ref, gwt_ref, gbc_ref, gbr_ref,
                  ng_ref, sk_ref, o_ref, xe_scr, qkv_scr, ks_scr, ct_scr, n_scr, m_scr):
    Tc, W = x_ref.shape[1], x_ref.shape[2]
    H = wq_ref.shape[0]
    dh = W // H
    pad = SUBLANES

    @pl.when(pl.program_id(1) == 0)
    def _():
        xe_scr[0:pad, :] = jnp.zeros((pad, W), F32)
        ct_scr[...] = jnp.zeros_like(ct_scr)
        n_scr[...] = jnp.zeros_like(n_scr)
        m_scr[...] = jnp.zeros_like(m_scr)

    xb = x_ref[0]
    x = xb.astype(F32)
    xe_scr[pad:pad + Tc, :] = x
    xc = cb_ref[...] + sum(xe_scr[pl.ds(pad - (ML_CONV - 1) + j, Tc), :] * cw_ref[j:j + 1, :]
                           for j in range(ML_CONV))
    xc = _silu(xc)
    xe_scr[0:pad, :] = x[Tc - pad:Tc, :]
    xcb = xc.astype(BF16)

    for h in range(H):
        cols = slice(h * dh, (h + 1) * dh)
        q = jnp.dot(xcb[:, cols], wq_ref[h], preferred_element_type=F32)
        k = jnp.dot(xcb[:, cols], wk_ref[h], preferred_element_type=F32)
        v = jnp.dot(xb[:, cols], wv_ref[h], preferred_element_type=F32)
        qkv_scr[:, h * dh:(h + 1) * dh] = q.astype(BF16)
        qkv_scr[:, W + h * dh:W + (h + 1) * dh] = k.astype(BF16)
        qkv_scr[:, 2 * W + h * dh:2 * W + (h + 1) * dh] = v.astype(BF16)
        ks_scr[:, cols] = (k * (dh ** -0.5)).astype(BF16)

    qkv = qkv_scr[...]
    g_col = jnp.dot(qkv, gw_ref[...], preferred_element_type=F32) + gbc_ref[...]
    g_row = lax.dot_general(gwt_ref[...], qkv, (((1,), (1,)), ((), ())),
                            preferred_element_type=F32) + gbr_ref[...]
    ti = lax.broadcasted_iota(jnp.int32, (Tc, Tc), 0)
    si = lax.broadcasted_iota(jnp.int32, (Tc, Tc), 1)
    causal = si <= ti
    lower = jnp.where(causal, 1.0, 0.0).astype(BF16)
    upper = jnp.where(ti <= si, 1.0, 0.0).astype(BF16)
    b_col = sum(jnp.dot(lower, p, preferred_element_type=F32) for p in _split3(_log_sigmoid(g_col)))
    b_row = sum(jnp.dot(p, upper, preferred_element_type=F32) for p in _split3(_log_sigmoid(g_row)))

    for h in range(H):
        cols = slice(h * dh, (h + 1) * dh)
        q = qkv_scr[:, h * dh:(h + 1) * dh]
        v = qkv_scr[:, 2 * W + h * dh:2 * W + (h + 1) * dh]
        ks = ks_scr[:, cols]
        bc = b_col[:, H + h:H + h + 1]
        br = b_row[H + h:H + h + 1, :]
        li_c = g_col[:, h:h + 1]
        li_r = g_row[h:h + 1, :]
        m_prev = m_scr[h][:, 0:1]

        s_qk = lax.dot_general(q, ks, (((1,), (1,)), ((), ())), preferred_element_type=F32)
        log_d = jnp.where(causal, bc - br + li_r, NEG_BIG)
        inter = bc + m_prev
        m_t = jnp.maximum(inter, jnp.max(log_d, axis=-1, keepdims=True))
        s = s_qk * jnp.exp(log_d - m_t)
        sc = jnp.exp(inter - m_t)
        ct = ct_scr[h]
        n = n_scr[h]
        num = sc * jnp.dot(q, ct.astype(BF16), preferred_element_type=F32)
        num = num + jnp.dot(s.astype(BF16), v, preferred_element_type=F32)
        dnm = sc * jnp.sum(q.astype(F32) * n, axis=-1, keepdims=True) + jnp.sum(s, axis=-1, keepdims=True)
        hh = num / jnp.maximum(jnp.abs(dnm), jnp.exp(-m_t))

        b_last = bc[Tc - 1:Tc, :]
        w_log = b_last - bc + li_c
        m_new = jnp.maximum(b_last + m_prev, jnp.max(w_log, axis=0, keepdims=True))
        kw = ks.astype(F32) * jnp.exp(w_log - m_new)
        decay = jnp.exp(b_last + m_prev - m_new)
        ct_scr[h] = decay * ct + lax.dot_general(kw.astype(BF16), v, (((0,), (0,)), ((), ())),
                                                 preferred_element_type=F32)
        n_scr[h] = decay * n + jnp.sum(kw, axis=0, keepdims=True)
        m_scr[h] = jnp.broadcast_to(m_new, (1, LANES))

        mu = jnp.mean(hh, axis=-1, keepdims=True)
        cen = hh - mu
        var = jnp.mean(cen * cen, axis=-1, keepdims=True)
        out = cen * lax.rsqrt(var + HEAD_NORM_EPS) * ng_ref[:, cols] + sk_ref[:, cols] * xc[:, cols]
        o_ref[0, :, cols] = (out * _silu(z_ref[0, :, cols].astype(F32))).astype(o_ref.dtype)


def _mlstm_branch(proj, conv_w, conv_b, wq, wk, wv, gate_w, gate_b, norm_g, skip):
    B, L, N = proj.shape
    W = N // 4
    H = wq.shape[0]
    dh = W // H
    Tc = min(ML_CHUNK, L)
    gw = jnp.zeros((3 * W, LANES), F32).at[:, :2 * H].set(gate_w.astype(F32)).astype(BF16)
    gwt = jnp.transpose(gate_w.astype(F32)).astype(BF16)
    gbc = jnp.zeros((1, LANES), F32).at[0, :2 * H].set(gate_b.astype(F32))
    gbr = gate_b.astype(F32).reshape(2 * H, 1)
    c2 = lambda b, c: (0, 0)
    c3 = lambda b, c: (0, 0, 0)
    return pl.pallas_call(
        _mlstm_kernel,
        grid=(B, L // Tc),
        in_specs=[
            pl.BlockSpec((1, Tc, W), lambda b, c: (b, c, 2)),
            pl.BlockSpec((1, Tc, W), lambda b, c: (b, c, 3)),
            pl.BlockSpec((ML_CONV, W), c2),
            pl.BlockSpec((1, W), c2),
            pl.BlockSpec((H, dh, dh), c3),
            pl.BlockSpec((H, dh, dh), c3),
            pl.BlockSpec((H, dh, dh), c3),
            pl.BlockSpec((3 * W, LANES), c2),
            pl.BlockSpec((2 * H, 3 * W), c2),
            pl.BlockSpec((1, LANES), c2),
            pl.BlockSpec((2 * H, 1), c2),
            pl.BlockSpec((1, W), c2),
            pl.BlockSpec((1, W), c2),
        ],
        out_specs=pl.BlockSpec((1, Tc, W), lambda b, c: (b, c, 0)),
        out_shape=jax.ShapeDtypeStruct((B, L, W), BF16),
        scratch_shapes=[
            pltpu.VMEM((SUBLANES + Tc, W), F32),
            pltpu.VMEM((Tc, 3 * W), BF16),
            pltpu.VMEM((Tc, W), BF16),
            pltpu.VMEM((H, dh, dh), F32),
            pltpu.VMEM((H, 1, dh), F32),
            pltpu.VMEM((H, 1, LANES), F32),
        ],
        compiler_params=_params("parallel", "arbitrary"),
        name="mlstm",
    )(proj, proj, conv_w.astype(F32), conv_b.reshape(1, W).astype(F32), wq.astype(BF16), wk.astype(BF16),
      wv.astype(BF16), gw, gwt, gbc, gbr, norm_g.reshape(1, W).astype(F32), skip.reshape(1, W).astype(F32))


def _attn_kernel(q_ref, k_ref, v_ref, z_ref, bias_ref, lam_ref, ng_ref, o_ref, m_scr, l_scr, acc_scr, *,
                 mode, out_scale):
    tq = q_ref.shape[2]
    tk = tq
    qi = pl.program_id(2)
    lane = lax.broadcasted_iota(jnp.int32, (1, LANES), 1)
    map0 = lane < ATTN_HEAD_DIM
    q = q_ref[0, 0]
    zero = jnp.zeros_like(q)
    q2 = jnp.concatenate([jnp.where(map0, q, zero), jnp.where(map0, zero, q)], axis=0)

    m_scr[...] = jnp.full(m_scr.shape, NEG_BIG, F32)
    l_scr[...] = jnp.zeros_like(l_scr)
    acc_scr[...] = jnp.zeros_like(acc_scr)

    def kv_step(j, carry):
        r = pl.multiple_of(j * tk, tk)
        k = k_ref[0, 0, pl.ds(r, tk), :]
        v = v_ref[0, 0, pl.ds(r, tk), :]
        s = lax.dot_general(q2, k, (((1,), (1,)), ((), ())), preferred_element_type=F32)
        s = (s.reshape(2, tq, tk) + bias_ref[qi - j][None]).reshape(2 * tq, tk)
        m_prev = m_scr[...]
        m_next = jnp.maximum(m_prev, jnp.max(s, axis=-1, keepdims=True))
        alpha = jnp.exp(m_prev - m_next)
        p = jnp.exp(s - pltpu.repeat(m_next, tk // LANES, 1))
        l_scr[...] = alpha * l_scr[...] + jnp.sum(p, axis=-1, keepdims=True)
        acc_scr[...] = alpha * acc_scr[...] + jnp.dot(p.astype(BF16), v, preferred_element_type=F32)
        m_scr[...] = m_next
        return carry

    lax.fori_loop(0, qi + 1, kv_step, 0)

    o = acc_scr[...] / l_scr[...]
    o0, o1 = o[:tq], o[tq:]
    if mode == "dilated":
        out = jnp.where(map0, o0, o1)
    else:
        out = o0 - lam_ref[...] * o1
        ms = jnp.mean(out * out, axis=-1, keepdims=True)
        out = out * lax.rsqrt(ms + HEAD_NORM_EPS) * ng_ref[...] * out_scale
    o_ref[0] = (out * _silu(z_ref[0, 0].astype(F32))).astype(o_ref.dtype)


def _dilated_bias(L, blk):
    n = L // blk
    r = np.arange(blk)[:, None]
    c = np.arange(blk)[None, :]
    out = np.full((n, blk, blk), NEG_BIG, np.float32)
    for d in range(n):
        delta = d * blk + r - c
        mult = np.zeros((blk, blk), np.int64)
        for window, dil in DIL_PAIRS:
            mult += (delta >= 0) & (delta % dil == 0) & (delta <= window)
        out[d] = np.where(mult > 0, np.log(np.maximum(mult, 1)), NEG_BIG)
    return jnp.asarray(out)


def _causal_bias(L, blk):
    n = L // blk
    r = np.arange(blk)[:, None]
    c = np.arange(blk)[None, :]
    out = np.zeros((n, blk, blk), np.float32)
    out[0] = np.where(r >= c, 0.0, NEG_BIG)
    return jnp.asarray(out)


def _attention(proj, q0, k0, v0, z0, n_blocks, bias, lam, norm_g, mode, out_scale):
    B, _, L, _ = proj.shape
    tq = bias.shape[1]
    kern = functools.partial(_attn_kernel, mode=mode, out_scale=out_scale)
    c2 = lambda b, n, i: (0, 0)
    return pl.pallas_call(
        kern,
        grid=(B, n_blocks, L // tq),
        in_specs=[
            pl.BlockSpec((1, 1, tq, LANES), lambda b, n, i: (b, q0 + n, i, 0)),
            pl.BlockSpec((1, 1, L, LANES), lambda b, n, i: (b, k0 + n, 0, 0)),
            pl.BlockSpec((1, 1, L, LANES), lambda b, n, i: (b, v0 + n, 0, 0)),
            pl.BlockSpec((1, 1, tq, LANES), lambda b, n, i: (b, z0 + n, i, 0)),
            pl.BlockSpec(bias.shape, lambda b, n, i: (0, 0, 0)),
            pl.BlockSpec((1, LANES), c2),
            pl.BlockSpec((1, LANES), c2),
        ],
        out_specs=pl.BlockSpec((1, tq, LANES), lambda b, n, i: (b, i, n)),
        out_shape=jax.ShapeDtypeStruct((B, L, n_blocks * LANES), BF16),
        scratch_shapes=[
            pltpu.VMEM((2 * tq, LANES), F32),
            pltpu.VMEM((2 * tq, LANES), F32),
            pltpu.VMEM((2 * tq, LANES), F32),
        ],
        compiler_params=_params("parallel", "parallel", "arbitrary"),
        name="attn_" + mode,
    )(proj, proj, proj, proj, bias, lam, norm_g)


def _rope_tables(L):
    dh = ATTN_HEAD_DIM
    inv = ROPE_THETA ** (-jnp.arange(0, dh, 2, dtype=F32) / dh)
    ang = jnp.arange(L, dtype=F32)[:, None] * inv[None, :]
    cos, sin = jnp.cos(ang), jnp.sin(ang)
    reps = LANES // dh
    cos_t = jnp.tile(jnp.concatenate([cos, cos], axis=-1), (1, reps))
    sin_t = jnp.tile(jnp.concatenate([-sin, sin], axis=-1), (1, reps))
    return cos_t, sin_t


def _even_layer(h, pre_g, post_g, w_in, s5_params, ml_params, w_out):
    proj = _norm_proj(h, pre_g, w_in.astype(BF16))
    a = _s5_branch(proj, *s5_params)
    b = _mlstm_branch(proj, *ml_params)
    return _out_proj(a, b, w_out, post_g, h)


def _odd_layer(h, pre_g, post_g, w_in, lq1, lk1, lq2, lk2, diff_norm, w_out, layer_idx):
    B, L, D = h.shape
    W = w_in.shape[1] // 8
    nblk = W // LANES
    cos_t, sin_t = _rope_tables(L)
    proj = _norm_proj_rope(h, pre_g, w_in.astype(BF16), cos_t, sin_t)
    blk = min(ATTN_BLOCK, L)
    lam_init = 0.8 - 0.6 * math.exp(-0.3 * layer_idx)
    lam = (jnp.exp(jnp.sum(lq1.astype(F32) * lk1.astype(F32)))
           - jnp.exp(jnp.sum(lq2.astype(F32) * lk2.astype(F32))) + lam_init)
    lam_row = jnp.full((1, LANES), lam, F32)
    ones_row = jnp.ones((1, LANES), F32)
    c_out = _attention(proj, 0, nblk, 2 * nblk, 3 * nblk, nblk, _dilated_bias(L, blk), ones_row, ones_row,
                       "dilated", 1.0)
    d_out = _attention(proj, 4 * nblk, 5 * nblk, 6 * nblk, 7 * nblk, nblk, _causal_bias(L, blk), lam_row,
                       diff_norm.reshape(1, LANES).astype(F32), "diff", 1.0 - lam_init)
    return _out_proj(c_out, d_out, w_out, post_g, h)


def kernel(x, pre_norm, post_norm, w_in_ab, s5_lambda_re, s5_lambda_im, s5_log_dt, s5_b_re, s5_b_im,
           s5_c_re, s5_c_im, s5_d, s5_glu_w, s5_glu_b, ml_conv_w, ml_conv_b, ml_wq, ml_wk, ml_wv,
           ml_gate_w, ml_gate_b, ml_norm, ml_skip, w_out_ab, w_in_cd, diff_lq1, diff_lk1, diff_lq2,
           diff_lk2, diff_norm, w_out_cd):
    depth = pre_norm.shape[0]
    h = x
    for l in range(depth):
        i = l // 2
        if l % 2 == 0:
            s5_params = (s5_lambda_re[i], s5_lambda_im[i], s5_log_dt[i], s5_b_re[i], s5_b_im[i],
                         s5_c_re[i], s5_c_im[i], s5_d[i], s5_glu_w[i], s5_glu_b[i])
            ml_params = (ml_conv_w[i], ml_conv_b[i], ml_wq[i], ml_wk[i], ml_wv[i], ml_gate_w[i],
                         ml_gate_b[i], ml_norm[i], ml_skip[i])
            h = _even_layer(h, pre_norm[l], post_norm[l], w_in_ab[i], s5_params, ml_params, w_out_ab[i])
        else:
            h = _odd_layer(h, pre_norm[l], post_norm[l], w_in_cd[i], diff_lq1[i], diff_lk1[i], diff_lq2[i],
                           diff_lk2[i], diff_norm[i], w_out_cd[i], l)
    return h
```

```python
import functools
import math

import numpy as np
import jax
import jax.numpy as jnp
from jax import lax
from jax.experimental import pallas as pl
from jax.experimental.pallas import tpu as pltpu

F32 = jnp.float32
BF16 = jnp.bfloat16

S5_GROUP = 16
S5_STATE = 64
ML_HEADS = 4
ML_CONV = 4
DIL_PAIRS = ((128, 1), (512, 4), (2048, 16))
ROPE_THETA = 10000.0
NORM_EPS = 1e-6
HEAD_NORM_EPS = 1e-5
ATTN_HEAD_DIM = 64
QUERY_SCALE = ATTN_HEAD_DIM ** -0.5 * math.log2(math.e)

LANES = 128
SUBLANES = 8
VMEM_LIMIT_BYTES = 56 * 1024 * 1024

NEG_BIG = -1e30

PROJ_ROWS = 512
PROJ_COLS = 512
S5_BATCH = SUBLANES
S5_TIME = 128
ML_CHUNK = 256
ATTN_BLOCK = 256


def _params(*semantics):
    return pltpu.CompilerParams(dimension_semantics=semantics, vmem_limit_bytes=VMEM_LIMIT_BYTES)


def _silu(x):
    return x * jax.nn.sigmoid(x)


def _norm_proj_kernel(x_ref, g_ref, w_ref, o_ref, z_scr):
    x = x_ref[0]
    ms = jnp.mean(x * x, axis=-1, keepdims=True)
    z_scr[...] = (x * lax.rsqrt(ms + NORM_EPS) * g_ref[...]).astype(BF16)
    n_out = w_ref.shape[1]
    for c in range(n_out // PROJ_COLS):
        cols = slice(c * PROJ_COLS, (c + 1) * PROJ_COLS)
        o_ref[0, :, cols] = jnp.dot(z_scr[...], w_ref[:, cols], preferred_element_type=F32).astype(o_ref.dtype)


def _norm_proj(x, g, w):
    B, L, D = x.shape
    N = w.shape[1]
    tm = min(PROJ_ROWS, L)
    return pl.pallas_call(
        _norm_proj_kernel,
        grid=(B, L // tm),
        in_specs=[
            pl.BlockSpec((1, tm, D), lambda b, i: (b, i, 0)),
            pl.BlockSpec((1, D), lambda b, i: (0, 0)),
            pl.BlockSpec((D, N), lambda b, i: (0, 0)),
        ],
        out_specs=pl.BlockSpec((1, tm, N), lambda b, i: (b, i, 0)),
        out_shape=jax.ShapeDtypeStruct((B, L, N), BF16),
        scratch_shapes=[pltpu.VMEM((tm, D), BF16)],
        compiler_params=_params("parallel", "parallel"),
        name="norm_proj_even",
    )(x, g.reshape(1, D), w)


def _swap_halves(x, first_half):
    return jnp.where(first_half, pltpu.roll(x, LANES - ATTN_HEAD_DIM // 2, 1), pltpu.roll(x, ATTN_HEAD_DIM // 2, 1))


ODD_CHUNKS = (("q", 0), ("k", 4), ("v", 0), ("z", 8), ("q", 12), ("k", 16), ("v", 4), ("z", 20))


def _norm_proj_rope_kernel(x_ref, g_ref, w_ref, cos_ref, sin_ref, o_ref, vt_ref, z_scr):
    x = x_ref[0]
    ms = jnp.mean(x * x, axis=-1, keepdims=True)
    z_scr[...] = (x * lax.rsqrt(ms + NORM_EPS) * g_ref[...]).astype(BF16)
    slabs = PROJ_COLS // LANES
    lane = lax.broadcasted_iota(jnp.int32, (1, LANES), 1)
    first_half = (lane % ATTN_HEAD_DIM) < (ATTN_HEAD_DIM // 2)
    for c, (kind, base) in enumerate(ODD_CHUNKS):
        y = jnp.dot(z_scr[...], w_ref[:, c * PROJ_COLS:(c + 1) * PROJ_COLS], preferred_element_type=F32)
        for s in range(slabs):
            t = y[:, s * LANES:(s + 1) * LANES]
            if kind in ("q", "k"):
                t = t * cos_ref[...] + _swap_halves(t, first_half) * sin_ref[...]
            if kind == "q":
                t = t * QUERY_SCALE
            if kind == "v":
                vt_ref[0, base + s] = jnp.transpose(t).astype(vt_ref.dtype)
            else:
                o_ref[0, base + s] = t.astype(o_ref.dtype)


def _norm_proj_rope(x, g, w, cos_t, sin_t):
    B, L, D = x.shape
    N = w.shape[1]
    assert N == len(ODD_CHUNKS) * PROJ_COLS
    tm = min(PROJ_ROWS, L)
    n_v = sum(kind == "v" for kind, _ in ODD_CHUNKS) * (PROJ_COLS // LANES)
    n_o = N // LANES - n_v
    return pl.pallas_call(
        _norm_proj_rope_kernel,
        grid=(B, L // tm),
        in_specs=[
            pl.BlockSpec((1, tm, D), lambda b, i: (b, i, 0)),
            pl.BlockSpec((1, D), lambda b, i: (0, 0)),
            pl.BlockSpec((D, N), lambda b, i: (0, 0)),
            pl.BlockSpec((tm, LANES), lambda b, i: (i, 0)),
            pl.BlockSpec((tm, LANES), lambda b, i: (i, 0)),
        ],
        out_specs=[
            pl.BlockSpec((1, n_o, tm, LANES), lambda b, i: (b, 0, i, 0)),
            pl.BlockSpec((1, n_v, LANES, tm), lambda b, i: (b, 0, 0, i)),
        ],
        out_shape=[
            jax.ShapeDtypeStruct((B, n_o, L, LANES), BF16),
            jax.ShapeDtypeStruct((B, n_v, LANES, L), BF16),
        ],
        scratch_shapes=[pltpu.VMEM((tm, D), BF16)],
        compiler_params=_params("parallel", "parallel"),
        name="norm_proj_odd",
    )(x, g.reshape(1, D), w, cos_t, sin_t)


def _out_proj_kernel(a_ref, b_ref, wa_ref, wb_ref, g_ref, h_ref, o_ref):
    y = jnp.dot(a_ref[0], wa_ref[...], preferred_element_type=F32)
    y = y + jnp.dot(b_ref[0], wb_ref[...], preferred_element_type=F32)
    ms = jnp.mean(y * y, axis=-1, keepdims=True)
    o_ref[0] = h_ref[0] + y * lax.rsqrt(ms + NORM_EPS) * g_ref[...]


def _out_proj(a, b, w_out, g, h):
    B, L, W = a.shape
    D = w_out.shape[1]
    tm = min(PROJ_ROWS, L)
    return pl.pallas_call(
        _out_proj_kernel,
        grid=(B, L // tm),
        in_specs=[
            pl.BlockSpec((1, tm, W), lambda b_, i: (b_, i, 0)),
            pl.BlockSpec((1, tm, W), lambda b_, i: (b_, i, 0)),
            pl.BlockSpec((W, D), lambda b_, i: (0, 0)),
            pl.BlockSpec((W, D), lambda b_, i: (0, 0)),
            pl.BlockSpec((1, D), lambda b_, i: (0, 0)),
            pl.BlockSpec((1, tm, D), lambda b_, i: (b_, i, 0)),
        ],
        out_specs=pl.BlockSpec((1, tm, D), lambda b_, i: (b_, i, 0)),
        out_shape=jax.ShapeDtypeStruct((B, L, D), F32),
        compiler_params=_params("parallel", "parallel"),
        name="out_proj",
    )(a, b, w_out[:W].astype(BF16), w_out[W:].astype(BF16), g.reshape(1, D), h)


def _s5_kernel(u_ref, z_ref, bmat_ref, cmat_ref, ar_ref, ai_ref, d_ref, gw_ref, gb_ref, o_ref,
               us_scr, xs_scr, ys_scr, yn_scr, st_scr):
    nb, T, W = u_ref.shape
    n_blk = W // LANES
    half = xs_scr.shape[1] // 2
    rows = nb * T

    @pl.when(pl.program_id(1) == 0)
    def _():
        st_scr[...] = jnp.zeros_like(st_scr)

    for b in range(nb):
        ub = u_ref[b].astype(F32)
        for k in range(n_blk):
            us_scr[k, pl.ds(b, T, stride=nb), :] = ub[:, k * LANES:(k + 1) * LANES]

    for k in range(n_blk):
        xs_scr[...] = jnp.dot(us_scr[k].astype(BF16), bmat_ref[k], preferred_element_type=F32)
        ar = ar_ref[k]
        ai = ai_ref[k]

        def step(t, carry):
            xr, xi = carry
            r = pl.multiple_of(t * nb, nb)
            nr = ar * xr - ai * xi + xs_scr[pl.ds(r, nb), 0:half]
            ni = ar * xi + ai * xr + xs_scr[pl.ds(r, nb), half:2 * half]
            xs_scr[pl.ds(r, nb), 0:half] = nr
            xs_scr[pl.ds(r, nb), half:2 * half] = ni
            return nr, ni

        xr, xi = lax.fori_loop(0, T, step, (st_scr[k, 0], st_scr[k, 1]), unroll=2)
        st_scr[k, 0] = xr
        st_scr[k, 1] = xi
        ys_scr[k] = jnp.dot(xs_scr[...].astype(BF16), cmat_ref[k], preferred_element_type=F32)

    for b in range(nb):
        for k in range(n_blk):
            yn_scr[b * T:(b + 1) * T, k * LANES:(k + 1) * LANES] = ys_scr[k, pl.ds(b, T, stride=nb), :]

    u = u_ref[...].reshape(rows, W).astype(F32)
    y = yn_scr[...] + d_ref[...] * u
    g = jax.nn.gelu(y)
    gate = jnp.dot(g.astype(BF16), gw_ref[...], preferred_element_type=F32) + gb_ref[...]
    out = g * jax.nn.sigmoid(gate) * _silu(z_ref[...].reshape(rows, W).astype(F32))
    o_ref[...] = out.reshape(nb, T, W).astype(o_ref.dtype)


def _s5_weights(lam_re, lam_im, log_dt, b_re, b_im, c_re, c_im):
    G, P = lam_re.shape
    gpb = LANES // S5_GROUP
    n_blk = G // gpb
    lr, li = lam_re.astype(F32), lam_im.astype(F32)
    dt = jnp.exp(log_dt.astype(F32))[:, None]
    mag = jnp.exp(lr * dt)
    ab_r, ab_i = mag * jnp.cos(li * dt), mag * jnp.sin(li * dt)
    den = lr * lr + li * li
    f_r = ((ab_r - 1.0) * lr + ab_i * li) / den
    f_i = (ab_i * lr - (ab_r - 1.0) * li) / den
    br, bi = b_re.astype(F32), b_im.astype(F32)
    bb_r = f_r[..., None] * br - f_i[..., None] * bi
    bb_i = f_r[..., None] * bi + f_i[..., None] * br
    eye = jnp.eye(gpb, dtype=F32)
    bb = jnp.stack([bb_r, bb_i]).reshape(2, n_blk, gpb, P, S5_GROUP)
    bb = jnp.transpose(bb, (1, 2, 4, 0, 3))
    bmat = bb[:, :, :, :, None, :] * eye[None, :, None, None, :, None]
    bmat = bmat.reshape(n_blk, LANES, 2 * gpb * P).astype(BF16)
    cc = jnp.stack([c_re.astype(F32), -c_im.astype(F32)]).reshape(2, n_blk, gpb, S5_GROUP, P)
    cc = jnp.transpose(cc, (1, 0, 2, 4, 3))
    cmat = cc[:, :, :, :, None, :] * eye[None, None, :, None, :, None]
    cmat = cmat.reshape(n_blk, 2 * gpb * P, LANES).astype(BF16)
    ar = jnp.broadcast_to(ab_r.reshape(n_blk, 1, gpb * P), (n_blk, S5_BATCH, gpb * P))
    ai = jnp.broadcast_to(ab_i.reshape(n_blk, 1, gpb * P), (n_blk, S5_BATCH, gpb * P))
    return bmat, cmat, ar, ai


def _s5_branch(proj, lam_re, lam_im, log_dt, b_re, b_im, c_re, c_im, d_skip, glu_w, glu_b):
    B, L, N = proj.shape
    W = N // 4
    nb = S5_BATCH
    T = min(S5_TIME, L)
    n_blk = W // LANES
    bmat, cmat, ar, ai = _s5_weights(lam_re, lam_im, log_dt, b_re, b_im, c_re, c_im)
    S2 = bmat.shape[2]
    const3 = lambda g, c: (0, 0, 0)
    const2 = lambda g, c: (0, 0)
    return pl.pallas_call(
        _s5_kernel,
        grid=(B // nb, L // T),
        in_specs=[
            pl.BlockSpec((nb, T, W), lambda g, c: (g, c, 0)),
            pl.BlockSpec((nb, T, W), lambda g, c: (g, c, 1)),
            pl.BlockSpec(bmat.shape, const3),
            pl.BlockSpec(cmat.shape, const3),
            pl.BlockSpec(ar.shape, const3),
            pl.BlockSpec(ai.shape, const3),
            pl.BlockSpec((1, W), const2),
            pl.BlockSpec((W, W), const2),
            pl.BlockSpec((1, W), const2),
        ],
        out_specs=pl.BlockSpec((nb, T, W), lambda g, c: (g, c, 0)),
        out_shape=jax.ShapeDtypeStruct((B, L, W), BF16),
        scratch_shapes=[
            pltpu.VMEM((n_blk, nb * T, LANES), F32),
            pltpu.VMEM((nb * T, S2), F32),
            pltpu.VMEM((n_blk, nb * T, LANES), F32),
            pltpu.VMEM((nb * T, W), F32),
            pltpu.VMEM((n_blk, 2, nb, S2 // 2), F32),
        ],
        compiler_params=_params("parallel", "arbitrary"),
        name="s5_scan",
    )(proj, proj, bmat, cmat, ar, ai, d_skip.reshape(1, W).astype(F32), glu_w.astype(BF16),
      glu_b.reshape(1, W).astype(F32))


def _log_sigmoid(x):
    return jnp.minimum(x, 0.0) - jnp.log1p(jnp.exp(-jnp.abs(x)))


def _split3(x):
    hi = x.astype(BF16)
    r1 = x - hi.astype(F32)
    mid = r1.astype(BF16)
    lo = (r1 - mid.astype(F32)).astype(BF16)
    return hi, mid, lo


def _mlstm_kernel(x_ref, z_ref, cw_ref, cb_ref, wq_ref, wk_ref, wv_ref, gw_ref, gwt_ref, gbc_ref, gbr_ref,
                  ng_ref, sk_ref, o_ref, xe_scr, qkv_scr, ks_scr, ct_scr, n_scr, m_scr):
    Tc, W = x_ref.shape[1], x_ref.shape[2]
    H = wq_ref.shape[0]
    dh = W // H
    pad = SUBLANES

    @pl.when(pl.program_id(1) == 0)
    def _():
        xe_scr[0:pad, :] = jnp.zeros((pad, W), F32)
        ct_scr[...] = jnp.zeros_like(ct_scr)
        n_scr[...] = jnp.zeros_like(n_scr)
        m_scr[...] = jnp.zeros_like(m_scr)

    xb = x_ref[0]
    x = xb.astype(F32)
    xe_scr[pad:pad + Tc, :] = x
    xc = cb_ref[...] + sum(xe_scr[pl.ds(pad - (ML_CONV - 1) + j, Tc), :] * cw_ref[j:j + 1, :]
                           for j in range(ML_CONV))
    xc = _silu(xc)
    xe_scr[0:pad, :] = x[Tc - pad:Tc, :]
    xcb = xc.astype(BF16)

    for h in range(H):
        cols = slice(h * dh, (h + 1) * dh)
        q = jnp.dot(xcb[:, cols], wq_ref[h], preferred_element_type=F32)
        k = jnp.dot(xcb[:, cols], wk_ref[h], preferred_element_type=F32)
        v = jnp.dot(xb[:, cols], wv_ref[h], preferred_element_type=F32)
        qkv_scr[:, h * dh:(h + 1) * dh] = q.astype(BF16)
        qkv_scr[:, W + h * dh:W + (h + 1) * dh] = k.astype(BF16)
        qkv_scr[:, 2 * W + h * dh:2 * W + (h + 1) * dh] = v.astype(BF16)
        ks_scr[:, cols] = (k * (dh ** -0.5)).astype(BF16)

    qkv = qkv_scr[...]
    g_col = jnp.dot(qkv, gw_ref[...], preferred_element_type=F32) + gbc_ref[...]
    g_row = lax.dot_general(gwt_ref[...], qkv, (((1,), (1,)), ((), ())),
                            preferred_element_type=F32) + gbr_ref[...]
    ti = lax.broadcasted_iota(jnp.int32, (Tc, Tc), 0)
    si = lax.broadcasted_iota(jnp.int32, (Tc, Tc), 1)
    causal = si <= ti
    lower = jnp.where(causal, 1.0, 0.0).astype(BF16)
    upper = jnp.where(ti <= si, 1.0, 0.0).astype(BF16)
    b_col = sum(jnp.dot(lower, p, preferred_element_type=F32) for p in _split3(_log_sigmoid(g_col)))
    b_row = sum(jnp.dot(p, upper, preferred_element_type=F32) for p in _split3(_log_sigmoid(g_row)))

    for h in range(H):
        cols = slice(h * dh, (h + 1) * dh)
        q = qkv_scr[:, h * dh:(h + 1) * dh]
        v = qkv_scr[:, 2 * W + h * dh:2 * W + (h + 1) * dh]
        ks = ks_scr[:, cols]
        bc = b_col[:, H + h:H + h + 1]
        br = b_row[H + h:H + h + 1, :]
        li_c = g_col[:, h:h + 1]
        li_r = g_row[h:h + 1, :]
        m_prev = m_scr[h][:, 0:1]

        s_qk = lax.dot_general(q, ks, (((1,), (1,)), ((), ())), preferred_element_type=F32)
        log_d = jnp.where(causal, bc - br + li_r, NEG_BIG)
        inter = bc + m_prev
        m_t = jnp.maximum(inter, jnp.max(log_d, axis=-1, keepdims=True))
        s = s_qk * jnp.exp(log_d - m_t)
        sc = jnp.exp(inter - m_t)
        ct = ct_scr[h]
        n = n_scr[h]
        num = sc * jnp.dot(q, ct.astype(BF16), preferred_element_type=F32)
        num = num + jnp.dot(s.astype(BF16), v, preferred_element_type=F32)
        dnm = sc * jnp.sum(q.astype(F32) * n, axis=-1, keepdims=True) + jnp.sum(s, axis=-1, keepdims=True)
        hh = num / jnp.maximum(jnp.abs(dnm), jnp.exp(-m_t))

        b_last = bc[Tc - 1:Tc, :]
        w_log = b_last - bc + li_c
        m_new = jnp.maximum(b_last + m_prev, jnp.max(w_log, axis=0, keepdims=True))
        kw = ks.astype(F32) * jnp.exp(w_log - m_new)
        decay = jnp.exp(b_last + m_prev - m_new)
        ct_scr[h] = decay * ct + lax.dot_general(kw.astype(BF16), v, (((0,), (0,)), ((), ())),
                                                 preferred_element_type=F32)
        n_scr[h] = decay * n + jnp.sum(kw, axis=0, keepdims=True)
        m_scr[h] = jnp.broadcast_to(m_new, (1, LANES))

        mu = jnp.mean(hh, axis=-1, keepdims=True)
        cen = hh - mu
        var = jnp.mean(cen * cen, axis=-1, keepdims=True)
        out = cen * lax.rsqrt(var + HEAD_NORM_EPS) * ng_ref[:, cols] + sk_ref[:, cols] * xc[:, cols]
        o_ref[0, :, cols] = (out * _silu(z_ref[0, :, cols].astype(F32))).astype(o_ref.dtype)


def _mlstm_branch(proj, conv_w, conv_b, wq, wk, wv, gate_w, gate_b, norm_g, skip):
    B, L, N = proj.shape
    W = N // 4
    H = wq.shape[0]
    dh = W // H
    Tc = min(ML_CHUNK, L)
    gw = jnp.zeros((3 * W, LANES), F32).at[:, :2 * H].set(gate_w.astype(F32)).astype(BF16)
    gwt = jnp.transpose(gate_w.astype(F32)).astype(BF16)
    gbc = jnp.zeros((1, LANES), F32).at[0, :2 * H].set(gate_b.astype(F32))
    gbr = gate_b.astype(F32).reshape(2 * H, 1)
    c2 = lambda b, c: (0, 0)
    c3 = lambda b, c: (0, 0, 0)
    return pl.pallas_call(
        _mlstm_kernel,
        grid=(B, L // Tc),
        in_specs=[
            pl.BlockSpec((1, Tc, W), lambda b, c: (b, c, 2)),
            pl.BlockSpec((1, Tc, W), lambda b, c: (b, c, 3)),
            pl.BlockSpec((ML_CONV, W), c2),
            pl.BlockSpec((1, W), c2),
            pl.BlockSpec((H, dh, dh), c3),
            pl.BlockSpec((H, dh, dh), c3),
            pl.BlockSpec((H, dh, dh), c3),
            pl.BlockSpec((3 * W, LANES), c2),
            pl.BlockSpec((2 * H, 3 * W), c2),
            pl.BlockSpec((1, LANES), c2),
            pl.BlockSpec((2 * H, 1), c2),
            pl.BlockSpec((1, W), c2),
            pl.BlockSpec((1, W), c2),
        ],
        out_specs=pl.BlockSpec((1, Tc, W), lambda b, c: (b, c, 0)),
        out_shape=jax.ShapeDtypeStruct((B, L, W), BF16),
        scratch_shapes=[
            pltpu.VMEM((SUBLANES + Tc, W), F32),
            pltpu.VMEM((Tc, 3 * W), BF16),
            pltpu.VMEM((Tc, W), BF16),
            pltpu.VMEM((H, dh, dh), F32),
            pltpu.VMEM((H, 1, dh), F32),
            pltpu.VMEM((H, 1, LANES), F32),
        ],
        compiler_params=_params("parallel", "arbitrary"),
        name="mlstm",
    )(proj, proj, conv_w.astype(F32), conv_b.reshape(1, W).astype(F32), wq.astype(BF16), wk.astype(BF16),
      wv.astype(BF16), gw, gwt, gbc, gbr, norm_g.reshape(1, W).astype(F32), skip.reshape(1, W).astype(F32))


def _attn_kernel(q_ref, k_ref, vt_ref, z_ref, bias_ref, lam_ref, ng_ref, o_ref, m_scr, acc_scr, s_scr, *,
                 mode, out_scale):
    tq = q_ref.shape[2]
    tk = tq
    qi = pl.program_id(2)
    lane = lax.broadcasted_iota(jnp.int32, (1, LANES), 1)
    map0 = lane < ATTN_HEAD_DIM
    q = q_ref[0, 0]
    zero = jnp.zeros_like(q)
    q2 = jnp.concatenate([jnp.where(map0, q, zero), jnp.where(map0, zero, q)], axis=0)

    m_scr[...] = jnp.full(m_scr.shape, NEG_BIG, F32)
    acc_scr[...] = jnp.zeros_like(acc_scr)
    ones_rows = jnp.ones((SUBLANES, tk), BF16)

    def scores(slot, j, biased):
        r = pl.multiple_of(j * tk, tk)
        k = k_ref[0, 0, pl.ds(r, tk), :]
        s = lax.dot_general(k, q2, (((1,), (1,)), ((), ())), preferred_element_type=F32)
        s_scr[slot] = s + bias_ref[qi - j + 1] if biased else s

    def update(slot, j):
        r = pl.multiple_of(j * tk, tk)
        vt = vt_ref[0, 0, :, pl.ds(r, tk)]
        s = s_scr[slot]
        m_prev = m_scr[...]
        m_next = jnp.maximum(m_prev, jnp.max(s, axis=0, keepdims=True))
        alpha = jnp.exp2(m_prev - m_next)
        p = jnp.exp2(s - m_next).astype(BF16)
        pv = jnp.dot(jnp.concatenate([vt, ones_rows], axis=0), p, preferred_element_type=F32)
        acc_scr[...] = alpha * acc_scr[...] + pv
        m_scr[...] = m_next

    always = mode == "dilated"
    n_pairs = (qi + 2) // 2
    scores(0, 0, True)

    def pair(i, carry):
        j0 = 2 * i
        scores(1, j0 + 1, always)
        update(0, j0)
        scores(0, j0 + 2, True)
        update(1, j0 + 1)
        return carry

    lax.fori_loop(0, n_pairs - 1, pair, 0)
    j_last = 2 * (n_pairs - 1)
    scores(1, j_last + 1, True)
    update(0, j_last)
    update(1, j_last + 1)

    acc = acc_scr[...]
    o = acc[:LANES] / acc[LANES:LANES + 1]
    o0, o1 = o[:, :tq], o[:, tq:]
    if mode == "dilated":
        feat = lax.broadcasted_iota(jnp.int32, (LANES, 1), 0)
        out = jnp.transpose(jnp.where(feat < ATTN_HEAD_DIM, o0, o1))
    else:
        out = jnp.transpose(o0 - lam_ref[:, 0:1] * o1)
        ms = jnp.mean(out * out, axis=-1, keepdims=True)
        out = out * lax.rsqrt(ms + HEAD_NORM_EPS) * ng_ref[...] * out_scale
    o_ref[0] = (out * _silu(z_ref[0, 0].astype(F32))).astype(o_ref.dtype)


def _dilated_multiplicity(delta):
    mult = np.zeros(delta.shape, np.int64)
    for window, dil in DIL_PAIRS:
        mult += (delta >= 0) & (delta % dil == 0) & (delta <= window)
    return mult


def _bias_tables(L, blk, mode):
    n = L // blk
    key = np.arange(blk)[:, None]
    qry = np.arange(blk)[None, :]
    out = np.full((n + 1, blk, blk), NEG_BIG, np.float32)
    for d in range(n):
        delta = d * blk + qry - key
        mult = _dilated_multiplicity(delta) if mode == "dilated" else (delta >= 0).astype(np.int64)
        out[d + 1] = np.where(mult > 0, np.log2(np.maximum(mult, 1)), NEG_BIG)
    return jnp.asarray(np.concatenate([out, out], axis=2))


def _attention(qkz, vt, q0, k0, v0, z0, n_blocks, bias, lam, norm_g, mode, out_scale):
    B, _, L, _ = qkz.shape
    tq = bias.shape[1]
    assert (L // tq) % 2 == 0
    kern = functools.partial(_attn_kernel, mode=mode, out_scale=out_scale)
    c2 = lambda b, n, i: (0, 0)
    return pl.pallas_call(
        kern,
        grid=(B, n_blocks, L // tq),
        in_specs=[
            pl.BlockSpec((1, 1, tq, LANES), lambda b, n, i: (b, q0 + n, i, 0)),
            pl.BlockSpec((1, 1, L, LANES), lambda b, n, i: (b, k0 + n, 0, 0)),
            pl.BlockSpec((1, 1, LANES, L), lambda b, n, i: (b, v0 + n, 0, 0)),
            pl.BlockSpec((1, 1, tq, LANES), lambda b, n, i: (b, z0 + n, i, 0)),
            pl.BlockSpec(bias.shape, lambda b, n, i: (0, 0, 0)),
            pl.BlockSpec((1, LANES), c2),
            pl.BlockSpec((1, LANES), c2),
        ],
        out_specs=pl.BlockSpec((1, tq, LANES), lambda b, n, i: (b, i, n)),
        out_shape=jax.ShapeDtypeStruct((B, L, n_blocks * LANES), BF16),
        scratch_shapes=[
            pltpu.VMEM((1, 2 * tq), F32),
            pltpu.VMEM((LANES + SUBLANES, 2 * tq), F32),
            pltpu.VMEM((2, tq, 2 * tq), F32),
        ],
        compiler_params=_params("parallel", "parallel", "arbitrary"),
        name="attn_" + mode,
    )(qkz, qkz, vt, qkz, bias, lam, norm_g)


def _rope_tables(L):
    dh = ATTN_HEAD_DIM
    inv = ROPE_THETA ** (-jnp.arange(0, dh, 2, dtype=F32) / dh)
    ang = jnp.arange(L, dtype=F32)[:, None] * inv[None, :]
    cos, sin = jnp.cos(ang), jnp.sin(ang)
    reps = LANES // dh
    cos_t = jnp.tile(jnp.concatenate([cos, cos], axis=-1), (1, reps))
    sin_t = jnp.tile(jnp.concatenate([-sin, sin], axis=-1), (1, reps))
    return cos_t, sin_t


def _even_layer(h, pre_g, post_g, w_in, s5_params, ml_params, w_out):
    proj = _norm_proj(h, pre_g, w_in.astype(BF16))
    a = _s5_branch(proj, *s5_params)
    b = _mlstm_branch(proj, *ml_params)
    return _out_proj(a, b, w_out, post_g, h)


def _odd_layer(h, pre_g, post_g, w_in, lq1, lk1, lq2, lk2, diff_norm, w_out, layer_idx):
    B, L, D = h.shape
    nblk = PROJ_COLS // LANES
    cos_t, sin_t = _rope_tables(L)
    qkz, vt = _norm_proj_rope(h, pre_g, w_in.astype(BF16), cos_t, sin_t)
    blk = min(ATTN_BLOCK, L)
    lam_init = 0.8 - 0.6 * math.exp(-0.3 * layer_idx)
    lam = (jnp.exp(jnp.sum(lq1.astype(F32) * lk1.astype(F32)))
           - jnp.exp(jnp.sum(lq2.astype(F32) * lk2.astype(F32))) + lam_init)
    lam_row = jnp.full((1, LANES), lam, F32)
    ones_row = jnp.ones((1, LANES), F32)
    base = [b for _, b in ODD_CHUNKS]
    c_out = _attention(qkz, vt, base[0], base[1], base[2], base[3], nblk, _bias_tables(L, blk, "dilated"),
                       ones_row, ones_row, "dilated", 1.0)
    d_out = _attention(qkz, vt, base[4], base[5], base[6], base[7], nblk, _bias_tables(L, blk, "diff"),
                       lam_row, diff_norm.reshape(1, LANES).astype(F32), "diff", 1.0 - lam_init)
    return _out_proj(c_out, d_out, w_out, post_g, h)


def kernel(x, pre_norm, post_norm, w_in_ab, s5_lambda_re, s5_lambda_im, s5_log_dt, s5_b_re, s5_b_im,
           s5_c_re, s5_c_im, s5_d, s5_glu_w, s5_glu_b, ml_conv_w, ml_conv_b, ml_wq, ml_wk, ml_wv,
           ml_gate_w, ml_gate_b, ml_norm, ml_skip, w_out_ab, w_in_cd, diff_lq1, diff_lk1, diff_lq2,
           diff_lk2, diff_norm, w_out_cd):
    depth = pre_norm.shape[0]
    h = x
    for l in range(depth):
        i = l // 2
        if l % 2 == 0:
            s5_params = (s5_lambda_re[i], s5_lambda_im[i], s5_log_dt[i], s5_b_re[i], s5_b_im[i],
                         s5_c_re[i], s5_c_im[i], s5_d[i], s5_glu_w[i], s5_glu_b[i])
            ml_params = (ml_conv_w[i], ml_conv_b[i], ml_wq[i], ml_wk[i], ml_wv[i], ml_gate_w[i],
                         ml_gate_b[i], ml_norm[i], ml_skip[i])
            h = _even_layer(h, pre_norm[l], post_norm[l], w_in_ab[i], s5_params, ml_params, w_out_ab[i])
        else:
            h = _odd_layer(h, pre_norm[l], post_norm[l], w_in_cd[i], diff_lq1[i], diff_lk1[i], diff_lq2[i],
                           diff_lk2[i], diff_norm[i], w_out_cd[i], l)
    return h
```

```python
import functools
import math

import numpy as np
import jax
import jax.numpy as jnp
from jax import lax
from jax.experimental import pallas as pl
from jax.experimental.pallas import tpu as pltpu

F32 = jnp.float32
BF16 = jnp.bfloat16

S5_GROUP = 16
S5_STATE = 64
ML_HEADS = 4
ML_CONV = 4
DIL_PAIRS = ((128, 1), (512, 4), (2048, 16))
ROPE_THETA = 10000.0
NORM_EPS = 1e-6
HEAD_NORM_EPS = 1e-5
ATTN_HEAD_DIM = 64
QUERY_SCALE = ATTN_HEAD_DIM ** -0.5 * math.log2(math.e)

LANES = 128
SUBLANES = 8
VMEM_LIMIT_BYTES = 56 * 1024 * 1024

NEG_BIG = -1e30

PROJ_ROWS = 512
PROJ_COLS = 512
S5_BATCH = SUBLANES
S5_TIME = 128
ML_CHUNK = 256
ATTN_BLOCK = 256
ATTN_LOOKAHEAD = 2


def _params(*semantics):
    return pltpu.CompilerParams(dimension_semantics=semantics, vmem_limit_bytes=VMEM_LIMIT_BYTES)


def _silu(x):
    return x * jax.nn.sigmoid(x)


def _norm_proj_kernel(x_ref, g_ref, w_ref, o_ref, z_scr):
    x = x_ref[0]
    ms = jnp.mean(x * x, axis=-1, keepdims=True)
    z_scr[...] = (x * lax.rsqrt(ms + NORM_EPS) * g_ref[...]).astype(BF16)
    n_out = w_ref.shape[1]
    for c in range(n_out // PROJ_COLS):
        cols = slice(c * PROJ_COLS, (c + 1) * PROJ_COLS)
        o_ref[0, :, cols] = jnp.dot(z_scr[...], w_ref[:, cols], preferred_element_type=F32).astype(o_ref.dtype)


def _norm_proj(x, g, w):
    B, L, D = x.shape
    N = w.shape[1]
    tm = min(PROJ_ROWS, L)
    return pl.pallas_call(
        _norm_proj_kernel,
        grid=(B, L // tm),
        in_specs=[
            pl.BlockSpec((1, tm, D), lambda b, i: (b, i, 0)),
            pl.BlockSpec((1, D), lambda b, i: (0, 0)),
            pl.BlockSpec((D, N), lambda b, i: (0, 0)),
        ],
        out_specs=pl.BlockSpec((1, tm, N), lambda b, i: (b, i, 0)),
        out_shape=jax.ShapeDtypeStruct((B, L, N), BF16),
        scratch_shapes=[pltpu.VMEM((tm, D), BF16)],
        compiler_params=_params("parallel", "parallel"),
        name="norm_proj_even",
    )(x, g.reshape(1, D), w)


def _swap_halves(x, first_half):
    return jnp.where(first_half, pltpu.roll(x, LANES - ATTN_HEAD_DIM // 2, 1), pltpu.roll(x, ATTN_HEAD_DIM // 2, 1))


ODD_CHUNKS = (("q", 0), ("k", 4), ("v", 0), ("z", 8), ("q", 12), ("k", 16), ("v", 4), ("z", 20))


def _norm_proj_rope_kernel(x_ref, g_ref, w_ref, cos_ref, sin_ref, o_ref, vt_ref, z_scr):
    x = x_ref[0]
    ms = jnp.mean(x * x, axis=-1, keepdims=True)
    z_scr[...] = (x * lax.rsqrt(ms + NORM_EPS) * g_ref[...]).astype(BF16)
    slabs = PROJ_COLS // LANES
    lane = lax.broadcasted_iota(jnp.int32, (1, LANES), 1)
    first_half = (lane % ATTN_HEAD_DIM) < (ATTN_HEAD_DIM // 2)
    for c, (kind, base) in enumerate(ODD_CHUNKS):
        y = jnp.dot(z_scr[...], w_ref[:, c * PROJ_COLS:(c + 1) * PROJ_COLS], preferred_element_type=F32)
        for s in range(slabs):
            t = y[:, s * LANES:(s + 1) * LANES]
            if kind in ("q", "k"):
                t = t * cos_ref[...] + _swap_halves(t, first_half) * sin_ref[...]
            if kind == "q":
                t = t * QUERY_SCALE
            if kind == "v":
                vt_ref[0, base + s] = jnp.transpose(t).astype(vt_ref.dtype)
            else:
                o_ref[0, base + s] = t.astype(o_ref.dtype)


def _norm_proj_rope(x, g, w, cos_t, sin_t):
    B, L, D = x.shape
    N = w.shape[1]
    assert N == len(ODD_CHUNKS) * PROJ_COLS
    tm = min(PROJ_ROWS, L)
    n_v = sum(kind == "v" for kind, _ in ODD_CHUNKS) * (PROJ_COLS // LANES)
    n_o = N // LANES - n_v
    return pl.pallas_call(
        _norm_proj_rope_kernel,
        grid=(B, L // tm),
        in_specs=[
            pl.BlockSpec((1, tm, D), lambda b, i: (b, i, 0)),
            pl.BlockSpec((1, D), lambda b, i: (0, 0)),
            pl.BlockSpec((D, N), lambda b, i: (0, 0)),
            pl.BlockSpec((tm, LANES), lambda b, i: (i, 0)),
            pl.BlockSpec((tm, LANES), lambda b, i: (i, 0)),
        ],
        out_specs=[
            pl.BlockSpec((1, n_o, tm, LANES), lambda b, i: (b, 0, i, 0)),
            pl.BlockSpec((1, n_v, LANES, tm), lambda b, i: (b, 0, 0, i)),
        ],
        out_shape=[
            jax.ShapeDtypeStruct((B, n_o, L, LANES), BF16),
            jax.ShapeDtypeStruct((B, n_v, LANES, L), BF16),
        ],
        scratch_shapes=[pltpu.VMEM((tm, D), BF16)],
        compiler_params=_params("parallel", "parallel"),
        name="norm_proj_odd",
    )(x, g.reshape(1, D), w, cos_t, sin_t)


def _out_proj_kernel(a_ref, b_ref, wa_ref, wb_ref, g_ref, h_ref, o_ref):
    y = jnp.dot(a_ref[0], wa_ref[...], preferred_element_type=F32)
    y = y + jnp.dot(b_ref[0], wb_ref[...], preferred_element_type=F32)
    ms = jnp.mean(y * y, axis=-1, keepdims=True)
    o_ref[0] = h_ref[0] + y * lax.rsqrt(ms + NORM_EPS) * g_ref[...]


def _out_proj(a, b, w_out, g, h):
    B, L, W = a.shape
    D = w_out.shape[1]
    tm = min(PROJ_ROWS, L)
    return pl.pallas_call(
        _out_proj_kernel,
        grid=(B, L // tm),
        in_specs=[
            pl.BlockSpec((1, tm, W), lambda b_, i: (b_, i, 0)),
            pl.BlockSpec((1, tm, W), lambda b_, i: (b_, i, 0)),
            pl.BlockSpec((W, D), lambda b_, i: (0, 0)),
            pl.BlockSpec((W, D), lambda b_, i: (0, 0)),
            pl.BlockSpec((1, D), lambda b_, i: (0, 0)),
            pl.BlockSpec((1, tm, D), lambda b_, i: (b_, i, 0)),
        ],
        out_specs=pl.BlockSpec((1, tm, D), lambda b_, i: (b_, i, 0)),
        out_shape=jax.ShapeDtypeStruct((B, L, D), F32),
        compiler_params=_params("parallel", "parallel"),
        name="out_proj",
    )(a, b, w_out[:W].astype(BF16), w_out[W:].astype(BF16), g.reshape(1, D), h)


def _s5_kernel(u_ref, z_ref, bmat_ref, cmat_ref, ar_ref, ai_ref, d_ref, gw_ref, gb_ref, o_ref,
               us_scr, xs_scr, ys_scr, yn_scr, st_scr):
    nb, T, W = u_ref.shape
    n_blk = W // LANES
    half = xs_scr.shape[1] // 2
    rows = nb * T

    @pl.when(pl.program_id(1) == 0)
    def _():
        st_scr[...] = jnp.zeros_like(st_scr)

    for b in range(nb):
        ub = u_ref[b].astype(F32)
        for k in range(n_blk):
            us_scr[k, pl.ds(b, T, stride=nb), :] = ub[:, k * LANES:(k + 1) * LANES]

    for k in range(n_blk):
        xs_scr[...] = jnp.dot(us_scr[k].astype(BF16), bmat_ref[k], preferred_element_type=F32)
        ar = ar_ref[k]
        ai = ai_ref[k]

        def step(t, carry):
            xr, xi = carry
            r = pl.multiple_of(t * nb, nb)
            nr = ar * xr - ai * xi + xs_scr[pl.ds(r, nb), 0:half]
            ni = ar * xi + ai * xr + xs_scr[pl.ds(r, nb), half:2 * half]
            xs_scr[pl.ds(r, nb), 0:half] = nr
            xs_scr[pl.ds(r, nb), half:2 * half] = ni
            return nr, ni

        xr, xi = lax.fori_loop(0, T, step, (st_scr[k, 0], st_scr[k, 1]), unroll=2)
        st_scr[k, 0] = xr
        st_scr[k, 1] = xi
        ys_scr[k] = jnp.dot(xs_scr[...].astype(BF16), cmat_ref[k], preferred_element_type=F32)

    for b in range(nb):
        for k in range(n_blk):
            yn_scr[b * T:(b + 1) * T, k * LANES:(k + 1) * LANES] = ys_scr[k, pl.ds(b, T, stride=nb), :]

    u = u_ref[...].reshape(rows, W).astype(F32)
    y = yn_scr[...] + d_ref[...] * u
    g = jax.nn.gelu(y)
    gate = jnp.dot(g.astype(BF16), gw_ref[...], preferred_element_type=F32) + gb_ref[...]
    out = g * jax.nn.sigmoid(gate) * _silu(z_ref[...].reshape(rows, W).astype(F32))
    o_ref[...] = out.reshape(nb, T, W).astype(o_ref.dtype)


def _s5_weights(lam_re, lam_im, log_dt, b_re, b_im, c_re, c_im):
    G, P = lam_re.shape
    gpb = LANES // S5_GROUP
    n_blk = G // gpb
    lr, li = lam_re.astype(F32), lam_im.astype(F32)
    dt = jnp.exp(log_dt.astype(F32))[:, None]
    mag = jnp.exp(lr * dt)
    ab_r, ab_i = mag * jnp.cos(li * dt), mag * jnp.sin(li * dt)
    den = lr * lr + li * li
    f_r = ((ab_r - 1.0) * lr + ab_i * li) / den
    f_i = (ab_i * lr - (ab_r - 1.0) * li) / den
    br, bi = b_re.astype(F32), b_im.astype(F32)
    bb_r = f_r[..., None] * br - f_i[..., None] * bi
    bb_i = f_r[..., None] * bi + f_i[..., None] * br
    eye = jnp.eye(gpb, dtype=F32)
    bb = jnp.stack([bb_r, bb_i]).reshape(2, n_blk, gpb, P, S5_GROUP)
    bb = jnp.transpose(bb, (1, 2, 4, 0, 3))
    bmat = bb[:, :, :, :, None, :] * eye[None, :, None, None, :, None]
    bmat = bmat.reshape(n_blk, LANES, 2 * gpb * P).astype(BF16)
    cc = jnp.stack([c_re.astype(F32), -c_im.astype(F32)]).reshape(2, n_blk, gpb, S5_GROUP, P)
    cc = jnp.transpose(cc, (1, 0, 2, 4, 3))
    cmat = cc[:, :, :, :, None, :] * eye[None, None, :, None, :, None]
    cmat = cmat.reshape(n_blk, 2 * gpb * P, LANES).astype(BF16)
    ar = jnp.broadcast_to(ab_r.reshape(n_blk, 1, gpb * P), (n_blk, S5_BATCH, gpb * P))
    ai = jnp.broadcast_to(ab_i.reshape(n_blk, 1, gpb * P), (n_blk, S5_BATCH, gpb * P))
    return bmat, cmat, ar, ai


def _s5_branch(proj, lam_re, lam_im, log_dt, b_re, b_im, c_re, c_im, d_skip, glu_w, glu_b):
    B, L, N = proj.shape
    W = N // 4
    nb = S5_BATCH
    T = min(S5_TIME, L)
    n_blk = W // LANES
    bmat, cmat, ar, ai = _s5_weights(lam_re, lam_im, log_dt, b_re, b_im, c_re, c_im)
    S2 = bmat.shape[2]
    const3 = lambda g, c: (0, 0, 0)
    const2 = lambda g, c: (0, 0)
    return pl.pallas_call(
        _s5_kernel,
        grid=(B // nb, L // T),
        in_specs=[
            pl.BlockSpec((nb, T, W), lambda g, c: (g, c, 0)),
            pl.BlockSpec((nb, T, W), lambda g, c: (g, c, 1)),
            pl.BlockSpec(bmat.shape, const3),
            pl.BlockSpec(cmat.shape, const3),
            pl.BlockSpec(ar.shape, const3),
            pl.BlockSpec(ai.shape, const3),
            pl.BlockSpec((1, W), const2),
            pl.BlockSpec((W, W), const2),
            pl.BlockSpec((1, W), const2),
        ],
        out_specs=pl.BlockSpec((nb, T, W), lambda g, c: (g, c, 0)),
        out_shape=jax.ShapeDtypeStruct((B, L, W), BF16),
        scratch_shapes=[
            pltpu.VMEM((n_blk, nb * T, LANES), F32),
            pltpu.VMEM((nb * T, S2), F32),
            pltpu.VMEM((n_blk, nb * T, LANES), F32),
            pltpu.VMEM((nb * T, W), F32),
            pltpu.VMEM((n_blk, 2, nb, S2 // 2), F32),
        ],
        compiler_params=_params("parallel", "arbitrary"),
        name="s5_scan",
    )(proj, proj, bmat, cmat, ar, ai, d_skip.reshape(1, W).astype(F32), glu_w.astype(BF16),
      glu_b.reshape(1, W).astype(F32))


def _log_sigmoid(x):
    return jnp.minimum(x, 0.0) - jnp.log1p(jnp.exp(-jnp.abs(x)))


def _split3(x):
    hi = x.astype(BF16)
    r1 = x - hi.astype(F32)
    mid = r1.astype(BF16)
    lo = (r1 - mid.astype(F32)).astype(BF16)
    return hi, mid, lo


def _mlstm_kernel(x_ref, z_ref, cw_ref, cb_ref, wq_ref, wk_ref, wv_ref, gw_ref, gwt_ref, gbc_ref, gbr_ref,
                  ng_ref, sk_ref, o_ref, xe_scr, qkv_scr, ks_scr, ct_scr, n_scr, m_scr):
    Tc, W = x_ref.shape[1], x_ref.shape[2]
    H = wq_ref.shape[0]
    dh = W // H
    pad = SUBLANES

    @pl.when(pl.program_id(1) == 0)
    def _():
        xe_scr[0:pad, :] = jnp.zeros((pad, W), F32)
        ct_scr[...] = jnp.zeros_like(ct_scr)
        n_scr[...] = jnp.zeros_like(n_scr)
        m_scr[...] = jnp.zeros_like(m_scr)

    xb = x_ref[0]
    x = xb.astype(F32)
    xe_scr[pad:pad + Tc, :] = x
    xc = cb_ref[...] + sum(xe_scr[pl.ds(pad - (ML_CONV - 1) + j, Tc), :] * cw_ref[j:j + 1, :]
                           for j in range(ML_CONV))
    xc = _silu(xc)
    xe_scr[0:pad, :] = x[Tc - pad:Tc, :]
    xcb = xc.astype(BF16)

    for h in range(H):
        cols = slice(h * dh, (h + 1) * dh)
        q = jnp.dot(xcb[:, cols], wq_ref[h], preferred_element_type=F32)
        k = jnp.dot(xcb[:, cols], wk_ref[h], preferred_element_type=F32)
        v = jnp.dot(xb[:, cols], wv_ref[h], preferred_element_type=F32)
        qkv_scr[:, h * dh:(h + 1) * dh] = q.astype(BF16)
        qkv_scr[:, W + h * dh:W + (h + 1) * dh] = k.astype(BF16)
        qkv_scr[:, 2 * W + h * dh:2 * W + (h + 1) * dh] = v.astype(BF16)
        ks_scr[:, cols] = (k * (dh ** -0.5)).astype(BF16)

    qkv = qkv_scr[...]
    g_col = jnp.dot(qkv, gw_ref[...], preferred_element_type=F32) + gbc_ref[...]
    g_row = lax.dot_general(gwt_ref[...], qkv, (((1,), (1,)), ((), ())),
                            preferred_element_type=F32) + gbr_ref[...]
    ti = lax.broadcasted_iota(jnp.int32, (Tc, Tc), 0)
    si = lax.broadcasted_iota(jnp.int32, (Tc, Tc), 1)
    causal = si <= ti
    lower = jnp.where(causal, 1.0, 0.0).astype(BF16)
    upper = jnp.where(ti <= si, 1.0, 0.0).astype(BF16)
    b_col = sum(jnp.dot(lower, p, preferred_element_type=F32) for p in _split3(_log_sigmoid(g_col)))
    b_row = sum(jnp.dot(p, upper, preferred_element_type=F32) for p in _split3(_log_sigmoid(g_row)))

    for h in range(H):
        cols = slice(h * dh, (h + 1) * dh)
        q = qkv_scr[:, h * dh:(h + 1) * dh]
        v = qkv_scr[:, 2 * W + h * dh:2 * W + (h + 1) * dh]
        ks = ks_scr[:, cols]
        bc = b_col[:, H + h:H + h + 1]
        br = b_row[H + h:H + h + 1, :]
        li_c = g_col[:, h:h + 1]
        li_r = g_row[h:h + 1, :]
        m_prev = m_scr[h][:, 0:1]

        s_qk = lax.dot_general(q, ks, (((1,), (1,)), ((), ())), preferred_element_type=F32)
        log_d = jnp.where(causal, bc - br + li_r, NEG_BIG)
        inter = bc + m_prev
        m_t = jnp.maximum(inter, jnp.max(log_d, axis=-1, keepdims=True))
        s = s_qk * jnp.exp(log_d - m_t)
        sc = jnp.exp(inter - m_t)
        ct = ct_scr[h]
        n = n_scr[h]
        num = sc * jnp.dot(q, ct.astype(BF16), preferred_element_type=F32)
        num = num + jnp.dot(s.astype(BF16), v, preferred_element_type=F32)
        dnm = sc * jnp.sum(q.astype(F32) * n, axis=-1, keepdims=True) + jnp.sum(s, axis=-1, keepdims=True)
        hh = num / jnp.maximum(jnp.abs(dnm), jnp.exp(-m_t))

        b_last = bc[Tc - 1:Tc, :]
        w_log = b_last - bc + li_c
        m_new = jnp.maximum(b_last + m_prev, jnp.max(w_log, axis=0, keepdims=True))
        kw = ks.astype(F32) * jnp.exp(w_log - m_new)
        decay = jnp.exp(b_last + m_prev - m_new)
        ct_scr[h] = decay * ct + lax.dot_general(kw.astype(BF16), v, (((0,), (0,)), ((), ())),
                                                 preferred_element_type=F32)
        n_scr[h] = decay * n + jnp.sum(kw, axis=0, keepdims=True)
        m_scr[h] = jnp.broadcast_to(m_new, (1, LANES))

        mu = jnp.mean(hh, axis=-1, keepdims=True)
        cen = hh - mu
        var = jnp.mean(cen * cen, axis=-1, keepdims=True)
        out = cen * lax.rsqrt(var + HEAD_NORM_EPS) * ng_ref[:, cols] + sk_ref[:, cols] * xc[:, cols]
        o_ref[0, :, cols] = (out * _silu(z_ref[0, :, cols].astype(F32))).astype(o_ref.dtype)


def _mlstm_branch(proj, conv_w, conv_b, wq, wk, wv, gate_w, gate_b, norm_g, skip):
    B, L, N = proj.shape
    W = N // 4
    H = wq.shape[0]
    dh = W // H
    Tc = min(ML_CHUNK, L)
    gw = jnp.zeros((3 * W, LANES), F32).at[:, :2 * H].set(gate_w.astype(F32)).astype(BF16)
    gwt = jnp.transpose(gate_w.astype(F32)).astype(BF16)
    gbc = jnp.zeros((1, LANES), F32).at[0, :2 * H].set(gate_b.astype(F32))
    gbr = gate_b.astype(F32).reshape(2 * H, 1)
    c2 = lambda b, c: (0, 0)
    c3 = lambda b, c: (0, 0, 0)
    return pl.pallas_call(
        _mlstm_kernel,
        grid=(B, L // Tc),
        in_specs=[
            pl.BlockSpec((1, Tc, W), lambda b, c: (b, c, 2)),
            pl.BlockSpec((1, Tc, W), lambda b, c: (b, c, 3)),
            pl.BlockSpec((ML_CONV, W), c2),
            pl.BlockSpec((1, W), c2),
            pl.BlockSpec((H, dh, dh), c3),
            pl.BlockSpec((H, dh, dh), c3),
            pl.BlockSpec((H, dh, dh), c3),
            pl.BlockSpec((3 * W, LANES), c2),
            pl.BlockSpec((2 * H, 3 * W), c2),
            pl.BlockSpec((1, LANES), c2),
            pl.BlockSpec((2 * H, 1), c2),
            pl.BlockSpec((1, W), c2),
            pl.BlockSpec((1, W), c2),
        ],
        out_specs=pl.BlockSpec((1, Tc, W), lambda b, c: (b, c, 0)),
        out_shape=jax.ShapeDtypeStruct((B, L, W), BF16),
        scratch_shapes=[
            pltpu.VMEM((SUBLANES + Tc, W), F32),
            pltpu.VMEM((Tc, 3 * W), BF16),
            pltpu.VMEM((Tc, W), BF16),
            pltpu.VMEM((H, dh, dh), F32),
            pltpu.VMEM((H, 1, dh), F32),
            pltpu.VMEM((H, 1, LANES), F32),
        ],
        compiler_params=_params("parallel", "arbitrary"),
        name="mlstm",
    )(proj, proj, conv_w.astype(F32), conv_b.reshape(1, W).astype(F32), wq.astype(BF16), wk.astype(BF16),
      wv.astype(BF16), gw, gwt, gbc, gbr, norm_g.reshape(1, W).astype(F32), skip.reshape(1, W).astype(F32))


def _attn_kernel(q_ref, k_ref, vt_ref, z_ref, bias_ref, lam_ref, ng_ref, o_ref, m_scr, acc_scr, *,
                 mode, out_scale, blk, bias_index):
    L = q_ref.shape[2]
    n = L // blk
    lane = lax.broadcasted_iota(jnp.int32, (1, LANES), 1)
    map0 = lane < ATTN_HEAD_DIM
    feat = lax.broadcasted_iota(jnp.int32, (LANES, 1), 0)
    ones_rows = jnp.ones((SUBLANES, blk), BF16)

    def stacked_q(qi):
        q = q_ref[0, 0, qi * blk:(qi + 1) * blk, :]
        zero = jnp.zeros_like(q)
        return jnp.concatenate([jnp.where(map0, q, zero), jnp.where(map0, zero, q)], axis=0)

    def scores(qi, j):
        s = lax.dot_general(k_ref[0, 0, j * blk:(j + 1) * blk, :], q2[qi], (((1,), (1,)), ((), ())),
                            preferred_element_type=F32)
        e = bias_index(qi - j)
        return s if e is None else s + bias_ref[e]

    def update(qi, j, s):
        m_cur = jnp.max(s, axis=0, keepdims=True)
        m_next = m_cur if j == 0 else jnp.maximum(m_scr[qi], m_cur)
        p = jnp.exp2(s - m_next).astype(BF16)
        pv = jnp.dot(jnp.concatenate([vt_ref[0, 0, :, j * blk:(j + 1) * blk], ones_rows], axis=0), p,
                     preferred_element_type=F32)
        if j == 0:
            acc_scr[qi] = pv
        else:
            acc_scr[qi] = jnp.exp2(m_scr[qi] - m_next) * acc_scr[qi] + pv
        m_scr[qi] = m_next

    def finish(qi):
        rows = slice(qi * blk, (qi + 1) * blk)
        acc = acc_scr[qi]
        o = acc[:LANES] / acc[LANES:LANES + 1]
        o0, o1 = o[:, :blk], o[:, blk:]
        if mode == "dilated":
            out = jnp.transpose(jnp.where(feat < ATTN_HEAD_DIM, o0, o1))
        else:
            out = jnp.transpose(o0 - lam_ref[:, 0:1] * o1)
            ms = jnp.mean(out * out, axis=-1, keepdims=True)
            out = out * lax.rsqrt(ms + HEAD_NORM_EPS) * ng_ref[...] * out_scale
        o_ref[0, rows, :] = (out * _silu(z_ref[0, 0, rows, :].astype(F32))).astype(o_ref.dtype)

    steps = []
    for a in range(n // 2):
        b = n - 1 - a
        ja = 0
        for jb in range(b + 1):
            steps.append((b, jb))
            if ja <= a and (jb + 1) * (a + 1) >= (ja + 1) * (b + 1):
                steps.append((a, ja))
                ja += 1
    q2 = {qi: stacked_q(qi) for qi in range(n)}
    pending = {}
    for t in range(min(ATTN_LOOKAHEAD, len(steps))):
        pending[t] = scores(*steps[t])
    for t, (qi, j) in enumerate(steps):
        if t + ATTN_LOOKAHEAD < len(steps):
            pending[t + ATTN_LOOKAHEAD] = scores(*steps[t + ATTN_LOOKAHEAD])
        update(qi, j, pending.pop(t))
        if j == qi:
            finish(qi)


def _dilated_multiplicity(delta):
    mult = np.zeros(delta.shape, np.int64)
    for window, dil in DIL_PAIRS:
        mult += (delta >= 0) & (delta % dil == 0) & (delta <= window)
    return mult


def _bias_tables(L, blk, mode):
    n = L // blk
    key = np.arange(blk)[:, None]
    qry = np.arange(blk)[None, :]
    tables, index = [], {}
    for d in range(n):
        delta = d * blk + qry - key
        mult = _dilated_multiplicity(delta) if mode == "dilated" else (delta >= 0).astype(np.int64)
        t = np.where(mult > 0, np.log2(np.maximum(mult, 1)), NEG_BIG).astype(np.float32)
        if not t.any():
            index[d] = None
            continue
        for e, other in enumerate(tables):
            if np.array_equal(t, other):
                index[d] = e
                break
        else:
            index[d] = len(tables)
            tables.append(t)
    stacked = np.stack(tables)
    return jnp.asarray(np.concatenate([stacked, stacked], axis=2)), index


def _attention(qkz, vt, q0, k0, v0, z0, n_blocks, blk, lam, norm_g, mode, out_scale):
    B, _, L, _ = qkz.shape
    bias, index = _bias_tables(L, blk, mode)
    n_q = L // blk
    assert n_q % 2 == 0
    kern = functools.partial(_attn_kernel, mode=mode, out_scale=out_scale, blk=blk, bias_index=index.get)
    c2 = lambda b, n: (0, 0)
    return pl.pallas_call(
        kern,
        grid=(B, n_blocks),
        in_specs=[
            pl.BlockSpec((1, 1, L, LANES), lambda b, n: (b, q0 + n, 0, 0)),
            pl.BlockSpec((1, 1, L, LANES), lambda b, n: (b, k0 + n, 0, 0)),
            pl.BlockSpec((1, 1, LANES, L), lambda b, n: (b, v0 + n, 0, 0)),
            pl.BlockSpec((1, 1, L, LANES), lambda b, n: (b, z0 + n, 0, 0)),
            pl.BlockSpec(bias.shape, lambda b, n: (0, 0, 0)),
            pl.BlockSpec((1, LANES), c2),
            pl.BlockSpec((1, LANES), c2),
        ],
        out_specs=pl.BlockSpec((1, L, LANES), lambda b, n: (b, 0, n)),
        out_shape=jax.ShapeDtypeStruct((B, L, n_blocks * LANES), BF16),
        scratch_shapes=[
            pltpu.VMEM((n_q, 1, 2 * blk), F32),
            pltpu.VMEM((n_q, LANES + SUBLANES, 2 * blk), F32),
        ],
        compiler_params=_params("parallel", "parallel"),
        name="attn_" + mode,
    )(qkz, qkz, vt, qkz, bias, lam, norm_g)


def _rope_tables(L):
    dh = ATTN_HEAD_DIM
    inv = ROPE_THETA ** (-jnp.arange(0, dh, 2, dtype=F32) / dh)
    ang = jnp.arange(L, dtype=F32)[:, None] * inv[None, :]
    cos, sin = jnp.cos(ang), jnp.sin(ang)
    reps = LANES // dh
    cos_t = jnp.tile(jnp.concatenate([cos, cos], axis=-1), (1, reps))
    sin_t = jnp.tile(jnp.concatenate([-sin, sin], axis=-1), (1, reps))
    return cos_t, sin_t


def _even_layer(h, pre_g, post_g, w_in, s5_params, ml_params, w_out):
    proj = _norm_proj(h, pre_g, w_in.astype(BF16))
    a = _s5_branch(proj, *s5_params)
    b = _mlstm_branch(proj, *ml_params)
    return _out_proj(a, b, w_out, post_g, h)


def _odd_layer(h, pre_g, post_g, w_in, lq1, lk1, lq2, lk2, diff_norm, w_out, layer_idx):
    B, L, D = h.shape
    nblk = PROJ_COLS // LANES
    cos_t, sin_t = _rope_tables(L)
    qkz, vt = _norm_proj_rope(h, pre_g, w_in.astype(BF16), cos_t, sin_t)
    blk = min(ATTN_BLOCK, L)
    lam_init = 0.8 - 0.6 * math.exp(-0.3 * layer_idx)
    lam = (jnp.exp(jnp.sum(lq1.astype(F32) * lk1.astype(F32)))
           - jnp.exp(jnp.sum(lq2.astype(F32) * lk2.astype(F32))) + lam_init)
    lam_row = jnp.full((1, LANES), lam, F32)
    ones_row = jnp.ones((1, LANES), F32)
    base = [b for _, b in ODD_CHUNKS]
    c_out = _attention(qkz, vt, base[0], base[1], base[2], base[3], nblk, blk, ones_row, ones_row, "dilated", 1.0)
    d_out = _attention(qkz, vt, base[4], base[5], base[6], base[7], nblk, blk, lam_row,
                       diff_norm.reshape(1, LANES).astype(F32), "diff", 1.0 - lam_init)
    return _out_proj(c_out, d_out, w_out, post_g, h)


def kernel(x, pre_norm, post_norm, w_in_ab, s5_lambda_re, s5_lambda_im, s5_log_dt, s5_b_re, s5_b_im,
           s5_c_re, s5_c_im, s5_d, s5_glu_w, s5_glu_b, ml_conv_w, ml_conv_b, ml_wq, ml_wk, ml_wv,
           ml_gate_w, ml_gate_b, ml_norm, ml_skip, w_out_ab, w_in_cd, diff_lq1, diff_lk1, diff_lq2,
           diff_lk2, diff_norm, w_out_cd):
    depth = pre_norm.shape[0]
    h = x
    for l in range(depth):
        i = l // 2
        if l % 2 == 0:
            s5_params = (s5_lambda_re[i], s5_lambda_im[i], s5_log_dt[i], s5_b_re[i], s5_b_im[i],
                         s5_c_re[i], s5_c_im[i], s5_d[i], s5_glu_w[i], s5_glu_b[i])
            ml_params = (ml_conv_w[i], ml_conv_b[i], ml_wq[i], ml_wk[i], ml_wv[i], ml_gate_w[i],
                         ml_gate_b[i], ml_norm[i], ml_skip[i])
            h = _even_layer(h, pre_norm[l], post_norm[l], w_in_ab[i], s5_params, ml_params, w_out_ab[i])
        else:
            h = _odd_layer(h, pre_norm[l], post_norm[l], w_in_cd[i], diff_lq1[i], diff_lk1[i], diff_lq2[i],
                           diff_lk2[i], diff_norm[i], w_out_cd[i], l)
    return h
```

```python
import functools
import math

import numpy as np
import jax
import jax.numpy as jnp
from jax import lax
from jax.experimental import pallas as pl
from jax.experimental.pallas import tpu as pltpu

F32 = jnp.float32
BF16 = jnp.bfloat16

S5_GROUP = 16
S5_STATE = 64
ML_HEADS = 4
ML_CONV = 4
DIL_PAIRS = ((128, 1), (512, 4), (2048, 16))
ROPE_THETA = 10000.0
NORM_EPS = 1e-6
HEAD_NORM_EPS = 1e-5
ATTN_HEAD_DIM = 64
QUERY_SCALE = ATTN_HEAD_DIM ** -0.5 * math.log2(math.e)

LANES = 128
SUBLANES = 8
VMEM_LIMIT_BYTES = 56 * 1024 * 1024

NEG_BIG = -1e30

PROJ_ROWS = 512
PROJ_COLS = 512
S5_BATCH = SUBLANES
S5_TIME = 128
ML_CHUNK = 256
ATTN_BLOCK = 256
ATTN_LOOKAHEAD = 3


def _params(*semantics):
    return pltpu.CompilerParams(dimension_semantics=semantics, vmem_limit_bytes=VMEM_LIMIT_BYTES)


def _silu(x):
    return x * jax.nn.sigmoid(x)


def _norm_proj_kernel(x_ref, g_ref, w_ref, o_ref, z_scr):
    x = x_ref[0]
    ms = jnp.mean(x * x, axis=-1, keepdims=True)
    z_scr[...] = (x * lax.rsqrt(ms + NORM_EPS) * g_ref[...]).astype(BF16)
    n_out = w_ref.shape[1]
    for c in range(n_out // PROJ_COLS):
        cols = slice(c * PROJ_COLS, (c + 1) * PROJ_COLS)
        o_ref[0, :, cols] = jnp.dot(z_scr[...], w_ref[:, cols], preferred_element_type=F32).astype(o_ref.dtype)


def _norm_proj(x, g, w):
    B, L, D = x.shape
    N = w.shape[1]
    tm = min(PROJ_ROWS, L)
    return pl.pallas_call(
        _norm_proj_kernel,
        grid=(B, L // tm),
        in_specs=[
            pl.BlockSpec((1, tm, D), lambda b, i: (b, i, 0)),
            pl.BlockSpec((1, D), lambda b, i: (0, 0)),
            pl.BlockSpec((D, N), lambda b, i: (0, 0)),
        ],
        out_specs=pl.BlockSpec((1, tm, N), lambda b, i: (b, i, 0)),
        out_shape=jax.ShapeDtypeStruct((B, L, N), BF16),
        scratch_shapes=[pltpu.VMEM((tm, D), BF16)],
        compiler_params=_params("parallel", "parallel"),
        name="norm_proj_even",
    )(x, g.reshape(1, D), w)


def _swap_halves(x, first_half):
    return jnp.where(first_half, pltpu.roll(x, LANES - ATTN_HEAD_DIM // 2, 1), pltpu.roll(x, ATTN_HEAD_DIM // 2, 1))


ODD_CHUNKS = (("q", 0), ("k", 4), ("v", 0), ("z", 8), ("q", 12), ("k", 16), ("v", 4), ("z", 20))


def _norm_proj_rope_kernel(x_ref, g_ref, w_ref, cos_ref, sin_ref, o_ref, vt_ref, z_scr):
    x = x_ref[0]
    ms = jnp.mean(x * x, axis=-1, keepdims=True)
    z_scr[...] = (x * lax.rsqrt(ms + NORM_EPS) * g_ref[...]).astype(BF16)
    slabs = PROJ_COLS // LANES
    lane = lax.broadcasted_iota(jnp.int32, (1, LANES), 1)
    first_half = (lane % ATTN_HEAD_DIM) < (ATTN_HEAD_DIM // 2)
    for c, (kind, base) in enumerate(ODD_CHUNKS):
        y = jnp.dot(z_scr[...], w_ref[:, c * PROJ_COLS:(c + 1) * PROJ_COLS], preferred_element_type=F32)
        for s in range(slabs):
            t = y[:, s * LANES:(s + 1) * LANES]
            if kind in ("q", "k"):
                t = t * cos_ref[...] + _swap_halves(t, first_half) * sin_ref[...]
            if kind == "q":
                t = t * QUERY_SCALE
            if kind == "v":
                vt_ref[0, base + s] = jnp.transpose(t).astype(vt_ref.dtype)
            else:
                o_ref[0, base + s] = t.astype(o_ref.dtype)


def _norm_proj_rope(x, g, w, cos_t, sin_t):
    B, L, D = x.shape
    N = w.shape[1]
    assert N == len(ODD_CHUNKS) * PROJ_COLS
    tm = min(PROJ_ROWS, L)
    n_v = sum(kind == "v" for kind, _ in ODD_CHUNKS) * (PROJ_COLS // LANES)
    n_o = N // LANES - n_v
    return pl.pallas_call(
        _norm_proj_rope_kernel,
        grid=(B, L // tm),
        in_specs=[
            pl.BlockSpec((1, tm, D), lambda b, i: (b, i, 0)),
            pl.BlockSpec((1, D), lambda b, i: (0, 0)),
            pl.BlockSpec((D, N), lambda b, i: (0, 0)),
            pl.BlockSpec((tm, LANES), lambda b, i: (i, 0)),
            pl.BlockSpec((tm, LANES), lambda b, i: (i, 0)),
        ],
        out_specs=[
            pl.BlockSpec((1, n_o, tm, LANES), lambda b, i: (b, 0, i, 0)),
            pl.BlockSpec((1, n_v, LANES, tm), lambda b, i: (b, 0, 0, i)),
        ],
        out_shape=[
            jax.ShapeDtypeStruct((B, n_o, L, LANES), BF16),
            jax.ShapeDtypeStruct((B, n_v, LANES, L), BF16),
        ],
        scratch_shapes=[pltpu.VMEM((tm, D), BF16)],
        compiler_params=_params("parallel", "parallel"),
        name="norm_proj_odd",
    )(x, g.reshape(1, D), w, cos_t, sin_t)


def _out_proj_kernel(a_ref, b_ref, wa_ref, wb_ref, g_ref, h_ref, o_ref):
    y = jnp.dot(a_ref[0], wa_ref[...], preferred_element_type=F32)
    y = y + jnp.dot(b_ref[0], wb_ref[...], preferred_element_type=F32)
    ms = jnp.mean(y * y, axis=-1, keepdims=True)
    o_ref[0] = h_ref[0] + y * lax.rsqrt(ms + NORM_EPS) * g_ref[...]


def _out_proj(a, b, w_out, g, h):
    B, L, W = a.shape
    D = w_out.shape[1]
    tm = min(PROJ_ROWS, L)
    return pl.pallas_call(
        _out_proj_kernel,
        grid=(B, L // tm),
        in_specs=[
            pl.BlockSpec((1, tm, W), lambda b_, i: (b_, i, 0)),
            pl.BlockSpec((1, tm, W), lambda b_, i: (b_, i, 0)),
            pl.BlockSpec((W, D), lambda b_, i: (0, 0)),
            pl.BlockSpec((W, D), lambda b_, i: (0, 0)),
            pl.BlockSpec((1, D), lambda b_, i: (0, 0)),
            pl.BlockSpec((1, tm, D), lambda b_, i: (b_, i, 0)),
        ],
        out_specs=pl.BlockSpec((1, tm, D), lambda b_, i: (b_, i, 0)),
        out_shape=jax.ShapeDtypeStruct((B, L, D), F32),
        compiler_params=_params("parallel", "parallel"),
        name="out_proj",
    )(a, b, w_out[:W].astype(BF16), w_out[W:].astype(BF16), g.reshape(1, D), h)


def _s5_kernel(u_ref, z_ref, bmat_ref, cmat_ref, ar_ref, ai_ref, d_ref, gw_ref, gb_ref, o_ref,
               us_scr, xs_scr, ys_scr, yn_scr, st_scr):
    nb, T, W = u_ref.shape
    n_blk = W // LANES
    half = xs_scr.shape[1] // 2
    rows = nb * T

    @pl.when(pl.program_id(1) == 0)
    def _():
        st_scr[...] = jnp.zeros_like(st_scr)

    for b in range(nb):
        ub = u_ref[b].astype(F32)
        for k in range(n_blk):
            us_scr[k, pl.ds(b, T, stride=nb), :] = ub[:, k * LANES:(k + 1) * LANES]

    for k in range(n_blk):
        xs_scr[...] = jnp.dot(us_scr[k].astype(BF16), bmat_ref[k], preferred_element_type=F32)
        ar = ar_ref[k]
        ai = ai_ref[k]

        def step(t, carry):
            xr, xi = carry
            r = pl.multiple_of(t * nb, nb)
            nr = ar * xr - ai * xi + xs_scr[pl.ds(r, nb), 0:half]
            ni = ar * xi + ai * xr + xs_scr[pl.ds(r, nb), half:2 * half]
            xs_scr[pl.ds(r, nb), 0:half] = nr
            xs_scr[pl.ds(r, nb), half:2 * half] = ni
            return nr, ni

        xr, xi = lax.fori_loop(0, T, step, (st_scr[k, 0], st_scr[k, 1]), unroll=2)
        st_scr[k, 0] = xr
        st_scr[k, 1] = xi
        ys_scr[k] = jnp.dot(xs_scr[...].astype(BF16), cmat_ref[k], preferred_element_type=F32)

    for b in range(nb):
        for k in range(n_blk):
            yn_scr[b * T:(b + 1) * T, k * LANES:(k + 1) * LANES] = ys_scr[k, pl.ds(b, T, stride=nb), :]

    u = u_ref[...].reshape(rows, W).astype(F32)
    y = yn_scr[...] + d_ref[...] * u
    g = jax.nn.gelu(y)
    gate = jnp.dot(g.astype(BF16), gw_ref[...], preferred_element_type=F32) + gb_ref[...]
    out = g * jax.nn.sigmoid(gate) * _silu(z_ref[...].reshape(rows, W).astype(F32))
    o_ref[...] = out.reshape(nb, T, W).astype(o_ref.dtype)


def _s5_weights(lam_re, lam_im, log_dt, b_re, b_im, c_re, c_im):
    G, P = lam_re.shape
    gpb = LANES // S5_GROUP
    n_blk = G // gpb
    lr, li = lam_re.astype(F32), lam_im.astype(F32)
    dt = jnp.exp(log_dt.astype(F32))[:, None]
    mag = jnp.exp(lr * dt)
    ab_r, ab_i = mag * jnp.cos(li * dt), mag * jnp.sin(li * dt)
    den = lr * lr + li * li
    f_r = ((ab_r - 1.0) * lr + ab_i * li) / den
    f_i = (ab_i * lr - (ab_r - 1.0) * li) / den
    br, bi = b_re.astype(F32), b_im.astype(F32)
    bb_r = f_r[..., None] * br - f_i[..., None] * bi
    bb_i = f_r[..., None] * bi + f_i[..., None] * br
    eye = jnp.eye(gpb, dtype=F32)
    bb = jnp.stack([bb_r, bb_i]).reshape(2, n_blk, gpb, P, S5_GROUP)
    bb = jnp.transpose(bb, (1, 2, 4, 0, 3))
    bmat = bb[:, :, :, :, None, :] * eye[None, :, None, None, :, None]
    bmat = bmat.reshape(n_blk, LANES, 2 * gpb * P).astype(BF16)
    cc = jnp.stack([c_re.astype(F32), -c_im.astype(F32)]).reshape(2, n_blk, gpb, S5_GROUP, P)
    cc = jnp.transpose(cc, (1, 0, 2, 4, 3))
    cmat = cc[:, :, :, :, None, :] * eye[None, None, :, None, :, None]
    cmat = cmat.reshape(n_blk, 2 * gpb * P, LANES).astype(BF16)
    ar = jnp.broadcast_to(ab_r.reshape(n_blk, 1, gpb * P), (n_blk, S5_BATCH, gpb * P))
    ai = jnp.broadcast_to(ab_i.reshape(n_blk, 1, gpb * P), (n_blk, S5_BATCH, gpb * P))
    return bmat, cmat, ar, ai


def _s5_branch(proj, lam_re, lam_im, log_dt, b_re, b_im, c_re, c_im, d_skip, glu_w, glu_b):
    B, L, N = proj.shape
    W = N // 4
    nb = S5_BATCH
    T = min(S5_TIME, L)
    n_blk = W // LANES
    bmat, cmat, ar, ai = _s5_weights(lam_re, lam_im, log_dt, b_re, b_im, c_re, c_im)
    S2 = bmat.shape[2]
    const3 = lambda g, c: (0, 0, 0)
    const2 = lambda g, c: (0, 0)
    return pl.pallas_call(
        _s5_kernel,
        grid=(B // nb, L // T),
        in_specs=[
            pl.BlockSpec((nb, T, W), lambda g, c: (g, c, 0)),
            pl.BlockSpec((nb, T, W), lambda g, c: (g, c, 1)),
            pl.BlockSpec(bmat.shape, const3),
            pl.BlockSpec(cmat.shape, const3),
            pl.BlockSpec(ar.shape, const3),
            pl.BlockSpec(ai.shape, const3),
            pl.BlockSpec((1, W), const2),
            pl.BlockSpec((W, W), const2),
            pl.BlockSpec((1, W), const2),
        ],
        out_specs=pl.BlockSpec((nb, T, W), lambda g, c: (g, c, 0)),
        out_shape=jax.ShapeDtypeStruct((B, L, W), BF16),
        scratch_shapes=[
            pltpu.VMEM((n_blk, nb * T, LANES), F32),
            pltpu.VMEM((nb * T, S2), F32),
            pltpu.VMEM((n_blk, nb * T, LANES), F32),
            pltpu.VMEM((nb * T, W), F32),
            pltpu.VMEM((n_blk, 2, nb, S2 // 2), F32),
        ],
        compiler_params=_params("parallel", "arbitrary"),
        name="s5_scan",
    )(proj, proj, bmat, cmat, ar, ai, d_skip.reshape(1, W).astype(F32), glu_w.astype(BF16),
      glu_b.reshape(1, W).astype(F32))


def _log_sigmoid(x):
    return jnp.minimum(x, 0.0) - jnp.log1p(jnp.exp(-jnp.abs(x)))


def _split3(x):
    hi = x.astype(BF16)
    r1 = x - hi.astype(F32)
    mid = r1.astype(BF16)
    lo = (r1 - mid.astype(F32)).astype(BF16)
    return hi, mid, lo


def _mlstm_kernel(x_ref, z_ref, cw_ref, cb_ref, wq_ref, wk_ref, wv_ref, gw_ref, gwt_ref, gbc_ref, gbr_ref,
                  ng_ref, sk_ref, o_ref, xe_scr, qkv_scr, ks_scr, va_scr, ct_scr, m_scr):
    Tc, W = x_ref.shape[1], x_ref.shape[2]
    H = wq_ref.shape[0]
    dh = W // H
    pad = SUBLANES

    @pl.when(pl.program_id(1) == 0)
    def _():
        xe_scr[0:pad, :] = jnp.zeros((pad, W), F32)
        ct_scr[...] = jnp.zeros_like(ct_scr)
        m_scr[...] = jnp.zeros_like(m_scr)
        va_scr[:, :, dh:] = jnp.ones((H, Tc, dh), BF16)

    xb = x_ref[0]
    x = xb.astype(F32)
    xe_scr[pad:pad + Tc, :] = x
    xc = cb_ref[...] + sum(xe_scr[pl.ds(pad - (ML_CONV - 1) + j, Tc), :] * cw_ref[j:j + 1, :]
                           for j in range(ML_CONV))
    xc = _silu(xc)
    xe_scr[0:pad, :] = x[Tc - pad:Tc, :]
    xcb = xc.astype(BF16)

    for h in range(H):
        cols = slice(h * dh, (h + 1) * dh)
        q = jnp.dot(xcb[:, cols], wq_ref[h], preferred_element_type=F32)
        k = jnp.dot(xcb[:, cols], wk_ref[h], preferred_element_type=F32)
        v = jnp.dot(xb[:, cols], wv_ref[h], preferred_element_type=F32)
        qkv_scr[:, h * dh:(h + 1) * dh] = q.astype(BF16)
        qkv_scr[:, W + h * dh:W + (h + 1) * dh] = k.astype(BF16)
        qkv_scr[:, 2 * W + h * dh:2 * W + (h + 1) * dh] = v.astype(BF16)
        va_scr[h, :, :dh] = v.astype(BF16)
        ks_scr[:, cols] = (k * (dh ** -0.5)).astype(BF16)

    qkv = qkv_scr[...]
    g_col = jnp.dot(qkv, gw_ref[...], preferred_element_type=F32) + gbc_ref[...]
    g_row = lax.dot_general(gwt_ref[...], qkv, (((1,), (1,)), ((), ())),
                            preferred_element_type=F32) + gbr_ref[...]
    ti = lax.broadcasted_iota(jnp.int32, (Tc, Tc), 0)
    si = lax.broadcasted_iota(jnp.int32, (Tc, Tc), 1)
    causal = si <= ti
    lower = jnp.where(causal, 1.0, 0.0).astype(BF16)
    upper = jnp.where(ti <= si, 1.0, 0.0).astype(BF16)
    b_col = sum(jnp.dot(lower, p, preferred_element_type=F32) for p in _split3(_log_sigmoid(g_col)))
    b_row = sum(jnp.dot(p, upper, preferred_element_type=F32) for p in _split3(_log_sigmoid(g_row)))

    heads = range(H)
    q_h = [qkv_scr[:, h * dh:(h + 1) * dh] for h in heads]
    m_prev = [m_scr[h][:, 0:1] for h in heads]
    bc = [b_col[:, H + h:H + h + 1] for h in heads]
    b_last = [bc[h][Tc - 1:Tc, :] for h in heads]

    s_qk = [lax.dot_general(q_h[h], ks_scr[:, h * dh:(h + 1) * dh], (((1,), (1,)), ((), ())),
                            preferred_element_type=F32) for h in heads]
    q_ct = [jnp.dot(q_h[h], ct_scr[h].astype(BF16), preferred_element_type=F32) for h in heads]

    dmat, sc, floor = [], [], []
    for h in heads:
        log_d = jnp.where(causal, bc[h] - b_row[H + h:H + h + 1, :] + g_row[h:h + 1, :], NEG_BIG)
        inter = bc[h] + m_prev[h]
        m_t = jnp.maximum(inter, jnp.max(log_d, axis=-1, keepdims=True))
        dmat.append(jnp.exp(log_d - m_t))
        sc.append(jnp.exp(inter - m_t))
        floor.append(jnp.exp(-m_t))

    pv = [jnp.dot((s_qk[h] * dmat[h]).astype(BF16), va_scr[h], preferred_element_type=F32) for h in heads]

    for h in heads:
        w_log = b_last[h] - bc[h] + g_col[:, h:h + 1]
        m_new = jnp.maximum(b_last[h] + m_prev[h], jnp.max(w_log, axis=0, keepdims=True))
        kw = ks_scr[:, h * dh:(h + 1) * dh].astype(F32) * jnp.exp(w_log - m_new)
        decay = jnp.exp(b_last[h] + m_prev[h] - m_new)
        ct_scr[h] = decay * ct_scr[h] + lax.dot_general(kw.astype(BF16), va_scr[h], (((0,), (0,)), ((), ())),
                                                        preferred_element_type=F32)
        m_scr[h] = jnp.broadcast_to(m_new, (1, LANES))

    for h in heads:
        cols = slice(h * dh, (h + 1) * dh)
        tot = sc[h] * q_ct[h] + pv[h]
        hh = tot[:, :dh] / jnp.maximum(jnp.abs(tot[:, dh:]), floor[h])
        mu = jnp.mean(hh, axis=-1, keepdims=True)
        cen = hh - mu
        var = jnp.mean(cen * cen, axis=-1, keepdims=True)
        out = cen * lax.rsqrt(var + HEAD_NORM_EPS) * ng_ref[:, cols] + sk_ref[:, cols] * xc[:, cols]
        o_ref[0, :, cols] = (out * _silu(z_ref[0, :, cols].astype(F32))).astype(o_ref.dtype)


def _mlstm_branch(proj, conv_w, conv_b, wq, wk, wv, gate_w, gate_b, norm_g, skip):
    B, L, N = proj.shape
    W = N // 4
    H = wq.shape[0]
    dh = W // H
    Tc = min(ML_CHUNK, L)
    gw = jnp.zeros((3 * W, LANES), F32).at[:, :2 * H].set(gate_w.astype(F32)).astype(BF16)
    gwt = jnp.transpose(gate_w.astype(F32)).astype(BF16)
    gbc = jnp.zeros((1, LANES), F32).at[0, :2 * H].set(gate_b.astype(F32))
    gbr = gate_b.astype(F32).reshape(2 * H, 1)
    c2 = lambda b, c: (0, 0)
    c3 = lambda b, c: (0, 0, 0)
    return pl.pallas_call(
        _mlstm_kernel,
        grid=(B, L // Tc),
        in_specs=[
            pl.BlockSpec((1, Tc, W), lambda b, c: (b, c, 2)),
            pl.BlockSpec((1, Tc, W), lambda b, c: (b, c, 3)),
            pl.BlockSpec((ML_CONV, W), c2),
            pl.BlockSpec((1, W), c2),
            pl.BlockSpec((H, dh, dh), c3),
            pl.BlockSpec((H, dh, dh), c3),
            pl.BlockSpec((H, dh, dh), c3),
            pl.BlockSpec((3 * W, LANES), c2),
            pl.BlockSpec((2 * H, 3 * W), c2),
            pl.BlockSpec((1, LANES), c2),
            pl.BlockSpec((2 * H, 1), c2),
            pl.BlockSpec((1, W), c2),
            pl.BlockSpec((1, W), c2),
        ],
        out_specs=pl.BlockSpec((1, Tc, W), lambda b, c: (b, c, 0)),
        out_shape=jax.ShapeDtypeStruct((B, L, W), BF16),
        scratch_shapes=[
            pltpu.VMEM((SUBLANES + Tc, W), F32),
            pltpu.VMEM((Tc, 3 * W), BF16),
            pltpu.VMEM((Tc, W), BF16),
            pltpu.VMEM((H, Tc, 2 * dh), BF16),
            pltpu.VMEM((H, dh, 2 * dh), F32),
            pltpu.VMEM((H, 1, LANES), F32),
        ],
        compiler_params=_params("parallel", "arbitrary"),
        name="mlstm",
    )(proj, proj, conv_w.astype(F32), conv_b.reshape(1, W).astype(F32), wq.astype(BF16), wk.astype(BF16),
      wv.astype(BF16), gw, gwt, gbc, gbr, norm_g.reshape(1, W).astype(F32), skip.reshape(1, W).astype(F32))


def _attn_kernel(q_ref, k_ref, vt_ref, z_ref, bias_ref, lam_ref, ng_ref, o_ref, m_scr, acc_scr, *,
                 mode, out_scale, blk, bias_index):
    L = q_ref.shape[2]
    n = L // blk
    lane = lax.broadcasted_iota(jnp.int32, (1, LANES), 1)
    map0 = lane < ATTN_HEAD_DIM
    feat = lax.broadcasted_iota(jnp.int32, (LANES, 1), 0)
    ones_rows = jnp.ones((SUBLANES, blk), BF16)

    def stacked_q(qi):
        q = q_ref[0, 0, qi * blk:(qi + 1) * blk, :]
        zero = jnp.zeros_like(q)
        return jnp.concatenate([jnp.where(map0, q, zero), jnp.where(map0, zero, q)], axis=0)

    def scores(qi, j):
        s = lax.dot_general(k_ref[0, 0, j * blk:(j + 1) * blk, :], q2[qi], (((1,), (1,)), ((), ())),
                            preferred_element_type=F32)
        e = bias_index(qi - j)
        return s if e is None else s + bias_ref[e]

    def update(qi, j, s):
        m_cur = jnp.max(s, axis=0, keepdims=True)
        m_next = m_cur if j == 0 else jnp.maximum(m_scr[qi], m_cur)
        p = jnp.exp2(s - m_next).astype(BF16)
        pv = jnp.dot(jnp.concatenate([vt_ref[0, 0, :, j * blk:(j + 1) * blk], ones_rows], axis=0), p,
                     preferred_element_type=F32)
        if j == 0:
            acc_scr[qi] = pv
        else:
            acc_scr[qi] = jnp.exp2(m_scr[qi] - m_next) * acc_scr[qi] + pv
        m_scr[qi] = m_next

    def finish(qi):
        rows = slice(qi * blk, (qi + 1) * blk)
        acc = acc_scr[qi]
        o = acc[:LANES] / acc[LANES:LANES + 1]
        o0, o1 = o[:, :blk], o[:, blk:]
        if mode == "dilated":
            out = jnp.transpose(jnp.where(feat < ATTN_HEAD_DIM, o0, o1))
        else:
            out = jnp.transpose(o0 - lam_ref[:, 0:1] * o1)
            ms = jnp.mean(out * out, axis=-1, keepdims=True)
            out = out * lax.rsqrt(ms + HEAD_NORM_EPS) * ng_ref[...] * out_scale
        o_ref[0, rows, :] = (out * _silu(z_ref[0, 0, rows, :].astype(F32))).astype(o_ref.dtype)

    steps = []
    for a in range(n // 2):
        b = n - 1 - a
        ja = 0
        for jb in range(b + 1):
            steps.append((b, jb))
            if ja <= a and (jb + 1) * (a + 1) >= (ja + 1) * (b + 1):
                steps.append((a, ja))
                ja += 1
    q2 = {qi: stacked_q(qi) for qi in range(n)}
    pending = {}
    for t in range(min(ATTN_LOOKAHEAD, len(steps))):
        pending[t] = scores(*steps[t])
    for t, (qi, j) in enumerate(steps):
        if t + ATTN_LOOKAHEAD < len(steps):
            pending[t + ATTN_LOOKAHEAD] = scores(*steps[t + ATTN_LOOKAHEAD])
        update(qi, j, pending.pop(t))
        if j == qi:
            finish(qi)


def _dilated_multiplicity(delta):
    mult = np.zeros(delta.shape, np.int64)
    for window, dil in DIL_PAIRS:
        mult += (delta >= 0) & (delta % dil == 0) & (delta <= window)
    return mult


def _bias_tables(L, blk, mode):
    n = L // blk
    key = np.arange(blk)[:, None]
    qry = np.arange(blk)[None, :]
    tables, index = [], {}
    for d in range(n):
        delta = d * blk + qry - key
        mult = _dilated_multiplicity(delta) if mode == "dilated" else (delta >= 0).astype(np.int64)
        t = np.where(mult > 0, np.log2(np.maximum(mult, 1)), NEG_BIG).astype(np.float32)
        if not t.any():
            index[d] = None
            continue
        for e, other in enumerate(tables):
            if np.array_equal(t, other):
                index[d] = e
                break
        else:
            index[d] = len(tables)
            tables.append(t)
    stacked = np.stack(tables)
    return jnp.asarray(np.concatenate([stacked, stacked], axis=2)), index


def _attention(qkz, vt, q0, k0, v0, z0, n_blocks, blk, lam, norm_g, mode, out_scale):
    B, _, L, _ = qkz.shape
    bias, index = _bias_tables(L, blk, mode)
    n_q = L // blk
    assert n_q % 2 == 0
    kern = functools.partial(_attn_kernel, mode=mode, out_scale=out_scale, blk=blk, bias_index=index.get)
    c2 = lambda b, n: (0, 0)
    return pl.pallas_call(
        kern,
        grid=(B, n_blocks),
        in_specs=[
            pl.BlockSpec((1, 1, L, LANES), lambda b, n: (b, q0 + n, 0, 0)),
            pl.BlockSpec((1, 1, L, LANES), lambda b, n: (b, k0 + n, 0, 0)),
            pl.BlockSpec((1, 1, LANES, L), lambda b, n: (b, v0 + n, 0, 0)),
            pl.BlockSpec((1, 1, L, LANES), lambda b, n: (b, z0 + n, 0, 0)),
            pl.BlockSpec(bias.shape, lambda b, n: (0, 0, 0)),
            pl.BlockSpec((1, LANES), c2),
            pl.BlockSpec((1, LANES), c2),
        ],
        out_specs=pl.BlockSpec((1, L, LANES), lambda b, n: (b, 0, n)),
        out_shape=jax.ShapeDtypeStruct((B, L, n_blocks * LANES), BF16),
        scratch_shapes=[
            pltpu.VMEM((n_q, 1, 2 * blk), F32),
            pltpu.VMEM((n_q, LANES + SUBLANES, 2 * blk), F32),
        ],
        compiler_params=_params("parallel", "parallel"),
        name="attn_" + mode,
    )(qkz, qkz, vt, qkz, bias, lam, norm_g)


def _rope_tables(L):
    dh = ATTN_HEAD_DIM
    inv = ROPE_THETA ** (-jnp.arange(0, dh, 2, dtype=F32) / dh)
    ang = jnp.arange(L, dtype=F32)[:, None] * inv[None, :]
    cos, sin = jnp.cos(ang), jnp.sin(ang)
    reps = LANES // dh
    cos_t = jnp.tile(jnp.concatenate([cos, cos], axis=-1), (1, reps))
    sin_t = jnp.tile(jnp.concatenate([-sin, sin], axis=-1), (1, reps))
    return cos_t, sin_t


def _even_layer(h, pre_g, post_g, w_in, s5_params, ml_params, w_out):
    proj = _norm_proj(h, pre_g, w_in.astype(BF16))
    a = _s5_branch(proj, *s5_params)
    b = _mlstm_branch(proj, *ml_params)
    return _out_proj(a, b, w_out, post_g, h)


def _odd_layer(h, pre_g, post_g, w_in, lq1, lk1, lq2, lk2, diff_norm, w_out, layer_idx):
    B, L, D = h.shape
    nblk = PROJ_COLS // LANES
    cos_t, sin_t = _rope_tables(L)
    qkz, vt = _norm_proj_rope(h, pre_g, w_in.astype(BF16), cos_t, sin_t)
    blk = min(ATTN_BLOCK, L)
    lam_init = 0.8 - 0.6 * math.exp(-0.3 * layer_idx)
    lam = (jnp.exp(jnp.sum(lq1.astype(F32) * lk1.astype(F32)))
           - jnp.exp(jnp.sum(lq2.astype(F32) * lk2.astype(F32))) + lam_init)
    lam_row = jnp.full((1, LANES), lam, F32)
    ones_row = jnp.ones((1, LANES), F32)
    base = [b for _, b in ODD_CHUNKS]
    c_out = _attention(qkz, vt, base[0], base[1], base[2], base[3], nblk, blk, ones_row, ones_row, "dilated", 1.0)
    d_out = _attention(qkz, vt, base[4], base[5], base[6], base[7], nblk, blk, lam_row,
                       diff_norm.reshape(1, LANES).astype(F32), "diff", 1.0 - lam_init)
    return _out_proj(c_out, d_out, w_out, post_g, h)


def kernel(x, pre_norm, post_norm, w_in_ab, s5_lambda_re, s5_lambda_im, s5_log_dt, s5_b_re, s5_b_im,
           s5_c_re, s5_c_im, s5_d, s5_glu_w, s5_glu_b, ml_conv_w, ml_conv_b, ml_wq, ml_wk, ml_wv,
           ml_gate_w, ml_gate_b, ml_norm, ml_skip, w_out_ab, w_in_cd, diff_lq1, diff_lk1, diff_lq2,
           diff_lk2, diff_norm, w_out_cd):
    depth = pre_norm.shape[0]
    h = x
    for l in range(depth):
        i = l // 2
        if l % 2 == 0:
            s5_params = (s5_lambda_re[i], s5_lambda_im[i], s5_log_dt[i], s5_b_re[i], s5_b_im[i],
                         s5_c_re[i], s5_c_im[i], s5_d[i], s5_glu_w[i], s5_glu_b[i])
            ml_params = (ml_conv_w[i], ml_conv_b[i], ml_wq[i], ml_wk[i], ml_wv[i], ml_gate_w[i],
                         ml_gate_b[i], ml_norm[i], ml_skip[i])
            h = _even_layer(h, pre_norm[l], post_norm[l], w_in_ab[i], s5_params, ml_params, w_out_ab[i])
        else:
            h = _odd_layer(h, pre_norm[l], post_norm[l], w_in_cd[i], diff_lq1[i], diff_lk1[i], diff_lq2[i],
                           diff_lk2[i], diff_norm[i], w_out_cd[i], l)
    return h
```

```python
import functools
import math

import numpy as np
import jax
import jax.numpy as jnp
from jax import lax
from jax.experimental import pallas as pl
from jax.experimental.pallas import tpu as pltpu

F32 = jnp.float32
BF16 = jnp.bfloat16

S5_GROUP = 16
S5_STATE = 64
ML_HEADS = 4
ML_CONV = 4
DIL_PAIRS = ((128, 1), (512, 4), (2048, 16))
ROPE_THETA = 10000.0
NORM_EPS = 1e-6
HEAD_NORM_EPS = 1e-5
ATTN_HEAD_DIM = 64
QUERY_SCALE = ATTN_HEAD_DIM ** -0.5 * math.log2(math.e)

LANES = 128
SUBLANES = 8
VMEM_LIMIT_BYTES = 56 * 1024 * 1024

NEG_BIG = -1e30

PROJ_ROWS = 512
PROJ_COLS = 512
S5_BATCH = SUBLANES
S5_TIME = 128
S5_PIECE = 32
ML_CHUNK = 256
ATTN_BLOCK = 256
ATTN_LOOKAHEAD = 3


def _params(*semantics):
    return pltpu.CompilerParams(dimension_semantics=semantics, vmem_limit_bytes=VMEM_LIMIT_BYTES)


def _silu(x):
    return x * jax.nn.sigmoid(x)


def _norm_proj_kernel(x_ref, g_ref, w_ref, o_ref, z_scr):
    x = x_ref[0]
    ms = jnp.mean(x * x, axis=-1, keepdims=True)
    z_scr[...] = (x * lax.rsqrt(ms + NORM_EPS) * g_ref[...]).astype(BF16)
    n_out = w_ref.shape[1]
    for c in range(n_out // PROJ_COLS):
        cols = slice(c * PROJ_COLS, (c + 1) * PROJ_COLS)
        o_ref[0, :, cols] = jnp.dot(z_scr[...], w_ref[:, cols], preferred_element_type=F32).astype(o_ref.dtype)


def _norm_proj(x, g, w):
    B, L, D = x.shape
    N = w.shape[1]
    tm = min(PROJ_ROWS, L)
    return pl.pallas_call(
        _norm_proj_kernel,
        grid=(B, L // tm),
        in_specs=[
            pl.BlockSpec((1, tm, D), lambda b, i: (b, i, 0)),
            pl.BlockSpec((1, D), lambda b, i: (0, 0)),
            pl.BlockSpec((D, N), lambda b, i: (0, 0)),
        ],
        out_specs=pl.BlockSpec((1, tm, N), lambda b, i: (b, i, 0)),
        out_shape=jax.ShapeDtypeStruct((B, L, N), BF16),
        scratch_shapes=[pltpu.VMEM((tm, D), BF16)],
        compiler_params=_params("parallel", "parallel"),
        name="norm_proj_even",
    )(x, g.reshape(1, D), w)


def _swap_halves(x, first_half):
    return jnp.where(first_half, pltpu.roll(x, LANES - ATTN_HEAD_DIM // 2, 1), pltpu.roll(x, ATTN_HEAD_DIM // 2, 1))


ODD_CHUNKS = (("q", 0), ("k", 4), ("v", 0), ("z", 8), ("q", 12), ("k", 16), ("v", 4), ("z", 20))


def _norm_proj_rope_kernel(x_ref, g_ref, w_ref, cos_ref, sin_ref, o_ref, vt_ref, z_scr):
    x = x_ref[0]
    ms = jnp.mean(x * x, axis=-1, keepdims=True)
    z_scr[...] = (x * lax.rsqrt(ms + NORM_EPS) * g_ref[...]).astype(BF16)
    slabs = PROJ_COLS // LANES
    lane = lax.broadcasted_iota(jnp.int32, (1, LANES), 1)
    first_half = (lane % ATTN_HEAD_DIM) < (ATTN_HEAD_DIM // 2)
    for c, (kind, base) in enumerate(ODD_CHUNKS):
        y = jnp.dot(z_scr[...], w_ref[:, c * PROJ_COLS:(c + 1) * PROJ_COLS], preferred_element_type=F32)
        for s in range(slabs):
            t = y[:, s * LANES:(s + 1) * LANES]
            if kind in ("q", "k"):
                t = t * cos_ref[...] + _swap_halves(t, first_half) * sin_ref[...]
            if kind == "q":
                t = t * QUERY_SCALE
            if kind == "v":
                vt_ref[0, base + s] = jnp.transpose(t).astype(vt_ref.dtype)
            else:
                o_ref[0, base + s] = t.astype(o_ref.dtype)


def _norm_proj_rope(x, g, w, cos_t, sin_t):
    B, L, D = x.shape
    N = w.shape[1]
    assert N == len(ODD_CHUNKS) * PROJ_COLS
    tm = min(PROJ_ROWS, L)
    n_v = sum(kind == "v" for kind, _ in ODD_CHUNKS) * (PROJ_COLS // LANES)
    n_o = N // LANES - n_v
    return pl.pallas_call(
        _norm_proj_rope_kernel,
        grid=(B, L // tm),
        in_specs=[
            pl.BlockSpec((1, tm, D), lambda b, i: (b, i, 0)),
            pl.BlockSpec((1, D), lambda b, i: (0, 0)),
            pl.BlockSpec((D, N), lambda b, i: (0, 0)),
            pl.BlockSpec((tm, LANES), lambda b, i: (i, 0)),
            pl.BlockSpec((tm, LANES), lambda b, i: (i, 0)),
        ],
        out_specs=[
            pl.BlockSpec((1, n_o, tm, LANES), lambda b, i: (b, 0, i, 0)),
            pl.BlockSpec((1, n_v, LANES, tm), lambda b, i: (b, 0, 0, i)),
        ],
        out_shape=[
            jax.ShapeDtypeStruct((B, n_o, L, LANES), BF16),
            jax.ShapeDtypeStruct((B, n_v, LANES, L), BF16),
        ],
        scratch_shapes=[pltpu.VMEM((tm, D), BF16)],
        compiler_params=_params("parallel", "parallel"),
        name="norm_proj_odd",
    )(x, g.reshape(1, D), w, cos_t, sin_t)


def _out_proj_kernel(a_ref, b_ref, wa_ref, wb_ref, g_ref, h_ref, o_ref):
    y = jnp.dot(a_ref[0], wa_ref[...], preferred_element_type=F32)
    y = y + jnp.dot(b_ref[0], wb_ref[...], preferred_element_type=F32)
    ms = jnp.mean(y * y, axis=-1, keepdims=True)
    o_ref[0] = h_ref[0] + y * lax.rsqrt(ms + NORM_EPS) * g_ref[...]


def _out_proj(a, b, w_out, g, h):
    B, L, W = a.shape
    D = w_out.shape[1]
    tm = min(PROJ_ROWS, L)
    return pl.pallas_call(
        _out_proj_kernel,
        grid=(B, L // tm),
        in_specs=[
            pl.BlockSpec((1, tm, W), lambda b_, i: (b_, i, 0)),
            pl.BlockSpec((1, tm, W), lambda b_, i: (b_, i, 0)),
            pl.BlockSpec((W, D), lambda b_, i: (0, 0)),
            pl.BlockSpec((W, D), lambda b_, i: (0, 0)),
            pl.BlockSpec((1, D), lambda b_, i: (0, 0)),
            pl.BlockSpec((1, tm, D), lambda b_, i: (b_, i, 0)),
        ],
        out_specs=pl.BlockSpec((1, tm, D), lambda b_, i: (b_, i, 0)),
        out_shape=jax.ShapeDtypeStruct((B, L, D), F32),
        compiler_params=_params("parallel", "parallel"),
        name="out_proj",
    )(a, b, w_out[:W].astype(BF16), w_out[W:].astype(BF16), g.reshape(1, D), h)


def _s5_kernel(u_ref, z_ref, bmat_ref, cmat_ref, ar_ref, ai_ref, d_ref, gw_ref, gb_ref, o_ref,
               us_scr, zs_scr, xs_scr, ys_scr, os_scr, st_scr):
    nb, T, W = u_ref.shape
    n_blk = W // LANES
    half = xs_scr.shape[2] // 2
    n_t = T // S5_PIECE
    piece_rows = S5_PIECE * nb

    @pl.when(pl.program_id(1) == 0)
    def _():
        st_scr[...] = jnp.zeros_like(st_scr)

    for b in range(nb):
        ub = u_ref[b].astype(F32)
        zb = z_ref[b].astype(F32)
        for k in range(n_blk):
            us_scr[k, pl.ds(b, T, stride=nb), :] = ub[:, k * LANES:(k + 1) * LANES]
            zs_scr[k, pl.ds(b, T, stride=nb), :] = zb[:, k * LANES:(k + 1) * LANES]

    def rows_of(i):
        return slice(i * piece_rows, (i + 1) * piece_rows)

    def input_matmul(i, k):
        xs_scr[k, rows_of(i), :] = jnp.dot(us_scr[k, rows_of(i), :].astype(BF16), bmat_ref[k],
                                           preferred_element_type=F32)

    def scan(i, k):
        ar, ai = ar_ref[k], ai_ref[k]
        xr, xi = st_scr[k, 0], st_scr[k, 1]
        for t in range(S5_PIECE):
            r = slice(i * piece_rows + t * nb, i * piece_rows + (t + 1) * nb)
            xr, xi = (ar * xr - ai * xi + xs_scr[k, r, 0:half],
                      ar * xi + ai * xr + xs_scr[k, r, half:2 * half])
            xs_scr[k, r, 0:half] = xr
            xs_scr[k, r, half:2 * half] = xi
        st_scr[k, 0] = xr
        st_scr[k, 1] = xi

    def output_matmul(i, k):
        ys_scr[k, rows_of(i), :] = jnp.dot(xs_scr[k, rows_of(i), :].astype(BF16), cmat_ref[k],
                                           preferred_element_type=F32)

    def glu(i):
        cat = lambda ref: jnp.concatenate([ref[k, rows_of(i), :] for k in range(n_blk)], axis=1)
        y = cat(ys_scr) + d_ref[...] * cat(us_scr)
        g = jax.nn.gelu(y)
        gate = jnp.dot(g.astype(BF16), gw_ref[...], preferred_element_type=F32) + gb_ref[...]
        out = g * jax.nn.sigmoid(gate) * _silu(cat(zs_scr))
        for k in range(n_blk):
            os_scr[k, rows_of(i), :] = out[:, k * LANES:(k + 1) * LANES]

    def drain(p):
        i, k = pieces[p]
        output_matmul(i, k)
        if k == n_blk - 1:
            glu(i)

    pieces = [(i, k) for i in range(n_t) for k in range(n_blk)]
    input_matmul(*pieces[0])
    for p, (i, k) in enumerate(pieces):
        if p + 1 < len(pieces):
            input_matmul(*pieces[p + 1])
        if p >= 1:
            drain(p - 1)
        scan(i, k)
    drain(len(pieces) - 1)

    for b in range(nb):
        o_ref[b] = jnp.concatenate([os_scr[k, pl.ds(b, T, stride=nb), :] for k in range(n_blk)],
                                   axis=1).astype(o_ref.dtype)


def _s5_weights(lam_re, lam_im, log_dt, b_re, b_im, c_re, c_im):
    G, P = lam_re.shape
    gpb = LANES // S5_GROUP
    n_blk = G // gpb
    lr, li = lam_re.astype(F32), lam_im.astype(F32)
    dt = jnp.exp(log_dt.astype(F32))[:, None]
    mag = jnp.exp(lr * dt)
    ab_r, ab_i = mag * jnp.cos(li * dt), mag * jnp.sin(li * dt)
    den = lr * lr + li * li
    f_r = ((ab_r - 1.0) * lr + ab_i * li) / den
    f_i = (ab_i * lr - (ab_r - 1.0) * li) / den
    br, bi = b_re.astype(F32), b_im.astype(F32)
    bb_r = f_r[..., None] * br - f_i[..., None] * bi
    bb_i = f_r[..., None] * bi + f_i[..., None] * br
    eye = jnp.eye(gpb, dtype=F32)
    bb = jnp.stack([bb_r, bb_i]).reshape(2, n_blk, gpb, P, S5_GROUP)
    bb = jnp.transpose(bb, (1, 2, 4, 0, 3))
    bmat = bb[:, :, :, :, None, :] * eye[None, :, None, None, :, None]
    bmat = bmat.reshape(n_blk, LANES, 2 * gpb * P).astype(BF16)
    cc = jnp.stack([c_re.astype(F32), -c_im.astype(F32)]).reshape(2, n_blk, gpb, S5_GROUP, P)
    cc = jnp.transpose(cc, (1, 0, 2, 4, 3))
    cmat = cc[:, :, :, :, None, :] * eye[None, None, :, None, :, None]
    cmat = cmat.reshape(n_blk, 2 * gpb * P, LANES).astype(BF16)
    ar = jnp.broadcast_to(ab_r.reshape(n_blk, 1, gpb * P), (n_blk, S5_BATCH, gpb * P))
    ai = jnp.broadcast_to(ab_i.reshape(n_blk, 1, gpb * P), (n_blk, S5_BATCH, gpb * P))
    return bmat, cmat, ar, ai


def _s5_branch(proj, lam_re, lam_im, log_dt, b_re, b_im, c_re, c_im, d_skip, glu_w, glu_b):
    B, L, N = proj.shape
    W = N // 4
    nb = S5_BATCH
    T = min(S5_TIME, L)
    n_blk = W // LANES
    bmat, cmat, ar, ai = _s5_weights(lam_re, lam_im, log_dt, b_re, b_im, c_re, c_im)
    S2 = bmat.shape[2]
    const3 = lambda g, c: (0, 0, 0)
    const2 = lambda g, c: (0, 0)
    return pl.pallas_call(
        _s5_kernel,
        grid=(B // nb, L // T),
        in_specs=[
            pl.BlockSpec((nb, T, W), lambda g, c: (g, c, 0)),
            pl.BlockSpec((nb, T, W), lambda g, c: (g, c, 1)),
            pl.BlockSpec(bmat.shape, const3),
            pl.BlockSpec(cmat.shape, const3),
            pl.BlockSpec(ar.shape, const3),
            pl.BlockSpec(ai.shape, const3),
            pl.BlockSpec((1, W), const2),
            pl.BlockSpec((W, W), const2),
            pl.BlockSpec((1, W), const2),
        ],
        out_specs=pl.BlockSpec((nb, T, W), lambda g, c: (g, c, 0)),
        out_shape=jax.ShapeDtypeStruct((B, L, W), BF16),
        scratch_shapes=[
            pltpu.VMEM((n_blk, nb * T, LANES), F32),
            pltpu.VMEM((n_blk, nb * T, LANES), F32),
            pltpu.VMEM((n_blk, nb * T, S2), F32),
            pltpu.VMEM((n_blk, nb * T, LANES), F32),
            pltpu.VMEM((n_blk, nb * T, LANES), F32),
            pltpu.VMEM((n_blk, 2, nb, S2 // 2), F32),
        ],
        compiler_params=_params("parallel", "arbitrary"),
        name="s5_scan",
    )(proj, proj, bmat, cmat, ar, ai, d_skip.reshape(1, W).astype(F32), glu_w.astype(BF16),
      glu_b.reshape(1, W).astype(F32))


def _log_sigmoid(x):
    return jnp.minimum(x, 0.0) - jnp.log1p(jnp.exp(-jnp.abs(x)))


def _split3(x):
    hi = x.astype(BF16)
    r1 = x - hi.astype(F32)
    mid = r1.astype(BF16)
    lo = (r1 - mid.astype(F32)).astype(BF16)
    return hi, mid, lo


def _mlstm_kernel(x_ref, z_ref, cw_ref, cb_ref, wq_ref, wk_ref, wv_ref, gw_ref, gwt_ref, gbc_ref, gbr_ref,
                  ng_ref, sk_ref, o_ref, xe_scr, qkv_scr, ks_scr, va_scr, ct_scr, m_scr):
    Tc, W = x_ref.shape[1], x_ref.shape[2]
    H = wq_ref.shape[0]
    dh = W // H
    pad = SUBLANES

    @pl.when(pl.program_id(1) == 0)
    def _():
        xe_scr[0:pad, :] = jnp.zeros((pad, W), F32)
        ct_scr[...] = jnp.zeros_like(ct_scr)
        m_scr[...] = jnp.zeros_like(m_scr)
        va_scr[:, :, dh:] = jnp.ones((H, Tc, dh), BF16)

    xb = x_ref[0]
    x = xb.astype(F32)
    xe_scr[pad:pad + Tc, :] = x
    xc = cb_ref[...] + sum(xe_scr[pl.ds(pad - (ML_CONV - 1) + j, Tc), :] * cw_ref[j:j + 1, :]
                           for j in range(ML_CONV))
    xc = _silu(xc)
    xe_scr[0:pad, :] = x[Tc - pad:Tc, :]
    xcb = xc.astype(BF16)

    for h in range(H):
        cols = slice(h * dh, (h + 1) * dh)
        q = jnp.dot(xcb[:, cols], wq_ref[h], preferred_element_type=F32)
        k = jnp.dot(xcb[:, cols], wk_ref[h], preferred_element_type=F32)
        v = jnp.dot(xb[:, cols], wv_ref[h], preferred_element_type=F32)
        qkv_scr[:, h * dh:(h + 1) * dh] = q.astype(BF16)
        qkv_scr[:, W + h * dh:W + (h + 1) * dh] = k.astype(BF16)
        qkv_scr[:, 2 * W + h * dh:2 * W + (h + 1) * dh] = v.astype(BF16)
        va_scr[h, :, :dh] = v.astype(BF16)
        ks_scr[:, cols] = (k * (dh ** -0.5)).astype(BF16)

    qkv = qkv_scr[...]
    g_col = jnp.dot(qkv, gw_ref[...], preferred_element_type=F32) + gbc_ref[...]
    g_row = lax.dot_general(gwt_ref[...], qkv, (((1,), (1,)), ((), ())),
                            preferred_element_type=F32) + gbr_ref[...]
    ti = lax.broadcasted_iota(jnp.int32, (Tc, Tc), 0)
    si = lax.broadcasted_iota(jnp.int32, (Tc, Tc), 1)
    causal = si <= ti
    lower = jnp.where(causal, 1.0, 0.0).astype(BF16)
    upper = jnp.where(ti <= si, 1.0, 0.0).astype(BF16)
    b_col = sum(jnp.dot(lower, p, preferred_element_type=F32) for p in _split3(_log_sigmoid(g_col)))
    b_row = sum(jnp.dot(p, upper, preferred_element_type=F32) for p in _split3(_log_sigmoid(g_row)))

    heads = range(H)
    q_h = [qkv_scr[:, h * dh:(h + 1) * dh] for h in heads]
    m_prev = [m_scr[h][:, 0:1] for h in heads]
    bc = [b_col[:, H + h:H + h + 1] for h in heads]
    b_last = [bc[h][Tc - 1:Tc, :] for h in heads]

    s_qk = [lax.dot_general(q_h[h], ks_scr[:, h * dh:(h + 1) * dh], (((1,), (1,)), ((), ())),
                            preferred_element_type=F32) for h in heads]
    q_ct = [jnp.dot(q_h[h], ct_scr[h].astype(BF16), preferred_element_type=F32) for h in heads]

    dmat, sc, floor = [], [], []
    for h in heads:
        log_d = jnp.where(causal, bc[h] - b_row[H + h:H + h + 1, :] + g_row[h:h + 1, :], NEG_BIG)
        inter = bc[h] + m_prev[h]
        m_t = jnp.maximum(inter, jnp.max(log_d, axis=-1, keepdims=True))
        dmat.append(jnp.exp(log_d - m_t))
        sc.append(jnp.exp(inter - m_t))
        floor.append(jnp.exp(-m_t))

    pv = [jnp.dot((s_qk[h] * dmat[h]).astype(BF16), va_scr[h], preferred_element_type=F32) for h in heads]

    for h in heads:
        w_log = b_last[h] - bc[h] + g_col[:, h:h + 1]
        m_new = jnp.maximum(b_last[h] + m_prev[h], jnp.max(w_log, axis=0, keepdims=True))
        kw = ks_scr[:, h * dh:(h + 1) * dh].astype(F32) * jnp.exp(w_log - m_new)
        decay = jnp.exp(b_last[h] + m_prev[h] - m_new)
        ct_scr[h] = decay * ct_scr[h] + lax.dot_general(kw.astype(BF16), va_scr[h], (((0,), (0,)), ((), ())),
                                                        preferred_element_type=F32)
        m_scr[h] = jnp.broadcast_to(m_new, (1, LANES))

    for h in heads:
        cols = slice(h * dh, (h + 1) * dh)
        tot = sc[h] * q_ct[h] + pv[h]
        hh = tot[:, :dh] / jnp.maximum(jnp.abs(tot[:, dh:]), floor[h])
        mu = jnp.mean(hh, axis=-1, keepdims=True)
        cen = hh - mu
        var = jnp.mean(cen * cen, axis=-1, keepdims=True)
        out = cen * lax.rsqrt(var + HEAD_NORM_EPS) * ng_ref[:, cols] + sk_ref[:, cols] * xc[:, cols]
        o_ref[0, :, cols] = (out * _silu(z_ref[0, :, cols].astype(F32))).astype(o_ref.dtype)


def _mlstm_branch(proj, conv_w, conv_b, wq, wk, wv, gate_w, gate_b, norm_g, skip):
    B, L, N = proj.shape
    W = N // 4
    H = wq.shape[0]
    dh = W // H
    Tc = min(ML_CHUNK, L)
    gw = jnp.zeros((3 * W, LANES), F32).at[:, :2 * H].set(gate_w.astype(F32)).astype(BF16)
    gwt = jnp.transpose(gate_w.astype(F32)).astype(BF16)
    gbc = jnp.zeros((1, LANES), F32).at[0, :2 * H].set(gate_b.astype(F32))
    gbr = gate_b.astype(F32).reshape(2 * H, 1)
    c2 = lambda b, c: (0, 0)
    c3 = lambda b, c: (0, 0, 0)
    return pl.pallas_call(
        _mlstm_kernel,
        grid=(B, L // Tc),
        in_specs=[
            pl.BlockSpec((1, Tc, W), lambda b, c: (b, c, 2)),
            pl.BlockSpec((1, Tc, W), lambda b, c: (b, c, 3)),
            pl.BlockSpec((ML_CONV, W), c2),
            pl.BlockSpec((1, W), c2),
            pl.BlockSpec((H, dh, dh), c3),
            pl.BlockSpec((H, dh, dh), c3),
            pl.BlockSpec((H, dh, dh), c3),
            pl.BlockSpec((3 * W, LANES), c2),
            pl.BlockSpec((2 * H, 3 * W), c2),
            pl.BlockSpec((1, LANES), c2),
            pl.BlockSpec((2 * H, 1), c2),
            pl.BlockSpec((1, W), c2),
            pl.BlockSpec((1, W), c2),
        ],
        out_specs=pl.BlockSpec((1, Tc, W), lambda b, c: (b, c, 0)),
        out_shape=jax.ShapeDtypeStruct((B, L, W), BF16),
        scratch_shapes=[
            pltpu.VMEM((SUBLANES + Tc, W), F32),
            pltpu.VMEM((Tc, 3 * W), BF16),
            pltpu.VMEM((Tc, W), BF16),
            pltpu.VMEM((H, Tc, 2 * dh), BF16),
            pltpu.VMEM((H, dh, 2 * dh), F32),
            pltpu.VMEM((H, 1, LANES), F32),
        ],
        compiler_params=_params("parallel", "arbitrary"),
        name="mlstm",
    )(proj, proj, conv_w.astype(F32), conv_b.reshape(1, W).astype(F32), wq.astype(BF16), wk.astype(BF16),
      wv.astype(BF16), gw, gwt, gbc, gbr, norm_g.reshape(1, W).astype(F32), skip.reshape(1, W).astype(F32))


def _attn_kernel(q_ref, k_ref, vt_ref, z_ref, bias_ref, lam_ref, ng_ref, o_ref, m_scr, acc_scr, *,
                 mode, out_scale, blk, bias_index):
    L = q_ref.shape[2]
    n = L // blk
    lane = lax.broadcasted_iota(jnp.int32, (1, LANES), 1)
    map0 = lane < ATTN_HEAD_DIM
    feat = lax.broadcasted_iota(jnp.int32, (LANES, 1), 0)
    ones_rows = jnp.ones((SUBLANES, blk), BF16)

    def stacked_q(qi):
        q = q_ref[0, 0, qi * blk:(qi + 1) * blk, :]
        zero = jnp.zeros_like(q)
        return jnp.concatenate([jnp.where(map0, q, zero), jnp.where(map0, zero, q)], axis=0)

    def scores(qi, j):
        s = lax.dot_general(k_ref[0, 0, j * blk:(j + 1) * blk, :], q2[qi], (((1,), (1,)), ((), ())),
                            preferred_element_type=F32)
        e = bias_index(qi - j)
        return s if e is None else s + bias_ref[e]

    def update(qi, j, s):
        m_cur = jnp.max(s, axis=0, keepdims=True)
        m_next = m_cur if j == 0 else jnp.maximum(m_scr[qi], m_cur)
        p = jnp.exp2(s - m_next).astype(BF16)
        pv = jnp.dot(jnp.concatenate([vt_ref[0, 0, :, j * blk:(j + 1) * blk], ones_rows], axis=0), p,
                     preferred_element_type=F32)
        if j == 0:
            acc_scr[qi] = pv
        else:
            acc_scr[qi] = jnp.exp2(m_scr[qi] - m_next) * acc_scr[qi] + pv
        m_scr[qi] = m_next

    def finish(qi):
        rows = slice(qi * blk, (qi + 1) * blk)
        acc = acc_scr[qi]
        o = acc[:LANES] / acc[LANES:LANES + 1]
        o0, o1 = o[:, :blk], o[:, blk:]
        if mode == "dilated":
            out = jnp.transpose(jnp.where(feat < ATTN_HEAD_DIM, o0, o1))
        else:
            out = jnp.transpose(o0 - lam_ref[:, 0:1] * o1)
            ms = jnp.mean(out * out, axis=-1, keepdims=True)
            out = out * lax.rsqrt(ms + HEAD_NORM_EPS) * ng_ref[...] * out_scale
        o_ref[0, rows, :] = (out * _silu(z_ref[0, 0, rows, :].astype(F32))).astype(o_ref.dtype)

    steps = []
    for a in range(n // 2):
        b = n - 1 - a
        ja = 0
        for jb in range(b + 1):
            steps.append((b, jb))
            if ja <= a and (jb + 1) * (a + 1) >= (ja + 1) * (b + 1):
                steps.append((a, ja))
                ja += 1
    q2 = {qi: stacked_q(qi) for qi in range(n)}
    pending = {}
    for t in range(min(ATTN_LOOKAHEAD, len(steps))):
        pending[t] = scores(*steps[t])
    for t, (qi, j) in enumerate(steps):
        if t + ATTN_LOOKAHEAD < len(steps):
            pending[t + ATTN_LOOKAHEAD] = scores(*steps[t + ATTN_LOOKAHEAD])
        update(qi, j, pending.pop(t))
        if j == qi:
            finish(qi)


def _dilated_multiplicity(delta):
    mult = np.zeros(delta.shape, np.int64)
    for window, dil in DIL_PAIRS:
        mult += (delta >= 0) & (delta % dil == 0) & (delta <= window)
    return mult


def _bias_tables(L, blk, mode):
    n = L // blk
    key = np.arange(blk)[:, None]
    qry = np.arange(blk)[None, :]
    tables, index = [], {}
    for d in range(n):
        delta = d * blk + qry - key
        mult = _dilated_multiplicity(delta) if mode == "dilated" else (delta >= 0).astype(np.int64)
        t = np.where(mult > 0, np.log2(np.maximum(mult, 1)), NEG_BIG).astype(np.float32)
        if not t.any():
            index[d] = None
            continue
        for e, other in enumerate(tables):
            if np.array_equal(t, other):
                index[d] = e
                break
        else:
            index[d] = len(tables)
            tables.append(t)
    stacked = np.stack(tables)
    return jnp.asarray(np.concatenate([stacked, stacked], axis=2)), index


def _attention(qkz, vt, q0, k0, v0, z0, n_blocks, blk, lam, norm_g, mode, out_scale):
    B, _, L, _ = qkz.shape
    bias, index = _bias_tables(L, blk, mode)
    n_q = L // blk
    assert n_q % 2 == 0
    kern = functools.partial(_attn_kernel, mode=mode, out_scale=out_scale, blk=blk, bias_index=index.get)
    c2 = lambda b, n: (0, 0)
    return pl.pallas_call(
        kern,
        grid=(B, n_blocks),
        in_specs=[
            pl.BlockSpec((1, 1, L, LANES), lambda b, n: (b, q0 + n, 0, 0)),
            pl.BlockSpec((1, 1, L, LANES), lambda b, n: (b, k0 + n, 0, 0)),
            pl.BlockSpec((1, 1, LANES, L), lambda b, n: (b, v0 + n, 0, 0)),
            pl.BlockSpec((1, 1, L, LANES), lambda b, n: (b, z0 + n, 0, 0)),
            pl.BlockSpec(bias.shape, lambda b, n: (0, 0, 0)),
            pl.BlockSpec((1, LANES), c2),
            pl.BlockSpec((1, LANES), c2),
        ],
        out_specs=pl.BlockSpec((1, L, LANES), lambda b, n: (b, 0, n)),
        out_shape=jax.ShapeDtypeStruct((B, L, n_blocks * LANES), BF16),
        scratch_shapes=[
            pltpu.VMEM((n_q, 1, 2 * blk), F32),
            pltpu.VMEM((n_q, LANES + SUBLANES, 2 * blk), F32),
        ],
        compiler_params=_params("parallel", "parallel"),
        name="attn_" + mode,
    )(qkz, qkz, vt, qkz, bias, lam, norm_g)


def _rope_tables(L):
    dh = ATTN_HEAD_DIM
    inv = ROPE_THETA ** (-jnp.arange(0, dh, 2, dtype=F32) / dh)
    ang = jnp.arange(L, dtype=F32)[:, None] * inv[None, :]
    cos, sin = jnp.cos(ang), jnp.sin(ang)
    reps = LANES // dh
    cos_t = jnp.tile(jnp.concatenate([cos, cos], axis=-1), (1, reps))
    sin_t = jnp.tile(jnp.concatenate([-sin, sin], axis=-1), (1, reps))
    return cos_t, sin_t


def _even_layer(h, pre_g, post_g, w_in, s5_params, ml_params, w_out):
    proj = _norm_proj(h, pre_g, w_in.astype(BF16))
    a = _s5_branch(proj, *s5_params)
    b = _mlstm_branch(proj, *ml_params)
    return _out_proj(a, b, w_out, post_g, h)


def _odd_layer(h, pre_g, post_g, w_in, lq1, lk1, lq2, lk2, diff_norm, w_out, layer_idx):
    B, L, D = h.shape
    nblk = PROJ_COLS // LANES
    cos_t, sin_t = _rope_tables(L)
    qkz, vt = _norm_proj_rope(h, pre_g, w_in.astype(BF16), cos_t, sin_t)
    blk = min(ATTN_BLOCK, L)
    lam_init = 0.8 - 0.6 * math.exp(-0.3 * layer_idx)
    lam = (jnp.exp(jnp.sum(lq1.astype(F32) * lk1.astype(F32)))
           - jnp.exp(jnp.sum(lq2.astype(F32) * lk2.astype(F32))) + lam_init)
    lam_row = jnp.full((1, LANES), lam, F32)
    ones_row = jnp.ones((1, LANES), F32)
    base = [b for _, b in ODD_CHUNKS]
    c_out = _attention(qkz, vt, base[0], base[1], base[2], base[3], nblk, blk, ones_row, ones_row, "dilated", 1.0)
    d_out = _attention(qkz, vt, base[4], base[5], base[6], base[7], nblk, blk, lam_row,
                       diff_norm.reshape(1, LANES).astype(F32), "diff", 1.0 - lam_init)
    return _out_proj(c_out, d_out, w_out, post_g, h)


def kernel(x, pre_norm, post_norm, w_in_ab, s5_lambda_re, s5_lambda_im, s5_log_dt, s5_b_re, s5_b_im,
           s5_c_re, s5_c_im, s5_d, s5_glu_w, s5_glu_b, ml_conv_w, ml_conv_b, ml_wq, ml_wk, ml_wv,
           ml_gate_w, ml_gate_b, ml_norm, ml_skip, w_out_ab, w_in_cd, diff_lq1, diff_lk1, diff_lq2,
           diff_lk2, diff_norm, w_out_cd):
    depth = pre_norm.shape[0]
    h = x
    for l in range(depth):
        i = l // 2
        if l % 2 == 0:
            s5_params = (s5_lambda_re[i], s5_lambda_im[i], s5_log_dt[i], s5_b_re[i], s5_b_im[i],
                         s5_c_re[i], s5_c_im[i], s5_d[i], s5_glu_w[i], s5_glu_b[i])
            ml_params = (ml_conv_w[i], ml_conv_b[i], ml_wq[i], ml_wk[i], ml_wv[i], ml_gate_w[i],
                         ml_gate_b[i], ml_norm[i], ml_skip[i])
            h = _even_layer(h, pre_norm[l], post_norm[l], w_in_ab[i], s5_params, ml_params, w_out_ab[i])
        else:
            h = _odd_layer(h, pre_norm[l], post_norm[l], w_in_cd[i], diff_lq1[i], diff_lk1[i], diff_lq2[i],
                           diff_lk2[i], diff_norm[i], w_out_cd[i], l)
    return h
```

```python
import functools
import math

import numpy as np
import jax
import jax.numpy as jnp
from jax import lax
from jax.experimental import pallas as pl
from jax.experimental.pallas import tpu as pltpu

F32 = jnp.float32
BF16 = jnp.bfloat16

S5_GROUP = 16
S5_STATE = 64
ML_HEADS = 4
ML_CONV = 4
DIL_PAIRS = ((128, 1), (512, 4), (2048, 16))
ROPE_THETA = 10000.0
NORM_EPS = 1e-6
HEAD_NORM_EPS = 1e-5
ATTN_HEAD_DIM = 64
QUERY_SCALE = ATTN_HEAD_DIM ** -0.5 * math.log2(math.e)

LANES = 128
SUBLANES = 8
VMEM_LIMIT_BYTES = 56 * 1024 * 1024

NEG_BIG = -1e30

PROJ_ROWS = 512
OUT_PROJ_ROWS = 1024
PROJ_COLS = 512
S5_BATCH = SUBLANES
S5_TIME = 128
S5_PIECE = 32
ML_CHUNK = 256
ATTN_BLOCK = 256
ATTN_LOOKAHEAD = 3


def _params(*semantics):
    return pltpu.CompilerParams(dimension_semantics=semantics, vmem_limit_bytes=VMEM_LIMIT_BYTES)


def _silu(x):
    return x * jax.nn.sigmoid(x)


def _norm_proj_kernel(x_ref, g_ref, w_ref, o_ref, z_scr):
    x = x_ref[0]
    ms = jnp.mean(x * x, axis=-1, keepdims=True)
    z_scr[...] = (x * lax.rsqrt(ms + NORM_EPS) * g_ref[...]).astype(BF16)
    n_out = w_ref.shape[1]
    for c in range(n_out // PROJ_COLS):
        cols = slice(c * PROJ_COLS, (c + 1) * PROJ_COLS)
        o_ref[0, :, cols] = jnp.dot(z_scr[...], w_ref[:, cols], preferred_element_type=F32).astype(o_ref.dtype)


def _norm_proj(x, g, w):
    B, L, D = x.shape
    N = w.shape[1]
    tm = min(PROJ_ROWS, L)
    return pl.pallas_call(
        _norm_proj_kernel,
        grid=(B, L // tm),
        in_specs=[
            pl.BlockSpec((1, tm, D), lambda b, i: (b, i, 0)),
            pl.BlockSpec((1, D), lambda b, i: (0, 0)),
            pl.BlockSpec((D, N), lambda b, i: (0, 0)),
        ],
        out_specs=pl.BlockSpec((1, tm, N), lambda b, i: (b, i, 0)),
        out_shape=jax.ShapeDtypeStruct((B, L, N), BF16),
        scratch_shapes=[pltpu.VMEM((tm, D), BF16)],
        compiler_params=_params("parallel", "parallel"),
        name="norm_proj_even",
    )(x, g.reshape(1, D), w)


def _swap_halves(x, first_half):
    return jnp.where(first_half, pltpu.roll(x, LANES - ATTN_HEAD_DIM // 2, 1), pltpu.roll(x, ATTN_HEAD_DIM // 2, 1))


ODD_CHUNKS = (("q", 0), ("k", 4), ("v", 0), ("z", 8), ("q", 12), ("k", 16), ("v", 4), ("z", 20))


def _out_norm_proj_rope_kernel(a_ref, b_ref, wa_ref, wb_ref, gpost_ref, h_ref, g_ref, w_ref, cos_ref, sin_ref,
                               hout_ref, o_ref, vt_ref, z_scr):
    y = jnp.dot(a_ref[0], wa_ref[...], preferred_element_type=F32)
    y = y + jnp.dot(b_ref[0], wb_ref[...], preferred_element_type=F32)
    ms = jnp.mean(y * y, axis=-1, keepdims=True)
    x = h_ref[0] + y * lax.rsqrt(ms + NORM_EPS) * gpost_ref[...]
    hout_ref[0] = x
    ms = jnp.mean(x * x, axis=-1, keepdims=True)
    z_scr[...] = (x * lax.rsqrt(ms + NORM_EPS) * g_ref[...]).astype(BF16)
    slabs = PROJ_COLS // LANES
    lane = lax.broadcasted_iota(jnp.int32, (1, LANES), 1)
    first_half = (lane % ATTN_HEAD_DIM) < (ATTN_HEAD_DIM // 2)
    for c, (kind, base) in enumerate(ODD_CHUNKS):
        y = jnp.dot(z_scr[...], w_ref[:, c * PROJ_COLS:(c + 1) * PROJ_COLS], preferred_element_type=F32)
        for s in range(slabs):
            t = y[:, s * LANES:(s + 1) * LANES]
            if kind in ("q", "k"):
                t = t * cos_ref[...] + _swap_halves(t, first_half) * sin_ref[...]
            if kind == "q":
                t = t * QUERY_SCALE
            if kind == "v":
                vt_ref[0, base + s] = jnp.transpose(t).astype(vt_ref.dtype)
            else:
                o_ref[0, base + s] = t.astype(o_ref.dtype)


def _out_norm_proj_rope(a, b, w_out, g_post, h, g, w, cos_t, sin_t):
    B, L, D = h.shape
    Wb = a.shape[2]
    N = w.shape[1]
    assert N == len(ODD_CHUNKS) * PROJ_COLS
    tm = min(PROJ_ROWS, L)
    n_v = sum(kind == "v" for kind, _ in ODD_CHUNKS) * (PROJ_COLS // LANES)
    n_o = N // LANES - n_v
    row = lambda b_, i: (b_, i, 0)
    const = lambda b_, i: (0, 0)
    return pl.pallas_call(
        _out_norm_proj_rope_kernel,
        grid=(B, L // tm),
        in_specs=[
            pl.BlockSpec((1, tm, Wb), row),
            pl.BlockSpec((1, tm, Wb), row),
            pl.BlockSpec((Wb, D), const),
            pl.BlockSpec((Wb, D), const),
            pl.BlockSpec((1, D), const),
            pl.BlockSpec((1, tm, D), row),
            pl.BlockSpec((1, D), const),
            pl.BlockSpec((D, N), const),
            pl.BlockSpec((tm, LANES), lambda b_, i: (i, 0)),
            pl.BlockSpec((tm, LANES), lambda b_, i: (i, 0)),
        ],
        out_specs=[
            pl.BlockSpec((1, tm, D), row),
            pl.BlockSpec((1, n_o, tm, LANES), lambda b_, i: (b_, 0, i, 0)),
            pl.BlockSpec((1, n_v, LANES, tm), lambda b_, i: (b_, 0, 0, i)),
        ],
        out_shape=[
            jax.ShapeDtypeStruct((B, L, D), F32),
            jax.ShapeDtypeStruct((B, n_o, L, LANES), BF16),
            jax.ShapeDtypeStruct((B, n_v, LANES, L), BF16),
        ],
        scratch_shapes=[pltpu.VMEM((tm, D), BF16)],
        compiler_params=_params("parallel", "parallel"),
        name="out_norm_proj_odd",
    )(a, b, w_out[:Wb].astype(BF16), w_out[Wb:].astype(BF16), g_post.reshape(1, D), h, g.reshape(1, D), w,
      cos_t, sin_t)


def _out_proj_kernel(a_ref, b_ref, wa_ref, wb_ref, g_ref, h_ref, o_ref):
    y = jnp.dot(a_ref[0], wa_ref[...], preferred_element_type=F32)
    y = y + jnp.dot(b_ref[0], wb_ref[...], preferred_element_type=F32)
    ms = jnp.mean(y * y, axis=-1, keepdims=True)
    o_ref[0] = h_ref[0] + y * lax.rsqrt(ms + NORM_EPS) * g_ref[...]


def _out_proj(a, b, w_out, g, h):
    B, L, W = a.shape
    D = w_out.shape[1]
    tm = min(OUT_PROJ_ROWS, L)
    return pl.pallas_call(
        _out_proj_kernel,
        grid=(B, L // tm),
        in_specs=[
            pl.BlockSpec((1, tm, W), lambda b_, i: (b_, i, 0)),
            pl.BlockSpec((1, tm, W), lambda b_, i: (b_, i, 0)),
            pl.BlockSpec((W, D), lambda b_, i: (0, 0)),
            pl.BlockSpec((W, D), lambda b_, i: (0, 0)),
            pl.BlockSpec((1, D), lambda b_, i: (0, 0)),
            pl.BlockSpec((1, tm, D), lambda b_, i: (b_, i, 0)),
        ],
        out_specs=pl.BlockSpec((1, tm, D), lambda b_, i: (b_, i, 0)),
        out_shape=jax.ShapeDtypeStruct((B, L, D), F32),
        compiler_params=_params("parallel", "parallel"),
        name="out_proj",
    )(a, b, w_out[:W].astype(BF16), w_out[W:].astype(BF16), g.reshape(1, D), h)


def _s5_kernel(u_ref, z_ref, bmat_ref, cmat_ref, ar_ref, ai_ref, d_ref, gw_ref, gb_ref, o_ref,
               us_scr, zs_scr, xs_scr, ys_scr, os_scr, st_scr):
    nb, T, W = u_ref.shape
    n_blk = W // LANES
    half = xs_scr.shape[2] // 2
    n_t = T // S5_PIECE
    piece_rows = S5_PIECE * nb

    @pl.when(pl.program_id(1) == 0)
    def _():
        st_scr[...] = jnp.zeros_like(st_scr)

    for b in range(nb):
        ub = u_ref[b].astype(F32)
        zb = z_ref[b].astype(F32)
        for k in range(n_blk):
            us_scr[k, pl.ds(b, T, stride=nb), :] = ub[:, k * LANES:(k + 1) * LANES]
            zs_scr[k, pl.ds(b, T, stride=nb), :] = zb[:, k * LANES:(k + 1) * LANES]

    def rows_of(i):
        return slice(i * piece_rows, (i + 1) * piece_rows)

    def input_matmul(i, k):
        xs_scr[k, rows_of(i), :] = jnp.dot(us_scr[k, rows_of(i), :].astype(BF16), bmat_ref[k],
                                           preferred_element_type=F32)

    def scan(i, k):
        ar, ai = ar_ref[k], ai_ref[k]
        xr, xi = st_scr[k, 0], st_scr[k, 1]
        for t in range(S5_PIECE):
            r = slice(i * piece_rows + t * nb, i * piece_rows + (t + 1) * nb)
            xr, xi = (ar * xr - ai * xi + xs_scr[k, r, 0:half],
                      ar * xi + ai * xr + xs_scr[k, r, half:2 * half])
            xs_scr[k, r, 0:half] = xr
            xs_scr[k, r, half:2 * half] = xi
        st_scr[k, 0] = xr
        st_scr[k, 1] = xi

    def output_matmul(i, k):
        ys_scr[k, rows_of(i), :] = jnp.dot(xs_scr[k, rows_of(i), :].astype(BF16), cmat_ref[k],
                                           preferred_element_type=F32)

    def glu(i):
        cat = lambda ref: jnp.concatenate([ref[k, rows_of(i), :] for k in range(n_blk)], axis=1)
        y = cat(ys_scr) + d_ref[...] * cat(us_scr)
        g = jax.nn.gelu(y)
        gate = jnp.dot(g.astype(BF16), gw_ref[...], preferred_element_type=F32) + gb_ref[...]
        out = g * jax.nn.sigmoid(gate) * _silu(cat(zs_scr))
        for k in range(n_blk):
            os_scr[k, rows_of(i), :] = out[:, k * LANES:(k + 1) * LANES]

    def drain(p):
        i, k = pieces[p]
        output_matmul(i, k)
        if k == n_blk - 1:
            glu(i)

    pieces = [(i, k) for i in range(n_t) for k in range(n_blk)]
    input_matmul(*pieces[0])
    for p, (i, k) in enumerate(pieces):
        if p + 1 < len(pieces):
            input_matmul(*pieces[p + 1])
        if p >= 1:
            drain(p - 1)
        scan(i, k)
    drain(len(pieces) - 1)

    for b in range(nb):
        o_ref[b] = jnp.concatenate([os_scr[k, pl.ds(b, T, stride=nb), :] for k in range(n_blk)],
                                   axis=1).astype(o_ref.dtype)


def _s5_weights(lam_re, lam_im, log_dt, b_re, b_im, c_re, c_im):
    G, P = lam_re.shape
    gpb = LANES // S5_GROUP
    n_blk = G // gpb
    lr, li = lam_re.astype(F32), lam_im.astype(F32)
    dt = jnp.exp(log_dt.astype(F32))[:, None]
    mag = jnp.exp(lr * dt)
    ab_r, ab_i = mag * jnp.cos(li * dt), mag * jnp.sin(li * dt)
    den = lr * lr + li * li
    f_r = ((ab_r - 1.0) * lr + ab_i * li) / den
    f_i = (ab_i * lr - (ab_r - 1.0) * li) / den
    br, bi = b_re.astype(F32), b_im.astype(F32)
    bb_r = f_r[..., None] * br - f_i[..., None] * bi
    bb_i = f_r[..., None] * bi + f_i[..., None] * br
    eye = jnp.eye(gpb, dtype=F32)
    bb = jnp.stack([bb_r, bb_i]).reshape(2, n_blk, gpb, P, S5_GROUP)
    bb = jnp.transpose(bb, (1, 2, 4, 0, 3))
    bmat = bb[:, :, :, :, None, :] * eye[None, :, None, None, :, None]
    bmat = bmat.reshape(n_blk, LANES, 2 * gpb * P).astype(BF16)
    cc = jnp.stack([c_re.astype(F32), -c_im.astype(F32)]).reshape(2, n_blk, gpb, S5_GROUP, P)
    cc = jnp.transpose(cc, (1, 0, 2, 4, 3))
    cmat = cc[:, :, :, :, None, :] * eye[None, None, :, None, :, None]
    cmat = cmat.reshape(n_blk, 2 * gpb * P, LANES).astype(BF16)
    ar = jnp.broadcast_to(ab_r.reshape(n_blk, 1, gpb * P), (n_blk, S5_BATCH, gpb * P))
    ai = jnp.broadcast_to(ab_i.reshape(n_blk, 1, gpb * P), (n_blk, S5_BATCH, gpb * P))
    return bmat, cmat, ar, ai


def _s5_branch(proj, lam_re, lam_im, log_dt, b_re, b_im, c_re, c_im, d_skip, glu_w, glu_b):
    B, L, N = proj.shape
    W = N // 4
    nb = S5_BATCH
    T = min(S5_TIME, L)
    n_blk = W // LANES
    bmat, cmat, ar, ai = _s5_weights(lam_re, lam_im, log_dt, b_re, b_im, c_re, c_im)
    S2 = bmat.shape[2]
    const3 = lambda g, c: (0, 0, 0)
    const2 = lambda g, c: (0, 0)
    return pl.pallas_call(
        _s5_kernel,
        grid=(B // nb, L // T),
        in_specs=[
            pl.BlockSpec((nb, T, W), lambda g, c: (g, c, 0)),
            pl.BlockSpec((nb, T, W), lambda g, c: (g, c, 1)),
            pl.BlockSpec(bmat.shape, const3),
            pl.BlockSpec(cmat.shape, const3),
            pl.BlockSpec(ar.shape, const3),
            pl.BlockSpec(ai.shape, const3),
            pl.BlockSpec((1, W), const2),
            pl.BlockSpec((W, W), const2),
            pl.BlockSpec((1, W), const2),
        ],
        out_specs=pl.BlockSpec((nb, T, W), lambda g, c: (g, c, 0)),
        out_shape=jax.ShapeDtypeStruct((B, L, W), BF16),
        scratch_shapes=[
            pltpu.VMEM((n_blk, nb * T, LANES), F32),
            pltpu.VMEM((n_blk, nb * T, LANES), F32),
            pltpu.VMEM((n_blk, nb * T, S2), F32),
            pltpu.VMEM((n_blk, nb * T, LANES), F32),
            pltpu.VMEM((n_blk, nb * T, LANES), F32),
            pltpu.VMEM((n_blk, 2, nb, S2 // 2), F32),
        ],
        compiler_params=_params("parallel", "arbitrary"),
        name="s5_scan",
    )(proj, proj, bmat, cmat, ar, ai, d_skip.reshape(1, W).astype(F32), glu_w.astype(BF16),
      glu_b.reshape(1, W).astype(F32))


def _log_sigmoid(x):
    return jnp.minimum(x, 0.0) - jnp.log1p(jnp.exp(-jnp.abs(x)))


def _split3(x):
    hi = x.astype(BF16)
    r1 = x - hi.astype(F32)
    mid = r1.astype(BF16)
    lo = (r1 - mid.astype(F32)).astype(BF16)
    return hi, mid, lo


def _mlstm_kernel(x_ref, z_ref, cw_ref, cb_ref, wq_ref, wk_ref, wv_ref, gw_ref, gwt_ref, gbc_ref, gbr_ref,
                  ng_ref, sk_ref, o_ref, xe_scr, qkv_scr, ks_scr, va_scr, ct_scr, m_scr):
    Tc, W = x_ref.shape[1], x_ref.shape[2]
    H = wq_ref.shape[0]
    dh = W // H
    pad = SUBLANES

    @pl.when(pl.program_id(1) == 0)
    def _():
        xe_scr[0:pad, :] = jnp.zeros((pad, W), F32)
        ct_scr[...] = jnp.zeros_like(ct_scr)
        m_scr[...] = jnp.zeros_like(m_scr)
        va_scr[:, :, dh:] = jnp.ones((H, Tc, dh), BF16)

    xb = x_ref[0]
    x = xb.astype(F32)
    xe_scr[pad:pad + Tc, :] = x
    xc = cb_ref[...] + sum(xe_scr[pl.ds(pad - (ML_CONV - 1) + j, Tc), :] * cw_ref[j:j + 1, :]
                           for j in range(ML_CONV))
    xc = _silu(xc)
    xe_scr[0:pad, :] = x[Tc - pad:Tc, :]
    xcb = xc.astype(BF16)

    for h in range(H):
        cols = slice(h * dh, (h + 1) * dh)
        q = jnp.dot(xcb[:, cols], wq_ref[h], preferred_element_type=F32)
        k = jnp.dot(xcb[:, cols], wk_ref[h], preferred_element_type=F32)
        v = jnp.dot(xb[:, cols], wv_ref[h], preferred_element_type=F32)
        qkv_scr[:, h * dh:(h + 1) * dh] = q.astype(BF16)
        qkv_scr[:, W + h * dh:W + (h + 1) * dh] = k.astype(BF16)
        qkv_scr[:, 2 * W + h * dh:2 * W + (h + 1) * dh] = v.astype(BF16)
        va_scr[h, :, :dh] = v.astype(BF16)
        ks_scr[:, cols] = (k * (dh ** -0.5)).astype(BF16)

    qkv = qkv_scr[...]
    g_col = jnp.dot(qkv, gw_ref[...], preferred_element_type=F32) + gbc_ref[...]
    g_row = lax.dot_general(gwt_ref[...], qkv, (((1,), (1,)), ((), ())),
                            preferred_element_type=F32) + gbr_ref[...]
    ti = lax.broadcasted_iota(jnp.int32, (Tc, Tc), 0)
    si = lax.broadcasted_iota(jnp.int32, (Tc, Tc), 1)
    causal = si <= ti
    lower = jnp.where(causal, 1.0, 0.0).astype(BF16)
    upper = jnp.where(ti <= si, 1.0, 0.0).astype(BF16)
    b_col = sum(jnp.dot(lower, p, preferred_element_type=F32) for p in _split3(_log_sigmoid(g_col)))
    b_row = sum(jnp.dot(p, upper, preferred_element_type=F32) for p in _split3(_log_sigmoid(g_row)))

    heads = range(H)
    q_h = [qkv_scr[:, h * dh:(h + 1) * dh] for h in heads]
    m_prev = [m_scr[h][:, 0:1] for h in heads]
    bc = [b_col[:, H + h:H + h + 1] for h in heads]
    b_last = [bc[h][Tc - 1:Tc, :] for h in heads]

    s_qk = [lax.dot_general(q_h[h], ks_scr[:, h * dh:(h + 1) * dh], (((1,), (1,)), ((), ())),
                            preferred_element_type=F32) for h in heads]
    q_ct = [jnp.dot(q_h[h], ct_scr[h].astype(BF16), preferred_element_type=F32) for h in heads]

    dmat, sc, floor = [], [], []
    for h in heads:
        log_d = jnp.where(causal, bc[h] - b_row[H + h:H + h + 1, :] + g_row[h:h + 1, :], NEG_BIG)
        inter = bc[h] + m_prev[h]
        m_t = jnp.maximum(inter, jnp.max(log_d, axis=-1, keepdims=True))
        dmat.append(jnp.exp(log_d - m_t))
        sc.append(jnp.exp(inter - m_t))
        floor.append(jnp.exp(-m_t))

    pv = [jnp.dot((s_qk[h] * dmat[h]).astype(BF16), va_scr[h], preferred_element_type=F32) for h in heads]

    for h in heads:
        w_log = b_last[h] - bc[h] + g_col[:, h:h + 1]
        m_new = jnp.maximum(b_last[h] + m_prev[h], jnp.max(w_log, axis=0, keepdims=True))
        kw = ks_scr[:, h * dh:(h + 1) * dh].astype(F32) * jnp.exp(w_log - m_new)
        decay = jnp.exp(b_last[h] + m_prev[h] - m_new)
        ct_scr[h] = decay * ct_scr[h] + lax.dot_general(kw.astype(BF16), va_scr[h], (((0,), (0,)), ((), ())),
                                                        preferred_element_type=F32)
        m_scr[h] = jnp.broadcast_to(m_new, (1, LANES))

    for h in heads:
        cols = slice(h * dh, (h + 1) * dh)
        tot = sc[h] * q_ct[h] + pv[h]
        hh = tot[:, :dh] / jnp.maximum(jnp.abs(tot[:, dh:]), floor[h])
        mu = jnp.mean(hh, axis=-1, keepdims=True)
        cen = hh - mu
        var = jnp.mean(cen * cen, axis=-1, keepdims=True)
        out = cen * lax.rsqrt(var + HEAD_NORM_EPS) * ng_ref[:, cols] + sk_ref[:, cols] * xc[:, cols]
        o_ref[0, :, cols] = (out * _silu(z_ref[0, :, cols].astype(F32))).astype(o_ref.dtype)


def _mlstm_branch(proj, conv_w, conv_b, wq, wk, wv, gate_w, gate_b, norm_g, skip):
    B, L, N = proj.shape
    W = N // 4
    H = wq.shape[0]
    dh = W // H
    Tc = min(ML_CHUNK, L)
    gw = jnp.zeros((3 * W, LANES), F32).at[:, :2 * H].set(gate_w.astype(F32)).astype(BF16)
    gwt = jnp.transpose(gate_w.astype(F32)).astype(BF16)
    gbc = jnp.zeros((1, LANES), F32).at[0, :2 * H].set(gate_b.astype(F32))
    gbr = gate_b.astype(F32).reshape(2 * H, 1)
    c2 = lambda b, c: (0, 0)
    c3 = lambda b, c: (0, 0, 0)
    return pl.pallas_call(
        _mlstm_kernel,
        grid=(B, L // Tc),
        in_specs=[
            pl.BlockSpec((1, Tc, W), lambda b, c: (b, c, 2)),
            pl.BlockSpec((1, Tc, W), lambda b, c: (b, c, 3)),
            pl.BlockSpec((ML_CONV, W), c2),
            pl.BlockSpec((1, W), c2),
            pl.BlockSpec((H, dh, dh), c3),
            pl.BlockSpec((H, dh, dh), c3),
            pl.BlockSpec((H, dh, dh), c3),
            pl.BlockSpec((3 * W, LANES), c2),
            pl.BlockSpec((2 * H, 3 * W), c2),
            pl.BlockSpec((1, LANES), c2),
            pl.BlockSpec((2 * H, 1), c2),
            pl.BlockSpec((1, W), c2),
            pl.BlockSpec((1, W), c2),
        ],
        out_specs=pl.BlockSpec((1, Tc, W), lambda b, c: (b, c, 0)),
        out_shape=jax.ShapeDtypeStruct((B, L, W), BF16),
        scratch_shapes=[
            pltpu.VMEM((SUBLANES + Tc, W), F32),
            pltpu.VMEM((Tc, 3 * W), BF16),
            pltpu.VMEM((Tc, W), BF16),
            pltpu.VMEM((H, Tc, 2 * dh), BF16),
            pltpu.VMEM((H, dh, 2 * dh), F32),
            pltpu.VMEM((H, 1, LANES), F32),
        ],
        compiler_params=_params("parallel", "arbitrary"),
        name="mlstm",
    )(proj, proj, conv_w.astype(F32), conv_b.reshape(1, W).astype(F32), wq.astype(BF16), wk.astype(BF16),
      wv.astype(BF16), gw, gwt, gbc, gbr, norm_g.reshape(1, W).astype(F32), skip.reshape(1, W).astype(F32))


def _attn_kernel(q_ref, k_ref, vt_ref, z_ref, bias_ref, lam_ref, ng_ref, o_ref, m_scr, acc_scr, *,
                 mode, out_scale, blk, bias_index):
    L = q_ref.shape[2]
    n = L // blk
    lane = lax.broadcasted_iota(jnp.int32, (1, LANES), 1)
    map0 = lane < ATTN_HEAD_DIM
    feat = lax.broadcasted_iota(jnp.int32, (LANES, 1), 0)
    ones_rows = jnp.ones((SUBLANES, blk), BF16)

    def stacked_q(qi):
        q = q_ref[0, 0, qi * blk:(qi + 1) * blk, :]
        zero = jnp.zeros_like(q)
        return jnp.concatenate([jnp.where(map0, q, zero), jnp.where(map0, zero, q)], axis=0)

    def scores(qi, j):
        s = lax.dot_general(k_ref[0, 0, j * blk:(j + 1) * blk, :], q2[qi], (((1,), (1,)), ((), ())),
                            preferred_element_type=F32)
        e = bias_index(qi - j)
        return s if e is None else s + bias_ref[e]

    def update(qi, j, s):
        m_cur = jnp.max(s, axis=0, keepdims=True)
        m_next = m_cur if j == 0 else jnp.maximum(m_scr[qi], m_cur)
        p = jnp.exp2(s - m_next).astype(BF16)
        pv = jnp.dot(jnp.concatenate([vt_ref[0, 0, :, j * blk:(j + 1) * blk], ones_rows], axis=0), p,
                     preferred_element_type=F32)
        if j == 0:
            acc_scr[qi] = pv
        else:
            acc_scr[qi] = jnp.exp2(m_scr[qi] - m_next) * acc_scr[qi] + pv
        m_scr[qi] = m_next

    def finish(qi):
        rows = slice(qi * blk, (qi + 1) * blk)
        acc = acc_scr[qi]
        o = acc[:LANES] / acc[LANES:LANES + 1]
        o0, o1 = o[:, :blk], o[:, blk:]
        if mode == "dilated":
            out = jnp.transpose(jnp.where(feat < ATTN_HEAD_DIM, o0, o1))
        else:
            out = jnp.transpose(o0 - lam_ref[:, 0:1] * o1)
            ms = jnp.mean(out * out, axis=-1, keepdims=True)
            out = out * lax.rsqrt(ms + HEAD_NORM_EPS) * ng_ref[...] * out_scale
        o_ref[0, rows, :] = (out * _silu(z_ref[0, 0, rows, :].astype(F32))).astype(o_ref.dtype)

    steps = []
    for a in range(n // 2):
        b = n - 1 - a
        ja = 0
        for jb in range(b + 1):
            steps.append((b, jb))
            if ja <= a and (jb + 1) * (a + 1) >= (ja + 1) * (b + 1):
                steps.append((a, ja))
                ja += 1
    q2 = {qi: stacked_q(qi) for qi in range(n)}
    pending = {}
    for t in range(min(ATTN_LOOKAHEAD, len(steps))):
        pending[t] = scores(*steps[t])
    for t, (qi, j) in enumerate(steps):
        if t + ATTN_LOOKAHEAD < len(steps):
            pending[t + ATTN_LOOKAHEAD] = scores(*steps[t + ATTN_LOOKAHEAD])
        update(qi, j, pending.pop(t))
        if j == qi:
            finish(qi)


def _dilated_multiplicity(delta):
    mult = np.zeros(delta.shape, np.int64)
    for window, dil in DIL_PAIRS:
        mult += (delta >= 0) & (delta % dil == 0) & (delta <= window)
    return mult


def _bias_tables(L, blk, mode):
    n = L // blk
    key = np.arange(blk)[:, None]
    qry = np.arange(blk)[None, :]
    tables, index = [], {}
    for d in range(n):
        delta = d * blk + qry - key
        mult = _dilated_multiplicity(delta) if mode == "dilated" else (delta >= 0).astype(np.int64)
        t = np.where(mult > 0, np.log2(np.maximum(mult, 1)), NEG_BIG).astype(np.float32)
        if not t.any():
            index[d] = None
            continue
        for e, other in enumerate(tables):
            if np.array_equal(t, other):
                index[d] = e
                break
        else:
            index[d] = len(tables)
            tables.append(t)
    stacked = np.stack(tables)
    return jnp.asarray(np.concatenate([stacked, stacked], axis=2)), index


def _attention(qkz, vt, q0, k0, v0, z0, n_blocks, blk, lam, norm_g, mode, out_scale):
    B, _, L, _ = qkz.shape
    bias, index = _bias_tables(L, blk, mode)
    n_q = L // blk
    assert n_q % 2 == 0
    kern = functools.partial(_attn_kernel, mode=mode, out_scale=out_scale, blk=blk, bias_index=index.get)
    c2 = lambda b, n: (0, 0)
    return pl.pallas_call(
        kern,
        grid=(B, n_blocks),
        in_specs=[
            pl.BlockSpec((1, 1, L, LANES), lambda b, n: (b, q0 + n, 0, 0)),
            pl.BlockSpec((1, 1, L, LANES), lambda b, n: (b, k0 + n, 0, 0)),
            pl.BlockSpec((1, 1, LANES, L), lambda b, n: (b, v0 + n, 0, 0)),
            pl.BlockSpec((1, 1, L, LANES), lambda b, n: (b, z0 + n, 0, 0)),
            pl.BlockSpec(bias.shape, lambda b, n: (0, 0, 0)),
            pl.BlockSpec((1, LANES), c2),
            pl.BlockSpec((1, LANES), c2),
        ],
        out_specs=pl.BlockSpec((1, L, LANES), lambda b, n: (b, 0, n)),
        out_shape=jax.ShapeDtypeStruct((B, L, n_blocks * LANES), BF16),
        scratch_shapes=[
            pltpu.VMEM((n_q, 1, 2 * blk), F32),
            pltpu.VMEM((n_q, LANES + SUBLANES, 2 * blk), F32),
        ],
        compiler_params=_params("parallel", "parallel"),
        name="attn_" + mode,
    )(qkz, qkz, vt, qkz, bias, lam, norm_g)


def _rope_tables(L):
    dh = ATTN_HEAD_DIM
    inv = ROPE_THETA ** (-jnp.arange(0, dh, 2, dtype=F32) / dh)
    ang = jnp.arange(L, dtype=F32)[:, None] * inv[None, :]
    cos, sin = jnp.cos(ang), jnp.sin(ang)
    reps = LANES // dh
    cos_t = jnp.tile(jnp.concatenate([cos, cos], axis=-1), (1, reps))
    sin_t = jnp.tile(jnp.concatenate([-sin, sin], axis=-1), (1, reps))
    return cos_t, sin_t


def _even_layer(h, pre_g, w_in, s5_params, ml_params):
    proj = _norm_proj(h, pre_g, w_in.astype(BF16))
    return _s5_branch(proj, *s5_params), _mlstm_branch(proj, *ml_params)


def _odd_layer(prev, h, pre_g, w_in, lq1, lk1, lq2, lk2, diff_norm, layer_idx):
    B, L, D = h.shape
    nblk = PROJ_COLS // LANES
    cos_t, sin_t = _rope_tables(L)
    h, qkz, vt = _out_norm_proj_rope(*prev, h, pre_g, w_in.astype(BF16), cos_t, sin_t)
    blk = min(ATTN_BLOCK, L)
    lam_init = 0.8 - 0.6 * math.exp(-0.3 * layer_idx)
    lam = (jnp.exp(jnp.sum(lq1.astype(F32) * lk1.astype(F32)))
           - jnp.exp(jnp.sum(lq2.astype(F32) * lk2.astype(F32))) + lam_init)
    lam_row = jnp.full((1, LANES), lam, F32)
    ones_row = jnp.ones((1, LANES), F32)
    base = [b for _, b in ODD_CHUNKS]
    c_out = _attention(qkz, vt, base[0], base[1], base[2], base[3], nblk, blk, ones_row, ones_row, "dilated", 1.0)
    d_out = _attention(qkz, vt, base[4], base[5], base[6], base[7], nblk, blk, lam_row,
                       diff_norm.reshape(1, LANES).astype(F32), "diff", 1.0 - lam_init)
    return h, c_out, d_out


def kernel(x, pre_norm, post_norm, w_in_ab, s5_lambda_re, s5_lambda_im, s5_log_dt, s5_b_re, s5_b_im,
           s5_c_re, s5_c_im, s5_d, s5_glu_w, s5_glu_b, ml_conv_w, ml_conv_b, ml_wq, ml_wk, ml_wv,
           ml_gate_w, ml_gate_b, ml_norm, ml_skip, w_out_ab, w_in_cd, diff_lq1, diff_lk1, diff_lq2,
           diff_lk2, diff_norm, w_out_cd):
    depth = pre_norm.shape[0]
    h = x
    prev = None
    for l in range(depth):
        i = l // 2
        if l % 2 == 0:
            if prev is not None:
                h = _out_proj(*prev, h)
            s5_params = (s5_lambda_re[i], s5_lambda_im[i], s5_log_dt[i], s5_b_re[i], s5_b_im[i],
                         s5_c_re[i], s5_c_im[i], s5_d[i], s5_glu_w[i], s5_glu_b[i])
            ml_params = (ml_conv_w[i], ml_conv_b[i], ml_wq[i], ml_wk[i], ml_wv[i], ml_gate_w[i],
                         ml_gate_b[i], ml_norm[i], ml_skip[i])
            a, b = _even_layer(h, pre_norm[l], w_in_ab[i], s5_params, ml_params)
            prev = (a, b, w_out_ab[i], post_norm[l])
        else:
            h, c, d = _odd_layer(prev, h, pre_norm[l], w_in_cd[i], diff_lq1[i], diff_lk1[i], diff_lq2[i],
                                 diff_lk2[i], diff_norm[i], l)
            prev = (c, d, w_out_cd[i], post_norm[l])
    return _out_proj(*prev, h)
```

```python
import functools
import math

import numpy as np
import jax
import jax.numpy as jnp
from jax import lax
from jax.experimental import pallas as pl
from jax.experimental.pallas import tpu as pltpu

F32 = jnp.float32
BF16 = jnp.bfloat16

S5_GROUP = 16
S5_STATE = 64
ML_HEADS = 4
ML_CONV = 4
DIL_PAIRS = ((128, 1), (512, 4), (2048, 16))
ROPE_THETA = 10000.0
NORM_EPS = 1e-6
HEAD_NORM_EPS = 1e-5
ATTN_HEAD_DIM = 64
QUERY_SCALE = ATTN_HEAD_DIM ** -0.5 * math.log2(math.e)

LANES = 128
SUBLANES = 8
VMEM_LIMIT_BYTES = 56 * 1024 * 1024

NEG_BIG = -1e30

PROJ_ROWS = 512
OUT_PROJ_ROWS = 1024
PROJ_COLS = 512
S5_BATCH = SUBLANES
S5_TIME = 128
S5_STRIDE = 2
S5_PIECE = 32
ML_CHUNK = 256
ATTN_BLOCK = 256
ATTN_LOOKAHEAD = 3


def _params(*semantics):
    return pltpu.CompilerParams(dimension_semantics=semantics, vmem_limit_bytes=VMEM_LIMIT_BYTES)


def _silu(x):
    return x * jax.nn.sigmoid(x)


def _norm_proj_kernel(x_ref, g_ref, w_ref, o_ref, z_scr):
    x = x_ref[0]
    ms = jnp.mean(x * x, axis=-1, keepdims=True)
    z_scr[...] = (x * lax.rsqrt(ms + NORM_EPS) * g_ref[...]).astype(BF16)
    n_out = w_ref.shape[1]
    for c in range(n_out // PROJ_COLS):
        cols = slice(c * PROJ_COLS, (c + 1) * PROJ_COLS)
        o_ref[0, :, cols] = jnp.dot(z_scr[...], w_ref[:, cols], preferred_element_type=F32).astype(o_ref.dtype)


def _norm_proj(x, g, w):
    B, L, D = x.shape
    N = w.shape[1]
    tm = min(PROJ_ROWS, L)
    return pl.pallas_call(
        _norm_proj_kernel,
        grid=(B, L // tm),
        in_specs=[
            pl.BlockSpec((1, tm, D), lambda b, i: (b, i, 0)),
            pl.BlockSpec((1, D), lambda b, i: (0, 0)),
            pl.BlockSpec((D, N), lambda b, i: (0, 0)),
        ],
        out_specs=pl.BlockSpec((1, tm, N), lambda b, i: (b, i, 0)),
        out_shape=jax.ShapeDtypeStruct((B, L, N), BF16),
        scratch_shapes=[pltpu.VMEM((tm, D), BF16)],
        compiler_params=_params("parallel", "parallel"),
        name="norm_proj_even",
    )(x, g.reshape(1, D), w)


def _swap_halves(x, first_half):
    return jnp.where(first_half, pltpu.roll(x, LANES - ATTN_HEAD_DIM // 2, 1), pltpu.roll(x, ATTN_HEAD_DIM // 2, 1))


ODD_CHUNKS = (("q", 0), ("k", 4), ("v", 0), ("z", 8), ("q", 12), ("k", 16), ("v", 4), ("z", 20))


def _out_norm_proj_rope_kernel(a_ref, b_ref, wa_ref, wb_ref, gpost_ref, h_ref, g_ref, w_ref, cos_ref, sin_ref,
                               hout_ref, o_ref, vt_ref, z_scr):
    y = jnp.dot(a_ref[0], wa_ref[...], preferred_element_type=F32)
    y = y + jnp.dot(b_ref[0], wb_ref[...], preferred_element_type=F32)
    ms = jnp.mean(y * y, axis=-1, keepdims=True)
    x = h_ref[0] + y * lax.rsqrt(ms + NORM_EPS) * gpost_ref[...]
    hout_ref[0] = x
    ms = jnp.mean(x * x, axis=-1, keepdims=True)
    z_scr[...] = (x * lax.rsqrt(ms + NORM_EPS) * g_ref[...]).astype(BF16)
    slabs = PROJ_COLS // LANES
    lane = lax.broadcasted_iota(jnp.int32, (1, LANES), 1)
    first_half = (lane % ATTN_HEAD_DIM) < (ATTN_HEAD_DIM // 2)
    for c, (kind, base) in enumerate(ODD_CHUNKS):
        y = jnp.dot(z_scr[...], w_ref[:, c * PROJ_COLS:(c + 1) * PROJ_COLS], preferred_element_type=F32)
        for s in range(slabs):
            t = y[:, s * LANES:(s + 1) * LANES]
            if kind in ("q", "k"):
                t = t * cos_ref[...] + _swap_halves(t, first_half) * sin_ref[...]
            if kind == "q":
                t = t * QUERY_SCALE
            if kind == "v":
                vt_ref[0, base + s] = jnp.transpose(t).astype(vt_ref.dtype)
            else:
                o_ref[0, base + s] = t.astype(o_ref.dtype)


def _out_norm_proj_rope(a, b, w_out, g_post, h, g, w, cos_t, sin_t):
    B, L, D = h.shape
    Wb = a.shape[2]
    N = w.shape[1]
    assert N == len(ODD_CHUNKS) * PROJ_COLS
    tm = min(PROJ_ROWS, L)
    n_v = sum(kind == "v" for kind, _ in ODD_CHUNKS) * (PROJ_COLS // LANES)
    n_o = N // LANES - n_v
    row = lambda b_, i: (b_, i, 0)
    const = lambda b_, i: (0, 0)
    return pl.pallas_call(
        _out_norm_proj_rope_kernel,
        grid=(B, L // tm),
        in_specs=[
            pl.BlockSpec((1, tm, Wb), row),
            pl.BlockSpec((1, tm, Wb), row),
            pl.BlockSpec((Wb, D), const),
            pl.BlockSpec((Wb, D), const),
            pl.BlockSpec((1, D), const),
            pl.BlockSpec((1, tm, D), row),
            pl.BlockSpec((1, D), const),
            pl.BlockSpec((D, N), const),
            pl.BlockSpec((tm, LANES), lambda b_, i: (i, 0)),
            pl.BlockSpec((tm, LANES), lambda b_, i: (i, 0)),
        ],
        out_specs=[
            pl.BlockSpec((1, tm, D), row),
            pl.BlockSpec((1, n_o, tm, LANES), lambda b_, i: (b_, 0, i, 0)),
            pl.BlockSpec((1, n_v, LANES, tm), lambda b_, i: (b_, 0, 0, i)),
        ],
        out_shape=[
            jax.ShapeDtypeStruct((B, L, D), F32),
            jax.ShapeDtypeStruct((B, n_o, L, LANES), BF16),
            jax.ShapeDtypeStruct((B, n_v, LANES, L), BF16),
        ],
        scratch_shapes=[pltpu.VMEM((tm, D), BF16)],
        compiler_params=_params("parallel", "parallel"),
        name="out_norm_proj_odd",
    )(a, b, w_out[:Wb].astype(BF16), w_out[Wb:].astype(BF16), g_post.reshape(1, D), h, g.reshape(1, D), w,
      cos_t, sin_t)


def _out_proj_kernel(a_ref, b_ref, wa_ref, wb_ref, g_ref, h_ref, o_ref):
    y = jnp.dot(a_ref[0], wa_ref[...], preferred_element_type=F32)
    y = y + jnp.dot(b_ref[0], wb_ref[...], preferred_element_type=F32)
    ms = jnp.mean(y * y, axis=-1, keepdims=True)
    o_ref[0] = h_ref[0] + y * lax.rsqrt(ms + NORM_EPS) * g_ref[...]


def _out_proj(a, b, w_out, g, h):
    B, L, W = a.shape
    D = w_out.shape[1]
    tm = min(OUT_PROJ_ROWS, L)
    return pl.pallas_call(
        _out_proj_kernel,
        grid=(B, L // tm),
        in_specs=[
            pl.BlockSpec((1, tm, W), lambda b_, i: (b_, i, 0)),
            pl.BlockSpec((1, tm, W), lambda b_, i: (b_, i, 0)),
            pl.BlockSpec((W, D), lambda b_, i: (0, 0)),
            pl.BlockSpec((W, D), lambda b_, i: (0, 0)),
            pl.BlockSpec((1, D), lambda b_, i: (0, 0)),
            pl.BlockSpec((1, tm, D), lambda b_, i: (b_, i, 0)),
        ],
        out_specs=pl.BlockSpec((1, tm, D), lambda b_, i: (b_, i, 0)),
        out_shape=jax.ShapeDtypeStruct((B, L, D), F32),
        compiler_params=_params("parallel", "parallel"),
        name="out_proj",
    )(a, b, w_out[:W].astype(BF16), w_out[W:].astype(BF16), g.reshape(1, D), h)


def _s5_kernel(u_ref, z_ref, bin_ref, wx_ref, wu_ref, ar_ref, ai_ref, d_ref, gw_ref, gb_ref, o_ref,
               tmp_scr, us_scr, zs_scr, xs_scr, ys_scr, os_scr, st_scr):
    nb, T, W = u_ref.shape
    n_blk = W // LANES
    r = S5_STRIDE
    n_grp = T // r
    half = xs_scr.shape[2] // 2
    n_t = n_grp // S5_PIECE
    piece_rows = S5_PIECE * nb

    @pl.when(pl.program_id(1) == 0)
    def _():
        st_scr[...] = jnp.zeros_like(st_scr)

    for b in range(nb):
        for src, dst in ((u_ref, us_scr), (z_ref, zs_scr)):
            xb = src[b].astype(F32)
            for k in range(n_blk):
                tmp_scr[k] = xb[:, k * LANES:(k + 1) * LANES]
            for k in range(n_blk):
                for j in range(r):
                    dst[k, j, pl.ds(b, n_grp, stride=nb), :] = tmp_scr[k, pl.ds(j, n_grp, stride=r), :]

    def rows_of(i):
        return slice(i * piece_rows, (i + 1) * piece_rows)

    def group_inputs(i, k):
        return jnp.concatenate([us_scr[k, j, rows_of(i), :] for j in range(r)], axis=1).astype(BF16)

    def input_matmul(i, k):
        xs_scr[k, rows_of(i), :] = jnp.dot(group_inputs(i, k), bin_ref[k], preferred_element_type=F32)

    def scan(i, k):
        ar, ai = ar_ref[k], ai_ref[k]
        xr, xi = st_scr[k, 0], st_scr[k, 1]
        for t in range(S5_PIECE):
            rs = slice(i * piece_rows + t * nb, i * piece_rows + (t + 1) * nb)
            cr, ci = xs_scr[k, rs, 0:half], xs_scr[k, rs, half:2 * half]
            xs_scr[k, rs, 0:half] = xr
            xs_scr[k, rs, half:2 * half] = xi
            xr, xi = ar * xr - ai * xi + cr, ar * xi + ai * xr + ci
        st_scr[k, 0] = xr
        st_scr[k, 1] = xi

    def output_matmul(i, k):
        y = jnp.dot(xs_scr[k, rows_of(i), :].astype(BF16), wx_ref[k], preferred_element_type=F32)
        y = y + jnp.dot(group_inputs(i, k), wu_ref[k], preferred_element_type=F32)
        for j in range(r):
            ys_scr[k, j, rows_of(i), :] = y[:, j * LANES:(j + 1) * LANES]

    def glu(i):
        def tokens(ref):
            return jnp.concatenate(
                [jnp.concatenate([ref[k, j, rows_of(i), :] for k in range(n_blk)], axis=1) for j in range(r)], axis=0)
        y = tokens(ys_scr) + d_ref[...] * tokens(us_scr)
        g = jax.nn.gelu(y)
        gate = jnp.dot(g.astype(BF16), gw_ref[...], preferred_element_type=F32) + gb_ref[...]
        out = g * jax.nn.sigmoid(gate) * _silu(tokens(zs_scr))
        for j in range(r):
            for k in range(n_blk):
                os_scr[k, j, rows_of(i), :] = out[j * piece_rows:(j + 1) * piece_rows, k * LANES:(k + 1) * LANES]

    def drain(p):
        i, k = pieces[p]
        output_matmul(i, k)
        if k == n_blk - 1:
            glu(i)

    pieces = [(i, k) for i in range(n_t) for k in range(n_blk)]
    input_matmul(*pieces[0])
    for p, (i, k) in enumerate(pieces):
        if p + 1 < len(pieces):
            input_matmul(*pieces[p + 1])
        if p >= 1:
            drain(p - 1)
        scan(i, k)
    drain(len(pieces) - 1)

    for b in range(nb):
        for k in range(n_blk):
            for j in range(r):
                tmp_scr[k, pl.ds(j, n_grp, stride=r), :] = os_scr[k, j, pl.ds(b, n_grp, stride=nb), :]
        o_ref[b] = jnp.concatenate([tmp_scr[k] for k in range(n_blk)], axis=1).astype(o_ref.dtype)


def _s5_weights(lam_re, lam_im, log_dt, b_re, b_im, c_re, c_im):
    G, P = lam_re.shape
    r = S5_STRIDE
    gpb = LANES // S5_GROUP
    n_blk = G // gpb
    lr, li = lam_re.astype(F32), lam_im.astype(F32)
    dt = jnp.exp(log_dt.astype(F32))[:, None]
    mag = jnp.exp(lr * dt)
    ab_r, ab_i = mag * jnp.cos(li * dt), mag * jnp.sin(li * dt)
    den = lr * lr + li * li
    f_r = ((ab_r - 1.0) * lr + ab_i * li) / den
    f_i = (ab_i * lr - (ab_r - 1.0) * li) / den
    br, bi = b_re.astype(F32), b_im.astype(F32)
    bb_r = f_r[..., None] * br - f_i[..., None] * bi
    bb_i = f_r[..., None] * bi + f_i[..., None] * br
    cr, ci = c_re.astype(F32), c_im.astype(F32)
    eye = jnp.eye(gpb, dtype=F32)

    pows = [(jnp.ones_like(ab_r), jnp.zeros_like(ab_i))]
    for _ in range(r):
        pr, pi = pows[-1]
        pows.append((pr * ab_r - pi * ab_i, pr * ab_i + pi * ab_r))

    def pack_in(xr, xi):
        x = jnp.stack([xr, xi]).reshape(2, n_blk, gpb, P, S5_GROUP)
        x = jnp.transpose(x, (1, 2, 4, 0, 3))
        x = x[:, :, :, :, None, :] * eye[None, :, None, None, :, None]
        return x.reshape(n_blk, LANES, 2 * gpb * P)

    def pack_out(er, ei):
        x = jnp.stack([er, ei]).reshape(2, n_blk, gpb, S5_GROUP, P)
        x = jnp.transpose(x, (1, 0, 2, 4, 3))
        x = x[:, :, :, :, None, :] * eye[None, None, :, None, :, None]
        return x.reshape(n_blk, 2 * gpb * P, LANES)

    def pack_direct(kq):
        x = jnp.transpose(kq.reshape(n_blk, gpb, S5_GROUP, S5_GROUP), (0, 1, 3, 2))
        x = x[:, :, :, None, :] * eye[None, :, None, :, None]
        return x.reshape(n_blk, LANES, LANES)

    w_in = []
    for j in range(r):
        pr, pi = pows[r - 1 - j]
        w_in.append(pack_in(pr[..., None] * bb_r - pi[..., None] * bb_i, pr[..., None] * bb_i + pi[..., None] * bb_r))
    w_in = jnp.concatenate(w_in, axis=1).astype(BF16)

    w_x = []
    for j in range(r):
        pr, pi = pows[j + 1]
        er = cr * pr[:, None, :] - ci * pi[:, None, :]
        ei = cr * pi[:, None, :] + ci * pr[:, None, :]
        w_x.append(pack_out(er, -ei))
    w_x = jnp.concatenate(w_x, axis=2).astype(BF16)

    direct = []
    for q in range(r):
        pr, pi = pows[q]
        er = cr * pr[:, None, :] - ci * pi[:, None, :]
        ei = cr * pi[:, None, :] + ci * pr[:, None, :]
        direct.append(pack_direct(jnp.einsum('gop,gpi->goi', er, bb_r) - jnp.einsum('gop,gpi->goi', ei, bb_i)))
    zero = jnp.zeros_like(direct[0])
    w_u = jnp.concatenate(
        [jnp.concatenate([direct[j - i] if i <= j else zero for j in range(r)], axis=2) for i in range(r)],
        axis=1).astype(BF16)

    ar_r, ai_r = pows[r]
    ar = jnp.broadcast_to(ar_r.reshape(n_blk, 1, gpb * P), (n_blk, S5_BATCH, gpb * P))
    ai = jnp.broadcast_to(ai_r.reshape(n_blk, 1, gpb * P), (n_blk, S5_BATCH, gpb * P))
    return w_in, w_x, w_u, ar, ai


def _s5_branch(proj, lam_re, lam_im, log_dt, b_re, b_im, c_re, c_im, d_skip, glu_w, glu_b):
    B, L, N = proj.shape
    W = N // 4
    nb = S5_BATCH
    T = min(S5_TIME, L)
    r = S5_STRIDE
    n_blk = W // LANES
    w_in, w_x, w_u, ar, ai = _s5_weights(lam_re, lam_im, log_dt, b_re, b_im, c_re, c_im)
    S2 = w_in.shape[2]
    rows = nb * T // r
    const3 = lambda g, c: (0, 0, 0)
    const2 = lambda g, c: (0, 0)
    return pl.pallas_call(
        _s5_kernel,
        grid=(B // nb, L // T),
        in_specs=[
            pl.BlockSpec((nb, T, W), lambda g, c: (g, c, 0)),
            pl.BlockSpec((nb, T, W), lambda g, c: (g, c, 1)),
            pl.BlockSpec(w_in.shape, const3),
            pl.BlockSpec(w_x.shape, const3),
            pl.BlockSpec(w_u.shape, const3),
            pl.BlockSpec(ar.shape, const3),
            pl.BlockSpec(ai.shape, const3),
            pl.BlockSpec((1, W), const2),
            pl.BlockSpec((W, W), const2),
            pl.BlockSpec((1, W), const2),
        ],
        out_specs=pl.BlockSpec((nb, T, W), lambda g, c: (g, c, 0)),
        out_shape=jax.ShapeDtypeStruct((B, L, W), BF16),
        scratch_shapes=[
            pltpu.VMEM((n_blk, T, LANES), F32),
            pltpu.VMEM((n_blk, r, rows, LANES), F32),
            pltpu.VMEM((n_blk, r, rows, LANES), F32),
            pltpu.VMEM((n_blk, rows, S2), F32),
            pltpu.VMEM((n_blk, r, rows, LANES), F32),
            pltpu.VMEM((n_blk, r, rows, LANES), F32),
            pltpu.VMEM((n_blk, 2, nb, S2 // 2), F32),
        ],
        compiler_params=_params("parallel", "arbitrary"),
        name="s5_scan",
    )(proj, proj, w_in, w_x, w_u, ar, ai, d_skip.reshape(1, W).astype(F32), glu_w.astype(BF16),
      glu_b.reshape(1, W).astype(F32))


def _log_sigmoid(x):
    return jnp.minimum(x, 0.0) - jnp.log1p(jnp.exp(-jnp.abs(x)))


def _split3(x):
    hi = x.astype(BF16)
    r1 = x - hi.astype(F32)
    mid = r1.astype(BF16)
    lo = (r1 - mid.astype(F32)).astype(BF16)
    return hi, mid, lo


def _mlstm_kernel(x_ref, z_ref, cw_ref, cb_ref, wq_ref, wk_ref, wv_ref, gw_ref, gwt_ref, gbc_ref, gbr_ref,
                  ng_ref, sk_ref, o_ref, xe_scr, qkv_scr, ks_scr, va_scr, ct_scr, m_scr):
    Tc, W = x_ref.shape[1], x_ref.shape[2]
    H = wq_ref.shape[0]
    dh = W // H
    pad = SUBLANES

    @pl.when(pl.program_id(1) == 0)
    def _():
        xe_scr[0:pad, :] = jnp.zeros((pad, W), F32)
        ct_scr[...] = jnp.zeros_like(ct_scr)
        m_scr[...] = jnp.zeros_like(m_scr)
        va_scr[:, :, dh:] = jnp.ones((H, Tc, dh), BF16)

    xb = x_ref[0]
    x = xb.astype(F32)
    xe_scr[pad:pad + Tc, :] = x
    xc = cb_ref[...] + sum(xe_scr[pl.ds(pad - (ML_CONV - 1) + j, Tc), :] * cw_ref[j:j + 1, :]
                           for j in range(ML_CONV))
    xc = _silu(xc)
    xe_scr[0:pad, :] = x[Tc - pad:Tc, :]
    xcb = xc.astype(BF16)

    for h in range(H):
        cols = slice(h * dh, (h + 1) * dh)
        q = jnp.dot(xcb[:, cols], wq_ref[h], preferred_element_type=F32)
        k = jnp.dot(xcb[:, cols], wk_ref[h], preferred_element_type=F32)
        v = jnp.dot(xb[:, cols], wv_ref[h], preferred_element_type=F32)
        qkv_scr[:, h * dh:(h + 1) * dh] = q.astype(BF16)
        qkv_scr[:, W + h * dh:W + (h + 1) * dh] = k.astype(BF16)
        qkv_scr[:, 2 * W + h * dh:2 * W + (h + 1) * dh] = v.astype(BF16)
        va_scr[h, :, :dh] = v.astype(BF16)
        ks_scr[:, cols] = (k * (dh ** -0.5)).astype(BF16)

    qkv = qkv_scr[...]
    g_col = jnp.dot(qkv, gw_ref[...], preferred_element_type=F32) + gbc_ref[...]
    g_row = lax.dot_general(gwt_ref[...], qkv, (((1,), (1,)), ((), ())),
                            preferred_element_type=F32) + gbr_ref[...]
    ti = lax.broadcasted_iota(jnp.int32, (Tc, Tc), 0)
    si = lax.broadcasted_iota(jnp.int32, (Tc, Tc), 1)
    causal = si <= ti
    lower = jnp.where(causal, 1.0, 0.0).astype(BF16)
    upper = jnp.where(ti <= si, 1.0, 0.0).astype(BF16)
    b_col = sum(jnp.dot(lower, p, preferred_element_type=F32) for p in _split3(_log_sigmoid(g_col)))
    b_row = sum(jnp.dot(p, upper, preferred_element_type=F32) for p in _split3(_log_sigmoid(g_row)))

    heads = range(H)
    q_h = [qkv_scr[:, h * dh:(h + 1) * dh] for h in heads]
    m_prev = [m_scr[h][:, 0:1] for h in heads]
    bc = [b_col[:, H + h:H + h + 1] for h in heads]
    b_last = [bc[h][Tc - 1:Tc, :] for h in heads]

    s_qk = [lax.dot_general(q_h[h], ks_scr[:, h * dh:(h + 1) * dh], (((1,), (1,)), ((), ())),
                            preferred_element_type=F32) for h in heads]
    q_ct = [jnp.dot(q_h[h], ct_scr[h].astype(BF16), preferred_element_type=F32) for h in heads]

    dmat, sc, floor = [], [], []
    for h in heads:
        log_d = jnp.where(causal, bc[h] - b_row[H + h:H + h + 1, :] + g_row[h:h + 1, :], NEG_BIG)
        inter = bc[h] + m_prev[h]
        m_t = jnp.maximum(inter, jnp.max(log_d, axis=-1, keepdims=True))
        dmat.append(jnp.exp(log_d - m_t))
        sc.append(jnp.exp(inter - m_t))
        floor.append(jnp.exp(-m_t))

    pv = [jnp.dot((s_qk[h] * dmat[h]).astype(BF16), va_scr[h], preferred_element_type=F32) for h in heads]

    for h in heads:
        w_log = b_last[h] - bc[h] + g_col[:, h:h + 1]
        m_new = jnp.maximum(b_last[h] + m_prev[h], jnp.max(w_log, axis=0, keepdims=True))
        kw = ks_scr[:, h * dh:(h + 1) * dh].astype(F32) * jnp.exp(w_log - m_new)
        decay = jnp.exp(b_last[h] + m_prev[h] - m_new)
        ct_scr[h] = decay * ct_scr[h] + lax.dot_general(kw.astype(BF16), va_scr[h], (((0,), (0,)), ((), ())),
                                                        preferred_element_type=F32)
        m_scr[h] = jnp.broadcast_to(m_new, (1, LANES))

    for h in heads:
        cols = slice(h * dh, (h + 1) * dh)
        tot = sc[h] * q_ct[h] + pv[h]
        hh = tot[:, :dh] / jnp.maximum(jnp.abs(tot[:, dh:]), floor[h])
        mu = jnp.mean(hh, axis=-1, keepdims=True)
        cen = hh - mu
        var = jnp.mean(cen * cen, axis=-1, keepdims=True)
        out = cen * lax.rsqrt(var + HEAD_NORM_EPS) * ng_ref[:, cols] + sk_ref[:, cols] * xc[:, cols]
        o_ref[0, :, cols] = (out * _silu(z_ref[0, :, cols].astype(F32))).astype(o_ref.dtype)


def _mlstm_branch(proj, conv_w, conv_b, wq, wk, wv, gate_w, gate_b, norm_g, skip):
    B, L, N = proj.shape
    W = N // 4
    H = wq.shape[0]
    dh = W // H
    Tc = min(ML_CHUNK, L)
    gw = jnp.zeros((3 * W, LANES), F32).at[:, :2 * H].set(gate_w.astype(F32)).astype(BF16)
    gwt = jnp.transpose(gate_w.astype(F32)).astype(BF16)
    gbc = jnp.zeros((1, LANES), F32).at[0, :2 * H].set(gate_b.astype(F32))
    gbr = gate_b.astype(F32).reshape(2 * H, 1)
    c2 = lambda b, c: (0, 0)
    c3 = lambda b, c: (0, 0, 0)
    return pl.pallas_call(
        _mlstm_kernel,
        grid=(B, L // Tc),
        in_specs=[
            pl.BlockSpec((1, Tc, W), lambda b, c: (b, c, 2)),
            pl.BlockSpec((1, Tc, W), lambda b, c: (b, c, 3)),
            pl.BlockSpec((ML_CONV, W), c2),
            pl.BlockSpec((1, W), c2),
            pl.BlockSpec((H, dh, dh), c3),
            pl.BlockSpec((H, dh, dh), c3),
            pl.BlockSpec((H, dh, dh), c3),
            pl.BlockSpec((3 * W, LANES), c2),
            pl.BlockSpec((2 * H, 3 * W), c2),
            pl.BlockSpec((1, LANES), c2),
            pl.BlockSpec((2 * H, 1), c2),
            pl.BlockSpec((1, W), c2),
            pl.BlockSpec((1, W), c2),
        ],
        out_specs=pl.BlockSpec((1, Tc, W), lambda b, c: (b, c, 0)),
        out_shape=jax.ShapeDtypeStruct((B, L, W), BF16),
        scratch_shapes=[
            pltpu.VMEM((SUBLANES + Tc, W), F32),
            pltpu.VMEM((Tc, 3 * W), BF16),
            pltpu.VMEM((Tc, W), BF16),
            pltpu.VMEM((H, Tc, 2 * dh), BF16),
            pltpu.VMEM((H, dh, 2 * dh), F32),
            pltpu.VMEM((H, 1, LANES), F32),
        ],
        compiler_params=_params("parallel", "arbitrary"),
        name="mlstm",
    )(proj, proj, conv_w.astype(F32), conv_b.reshape(1, W).astype(F32), wq.astype(BF16), wk.astype(BF16),
      wv.astype(BF16), gw, gwt, gbc, gbr, norm_g.reshape(1, W).astype(F32), skip.reshape(1, W).astype(F32))


def _attn_kernel(q_ref, k_ref, vt_ref, z_ref, bias_ref, lam_ref, ng_ref, o_ref, m_scr, acc_scr, *,
                 mode, out_scale, blk, bias_index):
    L = q_ref.shape[2]
    n = L // blk
    lane = lax.broadcasted_iota(jnp.int32, (1, LANES), 1)
    map0 = lane < ATTN_HEAD_DIM
    feat = lax.broadcasted_iota(jnp.int32, (LANES, 1), 0)
    ones_rows = jnp.ones((SUBLANES, blk), BF16)

    def stacked_q(qi):
        q = q_ref[0, 0, qi * blk:(qi + 1) * blk, :]
        zero = jnp.zeros_like(q)
        return jnp.concatenate([jnp.where(map0, q, zero), jnp.where(map0, zero, q)], axis=0)

    def scores(qi, j):
        s = lax.dot_general(k_ref[0, 0, j * blk:(j + 1) * blk, :], q2[qi], (((1,), (1,)), ((), ())),
                            preferred_element_type=F32)
        e = bias_index(qi - j)
        return s if e is None else s + bias_ref[e]

    def update(qi, j, s):
        m_cur = jnp.max(s, axis=0, keepdims=True)
        m_next = m_cur if j == 0 else jnp.maximum(m_scr[qi], m_cur)
        p = jnp.exp2(s - m_next).astype(BF16)
        pv = jnp.dot(jnp.concatenate([vt_ref[0, 0, :, j * blk:(j + 1) * blk], ones_rows], axis=0), p,
                     preferred_element_type=F32)
        if j == 0:
            acc_scr[qi] = pv
        else:
            acc_scr[qi] = jnp.exp2(m_scr[qi] - m_next) * acc_scr[qi] + pv
        m_scr[qi] = m_next

    def finish(qi):
        rows = slice(qi * blk, (qi + 1) * blk)
        acc = acc_scr[qi]
        o = acc[:LANES] / acc[LANES:LANES + 1]
        o0, o1 = o[:, :blk], o[:, blk:]
        if mode == "dilated":
            out = jnp.transpose(jnp.where(feat < ATTN_HEAD_DIM, o0, o1))
        else:
            out = jnp.transpose(o0 - lam_ref[:, 0:1] * o1)
            ms = jnp.mean(out * out, axis=-1, keepdims=True)
            out = out * lax.rsqrt(ms + HEAD_NORM_EPS) * ng_ref[...] * out_scale
        o_ref[0, rows, :] = (out * _silu(z_ref[0, 0, rows, :].astype(F32))).astype(o_ref.dtype)

    steps = []
    for a in range(n // 2):
        b = n - 1 - a
        ja = 0
        for jb in range(b + 1):
            steps.append((b, jb))
            if ja <= a and (jb + 1) * (a + 1) >= (ja + 1) * (b + 1):
                steps.append((a, ja))
                ja += 1
    q2 = {qi: stacked_q(qi) for qi in range(n)}
    pending = {}
    for t in range(min(ATTN_LOOKAHEAD, len(steps))):
        pending[t] = scores(*steps[t])
    for t, (qi, j) in enumerate(steps):
        if t + ATTN_LOOKAHEAD < len(steps):
            pending[t + ATTN_LOOKAHEAD] = scores(*steps[t + ATTN_LOOKAHEAD])
        update(qi, j, pending.pop(t))
        if j == qi:
            finish(qi)


def _dilated_multiplicity(delta):
    mult = np.zeros(delta.shape, np.int64)
    for window, dil in DIL_PAIRS:
        mult += (delta >= 0) & (delta % dil == 0) & (delta <= window)
    return mult


def _bias_tables(L, blk, mode):
    n = L // blk
    key = np.arange(blk)[:, None]
    qry = np.arange(blk)[None, :]
    tables, index = [], {}
    for d in range(n):
        delta = d * blk + qry - key
        mult = _dilated_multiplicity(delta) if mode == "dilated" else (delta >= 0).astype(np.int64)
        t = np.where(mult > 0, np.log2(np.maximum(mult, 1)), NEG_BIG).astype(np.float32)
        if not t.any():
            index[d] = None
            continue
        for e, other in enumerate(tables):
            if np.array_equal(t, other):
                index[d] = e
                break
        else:
            index[d] = len(tables)
            tables.append(t)
    stacked = np.stack(tables)
    return jnp.asarray(np.concatenate([stacked, stacked], axis=2)), index


def _attention(qkz, vt, q0, k0, v0, z0, n_blocks, blk, lam, norm_g, mode, out_scale):
    B, _, L, _ = qkz.shape
    bias, index = _bias_tables(L, blk, mode)
    n_q = L // blk
    assert n_q % 2 == 0
    kern = functools.partial(_attn_kernel, mode=mode, out_scale=out_scale, blk=blk, bias_index=index.get)
    c2 = lambda b, n: (0, 0)
    return pl.pallas_call(
        kern,
        grid=(B, n_blocks),
        in_specs=[
            pl.BlockSpec((1, 1, L, LANES), lambda b, n: (b, q0 + n, 0, 0)),
            pl.BlockSpec((1, 1, L, LANES), lambda b, n: (b, k0 + n, 0, 0)),
            pl.BlockSpec((1, 1, LANES, L), lambda b, n: (b, v0 + n, 0, 0)),
            pl.BlockSpec((1, 1, L, LANES), lambda b, n: (b, z0 + n, 0, 0)),
            pl.BlockSpec(bias.shape, lambda b, n: (0, 0, 0)),
            pl.BlockSpec((1, LANES), c2),
            pl.BlockSpec((1, LANES), c2),
        ],
        out_specs=pl.BlockSpec((1, L, LANES), lambda b, n: (b, 0, n)),
        out_shape=jax.ShapeDtypeStruct((B, L, n_blocks * LANES), BF16),
        scratch_shapes=[
            pltpu.VMEM((n_q, 1, 2 * blk), F32),
            pltpu.VMEM((n_q, LANES + SUBLANES, 2 * blk), F32),
        ],
        compiler_params=_params("parallel", "parallel"),
        name="attn_" + mode,
    )(qkz, qkz, vt, qkz, bias, lam, norm_g)


def _rope_tables(L):
    dh = ATTN_HEAD_DIM
    inv = ROPE_THETA ** (-jnp.arange(0, dh, 2, dtype=F32) / dh)
    ang = jnp.arange(L, dtype=F32)[:, None] * inv[None, :]
    cos, sin = jnp.cos(ang), jnp.sin(ang)
    reps = LANES // dh
    cos_t = jnp.tile(jnp.concatenate([cos, cos], axis=-1), (1, reps))
    sin_t = jnp.tile(jnp.concatenate([-sin, sin], axis=-1), (1, reps))
    return cos_t, sin_t


def _even_layer(h, pre_g, w_in, s5_params, ml_params):
    proj = _norm_proj(h, pre_g, w_in.astype(BF16))
    return _s5_branch(proj, *s5_params), _mlstm_branch(proj, *ml_params)


def _odd_layer(prev, h, pre_g, w_in, lq1, lk1, lq2, lk2, diff_norm, layer_idx):
    B, L, D = h.shape
    nblk = PROJ_COLS // LANES
    cos_t, sin_t = _rope_tables(L)
    h, qkz, vt = _out_norm_proj_rope(*prev, h, pre_g, w_in.astype(BF16), cos_t, sin_t)
    blk = min(ATTN_BLOCK, L)
    lam_init = 0.8 - 0.6 * math.exp(-0.3 * layer_idx)
    lam = (jnp.exp(jnp.sum(lq1.astype(F32) * lk1.astype(F32)))
           - jnp.exp(jnp.sum(lq2.astype(F32) * lk2.astype(F32))) + lam_init)
    lam_row = jnp.full((1, LANES), lam, F32)
    ones_row = jnp.ones((1, LANES), F32)
    base = [b for _, b in ODD_CHUNKS]
    c_out = _attention(qkz, vt, base[0], base[1], base[2], base[3], nblk, blk, ones_row, ones_row, "dilated", 1.0)
    d_out = _attention(qkz, vt, base[4], base[5], base[6], base[7], nblk, blk, lam_row,
                       diff_norm.reshape(1, LANES).astype(F32), "diff", 1.0 - lam_init)
    return h, c_out, d_out


def kernel(x, pre_norm, post_norm, w_in_ab, s5_lambda_re, s5_lambda_im, s5_log_dt, s5_b_re, s5_b_im,
           s5_c_re, s5_c_im, s5_d, s5_glu_w, s5_glu_b, ml_conv_w, ml_conv_b, ml_wq, ml_wk, ml_wv,
           ml_gate_w, ml_gate_b, ml_norm, ml_skip, w_out_ab, w_in_cd, diff_lq1, diff_lk1, diff_lq2,
           diff_lk2, diff_norm, w_out_cd):
    depth = pre_norm.shape[0]
    h = x
    prev = None
    for l in range(depth):
        i = l // 2
        if l % 2 == 0:
            if prev is not None:
                h = _out_proj(*prev, h)
            s5_params = (s5_lambda_re[i], s5_lambda_im[i], s5_log_dt[i], s5_b_re[i], s5_b_im[i],
                         s5_c_re[i], s5_c_im[i], s5_d[i], s5_glu_w[i], s5_glu_b[i])
            ml_params = (ml_conv_w[i], ml_conv_b[i], ml_wq[i], ml_wk[i], ml_wv[i], ml_gate_w[i],
                         ml_gate_b[i], ml_norm[i], ml_skip[i])
            a, b = _even_layer(h, pre_norm[l], w_in_ab[i], s5_params, ml_params)
            prev = (a, b, w_out_ab[i], post_norm[l])
        else:
            h, c, d = _odd_layer(prev, h, pre_norm[l], w_in_cd[i], diff_lq1[i], diff_lk1[i], diff_lq2[i],
                                 diff_lk2[i], diff_norm[i], l)
            prev = (c, d, w_out_cd[i], post_norm[l])
    return _out_proj(*prev, h)
```

```python
import functools
import math

import numpy as np
import jax
import jax.numpy as jnp
from jax import lax
from jax.experimental import pallas as pl
from jax.experimental.pallas import tpu as pltpu

F32 = jnp.float32
BF16 = jnp.bfloat16

S5_GROUP = 16
S5_STATE = 64
ML_HEADS = 4
ML_CONV = 4
DIL_PAIRS = ((128, 1), (512, 4), (2048, 16))
ROPE_THETA = 10000.0
NORM_EPS = 1e-6
HEAD_NORM_EPS = 1e-5
ATTN_HEAD_DIM = 64
QUERY_SCALE = ATTN_HEAD_DIM ** -0.5 * math.log2(math.e)

LANES = 128
SUBLANES = 8
VMEM_LIMIT_BYTES = 56 * 1024 * 1024

NEG_BIG = -1e30

PROJ_ROWS = 512
OUT_PROJ_ROWS = 1024
PROJ_COLS = 512
S5_BATCH = SUBLANES
S5_TIME = 128
S5_STRIDE = 2
S5_PIECE = 32
ML_CHUNK = 256
ATTN_BLOCK = 256
ATTN_LOOKAHEAD = 3


def _params(*semantics):
    return pltpu.CompilerParams(dimension_semantics=semantics, vmem_limit_bytes=VMEM_LIMIT_BYTES)


def _silu(x):
    return x * jax.nn.sigmoid(x)


def _norm_proj_kernel(x_ref, g_ref, w_ref, o_ref, z_scr):
    x = x_ref[0]
    ms = jnp.mean(x * x, axis=-1, keepdims=True)
    z_scr[...] = (x * lax.rsqrt(ms + NORM_EPS) * g_ref[...]).astype(BF16)
    n_out = w_ref.shape[1]
    for c in range(n_out // PROJ_COLS):
        cols = slice(c * PROJ_COLS, (c + 1) * PROJ_COLS)
        o_ref[0, :, cols] = jnp.dot(z_scr[...], w_ref[:, cols], preferred_element_type=F32).astype(o_ref.dtype)


def _norm_proj(x, g, w):
    B, L, D = x.shape
    N = w.shape[1]
    tm = min(PROJ_ROWS, L)
    return pl.pallas_call(
        _norm_proj_kernel,
        grid=(B, L // tm),
        in_specs=[
            pl.BlockSpec((1, tm, D), lambda b, i: (b, i, 0)),
            pl.BlockSpec((1, D), lambda b, i: (0, 0)),
            pl.BlockSpec((D, N), lambda b, i: (0, 0)),
        ],
        out_specs=pl.BlockSpec((1, tm, N), lambda b, i: (b, i, 0)),
        out_shape=jax.ShapeDtypeStruct((B, L, N), BF16),
        scratch_shapes=[pltpu.VMEM((tm, D), BF16)],
        compiler_params=_params("parallel", "parallel"),
        name="norm_proj_even",
    )(x, g.reshape(1, D), w)


def _swap_halves(x, first_half):
    return jnp.where(first_half, pltpu.roll(x, LANES - ATTN_HEAD_DIM // 2, 1), pltpu.roll(x, ATTN_HEAD_DIM // 2, 1))


ODD_CHUNKS = (("q", 0), ("k", 4), ("v", 0), ("z", 8), ("q", 12), ("k", 16), ("v", 4), ("z", 20))


def _out_norm_proj_rope_kernel(a_ref, b_ref, wa_ref, wb_ref, gpost_ref, h_ref, g_ref, w_ref, cos_ref, sin_ref,
                               hout_ref, o_ref, vt_ref, z_scr):
    y = jnp.dot(a_ref[0], wa_ref[...], preferred_element_type=F32)
    y = y + jnp.dot(b_ref[0], wb_ref[...], preferred_element_type=F32)
    ms = jnp.mean(y * y, axis=-1, keepdims=True)
    x = h_ref[0] + y * lax.rsqrt(ms + NORM_EPS) * gpost_ref[...]
    hout_ref[0] = x
    ms = jnp.mean(x * x, axis=-1, keepdims=True)
    z_scr[...] = (x * lax.rsqrt(ms + NORM_EPS) * g_ref[...]).astype(BF16)
    slabs = PROJ_COLS // LANES
    lane = lax.broadcasted_iota(jnp.int32, (1, LANES), 1)
    first_half = (lane % ATTN_HEAD_DIM) < (ATTN_HEAD_DIM // 2)
    for c, (kind, base) in enumerate(ODD_CHUNKS):
        y = jnp.dot(z_scr[...], w_ref[:, c * PROJ_COLS:(c + 1) * PROJ_COLS], preferred_element_type=F32)
        for s in range(slabs):
            t = y[:, s * LANES:(s + 1) * LANES]
            if kind in ("q", "k"):
                t = t * cos_ref[...] + _swap_halves(t, first_half) * sin_ref[...]
            if kind == "q":
                t = t * QUERY_SCALE
            if kind == "v":
                vt_ref[0, base + s] = jnp.transpose(t).astype(vt_ref.dtype)
            else:
                o_ref[0, base + s] = t.astype(o_ref.dtype)


def _out_norm_proj_rope(a, b, w_out, g_post, h, g, w, cos_t, sin_t):
    B, L, D = h.shape
    Wb = a.shape[2]
    N = w.shape[1]
    assert N == len(ODD_CHUNKS) * PROJ_COLS
    tm = min(PROJ_ROWS, L)
    n_v = sum(kind == "v" for kind, _ in ODD_CHUNKS) * (PROJ_COLS // LANES)
    n_o = N // LANES - n_v
    row = lambda b_, i: (b_, i, 0)
    const = lambda b_, i: (0, 0)
    return pl.pallas_call(
        _out_norm_proj_rope_kernel,
        grid=(B, L // tm),
        in_specs=[
            pl.BlockSpec((1, tm, Wb), row),
            pl.BlockSpec((1, tm, Wb), row),
            pl.BlockSpec((Wb, D), const),
            pl.BlockSpec((Wb, D), const),
            pl.BlockSpec((1, D), const),
            pl.BlockSpec((1, tm, D), row),
            pl.BlockSpec((1, D), const),
            pl.BlockSpec((D, N), const),
            pl.BlockSpec((tm, LANES), lambda b_, i: (i, 0)),
            pl.BlockSpec((tm, LANES), lambda b_, i: (i, 0)),
        ],
        out_specs=[
            pl.BlockSpec((1, tm, D), row),
            pl.BlockSpec((1, n_o, tm, LANES), lambda b_, i: (b_, 0, i, 0)),
            pl.BlockSpec((1, n_v, LANES, tm), lambda b_, i: (b_, 0, 0, i)),
        ],
        out_shape=[
            jax.ShapeDtypeStruct((B, L, D), F32),
            jax.ShapeDtypeStruct((B, n_o, L, LANES), BF16),
            jax.ShapeDtypeStruct((B, n_v, LANES, L), BF16),
        ],
        scratch_shapes=[pltpu.VMEM((tm, D), BF16)],
        compiler_params=_params("parallel", "parallel"),
        name="out_norm_proj_odd",
    )(a, b, w_out[:Wb].astype(BF16), w_out[Wb:].astype(BF16), g_post.reshape(1, D), h, g.reshape(1, D), w,
      cos_t, sin_t)


def _out_proj_kernel(a_ref, b_ref, wa_ref, wb_ref, g_ref, h_ref, o_ref):
    y = jnp.dot(a_ref[0], wa_ref[...], preferred_element_type=F32)
    y = y + jnp.dot(b_ref[0], wb_ref[...], preferred_element_type=F32)
    ms = jnp.mean(y * y, axis=-1, keepdims=True)
    o_ref[0] = h_ref[0] + y * lax.rsqrt(ms + NORM_EPS) * g_ref[...]


def _out_proj(a, b, w_out, g, h):
    B, L, W = a.shape
    D = w_out.shape[1]
    tm = min(OUT_PROJ_ROWS, L)
    return pl.pallas_call(
        _out_proj_kernel,
        grid=(B, L // tm),
        in_specs=[
            pl.BlockSpec((1, tm, W), lambda b_, i: (b_, i, 0)),
            pl.BlockSpec((1, tm, W), lambda b_, i: (b_, i, 0)),
            pl.BlockSpec((W, D), lambda b_, i: (0, 0)),
            pl.BlockSpec((W, D), lambda b_, i: (0, 0)),
            pl.BlockSpec((1, D), lambda b_, i: (0, 0)),
            pl.BlockSpec((1, tm, D), lambda b_, i: (b_, i, 0)),
        ],
        out_specs=pl.BlockSpec((1, tm, D), lambda b_, i: (b_, i, 0)),
        out_shape=jax.ShapeDtypeStruct((B, L, D), F32),
        compiler_params=_params("parallel", "parallel"),
        name="out_proj",
    )(a, b, w_out[:W].astype(BF16), w_out[W:].astype(BF16), g.reshape(1, D), h)


def _s5_kernel(u_ref, z_ref, bin_ref, wx_ref, wu_ref, ar_ref, ai_ref, d_ref, gw_ref, gb_ref, o_ref,
               tmp_scr, us_scr, zs_scr, xs_scr, ys_scr, os_scr, st_scr):
    nb, T, W = u_ref.shape
    n_blk = W // LANES
    r = S5_STRIDE
    n_grp = T // r
    half = xs_scr.shape[2] // 2
    n_t = n_grp // S5_PIECE
    piece_rows = S5_PIECE * nb

    @pl.when(pl.program_id(1) == 0)
    def _():
        st_scr[...] = jnp.zeros_like(st_scr)

    for b in range(nb):
        for src, dst in ((u_ref, us_scr), (z_ref, zs_scr)):
            xb = src[b].astype(F32)
            for k in range(n_blk):
                tmp_scr[k] = xb[:, k * LANES:(k + 1) * LANES]
            for k in range(n_blk):
                for j in range(r):
                    dst[k, j, pl.ds(b, n_grp, stride=nb), :] = tmp_scr[k, pl.ds(j, n_grp, stride=r), :]

    def rows_of(i):
        return slice(i * piece_rows, (i + 1) * piece_rows)

    def group_inputs(i, k):
        return jnp.concatenate([us_scr[k, j, rows_of(i), :] for j in range(r)], axis=1).astype(BF16)

    def input_matmul(i, k):
        xs_scr[k, rows_of(i), :] = jnp.dot(group_inputs(i, k), bin_ref[k], preferred_element_type=F32)

    def scan(i, k):
        ar, ai = ar_ref[k], ai_ref[k]
        xr, xi = st_scr[k, 0], st_scr[k, 1]
        for t in range(S5_PIECE):
            rs = slice(i * piece_rows + t * nb, i * piece_rows + (t + 1) * nb)
            cr, ci = xs_scr[k, rs, 0:half], xs_scr[k, rs, half:2 * half]
            xs_scr[k, rs, 0:half] = xr
            xs_scr[k, rs, half:2 * half] = xi
            xr, xi = ar * xr - ai * xi + cr, ar * xi + ai * xr + ci
        st_scr[k, 0] = xr
        st_scr[k, 1] = xi

    def output_matmul(i, k):
        y = jnp.dot(xs_scr[k, rows_of(i), :].astype(BF16), wx_ref[k], preferred_element_type=F32)
        y = y + jnp.dot(group_inputs(i, k), wu_ref[k], preferred_element_type=F32)
        for j in range(r):
            ys_scr[k, j, rows_of(i), :] = y[:, j * LANES:(j + 1) * LANES]

    def glu(i):
        def tokens(ref):
            return jnp.concatenate(
                [jnp.concatenate([ref[k, j, rows_of(i), :] for k in range(n_blk)], axis=1) for j in range(r)], axis=0)
        y = tokens(ys_scr) + d_ref[...] * tokens(us_scr)
        g = jax.nn.gelu(y)
        gate = jnp.dot(g.astype(BF16), gw_ref[...], preferred_element_type=F32) + gb_ref[...]
        out = g * jax.nn.sigmoid(gate) * _silu(tokens(zs_scr))
        for j in range(r):
            for k in range(n_blk):
                os_scr[k, j, rows_of(i), :] = out[j * piece_rows:(j + 1) * piece_rows, k * LANES:(k + 1) * LANES]

    def drain(p):
        i, k = pieces[p]
        output_matmul(i, k)
        if k == n_blk - 1:
            glu(i)

    pieces = [(i, k) for i in range(n_t) for k in range(n_blk)]
    input_matmul(*pieces[0])
    for p, (i, k) in enumerate(pieces):
        if p + 1 < len(pieces):
            input_matmul(*pieces[p + 1])
        if p >= 1:
            drain(p - 1)
        scan(i, k)
    drain(len(pieces) - 1)

    for b in range(nb):
        for k in range(n_blk):
            for j in range(r):
                tmp_scr[k, pl.ds(j, n_grp, stride=r), :] = os_scr[k, j, pl.ds(b, n_grp, stride=nb), :]
        o_ref[b] = jnp.concatenate([tmp_scr[k] for k in range(n_blk)], axis=1).astype(o_ref.dtype)


def _s5_weights(lam_re, lam_im, log_dt, b_re, b_im, c_re, c_im):
    G, P = lam_re.shape
    r = S5_STRIDE
    gpb = LANES // S5_GROUP
    n_blk = G // gpb
    lr, li = lam_re.astype(F32), lam_im.astype(F32)
    dt = jnp.exp(log_dt.astype(F32))[:, None]
    mag = jnp.exp(lr * dt)
    ab_r, ab_i = mag * jnp.cos(li * dt), mag * jnp.sin(li * dt)
    den = lr * lr + li * li
    f_r = ((ab_r - 1.0) * lr + ab_i * li) / den
    f_i = (ab_i * lr - (ab_r - 1.0) * li) / den
    br, bi = b_re.astype(F32), b_im.astype(F32)
    bb_r = f_r[..., None] * br - f_i[..., None] * bi
    bb_i = f_r[..., None] * bi + f_i[..., None] * br
    cr, ci = c_re.astype(F32), c_im.astype(F32)
    eye = jnp.eye(gpb, dtype=F32)

    pows = [(jnp.ones_like(ab_r), jnp.zeros_like(ab_i))]
    for _ in range(r):
        pr, pi = pows[-1]
        pows.append((pr * ab_r - pi * ab_i, pr * ab_i + pi * ab_r))

    def pack_in(xr, xi):
        x = jnp.stack([xr, xi]).reshape(2, n_blk, gpb, P, S5_GROUP)
        x = jnp.transpose(x, (1, 2, 4, 0, 3))
        x = x[:, :, :, :, None, :] * eye[None, :, None, None, :, None]
        return x.reshape(n_blk, LANES, 2 * gpb * P)

    def pack_out(er, ei):
        x = jnp.stack([er, ei]).reshape(2, n_blk, gpb, S5_GROUP, P)
        x = jnp.transpose(x, (1, 0, 2, 4, 3))
        x = x[:, :, :, :, None, :] * eye[None, None, :, None, :, None]
        return x.reshape(n_blk, 2 * gpb * P, LANES)

    def pack_direct(kq):
        x = jnp.transpose(kq.reshape(n_blk, gpb, S5_GROUP, S5_GROUP), (0, 1, 3, 2))
        x = x[:, :, :, None, :] * eye[None, :, None, :, None]
        return x.reshape(n_blk, LANES, LANES)

    w_in = []
    for j in range(r):
        pr, pi = pows[r - 1 - j]
        w_in.append(pack_in(pr[..., None] * bb_r - pi[..., None] * bb_i, pr[..., None] * bb_i + pi[..., None] * bb_r))
    w_in = jnp.concatenate(w_in, axis=1).astype(BF16)

    w_x = []
    for j in range(r):
        pr, pi = pows[j + 1]
        er = cr * pr[:, None, :] - ci * pi[:, None, :]
        ei = cr * pi[:, None, :] + ci * pr[:, None, :]
        w_x.append(pack_out(er, -ei))
    w_x = jnp.concatenate(w_x, axis=2).astype(BF16)

    direct = []
    for q in range(r):
        pr, pi = pows[q]
        er = cr * pr[:, None, :] - ci * pi[:, None, :]
        ei = cr * pi[:, None, :] + ci * pr[:, None, :]
        direct.append(pack_direct(jnp.einsum('gop,gpi->goi', er, bb_r) - jnp.einsum('gop,gpi->goi', ei, bb_i)))
    zero = jnp.zeros_like(direct[0])
    w_u = jnp.concatenate(
        [jnp.concatenate([direct[j - i] if i <= j else zero for j in range(r)], axis=2) for i in range(r)],
        axis=1).astype(BF16)

    ar_r, ai_r = pows[r]
    ar = jnp.broadcast_to(ar_r.reshape(n_blk, 1, gpb * P), (n_blk, S5_BATCH, gpb * P))
    ai = jnp.broadcast_to(ai_r.reshape(n_blk, 1, gpb * P), (n_blk, S5_BATCH, gpb * P))
    return w_in, w_x, w_u, ar, ai


def _s5_branch(proj, lam_re, lam_im, log_dt, b_re, b_im, c_re, c_im, d_skip, glu_w, glu_b):
    B, L, N = proj.shape
    W = N // 4
    nb = S5_BATCH
    T = min(S5_TIME, L)
    r = S5_STRIDE
    n_blk = W // LANES
    w_in, w_x, w_u, ar, ai = _s5_weights(lam_re, lam_im, log_dt, b_re, b_im, c_re, c_im)
    S2 = w_in.shape[2]
    rows = nb * T // r
    const3 = lambda g, c: (0, 0, 0)
    const2 = lambda g, c: (0, 0)
    return pl.pallas_call(
        _s5_kernel,
        grid=(B // nb, L // T),
        in_specs=[
            pl.BlockSpec((nb, T, W), lambda g, c: (g, c, 0)),
            pl.BlockSpec((nb, T, W), lambda g, c: (g, c, 1)),
            pl.BlockSpec(w_in.shape, const3),
            pl.BlockSpec(w_x.shape, const3),
            pl.BlockSpec(w_u.shape, const3),
            pl.BlockSpec(ar.shape, const3),
            pl.BlockSpec(ai.shape, const3),
            pl.BlockSpec((1, W), const2),
            pl.BlockSpec((W, W), const2),
            pl.BlockSpec((1, W), const2),
        ],
        out_specs=pl.BlockSpec((nb, T, W), lambda g, c: (g, c, 0)),
        out_shape=jax.ShapeDtypeStruct((B, L, W), BF16),
        scratch_shapes=[
            pltpu.VMEM((n_blk, T, LANES), F32),
            pltpu.VMEM((n_blk, r, rows, LANES), F32),
            pltpu.VMEM((n_blk, r, rows, LANES), F32),
            pltpu.VMEM((n_blk, rows, S2), F32),
            pltpu.VMEM((n_blk, r, rows, LANES), F32),
            pltpu.VMEM((n_blk, r, rows, LANES), F32),
            pltpu.VMEM((n_blk, 2, nb, S2 // 2), F32),
        ],
        compiler_params=_params("parallel", "arbitrary"),
        name="s5_scan",
    )(proj, proj, w_in, w_x, w_u, ar, ai, d_skip.reshape(1, W).astype(F32), glu_w.astype(BF16),
      glu_b.reshape(1, W).astype(F32))


def _log_sigmoid(x):
    return jnp.minimum(x, 0.0) - jnp.log1p(jnp.exp(-jnp.abs(x)))


def _split3(x):
    hi = x.astype(BF16)
    r1 = x - hi.astype(F32)
    mid = r1.astype(BF16)
    lo = (r1 - mid.astype(F32)).astype(BF16)
    return hi, mid, lo


def _mlstm_kernel(x_ref, z_ref, cw_ref, cb_ref, wq_ref, wk_ref, wv_ref, wvt_ref, gw_ref, gwt_ref, gbc_ref, gbr_ref,
                  ng_ref, sk_ref, o_ref, xe_scr, qkv_scr, ks_scr, vta_scr, ca_scr, m_scr):
    Tc, W = x_ref.shape[1], x_ref.shape[2]
    H = wq_ref.shape[0]
    dh = W // H
    pad = SUBLANES

    @pl.when(pl.program_id(1) == 0)
    def _():
        xe_scr[0:pad, :] = jnp.zeros((pad, W), F32)
        ca_scr[...] = jnp.zeros_like(ca_scr)
        m_scr[...] = jnp.zeros_like(m_scr)
        vta_scr[:, dh:, :] = jnp.ones((H, pad, Tc), BF16)

    xb = x_ref[0]
    x = xb.astype(F32)
    xe_scr[pad:pad + Tc, :] = x
    xc = cb_ref[...] + sum(xe_scr[pl.ds(pad - (ML_CONV - 1) + j, Tc), :] * cw_ref[j:j + 1, :]
                           for j in range(ML_CONV))
    xc = _silu(xc)
    xe_scr[0:pad, :] = x[Tc - pad:Tc, :]
    xcb = xc.astype(BF16)

    nt = (((1,), (1,)), ((), ()))
    for h in range(H):
        cols = slice(h * dh, (h + 1) * dh)
        q = jnp.dot(xcb[:, cols], wq_ref[h], preferred_element_type=F32)
        k = jnp.dot(xcb[:, cols], wk_ref[h], preferred_element_type=F32)
        v = jnp.dot(xb[:, cols], wv_ref[h], preferred_element_type=F32)
        qkv_scr[:, h * dh:(h + 1) * dh] = q.astype(BF16)
        qkv_scr[:, W + h * dh:W + (h + 1) * dh] = k.astype(BF16)
        qkv_scr[:, 2 * W + h * dh:2 * W + (h + 1) * dh] = v.astype(BF16)
        vta_scr[h, :dh, :] = lax.dot_general(wvt_ref[h], xb[:, cols], nt, preferred_element_type=F32).astype(BF16)
        ks_scr[:, cols] = (k * (dh ** -0.5)).astype(BF16)

    qkv = qkv_scr[...]
    g_col = jnp.dot(qkv, gw_ref[...], preferred_element_type=F32) + gbc_ref[...]
    g_row = lax.dot_general(gwt_ref[...], qkv, nt, preferred_element_type=F32) + gbr_ref[...]
    ti = lax.broadcasted_iota(jnp.int32, (Tc, Tc), 0)
    si = lax.broadcasted_iota(jnp.int32, (Tc, Tc), 1)
    lower = jnp.where(si <= ti, 1.0, 0.0).astype(BF16)
    allowed = ti <= si
    upper = jnp.where(allowed, 1.0, 0.0).astype(BF16)
    b_col = sum(jnp.dot(lower, p, preferred_element_type=F32) for p in _split3(_log_sigmoid(g_col)))
    b_row = sum(jnp.dot(p, upper, preferred_element_type=F32) for p in _split3(_log_sigmoid(g_row)))

    heads = range(H)
    q_h = [qkv_scr[:, h * dh:(h + 1) * dh] for h in heads]
    ks_h = [ks_scr[:, h * dh:(h + 1) * dh] for h in heads]
    m_prev = [m_scr[h][:, 0:1] for h in heads]
    br = [b_row[H + h:H + h + 1, :] for h in heads]
    li = [g_row[h:h + 1, :] for h in heads]
    b_last = [br[h][:, Tc - 1:Tc] for h in heads]

    s_qk = [lax.dot_general(ks_h[h], q_h[h], nt, preferred_element_type=F32) for h in heads]
    q_ca = [lax.dot_general(ca_scr[h].astype(BF16), q_h[h], nt, preferred_element_type=F32)
            for h in heads]

    dmat, sc, floor = [], [], []
    for h in heads:
        c_col = g_col[:, h:h + 1] - b_col[:, H + h:H + h + 1]
        log_d = jnp.where(allowed, br[h] + c_col, NEG_BIG)
        inter = br[h] + m_prev[h]
        m_t = jnp.maximum(inter, jnp.max(log_d, axis=0, keepdims=True))
        dmat.append(jnp.exp(log_d - m_t))
        sc.append(jnp.exp(inter - m_t))
        floor.append(jnp.exp(-m_t))

    pv = [jnp.dot(vta_scr[h], (s_qk[h] * dmat[h]).astype(BF16), preferred_element_type=F32) for h in heads]

    for h in heads:
        w_log = b_last[h] - br[h] + li[h]
        m_new = jnp.maximum(b_last[h] + m_prev[h], jnp.max(w_log, axis=1, keepdims=True))
        weighted = (vta_scr[h].astype(F32) * jnp.exp(w_log - m_new)).astype(BF16)
        decay = jnp.exp(b_last[h] + m_prev[h] - m_new)
        ca_scr[h] = decay * ca_scr[h] + jnp.dot(weighted, ks_h[h], preferred_element_type=F32)
        m_scr[h] = jnp.broadcast_to(m_new, (1, LANES))

    for h in heads:
        cols = slice(h * dh, (h + 1) * dh)
        tot = sc[h] * q_ca[h] + pv[h]
        hh = tot[:dh] / jnp.maximum(jnp.abs(tot[dh:dh + 1]), floor[h])
        mu = jnp.mean(hh, axis=0, keepdims=True)
        cen = hh - mu
        var = jnp.mean(cen * cen, axis=0, keepdims=True)
        hn = jnp.transpose(cen * lax.rsqrt(var + HEAD_NORM_EPS))
        out = hn * ng_ref[:, cols] + sk_ref[:, cols] * xc[:, cols]
        o_ref[0, :, cols] = (out * _silu(z_ref[0, :, cols].astype(F32))).astype(o_ref.dtype)


def _mlstm_branch(proj, conv_w, conv_b, wq, wk, wv, gate_w, gate_b, norm_g, skip):
    B, L, N = proj.shape
    W = N // 4
    H = wq.shape[0]
    dh = W // H
    Tc = min(ML_CHUNK, L)
    gw = jnp.zeros((3 * W, LANES), F32).at[:, :2 * H].set(gate_w.astype(F32)).astype(BF16)
    gwt = jnp.transpose(gate_w.astype(F32)).astype(BF16)
    gbc = jnp.zeros((1, LANES), F32).at[0, :2 * H].set(gate_b.astype(F32))
    gbr = gate_b.astype(F32).reshape(2 * H, 1)
    c2 = lambda b, c: (0, 0)
    c3 = lambda b, c: (0, 0, 0)
    return pl.pallas_call(
        _mlstm_kernel,
        grid=(B, L // Tc),
        in_specs=[
            pl.BlockSpec((1, Tc, W), lambda b, c: (b, c, 2)),
            pl.BlockSpec((1, Tc, W), lambda b, c: (b, c, 3)),
            pl.BlockSpec((ML_CONV, W), c2),
            pl.BlockSpec((1, W), c2),
            pl.BlockSpec((H, dh, dh), c3),
            pl.BlockSpec((H, dh, dh), c3),
            pl.BlockSpec((H, dh, dh), c3),
            pl.BlockSpec((H, dh, dh), c3),
            pl.BlockSpec((3 * W, LANES), c2),
            pl.BlockSpec((2 * H, 3 * W), c2),
            pl.BlockSpec((1, LANES), c2),
            pl.BlockSpec((2 * H, 1), c2),
            pl.BlockSpec((1, W), c2),
            pl.BlockSpec((1, W), c2),
        ],
        out_specs=pl.BlockSpec((1, Tc, W), lambda b, c: (b, c, 0)),
        out_shape=jax.ShapeDtypeStruct((B, L, W), BF16),
        scratch_shapes=[
            pltpu.VMEM((SUBLANES + Tc, W), F32),
            pltpu.VMEM((Tc, 3 * W), BF16),
            pltpu.VMEM((Tc, W), BF16),
            pltpu.VMEM((H, dh + SUBLANES, Tc), BF16),
            pltpu.VMEM((H, dh + SUBLANES, dh), F32),
            pltpu.VMEM((H, 1, LANES), F32),
        ],
        compiler_params=_params("parallel", "arbitrary"),
        name="mlstm",
    )(proj, proj, conv_w.astype(F32), conv_b.reshape(1, W).astype(F32), wq.astype(BF16), wk.astype(BF16),
      wv.astype(BF16), jnp.transpose(wv, (0, 2, 1)).astype(BF16), gw, gwt, gbc, gbr, norm_g.reshape(1, W).astype(F32), skip.reshape(1, W).astype(F32))


def _attn_kernel(q_ref, k_ref, vt_ref, z_ref, bias_ref, lam_ref, ng_ref, o_ref, m_scr, acc_scr, *,
                 mode, out_scale, blk, bias_index):
    L = q_ref.shape[2]
    n = L // blk
    lane = lax.broadcasted_iota(jnp.int32, (1, LANES), 1)
    map0 = lane < ATTN_HEAD_DIM
    feat = lax.broadcasted_iota(jnp.int32, (LANES, 1), 0)
    ones_rows = jnp.ones((SUBLANES, blk), BF16)

    def stacked_q(qi):
        q = q_ref[0, 0, qi * blk:(qi + 1) * blk, :]
        zero = jnp.zeros_like(q)
        return jnp.concatenate([jnp.where(map0, q, zero), jnp.where(map0, zero, q)], axis=0)

    def scores(qi, j):
        s = lax.dot_general(k_ref[0, 0, j * blk:(j + 1) * blk, :], q2[qi], (((1,), (1,)), ((), ())),
                            preferred_element_type=F32)
        e = bias_index(qi - j)
        return s if e is None else s + bias_ref[e]

    def update(qi, j, s):
        m_cur = jnp.max(s, axis=0, keepdims=True)
        m_next = m_cur if j == 0 else jnp.maximum(m_scr[qi], m_cur)
        p = jnp.exp2(s - m_next).astype(BF16)
        pv = jnp.dot(jnp.concatenate([vt_ref[0, 0, :, j * blk:(j + 1) * blk], ones_rows], axis=0), p,
                     preferred_element_type=F32)
        if j == 0:
            acc_scr[qi] = pv
        else:
            acc_scr[qi] = jnp.exp2(m_scr[qi] - m_next) * acc_scr[qi] + pv
        m_scr[qi] = m_next

    def finish(qi):
        rows = slice(qi * blk, (qi + 1) * blk)
        acc = acc_scr[qi]
        o = acc[:LANES] / acc[LANES:LANES + 1]
        o0, o1 = o[:, :blk], o[:, blk:]
        if mode == "dilated":
            out = jnp.transpose(jnp.where(feat < ATTN_HEAD_DIM, o0, o1))
        else:
            out = jnp.transpose(o0 - lam_ref[:, 0:1] * o1)
            ms = jnp.mean(out * out, axis=-1, keepdims=True)
            out = out * lax.rsqrt(ms + HEAD_NORM_EPS) * ng_ref[...] * out_scale
        o_ref[0, rows, :] = (out * _silu(z_ref[0, 0, rows, :].astype(F32))).astype(o_ref.dtype)

    steps = []
    for a in range(n // 2):
        b = n - 1 - a
        ja = 0
        for jb in range(b + 1):
            steps.append((b, jb))
            if ja <= a and (jb + 1) * (a + 1) >= (ja + 1) * (b + 1):
                steps.append((a, ja))
                ja += 1
    q2 = {qi: stacked_q(qi) for qi in range(n)}
    pending = {}
    for t in range(min(ATTN_LOOKAHEAD, len(steps))):
        pending[t] = scores(*steps[t])
    for t, (qi, j) in enumerate(steps):
        if t + ATTN_LOOKAHEAD < len(steps):
            pending[t + ATTN_LOOKAHEAD] = scores(*steps[t + ATTN_LOOKAHEAD])
        update(qi, j, pending.pop(t))
        if j == qi:
            finish(qi)


def _dilated_multiplicity(delta):
    mult = np.zeros(delta.shape, np.int64)
    for window, dil in DIL_PAIRS:
        mult += (delta >= 0) & (delta % dil == 0) & (delta <= window)
    return mult


def _bias_tables(L, blk, mode):
    n = L // blk
    key = np.arange(blk)[:, None]
    qry = np.arange(blk)[None, :]
    tables, index = [], {}
    for d in range(n):
        delta = d * blk + qry - key
        mult = _dilated_multiplicity(delta) if mode == "dilated" else (delta >= 0).astype(np.int64)
        t = np.where(mult > 0, np.log2(np.maximum(mult, 1)), NEG_BIG).astype(np.float32)
        if not t.any():
            index[d] = None
            continue
        for e, other in enumerate(tables):
            if np.array_equal(t, other):
                index[d] = e
                break
        else:
            index[d] = len(tables)
            tables.append(t)
    stacked = np.stack(tables)
    return jnp.asarray(np.concatenate([stacked, stacked], axis=2)), index


def _attention(qkz, vt, q0, k0, v0, z0, n_blocks, blk, lam, norm_g, mode, out_scale):
    B, _, L, _ = qkz.shape
    bias, index = _bias_tables(L, blk, mode)
    n_q = L // blk
    assert n_q % 2 == 0
    kern = functools.partial(_attn_kernel, mode=mode, out_scale=out_scale, blk=blk, bias_index=index.get)
    c2 = lambda b, n: (0, 0)
    return pl.pallas_call(
        kern,
        grid=(B, n_blocks),
        in_specs=[
            pl.BlockSpec((1, 1, L, LANES), lambda b, n: (b, q0 + n, 0, 0)),
            pl.BlockSpec((1, 1, L, LANES), lambda b, n: (b, k0 + n, 0, 0)),
            pl.BlockSpec((1, 1, LANES, L), lambda b, n: (b, v0 + n, 0, 0)),
            pl.BlockSpec((1, 1, L, LANES), lambda b, n: (b, z0 + n, 0, 0)),
            pl.BlockSpec(bias.shape, lambda b, n: (0, 0, 0)),
            pl.BlockSpec((1, LANES), c2),
            pl.BlockSpec((1, LANES), c2),
        ],
        out_specs=pl.BlockSpec((1, L, LANES), lambda b, n: (b, 0, n)),
        out_shape=jax.ShapeDtypeStruct((B, L, n_blocks * LANES), BF16),
        scratch_shapes=[
            pltpu.VMEM((n_q, 1, 2 * blk), F32),
            pltpu.VMEM((n_q, LANES + SUBLANES, 2 * blk), F32),
        ],
        compiler_params=_params("parallel", "parallel"),
        name="attn_" + mode,
    )(qkz, qkz, vt, qkz, bias, lam, norm_g)


def _rope_tables(L):
    dh = ATTN_HEAD_DIM
    inv = ROPE_THETA ** (-jnp.arange(0, dh, 2, dtype=F32) / dh)
    ang = jnp.arange(L, dtype=F32)[:, None] * inv[None, :]
    cos, sin = jnp.cos(ang), jnp.sin(ang)
    reps = LANES // dh
    cos_t = jnp.tile(jnp.concatenate([cos, cos], axis=-1), (1, reps))
    sin_t = jnp.tile(jnp.concatenate([-sin, sin], axis=-1), (1, reps))
    return cos_t, sin_t


def _even_layer(h, pre_g, w_in, s5_params, ml_params):
    proj = _norm_proj(h, pre_g, w_in.astype(BF16))
    return _s5_branch(proj, *s5_params), _mlstm_branch(proj, *ml_params)


def _odd_layer(prev, h, pre_g, w_in, lq1, lk1, lq2, lk2, diff_norm, layer_idx):
    B, L, D = h.shape
    nblk = PROJ_COLS // LANES
    cos_t, sin_t = _rope_tables(L)
    h, qkz, vt = _out_norm_proj_rope(*prev, h, pre_g, w_in.astype(BF16), cos_t, sin_t)
    blk = min(ATTN_BLOCK, L)
    lam_init = 0.8 - 0.6 * math.exp(-0.3 * layer_idx)
    lam = (jnp.exp(jnp.sum(lq1.astype(F32) * lk1.astype(F32)))
           - jnp.exp(jnp.sum(lq2.astype(F32) * lk2.astype(F32))) + lam_init)
    lam_row = jnp.full((1, LANES), lam, F32)
    ones_row = jnp.ones((1, LANES), F32)
    base = [b for _, b in ODD_CHUNKS]
    c_out = _attention(qkz, vt, base[0], base[1], base[2], base[3], nblk, blk, ones_row, ones_row, "dilated", 1.0)
    d_out = _attention(qkz, vt, base[4], base[5], base[6], base[7], nblk, blk, lam_row,
                       diff_norm.reshape(1, LANES).astype(F32), "diff", 1.0 - lam_init)
    return h, c_out, d_out


def kernel(x, pre_norm, post_norm, w_in_ab, s5_lambda_re, s5_lambda_im, s5_log_dt, s5_b_re, s5_b_im,
           s5_c_re, s5_c_im, s5_d, s5_glu_w, s5_glu_b, ml_conv_w, ml_conv_b, ml_wq, ml_wk, ml_wv,
           ml_gate_w, ml_gate_b, ml_norm, ml_skip, w_out_ab, w_in_cd, diff_lq1, diff_lk1, diff_lq2,
           diff_lk2, diff_norm, w_out_cd):
    depth = pre_norm.shape[0]
    h = x
    prev = None
    for l in range(depth):
        i = l // 2
        if l % 2 == 0:
            if prev is not None:
                h = _out_proj(*prev, h)
            s5_params = (s5_lambda_re[i], s5_lambda_im[i], s5_log_dt[i], s5_b_re[i], s5_b_im[i],
                         s5_c_re[i], s5_c_im[i], s5_d[i], s5_glu_w[i], s5_glu_b[i])
            ml_params = (ml_conv_w[i], ml_conv_b[i], ml_wq[i], ml_wk[i], ml_wv[i], ml_gate_w[i],
                         ml_gate_b[i], ml_norm[i], ml_skip[i])
            a, b = _even_layer(h, pre_norm[l], w_in_ab[i], s5_params, ml_params)
            prev = (a, b, w_out_ab[i], post_norm[l])
        else:
            h, c, d = _odd_layer(prev, h, pre_norm[l], w_in_cd[i], diff_lq1[i], diff_lk1[i], diff_lq2[i],
                                 diff_lk2[i], diff_norm[i], l)
            prev = (c, d, w_out_cd[i], post_norm[l])
    return _out_proj(*prev, h)
```

```python
import functools
import math

import numpy as np
import jax
import jax.numpy as jnp
from jax import lax
from jax.experimental import pallas as pl
from jax.experimental.pallas import tpu as pltpu

F32 = jnp.float32
BF16 = jnp.bfloat16

S5_GROUP = 16
S5_STATE = 64
ML_HEADS = 4
ML_CONV = 4
DIL_PAIRS = ((128, 1), (512, 4), (2048, 16))
ROPE_THETA = 10000.0
NORM_EPS = 1e-6
HEAD_NORM_EPS = 1e-5
ATTN_HEAD_DIM = 64
QUERY_SCALE = ATTN_HEAD_DIM ** -0.5 * math.log2(math.e)

LANES = 128
SUBLANES = 8
VMEM_LIMIT_BYTES = 56 * 1024 * 1024

NEG_BIG = -1e30

PROJ_ROWS = 512
OUT_PROJ_ROWS = 1024
PROJ_COLS = 512
S5_BATCH = SUBLANES
S5_TIME = 128
S5_STRIDE = 2
S5_PIECE = 32
ML_CHUNK = 256
ML_SUB = 4
ATTN_BLOCK = 256
ATTN_LOOKAHEAD = 3


def _params(*semantics):
    return pltpu.CompilerParams(dimension_semantics=semantics, vmem_limit_bytes=VMEM_LIMIT_BYTES)


def _silu(x):
    return x * jax.nn.sigmoid(x)


def _norm_proj_kernel(x_ref, g_ref, w_ref, o_ref, z_scr):
    x = x_ref[0]
    ms = jnp.mean(x * x, axis=-1, keepdims=True)
    z_scr[...] = (x * lax.rsqrt(ms + NORM_EPS) * g_ref[...]).astype(BF16)
    n_out = w_ref.shape[1]
    for c in range(n_out // PROJ_COLS):
        cols = slice(c * PROJ_COLS, (c + 1) * PROJ_COLS)
        o_ref[0, :, cols] = jnp.dot(z_scr[...], w_ref[:, cols], preferred_element_type=F32).astype(o_ref.dtype)


def _norm_proj(x, g, w):
    B, L, D = x.shape
    N = w.shape[1]
    tm = min(PROJ_ROWS, L)
    return pl.pallas_call(
        _norm_proj_kernel,
        grid=(B, L // tm),
        in_specs=[
            pl.BlockSpec((1, tm, D), lambda b, i: (b, i, 0)),
            pl.BlockSpec((1, D), lambda b, i: (0, 0)),
            pl.BlockSpec((D, N), lambda b, i: (0, 0)),
        ],
        out_specs=pl.BlockSpec((1, tm, N), lambda b, i: (b, i, 0)),
        out_shape=jax.ShapeDtypeStruct((B, L, N), BF16),
        scratch_shapes=[pltpu.VMEM((tm, D), BF16)],
        compiler_params=_params("parallel", "parallel"),
        name="norm_proj_even",
    )(x, g.reshape(1, D), w)


def _swap_halves(x, first_half):
    return jnp.where(first_half, pltpu.roll(x, LANES - ATTN_HEAD_DIM // 2, 1), pltpu.roll(x, ATTN_HEAD_DIM // 2, 1))


ODD_CHUNKS = (("q", 0), ("k", 4), ("v", 0), ("z", 8), ("q", 12), ("k", 16), ("v", 4), ("z", 20))


def _out_norm_proj_rope_kernel(a_ref, b_ref, wa_ref, wb_ref, gpost_ref, h_ref, g_ref, w_ref, cos_ref, sin_ref,
                               hout_ref, o_ref, vt_ref, z_scr):
    y = jnp.dot(a_ref[0], wa_ref[...], preferred_element_type=F32)
    y = y + jnp.dot(b_ref[0], wb_ref[...], preferred_element_type=F32)
    ms = jnp.mean(y * y, axis=-1, keepdims=True)
    x = h_ref[0] + y * lax.rsqrt(ms + NORM_EPS) * gpost_ref[...]
    hout_ref[0] = x
    ms = jnp.mean(x * x, axis=-1, keepdims=True)
    z_scr[...] = (x * lax.rsqrt(ms + NORM_EPS) * g_ref[...]).astype(BF16)
    slabs = PROJ_COLS // LANES
    lane = lax.broadcasted_iota(jnp.int32, (1, LANES), 1)
    first_half = (lane % ATTN_HEAD_DIM) < (ATTN_HEAD_DIM // 2)
    for c, (kind, base) in enumerate(ODD_CHUNKS):
        y = jnp.dot(z_scr[...], w_ref[:, c * PROJ_COLS:(c + 1) * PROJ_COLS], preferred_element_type=F32)
        for s in range(slabs):
            t = y[:, s * LANES:(s + 1) * LANES]
            if kind in ("q", "k"):
                t = t * cos_ref[...] + _swap_halves(t, first_half) * sin_ref[...]
            if kind == "q":
                t = t * QUERY_SCALE
            if kind == "v":
                vt_ref[0, base + s] = jnp.transpose(t).astype(vt_ref.dtype)
            else:
                o_ref[0, base + s] = t.astype(o_ref.dtype)


def _out_norm_proj_rope(a, b, w_out, g_post, h, g, w, cos_t, sin_t):
    B, L, D = h.shape
    Wb = a.shape[2]
    N = w.shape[1]
    assert N == len(ODD_CHUNKS) * PROJ_COLS
    tm = min(PROJ_ROWS, L)
    n_v = sum(kind == "v" for kind, _ in ODD_CHUNKS) * (PROJ_COLS // LANES)
    n_o = N // LANES - n_v
    row = lambda b_, i: (b_, i, 0)
    const = lambda b_, i: (0, 0)
    return pl.pallas_call(
        _out_norm_proj_rope_kernel,
        grid=(B, L // tm),
        in_specs=[
            pl.BlockSpec((1, tm, Wb), row),
            pl.BlockSpec((1, tm, Wb), row),
            pl.BlockSpec((Wb, D), const),
            pl.BlockSpec((Wb, D), const),
            pl.BlockSpec((1, D), const),
            pl.BlockSpec((1, tm, D), row),
            pl.BlockSpec((1, D), const),
            pl.BlockSpec((D, N), const),
            pl.BlockSpec((tm, LANES), lambda b_, i: (i, 0)),
            pl.BlockSpec((tm, LANES), lambda b_, i: (i, 0)),
        ],
        out_specs=[
            pl.BlockSpec((1, tm, D), row),
            pl.BlockSpec((1, n_o, tm, LANES), lambda b_, i: (b_, 0, i, 0)),
            pl.BlockSpec((1, n_v, LANES, tm), lambda b_, i: (b_, 0, 0, i)),
        ],
        out_shape=[
            jax.ShapeDtypeStruct((B, L, D), F32),
            jax.ShapeDtypeStruct((B, n_o, L, LANES), BF16),
            jax.ShapeDtypeStruct((B, n_v, LANES, L), BF16),
        ],
        scratch_shapes=[pltpu.VMEM((tm, D), BF16)],
        compiler_params=_params("parallel", "parallel"),
        name="out_norm_proj_odd",
    )(a, b, w_out[:Wb].astype(BF16), w_out[Wb:].astype(BF16), g_post.reshape(1, D), h, g.reshape(1, D), w,
      cos_t, sin_t)


def _out_proj_kernel(a_ref, b_ref, wa_ref, wb_ref, g_ref, h_ref, o_ref):
    y = jnp.dot(a_ref[0], wa_ref[...], preferred_element_type=F32)
    y = y + jnp.dot(b_ref[0], wb_ref[...], preferred_element_type=F32)
    ms = jnp.mean(y * y, axis=-1, keepdims=True)
    o_ref[0] = h_ref[0] + y * lax.rsqrt(ms + NORM_EPS) * g_ref[...]


def _out_proj(a, b, w_out, g, h):
    B, L, W = a.shape
    D = w_out.shape[1]
    tm = min(OUT_PROJ_ROWS, L)
    return pl.pallas_call(
        _out_proj_kernel,
        grid=(B, L // tm),
        in_specs=[
            pl.BlockSpec((1, tm, W), lambda b_, i: (b_, i, 0)),
            pl.BlockSpec((1, tm, W), lambda b_, i: (b_, i, 0)),
            pl.BlockSpec((W, D), lambda b_, i: (0, 0)),
            pl.BlockSpec((W, D), lambda b_, i: (0, 0)),
            pl.BlockSpec((1, D), lambda b_, i: (0, 0)),
            pl.BlockSpec((1, tm, D), lambda b_, i: (b_, i, 0)),
        ],
        out_specs=pl.BlockSpec((1, tm, D), lambda b_, i: (b_, i, 0)),
        out_shape=jax.ShapeDtypeStruct((B, L, D), F32),
        compiler_params=_params("parallel", "parallel"),
        name="out_proj",
    )(a, b, w_out[:W].astype(BF16), w_out[W:].astype(BF16), g.reshape(1, D), h)


def _s5_kernel(u_ref, z_ref, bin_ref, wx_ref, wu_ref, ar_ref, ai_ref, d_ref, gw_ref, gb_ref, o_ref,
               tmp_scr, us_scr, zs_scr, xs_scr, ys_scr, os_scr, st_scr):
    nb, T, W = u_ref.shape
    n_blk = W // LANES
    r = S5_STRIDE
    n_grp = T // r
    half = xs_scr.shape[2] // 2
    n_t = n_grp // S5_PIECE
    piece_rows = S5_PIECE * nb

    @pl.when(pl.program_id(1) == 0)
    def _():
        st_scr[...] = jnp.zeros_like(st_scr)

    for b in range(nb):
        for src, dst in ((u_ref, us_scr), (z_ref, zs_scr)):
            xb = src[b].astype(F32)
            for k in range(n_blk):
                tmp_scr[k] = xb[:, k * LANES:(k + 1) * LANES]
            for k in range(n_blk):
                for j in range(r):
                    dst[k, j, pl.ds(b, n_grp, stride=nb), :] = tmp_scr[k, pl.ds(j, n_grp, stride=r), :]

    def rows_of(i):
        return slice(i * piece_rows, (i + 1) * piece_rows)

    def group_inputs(i, k):
        return jnp.concatenate([us_scr[k, j, rows_of(i), :] for j in range(r)], axis=1).astype(BF16)

    def input_matmul(i, k):
        xs_scr[k, rows_of(i), :] = jnp.dot(group_inputs(i, k), bin_ref[k], preferred_element_type=F32)

    def scan(i, k):
        ar, ai = ar_ref[k], ai_ref[k]
        xr, xi = st_scr[k, 0], st_scr[k, 1]
        for t in range(S5_PIECE):
            rs = slice(i * piece_rows + t * nb, i * piece_rows + (t + 1) * nb)
            cr, ci = xs_scr[k, rs, 0:half], xs_scr[k, rs, half:2 * half]
            xs_scr[k, rs, 0:half] = xr
            xs_scr[k, rs, half:2 * half] = xi
            xr, xi = ar * xr - ai * xi + cr, ar * xi + ai * xr + ci
        st_scr[k, 0] = xr
        st_scr[k, 1] = xi

    def output_matmul(i, k):
        y = jnp.dot(xs_scr[k, rows_of(i), :].astype(BF16), wx_ref[k], preferred_element_type=F32)
        y = y + jnp.dot(group_inputs(i, k), wu_ref[k], preferred_element_type=F32)
        for j in range(r):
            ys_scr[k, j, rows_of(i), :] = y[:, j * LANES:(j + 1) * LANES]

    def glu(i):
        def tokens(ref):
            return jnp.concatenate(
                [jnp.concatenate([ref[k, j, rows_of(i), :] for k in range(n_blk)], axis=1) for j in range(r)], axis=0)
        y = tokens(ys_scr) + d_ref[...] * tokens(us_scr)
        g = jax.nn.gelu(y)
        gate = jnp.dot(g.astype(BF16), gw_ref[...], preferred_element_type=F32) + gb_ref[...]
        out = g * jax.nn.sigmoid(gate) * _silu(tokens(zs_scr))
        for j in range(r):
            for k in range(n_blk):
                os_scr[k, j, rows_of(i), :] = out[j * piece_rows:(j + 1) * piece_rows, k * LANES:(k + 1) * LANES]

    def drain(p):
        i, k = pieces[p]
        output_matmul(i, k)
        if k == n_blk - 1:
            glu(i)

    pieces = [(i, k) for i in range(n_t) for k in range(n_blk)]
    input_matmul(*pieces[0])
    for p, (i, k) in enumerate(pieces):
        if p + 1 < len(pieces):
            input_matmul(*pieces[p + 1])
        if p >= 1:
            drain(p - 1)
        scan(i, k)
    drain(len(pieces) - 1)

    for b in range(nb):
        for k in range(n_blk):
            for j in range(r):
                tmp_scr[k, pl.ds(j, n_grp, stride=r), :] = os_scr[k, j, pl.ds(b, n_grp, stride=nb), :]
        o_ref[b] = jnp.concatenate([tmp_scr[k] for k in range(n_blk)], axis=1).astype(o_ref.dtype)


def _s5_weights(lam_re, lam_im, log_dt, b_re, b_im, c_re, c_im):
    G, P = lam_re.shape
    r = S5_STRIDE
    gpb = LANES // S5_GROUP
    n_blk = G // gpb
    lr, li = lam_re.astype(F32), lam_im.astype(F32)
    dt = jnp.exp(log_dt.astype(F32))[:, None]
    mag = jnp.exp(lr * dt)
    ab_r, ab_i = mag * jnp.cos(li * dt), mag * jnp.sin(li * dt)
    den = lr * lr + li * li
    f_r = ((ab_r - 1.0) * lr + ab_i * li) / den
    f_i = (ab_i * lr - (ab_r - 1.0) * li) / den
    br, bi = b_re.astype(F32), b_im.astype(F32)
    bb_r = f_r[..., None] * br - f_i[..., None] * bi
    bb_i = f_r[..., None] * bi + f_i[..., None] * br
    cr, ci = c_re.astype(F32), c_im.astype(F32)
    eye = jnp.eye(gpb, dtype=F32)

    pows = [(jnp.ones_like(ab_r), jnp.zeros_like(ab_i))]
    for _ in range(r):
        pr, pi = pows[-1]
        pows.append((pr * ab_r - pi * ab_i, pr * ab_i + pi * ab_r))

    def pack_in(xr, xi):
        x = jnp.stack([xr, xi]).reshape(2, n_blk, gpb, P, S5_GROUP)
        x = jnp.transpose(x, (1, 2, 4, 0, 3))
        x = x[:, :, :, :, None, :] * eye[None, :, None, None, :, None]
        return x.reshape(n_blk, LANES, 2 * gpb * P)

    def pack_out(er, ei):
        x = jnp.stack([er, ei]).reshape(2, n_blk, gpb, S5_GROUP, P)
        x = jnp.transpose(x, (1, 0, 2, 4, 3))
        x = x[:, :, :, :, None, :] * eye[None, None, :, None, :, None]
        return x.reshape(n_blk, 2 * gpb * P, LANES)

    def pack_direct(kq):
        x = jnp.transpose(kq.reshape(n_blk, gpb, S5_GROUP, S5_GROUP), (0, 1, 3, 2))
        x = x[:, :, :, None, :] * eye[None, :, None, :, None]
        return x.reshape(n_blk, LANES, LANES)

    w_in = []
    for j in range(r):
        pr, pi = pows[r - 1 - j]
        w_in.append(pack_in(pr[..., None] * bb_r - pi[..., None] * bb_i, pr[..., None] * bb_i + pi[..., None] * bb_r))
    w_in = jnp.concatenate(w_in, axis=1).astype(BF16)

    w_x = []
    for j in range(r):
        pr, pi = pows[j + 1]
        er = cr * pr[:, None, :] - ci * pi[:, None, :]
        ei = cr * pi[:, None, :] + ci * pr[:, None, :]
        w_x.append(pack_out(er, -ei))
    w_x = jnp.concatenate(w_x, axis=2).astype(BF16)

    direct = []
    for q in range(r):
        pr, pi = pows[q]
        er = cr * pr[:, None, :] - ci * pi[:, None, :]
        ei = cr * pi[:, None, :] + ci * pr[:, None, :]
        direct.append(pack_direct(jnp.einsum('gop,gpi->goi', er, bb_r) - jnp.einsum('gop,gpi->goi', ei, bb_i)))
    zero = jnp.zeros_like(direct[0])
    w_u = jnp.concatenate(
        [jnp.concatenate([direct[j - i] if i <= j else zero for j in range(r)], axis=2) for i in range(r)],
        axis=1).astype(BF16)

    ar_r, ai_r = pows[r]
    ar = jnp.broadcast_to(ar_r.reshape(n_blk, 1, gpb * P), (n_blk, S5_BATCH, gpb * P))
    ai = jnp.broadcast_to(ai_r.reshape(n_blk, 1, gpb * P), (n_blk, S5_BATCH, gpb * P))
    return w_in, w_x, w_u, ar, ai


def _s5_branch(proj, lam_re, lam_im, log_dt, b_re, b_im, c_re, c_im, d_skip, glu_w, glu_b):
    B, L, N = proj.shape
    W = N // 4
    nb = S5_BATCH
    T = min(S5_TIME, L)
    r = S5_STRIDE
    n_blk = W // LANES
    w_in, w_x, w_u, ar, ai = _s5_weights(lam_re, lam_im, log_dt, b_re, b_im, c_re, c_im)
    S2 = w_in.shape[2]
    rows = nb * T // r
    const3 = lambda g, c: (0, 0, 0)
    const2 = lambda g, c: (0, 0)
    return pl.pallas_call(
        _s5_kernel,
        grid=(B // nb, L // T),
        in_specs=[
            pl.BlockSpec((nb, T, W), lambda g, c: (g, c, 0)),
            pl.BlockSpec((nb, T, W), lambda g, c: (g, c, 1)),
            pl.BlockSpec(w_in.shape, const3),
            pl.BlockSpec(w_x.shape, const3),
            pl.BlockSpec(w_u.shape, const3),
            pl.BlockSpec(ar.shape, const3),
            pl.BlockSpec(ai.shape, const3),
            pl.BlockSpec((1, W), const2),
            pl.BlockSpec((W, W), const2),
            pl.BlockSpec((1, W), const2),
        ],
        out_specs=pl.BlockSpec((nb, T, W), lambda g, c: (g, c, 0)),
        out_shape=jax.ShapeDtypeStruct((B, L, W), BF16),
        scratch_shapes=[
            pltpu.VMEM((n_blk, T, LANES), F32),
            pltpu.VMEM((n_blk, r, rows, LANES), F32),
            pltpu.VMEM((n_blk, r, rows, LANES), F32),
            pltpu.VMEM((n_blk, rows, S2), F32),
            pltpu.VMEM((n_blk, r, rows, LANES), F32),
            pltpu.VMEM((n_blk, r, rows, LANES), F32),
            pltpu.VMEM((n_blk, 2, nb, S2 // 2), F32),
        ],
        compiler_params=_params("parallel", "arbitrary"),
        name="s5_scan",
    )(proj, proj, w_in, w_x, w_u, ar, ai, d_skip.reshape(1, W).astype(F32), glu_w.astype(BF16),
      glu_b.reshape(1, W).astype(F32))


def _log_sigmoid(x):
    return jnp.minimum(x, 0.0) - jnp.log1p(jnp.exp(-jnp.abs(x)))


def _split3(x):
    hi = x.astype(BF16)
    r1 = x - hi.astype(F32)
    mid = r1.astype(BF16)
    lo = (r1 - mid.astype(F32)).astype(BF16)
    return hi, mid, lo


def _mlstm_kernel(x_ref, z_ref, cw_ref, cb_ref, wq_ref, wk_ref, wv_ref, wvt_ref, gw_ref, gwt_ref, gbc_ref, gbr_ref,
                  ng_ref, sk_ref, o_ref, xe_scr, qkv_scr, ks_scr, vta_scr, ca_scr, m_scr):
    n_sub, Tc = qkv_scr.shape[0], qkv_scr.shape[1]
    W = x_ref.shape[2]
    H = wq_ref.shape[0]
    dh = W // H
    pad = SUBLANES
    heads = range(H)
    nt = (((1,), (1,)), ((), ()))

    @pl.when(pl.program_id(1) == 0)
    def _():
        xe_scr[0:pad, :] = jnp.zeros((pad, W), F32)
        ca_scr[...] = jnp.zeros_like(ca_scr)
        m_scr[...] = jnp.zeros_like(m_scr)
        vta_scr[:, :, dh:, :] = jnp.ones((n_sub, H, pad, Tc), BF16)

    xe_scr[pad:, :] = x_ref[0].astype(F32)

    ti = lax.broadcasted_iota(jnp.int32, (Tc, Tc), 0)
    si = lax.broadcasted_iota(jnp.int32, (Tc, Tc), 1)
    lower = jnp.where(si <= ti, 1.0, 0.0).astype(BF16)
    allowed = ti <= si
    upper = jnp.where(allowed, 1.0, 0.0).astype(BF16)

    def front(s):
        r0 = s * Tc
        xb = x_ref[0, r0:r0 + Tc, :]
        xc = cb_ref[...] + sum(xe_scr[pl.ds(pad + r0 - (ML_CONV - 1) + j, Tc), :] * cw_ref[j:j + 1, :]
                               for j in range(ML_CONV))
        xc = _silu(xc)
        xcb = xc.astype(BF16)
        for h in heads:
            cols = slice(h * dh, (h + 1) * dh)
            q = jnp.dot(xcb[:, cols], wq_ref[h], preferred_element_type=F32)
            k = jnp.dot(xcb[:, cols], wk_ref[h], preferred_element_type=F32)
            v = jnp.dot(xb[:, cols], wv_ref[h], preferred_element_type=F32)
            qkv_scr[s, :, h * dh:(h + 1) * dh] = q.astype(BF16)
            qkv_scr[s, :, W + h * dh:W + (h + 1) * dh] = k.astype(BF16)
            qkv_scr[s, :, 2 * W + h * dh:2 * W + (h + 1) * dh] = v.astype(BF16)
            vta_scr[s, h, :dh, :] = lax.dot_general(wvt_ref[h], xb[:, cols], nt,
                                                    preferred_element_type=F32).astype(BF16)
            ks_scr[s, :, cols] = (k * (dh ** -0.5)).astype(BF16)
        qkv = qkv_scr[s]
        g_col = jnp.dot(qkv, gw_ref[...], preferred_element_type=F32) + gbc_ref[...]
        g_row = lax.dot_general(gwt_ref[...], qkv, nt, preferred_element_type=F32) + gbr_ref[...]
        b_col = sum(jnp.dot(lower, p, preferred_element_type=F32) for p in _split3(_log_sigmoid(g_col)))
        b_row = sum(jnp.dot(p, upper, preferred_element_type=F32) for p in _split3(_log_sigmoid(g_row)))
        return xc, g_col, g_row, b_col, b_row

    def rec_matmuls(s):
        q_h = [qkv_scr[s, :, h * dh:(h + 1) * dh] for h in heads]
        ks_h = [ks_scr[s, :, h * dh:(h + 1) * dh] for h in heads]
        s_qk = [lax.dot_general(ks_h[h], q_h[h], nt, preferred_element_type=F32) for h in heads]
        q_ca = [lax.dot_general(ca_scr[h].astype(BF16), q_h[h], nt, preferred_element_type=F32)
                for h in heads]
        return ks_h, s_qk, q_ca

    def rec_rest(s, mm, xc, g_col, g_row, b_col, b_row):
        ks_h, s_qk, q_ca = mm
        rows = slice(s * Tc, (s + 1) * Tc)
        m_prev = [m_scr[h][:, 0:1] for h in heads]
        br = [b_row[H + h:H + h + 1, :] for h in heads]
        li = [g_row[h:h + 1, :] for h in heads]
        b_last = [br[h][:, Tc - 1:Tc] for h in heads]
        dmat, sc, floor = [], [], []
        for h in heads:
            c_col = g_col[:, h:h + 1] - b_col[:, H + h:H + h + 1]
            log_d = jnp.where(allowed, br[h] + c_col, NEG_BIG)
            inter = br[h] + m_prev[h]
            m_t = jnp.maximum(inter, jnp.max(log_d, axis=0, keepdims=True))
            dmat.append(jnp.exp(log_d - m_t))
            sc.append(jnp.exp(inter - m_t))
            floor.append(jnp.exp(-m_t))
        pv = [jnp.dot(vta_scr[s, h], (s_qk[h] * dmat[h]).astype(BF16), preferred_element_type=F32) for h in heads]
        for h in heads:
            w_log = b_last[h] - br[h] + li[h]
            m_new = jnp.maximum(b_last[h] + m_prev[h], jnp.max(w_log, axis=1, keepdims=True))
            weighted = (vta_scr[s, h].astype(F32) * jnp.exp(w_log - m_new)).astype(BF16)
            decay = jnp.exp(b_last[h] + m_prev[h] - m_new)
            ca_scr[h] = decay * ca_scr[h] + jnp.dot(weighted, ks_h[h], preferred_element_type=F32)
            m_scr[h] = jnp.broadcast_to(m_new, (1, LANES))
        for h in heads:
            cols = slice(h * dh, (h + 1) * dh)
            tot = sc[h] * q_ca[h] + pv[h]
            hh = tot[:dh] / jnp.maximum(jnp.abs(tot[dh:dh + 1]), floor[h])
            mu = jnp.mean(hh, axis=0, keepdims=True)
            cen = hh - mu
            var = jnp.mean(cen * cen, axis=0, keepdims=True)
            hn = jnp.transpose(cen * lax.rsqrt(var + HEAD_NORM_EPS))
            out = hn * ng_ref[:, cols] + sk_ref[:, cols] * xc[:, cols]
            o_ref[0, rows, cols] = (out * _silu(z_ref[0, rows, cols].astype(F32))).astype(o_ref.dtype)

    ready = front(0)
    for s in range(n_sub):
        mm = rec_matmuls(s)
        following = front(s + 1) if s + 1 < n_sub else None
        rec_rest(s, mm, *ready)
        ready = following
    xe_scr[0:pad, :] = xe_scr[n_sub * Tc:n_sub * Tc + pad, :]


def _mlstm_branch(proj, conv_w, conv_b, wq, wk, wv, gate_w, gate_b, norm_g, skip):
    B, L, N = proj.shape
    W = N // 4
    H = wq.shape[0]
    dh = W // H
    Tc = min(ML_CHUNK, L)
    n_sub = min(ML_SUB, L // Tc)
    rows = n_sub * Tc
    gw = jnp.zeros((3 * W, LANES), F32).at[:, :2 * H].set(gate_w.astype(F32)).astype(BF16)
    gwt = jnp.transpose(gate_w.astype(F32)).astype(BF16)
    gbc = jnp.zeros((1, LANES), F32).at[0, :2 * H].set(gate_b.astype(F32))
    gbr = gate_b.astype(F32).reshape(2 * H, 1)
    c2 = lambda b, c: (0, 0)
    c3 = lambda b, c: (0, 0, 0)
    return pl.pallas_call(
        _mlstm_kernel,
        grid=(B, L // rows),
        in_specs=[
            pl.BlockSpec((1, rows, W), lambda b, c: (b, c, 2)),
            pl.BlockSpec((1, rows, W), lambda b, c: (b, c, 3)),
            pl.BlockSpec((ML_CONV, W), c2),
            pl.BlockSpec((1, W), c2),
            pl.BlockSpec((H, dh, dh), c3),
            pl.BlockSpec((H, dh, dh), c3),
            pl.BlockSpec((H, dh, dh), c3),
            pl.BlockSpec((H, dh, dh), c3),
            pl.BlockSpec((3 * W, LANES), c2),
            pl.BlockSpec((2 * H, 3 * W), c2),
            pl.BlockSpec((1, LANES), c2),
            pl.BlockSpec((2 * H, 1), c2),
            pl.BlockSpec((1, W), c2),
            pl.BlockSpec((1, W), c2),
        ],
        out_specs=pl.BlockSpec((1, rows, W), lambda b, c: (b, c, 0)),
        out_shape=jax.ShapeDtypeStruct((B, L, W), BF16),
        scratch_shapes=[
            pltpu.VMEM((SUBLANES + rows, W), F32),
            pltpu.VMEM((n_sub, Tc, 3 * W), BF16),
            pltpu.VMEM((n_sub, Tc, W), BF16),
            pltpu.VMEM((n_sub, H, dh + SUBLANES, Tc), BF16),
            pltpu.VMEM((H, dh + SUBLANES, dh), F32),
            pltpu.VMEM((H, 1, LANES), F32),
        ],
        compiler_params=_params("parallel", "arbitrary"),
        name="mlstm",
    )(proj, proj, conv_w.astype(F32), conv_b.reshape(1, W).astype(F32), wq.astype(BF16), wk.astype(BF16),
      wv.astype(BF16), jnp.transpose(wv, (0, 2, 1)).astype(BF16), gw, gwt, gbc, gbr, norm_g.reshape(1, W).astype(F32), skip.reshape(1, W).astype(F32))


def _attn_kernel(q_ref, k_ref, vt_ref, z_ref, bias_ref, lam_ref, ng_ref, o_ref, m_scr, acc_scr, *,
                 mode, out_scale, blk, bias_index):
    L = q_ref.shape[2]
    n = L // blk
    lane = lax.broadcasted_iota(jnp.int32, (1, LANES), 1)
    map0 = lane < ATTN_HEAD_DIM
    feat = lax.broadcasted_iota(jnp.int32, (LANES, 1), 0)
    ones_rows = jnp.ones((SUBLANES, blk), BF16)

    def stacked_q(qi):
        q = q_ref[0, 0, qi * blk:(qi + 1) * blk, :]
        zero = jnp.zeros_like(q)
        return jnp.concatenate([jnp.where(map0, q, zero), jnp.where(map0, zero, q)], axis=0)

    def scores(qi, j):
        s = lax.dot_general(k_ref[0, 0, j * blk:(j + 1) * blk, :], q2[qi], (((1,), (1,)), ((), ())),
                            preferred_element_type=F32)
        e = bias_index(qi - j)
        return s if e is None else s + bias_ref[e]

    def update(qi, j, s):
        m_cur = jnp.max(s, axis=0, keepdims=True)
        m_next = m_cur if j == 0 else jnp.maximum(m_scr[qi], m_cur)
        p = jnp.exp2(s - m_next).astype(BF16)
        pv = jnp.dot(jnp.concatenate([vt_ref[0, 0, :, j * blk:(j + 1) * blk], ones_rows], axis=0), p,
                     preferred_element_type=F32)
        if j == 0:
            acc_scr[qi] = pv
        else:
            acc_scr[qi] = jnp.exp2(m_scr[qi] - m_next) * acc_scr[qi] + pv
        m_scr[qi] = m_next

    def finish(qi):
        rows = slice(qi * blk, (qi + 1) * blk)
        acc = acc_scr[qi]
        o = acc[:LANES] / acc[LANES:LANES + 1]
        o0, o1 = o[:, :blk], o[:, blk:]
        if mode == "dilated":
            out = jnp.transpose(jnp.where(feat < ATTN_HEAD_DIM, o0, o1))
        else:
            out = jnp.transpose(o0 - lam_ref[:, 0:1] * o1)
            ms = jnp.mean(out * out, axis=-1, keepdims=True)
            out = out * lax.rsqrt(ms + HEAD_NORM_EPS) * ng_ref[...] * out_scale
        o_ref[0, rows, :] = (out * _silu(z_ref[0, 0, rows, :].astype(F32))).astype(o_ref.dtype)

    steps = []
    for a in range(n // 2):
        b = n - 1 - a
        ja = 0
        for jb in range(b + 1):
            steps.append((b, jb))
            if ja <= a and (jb + 1) * (a + 1) >= (ja + 1) * (b + 1):
                steps.append((a, ja))
                ja += 1
    q2 = {qi: stacked_q(qi) for qi in range(n)}
    pending = {}
    for t in range(min(ATTN_LOOKAHEAD, len(steps))):
        pending[t] = scores(*steps[t])
    for t, (qi, j) in enumerate(steps):
        if t + ATTN_LOOKAHEAD < len(steps):
            pending[t + ATTN_LOOKAHEAD] = scores(*steps[t + ATTN_LOOKAHEAD])
        update(qi, j, pending.pop(t))
        if j == qi:
            finish(qi)


def _dilated_multiplicity(delta):
    mult = np.zeros(delta.shape, np.int64)
    for window, dil in DIL_PAIRS:
        mult += (delta >= 0) & (delta % dil == 0) & (delta <= window)
    return mult


def _bias_tables(L, blk, mode):
    n = L // blk
    key = np.arange(blk)[:, None]
    qry = np.arange(blk)[None, :]
    tables, index = [], {}
    for d in range(n):
        delta = d * blk + qry - key
        mult = _dilated_multiplicity(delta) if mode == "dilated" else (delta >= 0).astype(np.int64)
        t = np.where(mult > 0, np.log2(np.maximum(mult, 1)), NEG_BIG).astype(np.float32)
        if not t.any():
            index[d] = None
            continue
        for e, other in enumerate(tables):
            if np.array_equal(t, other):
                index[d] = e
                break
        else:
            index[d] = len(tables)
            tables.append(t)
    stacked = np.stack(tables)
    return jnp.asarray(np.concatenate([stacked, stacked], axis=2)), index


def _attention(qkz, vt, q0, k0, v0, z0, n_blocks, blk, lam, norm_g, mode, out_scale):
    B, _, L, _ = qkz.shape
    bias, index = _bias_tables(L, blk, mode)
    n_q = L // blk
    assert n_q % 2 == 0
    kern = functools.partial(_attn_kernel, mode=mode, out_scale=out_scale, blk=blk, bias_index=index.get)
    c2 = lambda b, n: (0, 0)
    return pl.pallas_call(
        kern,
        grid=(B, n_blocks),
        in_specs=[
            pl.BlockSpec((1, 1, L, LANES), lambda b, n: (b, q0 + n, 0, 0)),
            pl.BlockSpec((1, 1, L, LANES), lambda b, n: (b, k0 + n, 0, 0)),
            pl.BlockSpec((1, 1, LANES, L), lambda b, n: (b, v0 + n, 0, 0)),
            pl.BlockSpec((1, 1, L, LANES), lambda b, n: (b, z0 + n, 0, 0)),
            pl.BlockSpec(bias.shape, lambda b, n: (0, 0, 0)),
            pl.BlockSpec((1, LANES), c2),
            pl.BlockSpec((1, LANES), c2),
        ],
        out_specs=pl.BlockSpec((1, L, LANES), lambda b, n: (b, 0, n)),
        out_shape=jax.ShapeDtypeStruct((B, L, n_blocks * LANES), BF16),
        scratch_shapes=[
            pltpu.VMEM((n_q, 1, 2 * blk), F32),
            pltpu.VMEM((n_q, LANES + SUBLANES, 2 * blk), F32),
        ],
        compiler_params=_params("parallel", "parallel"),
        name="attn_" + mode,
    )(qkz, qkz, vt, qkz, bias, lam, norm_g)


def _rope_tables(L):
    dh = ATTN_HEAD_DIM
    inv = ROPE_THETA ** (-jnp.arange(0, dh, 2, dtype=F32) / dh)
    ang = jnp.arange(L, dtype=F32)[:, None] * inv[None, :]
    cos, sin = jnp.cos(ang), jnp.sin(ang)
    reps = LANES // dh
    cos_t = jnp.tile(jnp.concatenate([cos, cos], axis=-1), (1, reps))
    sin_t = jnp.tile(jnp.concatenate([-sin, sin], axis=-1), (1, reps))
    return cos_t, sin_t


def _even_layer(h, pre_g, w_in, s5_params, ml_params):
    proj = _norm_proj(h, pre_g, w_in.astype(BF16))
    return _s5_branch(proj, *s5_params), _mlstm_branch(proj, *ml_params)


def _odd_layer(prev, h, pre_g, w_in, lq1, lk1, lq2, lk2, diff_norm, layer_idx):
    B, L, D = h.shape
    nblk = PROJ_COLS // LANES
    cos_t, sin_t = _rope_tables(L)
    h, qkz, vt = _out_norm_proj_rope(*prev, h, pre_g, w_in.astype(BF16), cos_t, sin_t)
    blk = min(ATTN_BLOCK, L)
    lam_init = 0.8 - 0.6 * math.exp(-0.3 * layer_idx)
    lam = (jnp.exp(jnp.sum(lq1.astype(F32) * lk1.astype(F32)))
           - jnp.exp(jnp.sum(lq2.astype(F32) * lk2.astype(F32))) + lam_init)
    lam_row = jnp.full((1, LANES), lam, F32)
    ones_row = jnp.ones((1, LANES), F32)
    base = [b for _, b in ODD_CHUNKS]
    c_out = _attention(qkz, vt, base[0], base[1], base[2], base[3], nblk, blk, ones_row, ones_row, "dilated", 1.0)
    d_out = _attention(qkz, vt, base[4], base[5], base[6], base[7], nblk, blk, lam_row,
                       diff_norm.reshape(1, LANES).astype(F32), "diff", 1.0 - lam_init)
    return h, c_out, d_out


def kernel(x, pre_norm, post_norm, w_in_ab, s5_lambda_re, s5_lambda_im, s5_log_dt, s5_b_re, s5_b_im,
           s5_c_re, s5_c_im, s5_d, s5_glu_w, s5_glu_b, ml_conv_w, ml_conv_b, ml_wq, ml_wk, ml_wv,
           ml_gate_w, ml_gate_b, ml_norm, ml_skip, w_out_ab, w_in_cd, diff_lq1, diff_lk1, diff_lq2,
           diff_lk2, diff_norm, w_out_cd):
    depth = pre_norm.shape[0]
    h = x
    prev = None
    for l in range(depth):
        i = l // 2
        if l % 2 == 0:
            if prev is not None:
                h = _out_proj(*prev, h)
            s5_params = (s5_lambda_re[i], s5_lambda_im[i], s5_log_dt[i], s5_b_re[i], s5_b_im[i],
                         s5_c_re[i], s5_c_im[i], s5_d[i], s5_glu_w[i], s5_glu_b[i])
            ml_params = (ml_conv_w[i], ml_conv_b[i], ml_wq[i], ml_wk[i], ml_wv[i], ml_gate_w[i],
                         ml_gate_b[i], ml_norm[i], ml_skip[i])
            a, b = _even_layer(h, pre_norm[l], w_in_ab[i], s5_params, ml_params)
            prev = (a, b, w_out_ab[i], post_norm[l])
        else:
            h, c, d = _odd_layer(prev, h, pre_norm[l], w_in_cd[i], diff_lq1[i], diff_lk1[i], diff_lq2[i],
                                 diff_lk2[i], diff_norm[i], l)
            prev = (c, d, w_out_cd[i], post_norm[l])
    return _out_proj(*prev, h)
```

```python
import functools
import math

import numpy as np
import jax
import jax.numpy as jnp
from jax import lax
from jax.experimental import pallas as pl
from jax.experimental.pallas import tpu as pltpu

F32 = jnp.float32
BF16 = jnp.bfloat16

S5_GROUP = 16
S5_STATE = 64
ML_HEADS = 4
ML_CONV = 4
DIL_PAIRS = ((128, 1), (512, 4), (2048, 16))
ROPE_THETA = 10000.0
NORM_EPS = 1e-6
HEAD_NORM_EPS = 1e-5
ATTN_HEAD_DIM = 64
QUERY_SCALE = ATTN_HEAD_DIM ** -0.5 * math.log2(math.e)

LANES = 128
SUBLANES = 8
VMEM_LIMIT_BYTES = 56 * 1024 * 1024

NEG_BIG = -1e30

PROJ_ROWS = 1024
OUT_PROJ_ROWS = 2048
PROJ_COLS = 512
S5_BATCH = SUBLANES
S5_TIME = 128
S5_STRIDE = 2
S5_PIECE = 32
ML_CHUNK = 256
ML_SUB = 4
ATTN_BLOCK = 256
ATTN_LOOKAHEAD = 3


def _params(*semantics):
    return pltpu.CompilerParams(dimension_semantics=semantics, vmem_limit_bytes=VMEM_LIMIT_BYTES)


def _silu(x):
    return x * jax.nn.sigmoid(x)


def _norm_proj_kernel(x_ref, g_ref, w_ref, o_ref, z_scr):
    x = x_ref[0]
    ms = jnp.mean(x * x, axis=-1, keepdims=True)
    z_scr[...] = (x * lax.rsqrt(ms + NORM_EPS) * g_ref[...]).astype(BF16)
    n_out = w_ref.shape[1]
    for c in range(n_out // PROJ_COLS):
        cols = slice(c * PROJ_COLS, (c + 1) * PROJ_COLS)
        o_ref[0, :, cols] = jnp.dot(z_scr[...], w_ref[:, cols], preferred_element_type=F32).astype(o_ref.dtype)


def _norm_proj(x, g, w):
    B, L, D = x.shape
    N = w.shape[1]
    tm = min(PROJ_ROWS, L)
    return pl.pallas_call(
        _norm_proj_kernel,
        grid=(B, L // tm),
        in_specs=[
            pl.BlockSpec((1, tm, D), lambda b, i: (b, i, 0)),
            pl.BlockSpec((1, D), lambda b, i: (0, 0)),
            pl.BlockSpec((D, N), lambda b, i: (0, 0)),
        ],
        out_specs=pl.BlockSpec((1, tm, N), lambda b, i: (b, i, 0)),
        out_shape=jax.ShapeDtypeStruct((B, L, N), BF16),
        scratch_shapes=[pltpu.VMEM((tm, D), BF16)],
        compiler_params=_params("parallel", "parallel"),
        name="norm_proj_even",
    )(x, g.reshape(1, D), w)


def _swap_halves(x, first_half):
    return jnp.where(first_half, pltpu.roll(x, LANES - ATTN_HEAD_DIM // 2, 1), pltpu.roll(x, ATTN_HEAD_DIM // 2, 1))


ODD_CHUNKS = (("q", 0), ("k", 4), ("v", 0), ("z", 8), ("q", 12), ("k", 16), ("v", 4), ("z", 20))


def _out_norm_proj_rope_kernel(a_ref, b_ref, wa_ref, wb_ref, gpost_ref, h_ref, g_ref, w_ref, cos_ref, sin_ref,
                               hout_ref, o_ref, vt_ref, z_scr):
    y = jnp.dot(a_ref[0], wa_ref[...], preferred_element_type=F32)
    y = y + jnp.dot(b_ref[0], wb_ref[...], preferred_element_type=F32)
    ms = jnp.mean(y * y, axis=-1, keepdims=True)
    x = h_ref[0] + y * lax.rsqrt(ms + NORM_EPS) * gpost_ref[...]
    hout_ref[0] = x
    ms = jnp.mean(x * x, axis=-1, keepdims=True)
    z_scr[...] = (x * lax.rsqrt(ms + NORM_EPS) * g_ref[...]).astype(BF16)
    slabs = PROJ_COLS // LANES
    lane = lax.broadcasted_iota(jnp.int32, (1, LANES), 1)
    first_half = (lane % ATTN_HEAD_DIM) < (ATTN_HEAD_DIM // 2)
    for c, (kind, base) in enumerate(ODD_CHUNKS):
        y = jnp.dot(z_scr[...], w_ref[:, c * PROJ_COLS:(c + 1) * PROJ_COLS], preferred_element_type=F32)
        for s in range(slabs):
            t = y[:, s * LANES:(s + 1) * LANES]
            if kind in ("q", "k"):
                t = t * cos_ref[...] + _swap_halves(t, first_half) * sin_ref[...]
            if kind == "q":
                t = t * QUERY_SCALE
            if kind == "v":
                vt_ref[0, base + s] = jnp.transpose(t).astype(vt_ref.dtype)
            else:
                o_ref[0, base + s] = t.astype(o_ref.dtype)


def _out_norm_proj_rope(a, b, w_out, g_post, h, g, w, cos_t, sin_t):
    B, L, D = h.shape
    Wb = a.shape[2]
    N = w.shape[1]
    assert N == len(ODD_CHUNKS) * PROJ_COLS
    tm = min(PROJ_ROWS, L)
    n_v = sum(kind == "v" for kind, _ in ODD_CHUNKS) * (PROJ_COLS // LANES)
    n_o = N // LANES - n_v
    row = lambda b_, i: (b_, i, 0)
    const = lambda b_, i: (0, 0)
    return pl.pallas_call(
        _out_norm_proj_rope_kernel,
        grid=(B, L // tm),
        in_specs=[
            pl.BlockSpec((1, tm, Wb), row),
            pl.BlockSpec((1, tm, Wb), row),
            pl.BlockSpec((Wb, D), const),
            pl.BlockSpec((Wb, D), const),
            pl.BlockSpec((1, D), const),
            pl.BlockSpec((1, tm, D), row),
            pl.BlockSpec((1, D), const),
            pl.BlockSpec((D, N), const),
            pl.BlockSpec((tm, LANES), lambda b_, i: (i, 0)),
            pl.BlockSpec((tm, LANES), lambda b_, i: (i, 0)),
        ],
        out_specs=[
            pl.BlockSpec((1, tm, D), row),
            pl.BlockSpec((1, n_o, tm, LANES), lambda b_, i: (b_, 0, i, 0)),
            pl.BlockSpec((1, n_v, LANES, tm), lambda b_, i: (b_, 0, 0, i)),
        ],
        out_shape=[
            jax.ShapeDtypeStruct((B, L, D), F32),
            jax.ShapeDtypeStruct((B, n_o, L, LANES), BF16),
            jax.ShapeDtypeStruct((B, n_v, LANES, L), BF16),
        ],
        scratch_shapes=[pltpu.VMEM((tm, D), BF16)],
        compiler_params=_params("parallel", "parallel"),
        name="out_norm_proj_odd",
    )(a, b, w_out[:Wb].astype(BF16), w_out[Wb:].astype(BF16), g_post.reshape(1, D), h, g.reshape(1, D), w,
      cos_t, sin_t)


def _out_proj_kernel(a_ref, b_ref, wa_ref, wb_ref, g_ref, h_ref, o_ref):
    y = jnp.dot(a_ref[0], wa_ref[...], preferred_element_type=F32)
    y = y + jnp.dot(b_ref[0], wb_ref[...], preferred_element_type=F32)
    ms = jnp.mean(y * y, axis=-1, keepdims=True)
    o_ref[0] = h_ref[0] + y * lax.rsqrt(ms + NORM_EPS) * g_ref[...]


def _out_proj(a, b, w_out, g, h):
    B, L, W = a.shape
    D = w_out.shape[1]
    tm = min(OUT_PROJ_ROWS, L)
    return pl.pallas_call(
        _out_proj_kernel,
        grid=(B, L // tm),
        in_specs=[
            pl.BlockSpec((1, tm, W), lambda b_, i: (b_, i, 0)),
            pl.BlockSpec((1, tm, W), lambda b_, i: (b_, i, 0)),
            pl.BlockSpec((W, D), lambda b_, i: (0, 0)),
            pl.BlockSpec((W, D), lambda b_, i: (0, 0)),
            pl.BlockSpec((1, D), lambda b_, i: (0, 0)),
            pl.BlockSpec((1, tm, D), lambda b_, i: (b_, i, 0)),
        ],
        out_specs=pl.BlockSpec((1, tm, D), lambda b_, i: (b_, i, 0)),
        out_shape=jax.ShapeDtypeStruct((B, L, D), F32),
        compiler_params=_params("parallel", "parallel"),
        name="out_proj",
    )(a, b, w_out[:W].astype(BF16), w_out[W:].astype(BF16), g.reshape(1, D), h)


def _s5_kernel(u_ref, z_ref, bin_ref, wx_ref, wu_ref, ar_ref, ai_ref, d_ref, gw_ref, gb_ref, o_ref,
               tmp_scr, us_scr, zs_scr, xs_scr, ys_scr, os_scr, st_scr):
    nb, T, W = u_ref.shape
    n_blk = W // LANES
    r = S5_STRIDE
    n_grp = T // r
    half = xs_scr.shape[2] // 2
    n_t = n_grp // S5_PIECE
    piece_rows = S5_PIECE * nb

    @pl.when(pl.program_id(1) == 0)
    def _():
        st_scr[...] = jnp.zeros_like(st_scr)

    for b in range(nb):
        for src, dst in ((u_ref, us_scr), (z_ref, zs_scr)):
            xb = src[b].astype(F32)
            for k in range(n_blk):
                tmp_scr[k] = xb[:, k * LANES:(k + 1) * LANES]
            for k in range(n_blk):
                for j in range(r):
                    dst[k, j, pl.ds(b, n_grp, stride=nb), :] = tmp_scr[k, pl.ds(j, n_grp, stride=r), :]

    def rows_of(i):
        return slice(i * piece_rows, (i + 1) * piece_rows)

    def group_inputs(i, k):
        return jnp.concatenate([us_scr[k, j, rows_of(i), :] for j in range(r)], axis=1).astype(BF16)

    def input_matmul(i, k):
        xs_scr[k, rows_of(i), :] = jnp.dot(group_inputs(i, k), bin_ref[k], preferred_element_type=F32)

    def scan(i, k):
        ar, ai = ar_ref[k], ai_ref[k]
        xr, xi = st_scr[k, 0], st_scr[k, 1]
        for t in range(S5_PIECE):
            rs = slice(i * piece_rows + t * nb, i * piece_rows + (t + 1) * nb)
            cr, ci = xs_scr[k, rs, 0:half], xs_scr[k, rs, half:2 * half]
            xs_scr[k, rs, 0:half] = xr
            xs_scr[k, rs, half:2 * half] = xi
            xr, xi = ar * xr - ai * xi + cr, ar * xi + ai * xr + ci
        st_scr[k, 0] = xr
        st_scr[k, 1] = xi

    def output_matmul(i, k):
        y = jnp.dot(xs_scr[k, rows_of(i), :].astype(BF16), wx_ref[k], preferred_element_type=F32)
        y = y + jnp.dot(group_inputs(i, k), wu_ref[k], preferred_element_type=F32)
        for j in range(r):
            ys_scr[k, j, rows_of(i), :] = y[:, j * LANES:(j + 1) * LANES]

    def glu(i):
        def tokens(ref):
            return jnp.concatenate(
                [jnp.concatenate([ref[k, j, rows_of(i), :] for k in range(n_blk)], axis=1) for j in range(r)], axis=0)
        y = tokens(ys_scr) + d_ref[...] * tokens(us_scr)
        g = jax.nn.gelu(y)
        gate = jnp.dot(g.astype(BF16), gw_ref[...], preferred_element_type=F32) + gb_ref[...]
        out = g * jax.nn.sigmoid(gate) * _silu(tokens(zs_scr))
        for j in range(r):
            for k in range(n_blk):
                os_scr[k, j, rows_of(i), :] = out[j * piece_rows:(j + 1) * piece_rows, k * LANES:(k + 1) * LANES]

    def drain(p):
        i, k = pieces[p]
        output_matmul(i, k)
        if k == n_blk - 1:
            glu(i)

    pieces = [(i, k) for i in range(n_t) for k in range(n_blk)]
    input_matmul(*pieces[0])
    for p, (i, k) in enumerate(pieces):
        if p + 1 < len(pieces):
            input_matmul(*pieces[p + 1])
        if p >= 1:
            drain(p - 1)
        scan(i, k)
    drain(len(pieces) - 1)

    for b in range(nb):
        for k in range(n_blk):
            for j in range(r):
                tmp_scr[k, pl.ds(j, n_grp, stride=r), :] = os_scr[k, j, pl.ds(b, n_grp, stride=nb), :]
        o_ref[b] = jnp.concatenate([tmp_scr[k] for k in range(n_blk)], axis=1).astype(o_ref.dtype)


def _s5_weights(lam_re, lam_im, log_dt, b_re, b_im, c_re, c_im):
    G, P = lam_re.shape
    r = S5_STRIDE
    gpb = LANES // S5_GROUP
    n_blk = G // gpb
    lr, li = lam_re.astype(F32), lam_im.astype(F32)
    dt = jnp.exp(log_dt.astype(F32))[:, None]
    mag = jnp.exp(lr * dt)
    ab_r, ab_i = mag * jnp.cos(li * dt), mag * jnp.sin(li * dt)
    den = lr * lr + li * li
    f_r = ((ab_r - 1.0) * lr + ab_i * li) / den
    f_i = (ab_i * lr - (ab_r - 1.0) * li) / den
    br, bi = b_re.astype(F32), b_im.astype(F32)
    bb_r = f_r[..., None] * br - f_i[..., None] * bi
    bb_i = f_r[..., None] * bi + f_i[..., None] * br
    cr, ci = c_re.astype(F32), c_im.astype(F32)
    eye = jnp.eye(gpb, dtype=F32)

    pows = [(jnp.ones_like(ab_r), jnp.zeros_like(ab_i))]
    for _ in range(r):
        pr, pi = pows[-1]
        pows.append((pr * ab_r - pi * ab_i, pr * ab_i + pi * ab_r))

    def pack_in(xr, xi):
        x = jnp.stack([xr, xi]).reshape(2, n_blk, gpb, P, S5_GROUP)
        x = jnp.transpose(x, (1, 2, 4, 0, 3))
        x = x[:, :, :, :, None, :] * eye[None, :, None, None, :, None]
        return x.reshape(n_blk, LANES, 2 * gpb * P)

    def pack_out(er, ei):
        x = jnp.stack([er, ei]).reshape(2, n_blk, gpb, S5_GROUP, P)
        x = jnp.transpose(x, (1, 0, 2, 4, 3))
        x = x[:, :, :, :, None, :] * eye[None, None, :, None, :, None]
        return x.reshape(n_blk, 2 * gpb * P, LANES)

    def pack_direct(kq):
        x = jnp.transpose(kq.reshape(n_blk, gpb, S5_GROUP, S5_GROUP), (0, 1, 3, 2))
        x = x[:, :, :, None, :] * eye[None, :, None, :, None]
        return x.reshape(n_blk, LANES, LANES)

    w_in = []
    for j in range(r):
        pr, pi = pows[r - 1 - j]
        w_in.append(pack_in(pr[..., None] * bb_r - pi[..., None] * bb_i, pr[..., None] * bb_i + pi[..., None] * bb_r))
    w_in = jnp.concatenate(w_in, axis=1).astype(BF16)

    w_x = []
    for j in range(r):
        pr, pi = pows[j + 1]
        er = cr * pr[:, None, :] - ci * pi[:, None, :]
        ei = cr * pi[:, None, :] + ci * pr[:, None, :]
        w_x.append(pack_out(er, -ei))
    w_x = jnp.concatenate(w_x, axis=2).astype(BF16)

    direct = []
    for q in range(r):
        pr, pi = pows[q]
        er = cr * pr[:, None, :] - ci * pi[:, None, :]
        ei = cr * pi[:, None, :] + ci * pr[:, None, :]
        direct.append(pack_direct(jnp.einsum('gop,gpi->goi', er, bb_r) - jnp.einsum('gop,gpi->goi', ei, bb_i)))
    zero = jnp.zeros_like(direct[0])
    w_u = jnp.concatenate(
        [jnp.concatenate([direct[j - i] if i <= j else zero for j in range(r)], axis=2) for i in range(r)],
        axis=1).astype(BF16)

    ar_r, ai_r = pows[r]
    ar = jnp.broadcast_to(ar_r.reshape(n_blk, 1, gpb * P), (n_blk, S5_BATCH, gpb * P))
    ai = jnp.broadcast_to(ai_r.reshape(n_blk, 1, gpb * P), (n_blk, S5_BATCH, gpb * P))
    return w_in, w_x, w_u, ar, ai


def _s5_branch(proj, lam_re, lam_im, log_dt, b_re, b_im, c_re, c_im, d_skip, glu_w, glu_b):
    B, L, N = proj.shape
    W = N // 4
    nb = S5_BATCH
    T = min(S5_TIME, L)
    r = S5_STRIDE
    n_blk = W // LANES
    w_in, w_x, w_u, ar, ai = _s5_weights(lam_re, lam_im, log_dt, b_re, b_im, c_re, c_im)
    S2 = w_in.shape[2]
    rows = nb * T // r
    const3 = lambda g, c: (0, 0, 0)
    const2 = lambda g, c: (0, 0)
    return pl.pallas_call(
        _s5_kernel,
        grid=(B // nb, L // T),
        in_specs=[
            pl.BlockSpec((nb, T, W), lambda g, c: (g, c, 0)),
            pl.BlockSpec((nb, T, W), lambda g, c: (g, c, 1)),
            pl.BlockSpec(w_in.shape, const3),
            pl.BlockSpec(w_x.shape, const3),
            pl.BlockSpec(w_u.shape, const3),
            pl.BlockSpec(ar.shape, const3),
            pl.BlockSpec(ai.shape, const3),
            pl.BlockSpec((1, W), const2),
            pl.BlockSpec((W, W), const2),
            pl.BlockSpec((1, W), const2),
        ],
        out_specs=pl.BlockSpec((nb, T, W), lambda g, c: (g, c, 0)),
        out_shape=jax.ShapeDtypeStruct((B, L, W), BF16),
        scratch_shapes=[
            pltpu.VMEM((n_blk, T, LANES), F32),
            pltpu.VMEM((n_blk, r, rows, LANES), F32),
            pltpu.VMEM((n_blk, r, rows, LANES), F32),
            pltpu.VMEM((n_blk, rows, S2), F32),
            pltpu.VMEM((n_blk, r, rows, LANES), F32),
            pltpu.VMEM((n_blk, r, rows, LANES), F32),
            pltpu.VMEM((n_blk, 2, nb, S2 // 2), F32),
        ],
        compiler_params=_params("parallel", "arbitrary"),
        name="s5_scan",
    )(proj, proj, w_in, w_x, w_u, ar, ai, d_skip.reshape(1, W).astype(F32), glu_w.astype(BF16),
      glu_b.reshape(1, W).astype(F32))


def _log_sigmoid(x):
    return jnp.minimum(x, 0.0) - jnp.log1p(jnp.exp(-jnp.abs(x)))


def _split3(x):
    hi = x.astype(BF16)
    r1 = x - hi.astype(F32)
    mid = r1.astype(BF16)
    lo = (r1 - mid.astype(F32)).astype(BF16)
    return hi, mid, lo


def _mlstm_kernel(x_ref, z_ref, cw_ref, cb_ref, wq_ref, wk_ref, wv_ref, wvt_ref, gw_ref, gwt_ref, gbc_ref, gbr_ref,
                  ng_ref, sk_ref, o_ref, xe_scr, qkv_scr, ks_scr, vta_scr, ca_scr, m_scr):
    n_sub, Tc = qkv_scr.shape[0], qkv_scr.shape[1]
    W = x_ref.shape[2]
    H = wq_ref.shape[0]
    dh = W // H
    pad = SUBLANES
    heads = range(H)
    nt = (((1,), (1,)), ((), ()))

    @pl.when(pl.program_id(1) == 0)
    def _():
        xe_scr[0:pad, :] = jnp.zeros((pad, W), F32)
        ca_scr[...] = jnp.zeros_like(ca_scr)
        m_scr[...] = jnp.zeros_like(m_scr)
        vta_scr[:, :, dh:, :] = jnp.ones((n_sub, H, pad, Tc), BF16)

    xe_scr[pad:, :] = x_ref[0].astype(F32)

    ti = lax.broadcasted_iota(jnp.int32, (Tc, Tc), 0)
    si = lax.broadcasted_iota(jnp.int32, (Tc, Tc), 1)
    lower = jnp.where(si <= ti, 1.0, 0.0).astype(BF16)
    allowed = ti <= si
    upper = jnp.where(allowed, 1.0, 0.0).astype(BF16)

    def front(s):
        r0 = s * Tc
        xb = x_ref[0, r0:r0 + Tc, :]
        xc = cb_ref[...] + sum(xe_scr[pl.ds(pad + r0 - (ML_CONV - 1) + j, Tc), :] * cw_ref[j:j + 1, :]
                               for j in range(ML_CONV))
        xc = _silu(xc)
        xcb = xc.astype(BF16)
        for h in heads:
            cols = slice(h * dh, (h + 1) * dh)
            q = jnp.dot(xcb[:, cols], wq_ref[h], preferred_element_type=F32)
            k = jnp.dot(xcb[:, cols], wk_ref[h], preferred_element_type=F32)
            v = jnp.dot(xb[:, cols], wv_ref[h], preferred_element_type=F32)
            qkv_scr[s, :, h * dh:(h + 1) * dh] = q.astype(BF16)
            qkv_scr[s, :, W + h * dh:W + (h + 1) * dh] = k.astype(BF16)
            qkv_scr[s, :, 2 * W + h * dh:2 * W + (h + 1) * dh] = v.astype(BF16)
            vta_scr[s, h, :dh, :] = lax.dot_general(wvt_ref[h], xb[:, cols], nt,
                                                    preferred_element_type=F32).astype(BF16)
            ks_scr[s, :, cols] = (k * (dh ** -0.5)).astype(BF16)
        qkv = qkv_scr[s]
        g_col = jnp.dot(qkv, gw_ref[...], preferred_element_type=F32) + gbc_ref[...]
        g_row = lax.dot_general(gwt_ref[...], qkv, nt, preferred_element_type=F32) + gbr_ref[...]
        b_col = sum(jnp.dot(lower, p, preferred_element_type=F32) for p in _split3(_log_sigmoid(g_col)))
        b_row = sum(jnp.dot(p, upper, preferred_element_type=F32) for p in _split3(_log_sigmoid(g_row)))
        return xc, g_col, g_row, b_col, b_row

    def rec_matmuls(s):
        q_h = [qkv_scr[s, :, h * dh:(h + 1) * dh] for h in heads]
        ks_h = [ks_scr[s, :, h * dh:(h + 1) * dh] for h in heads]
        s_qk = [lax.dot_general(ks_h[h], q_h[h], nt, preferred_element_type=F32) for h in heads]
        q_ca = [lax.dot_general(ca_scr[h].astype(BF16), q_h[h], nt, preferred_element_type=F32)
                for h in heads]
        return ks_h, s_qk, q_ca

    def rec_rest(s, mm, xc, g_col, g_row, b_col, b_row):
        ks_h, s_qk, q_ca = mm
        rows = slice(s * Tc, (s + 1) * Tc)
        m_prev = [m_scr[h][:, 0:1] for h in heads]
        br = [b_row[H + h:H + h + 1, :] for h in heads]
        li = [g_row[h:h + 1, :] for h in heads]
        b_last = [br[h][:, Tc - 1:Tc] for h in heads]
        dmat, sc, floor = [], [], []
        for h in heads:
            c_col = g_col[:, h:h + 1] - b_col[:, H + h:H + h + 1]
            log_d = jnp.where(allowed, br[h] + c_col, NEG_BIG)
            inter = br[h] + m_prev[h]
            m_t = jnp.maximum(inter, jnp.max(log_d, axis=0, keepdims=True))
            dmat.append(jnp.exp(log_d - m_t))
            sc.append(jnp.exp(inter - m_t))
            floor.append(jnp.exp(-m_t))
        pv = [jnp.dot(vta_scr[s, h], (s_qk[h] * dmat[h]).astype(BF16), preferred_element_type=F32) for h in heads]
        for h in heads:
            w_log = b_last[h] - br[h] + li[h]
            m_new = jnp.maximum(b_last[h] + m_prev[h], jnp.max(w_log, axis=1, keepdims=True))
            weighted = (vta_scr[s, h].astype(F32) * jnp.exp(w_log - m_new)).astype(BF16)
            decay = jnp.exp(b_last[h] + m_prev[h] - m_new)
            ca_scr[h] = decay * ca_scr[h] + jnp.dot(weighted, ks_h[h], preferred_element_type=F32)
            m_scr[h] = jnp.broadcast_to(m_new, (1, LANES))
        for h in heads:
            cols = slice(h * dh, (h + 1) * dh)
            tot = sc[h] * q_ca[h] + pv[h]
            hh = tot[:dh] / jnp.maximum(jnp.abs(tot[dh:dh + 1]), floor[h])
            mu = jnp.mean(hh, axis=0, keepdims=True)
            cen = hh - mu
            var = jnp.mean(cen * cen, axis=0, keepdims=True)
            hn = jnp.transpose(cen * lax.rsqrt(var + HEAD_NORM_EPS))
            out = hn * ng_ref[:, cols] + sk_ref[:, cols] * xc[:, cols]
            o_ref[0, rows, cols] = (out * _silu(z_ref[0, rows, cols].astype(F32))).astype(o_ref.dtype)

    ready = front(0)
    for s in range(n_sub):
        mm = rec_matmuls(s)
        following = front(s + 1) if s + 1 < n_sub else None
        rec_rest(s, mm, *ready)
        ready = following
    xe_scr[0:pad, :] = xe_scr[n_sub * Tc:n_sub * Tc + pad, :]


def _mlstm_branch(proj, conv_w, conv_b, wq, wk, wv, gate_w, gate_b, norm_g, skip):
    B, L, N = proj.shape
    W = N // 4
    H = wq.shape[0]
    dh = W // H
    Tc = min(ML_CHUNK, L)
    n_sub = min(ML_SUB, L // Tc)
    rows = n_sub * Tc
    gw = jnp.zeros((3 * W, LANES), F32).at[:, :2 * H].set(gate_w.astype(F32)).astype(BF16)
    gwt = jnp.transpose(gate_w.astype(F32)).astype(BF16)
    gbc = jnp.zeros((1, LANES), F32).at[0, :2 * H].set(gate_b.astype(F32))
    gbr = gate_b.astype(F32).reshape(2 * H, 1)
    c2 = lambda b, c: (0, 0)
    c3 = lambda b, c: (0, 0, 0)
    return pl.pallas_call(
        _mlstm_kernel,
        grid=(B, L // rows),
        in_specs=[
            pl.BlockSpec((1, rows, W), lambda b, c: (b, c, 2)),
            pl.BlockSpec((1, rows, W), lambda b, c: (b, c, 3)),
            pl.BlockSpec((ML_CONV, W), c2),
            pl.BlockSpec((1, W), c2),
            pl.BlockSpec((H, dh, dh), c3),
            pl.BlockSpec((H, dh, dh), c3),
            pl.BlockSpec((H, dh, dh), c3),
            pl.BlockSpec((H, dh, dh), c3),
            pl.BlockSpec((3 * W, LANES), c2),
            pl.BlockSpec((2 * H, 3 * W), c2),
            pl.BlockSpec((1, LANES), c2),
            pl.BlockSpec((2 * H, 1), c2),
            pl.BlockSpec((1, W), c2),
            pl.BlockSpec((1, W), c2),
        ],
        out_specs=pl.BlockSpec((1, rows, W), lambda b, c: (b, c, 0)),
        out_shape=jax.ShapeDtypeStruct((B, L, W), BF16),
        scratch_shapes=[
            pltpu.VMEM((SUBLANES + rows, W), F32),
            pltpu.VMEM((n_sub, Tc, 3 * W), BF16),
            pltpu.VMEM((n_sub, Tc, W), BF16),
            pltpu.VMEM((n_sub, H, dh + SUBLANES, Tc), BF16),
            pltpu.VMEM((H, dh + SUBLANES, dh), F32),
            pltpu.VMEM((H, 1, LANES), F32),
        ],
        compiler_params=_params("parallel", "arbitrary"),
        name="mlstm",
    )(proj, proj, conv_w.astype(F32), conv_b.reshape(1, W).astype(F32), wq.astype(BF16), wk.astype(BF16),
      wv.astype(BF16), jnp.transpose(wv, (0, 2, 1)).astype(BF16), gw, gwt, gbc, gbr, norm_g.reshape(1, W).astype(F32), skip.reshape(1, W).astype(F32))


def _attn_kernel(q_ref, k_ref, vt_ref, z_ref, bias_ref, lam_ref, ng_ref, o_ref, m_scr, acc_scr, *,
                 mode, out_scale, blk, bias_index):
    L = q_ref.shape[2]
    n = L // blk
    lane = lax.broadcasted_iota(jnp.int32, (1, LANES), 1)
    map0 = lane < ATTN_HEAD_DIM
    feat = lax.broadcasted_iota(jnp.int32, (LANES, 1), 0)
    ones_rows = jnp.ones((SUBLANES, blk), BF16)

    def stacked_q(qi):
        q = q_ref[0, 0, qi * blk:(qi + 1) * blk, :]
        zero = jnp.zeros_like(q)
        return jnp.concatenate([jnp.where(map0, q, zero), jnp.where(map0, zero, q)], axis=0)

    def scores(qi, j):
        s = lax.dot_general(k_ref[0, 0, j * blk:(j + 1) * blk, :], q2[qi], (((1,), (1,)), ((), ())),
                            preferred_element_type=F32)
        e = bias_index(qi - j)
        return s if e is None else s + bias_ref[e]

    def update(qi, j, s):
        m_cur = jnp.max(s, axis=0, keepdims=True)
        m_next = m_cur if j == 0 else jnp.maximum(m_scr[qi], m_cur)
        p = jnp.exp2(s - m_next).astype(BF16)
        pv = jnp.dot(jnp.concatenate([vt_ref[0, 0, :, j * blk:(j + 1) * blk], ones_rows], axis=0), p,
                     preferred_element_type=F32)
        if j == 0:
            acc_scr[qi] = pv
        else:
            acc_scr[qi] = jnp.exp2(m_scr[qi] - m_next) * acc_scr[qi] + pv
        m_scr[qi] = m_next

    def finish(qi):
        rows = slice(qi * blk, (qi + 1) * blk)
        acc = acc_scr[qi]
        o = acc[:LANES] / acc[LANES:LANES + 1]
        o0, o1 = o[:, :blk], o[:, blk:]
        if mode == "dilated":
            out = jnp.transpose(jnp.where(feat < ATTN_HEAD_DIM, o0, o1))
        else:
            out = jnp.transpose(o0 - lam_ref[:, 0:1] * o1)
            ms = jnp.mean(out * out, axis=-1, keepdims=True)
            out = out * lax.rsqrt(ms + HEAD_NORM_EPS) * ng_ref[...] * out_scale
        o_ref[0, rows, :] = (out * _silu(z_ref[0, 0, rows, :].astype(F32))).astype(o_ref.dtype)

    steps = []
    for a in range(n // 2):
        b = n - 1 - a
        ja = 0
        for jb in range(b + 1):
            steps.append((b, jb))
            if ja <= a and (jb + 1) * (a + 1) >= (ja + 1) * (b + 1):
                steps.append((a, ja))
                ja += 1
    q2 = {qi: stacked_q(qi) for qi in range(n)}
    pending = {}
    for t in range(min(ATTN_LOOKAHEAD, len(steps))):
        pending[t] = scores(*steps[t])
    for t, (qi, j) in enumerate(steps):
        if t + ATTN_LOOKAHEAD < len(steps):
            pending[t + ATTN_LOOKAHEAD] = scores(*steps[t + ATTN_LOOKAHEAD])
        update(qi, j, pending.pop(t))
        if j == qi:
            finish(qi)


def _dilated_multiplicity(delta):
    mult = np.zeros(delta.shape, np.int64)
    for window, dil in DIL_PAIRS:
        mult += (delta >= 0) & (delta % dil == 0) & (delta <= window)
    return mult


def _bias_tables(L, blk, mode):
    n = L // blk
    key = np.arange(blk)[:, None]
    qry = np.arange(blk)[None, :]
    tables, index = [], {}
    for d in range(n):
        delta = d * blk + qry - key
        mult = _dilated_multiplicity(delta) if mode == "dilated" else (delta >= 0).astype(np.int64)
        t = np.where(mult > 0, np.log2(np.maximum(mult, 1)), NEG_BIG).astype(np.float32)
        if not t.any():
            index[d] = None
            continue
        for e, other in enumerate(tables):
            if np.array_equal(t, other):
                index[d] = e
                break
        else:
            index[d] = len(tables)
            tables.append(t)
    stacked = np.stack(tables)
    return jnp.asarray(np.concatenate([stacked, stacked], axis=2)), index


def _attention(qkz, vt, q0, k0, v0, z0, n_blocks, blk, lam, norm_g, mode, out_scale):
    B, _, L, _ = qkz.shape
    bias, index = _bias_tables(L, blk, mode)
    n_q = L // blk
    assert n_q % 2 == 0
    kern = functools.partial(_attn_kernel, mode=mode, out_scale=out_scale, blk=blk, bias_index=index.get)
    c2 = lambda b, n: (0, 0)
    return pl.pallas_call(
        kern,
        grid=(B, n_blocks),
        in_specs=[
            pl.BlockSpec((1, 1, L, LANES), lambda b, n: (b, q0 + n, 0, 0)),
            pl.BlockSpec((1, 1, L, LANES), lambda b, n: (b, k0 + n, 0, 0)),
            pl.BlockSpec((1, 1, LANES, L), lambda b, n: (b, v0 + n, 0, 0)),
            pl.BlockSpec((1, 1, L, LANES), lambda b, n: (b, z0 + n, 0, 0)),
            pl.BlockSpec(bias.shape, lambda b, n: (0, 0, 0)),
            pl.BlockSpec((1, LANES), c2),
            pl.BlockSpec((1, LANES), c2),
        ],
        out_specs=pl.BlockSpec((1, L, LANES), lambda b, n: (b, 0, n)),
        out_shape=jax.ShapeDtypeStruct((B, L, n_blocks * LANES), BF16),
        scratch_shapes=[
            pltpu.VMEM((n_q, 1, 2 * blk), F32),
            pltpu.VMEM((n_q, LANES + SUBLANES, 2 * blk), F32),
        ],
        compiler_params=_params("parallel", "parallel"),
        name="attn_" + mode,
    )(qkz, qkz, vt, qkz, bias, lam, norm_g)


def _rope_tables(L):
    dh = ATTN_HEAD_DIM
    inv = ROPE_THETA ** (-jnp.arange(0, dh, 2, dtype=F32) / dh)
    ang = jnp.arange(L, dtype=F32)[:, None] * inv[None, :]
    cos, sin = jnp.cos(ang), jnp.sin(ang)
    reps = LANES // dh
    cos_t = jnp.tile(jnp.concatenate([cos, cos], axis=-1), (1, reps))
    sin_t = jnp.tile(jnp.concatenate([-sin, sin], axis=-1), (1, reps))
    return cos_t, sin_t


def _even_layer(h, pre_g, w_in, s5_params, ml_params):
    proj = _norm_proj(h, pre_g, w_in.astype(BF16))
    return _s5_branch(proj, *s5_params), _mlstm_branch(proj, *ml_params)


def _odd_layer(prev, h, pre_g, w_in, lq1, lk1, lq2, lk2, diff_norm, layer_idx):
    B, L, D = h.shape
    nblk = PROJ_COLS // LANES
    cos_t, sin_t = _rope_tables(L)
    h, qkz, vt = _out_norm_proj_rope(*prev, h, pre_g, w_in.astype(BF16), cos_t, sin_t)
    blk = min(ATTN_BLOCK, L)
    lam_init = 0.8 - 0.6 * math.exp(-0.3 * layer_idx)
    lam = (jnp.exp(jnp.sum(lq1.astype(F32) * lk1.astype(F32)))
           - jnp.exp(jnp.sum(lq2.astype(F32) * lk2.astype(F32))) + lam_init)
    lam_row = jnp.full((1, LANES), lam, F32)
    ones_row = jnp.ones((1, LANES), F32)
    base = [b for _, b in ODD_CHUNKS]
    c_out = _attention(qkz, vt, base[0], base[1], base[2], base[3], nblk, blk, ones_row, ones_row, "dilated", 1.0)
    d_out = _attention(qkz, vt, base[4], base[5], base[6], base[7], nblk, blk, lam_row,
                       diff_norm.reshape(1, LANES).astype(F32), "diff", 1.0 - lam_init)
    return h, c_out, d_out


def kernel(x, pre_norm, post_norm, w_in_ab, s5_lambda_re, s5_lambda_im, s5_log_dt, s5_b_re, s5_b_im,
           s5_c_re, s5_c_im, s5_d, s5_glu_w, s5_glu_b, ml_conv_w, ml_conv_b, ml_wq, ml_wk, ml_wv,
           ml_gate_w, ml_gate_b, ml_norm, ml_skip, w_out_ab, w_in_cd, diff_lq1, diff_lk1, diff_lq2,
           diff_lk2, diff_norm, w_out_cd):
    depth = pre_norm.shape[0]
    h = x
    prev = None
    for l in range(depth):
        i = l // 2
        if l % 2 == 0:
            if prev is not None:
                h = _out_proj(*prev, h)
            s5_params = (s5_lambda_re[i], s5_lambda_im[i], s5_log_dt[i], s5_b_re[i], s5_b_im[i],
                         s5_c_re[i], s5_c_im[i], s5_d[i], s5_glu_w[i], s5_glu_b[i])
            ml_params = (ml_conv_w[i], ml_conv_b[i], ml_wq[i], ml_wk[i], ml_wv[i], ml_gate_w[i],
                         ml_gate_b[i], ml_norm[i], ml_skip[i])
            a, b = _even_layer(h, pre_norm[l], w_in_ab[i], s5_params, ml_params)
            prev = (a, b, w_out_ab[i], post_norm[l])
        else:
            h, c, d = _odd_layer(prev, h, pre_norm[l], w_in_cd[i], diff_lq1[i], diff_lk1[i], diff_lq2[i],
                                 diff_lk2[i], diff_norm[i], l)
            prev = (c, d, w_out_cd[i], post_norm[l])
    return _out_proj(*prev, h)
```

```python
import functools
import math

import numpy as np
import jax
import jax.numpy as jnp
from jax import lax
from jax.experimental import pallas as pl
from jax.experimental.pallas import tpu as pltpu

F32 = jnp.float32
BF16 = jnp.bfloat16

S5_GROUP = 16
S5_STATE = 64
ML_HEADS = 4
ML_CONV = 4
DIL_PAIRS = ((128, 1), (512, 4), (2048, 16))
ROPE_THETA = 10000.0
NORM_EPS = 1e-6
HEAD_NORM_EPS = 1e-5
ATTN_HEAD_DIM = 64
QUERY_SCALE = ATTN_HEAD_DIM ** -0.5 * math.log2(math.e)

LANES = 128
SUBLANES = 8
VMEM_LIMIT_BYTES = 56 * 1024 * 1024

NEG_BIG = -1e30

PROJ_ROWS = 1024
OUT_PROJ_ROWS = 2048
PROJ_COLS = 512
S5_BATCH = SUBLANES
S5_TIME = 256
S5_STRIDE = 2
S5_PIECE = 32
ML_CHUNK = 256
ML_SUB = 4
ATTN_BLOCK = 256
ATTN_LOOKAHEAD = 3


def _params(*semantics):
    return pltpu.CompilerParams(dimension_semantics=semantics, vmem_limit_bytes=VMEM_LIMIT_BYTES)


def _silu(x):
    return x * jax.nn.sigmoid(x)


def _norm_proj_kernel(x_ref, g_ref, w_ref, o_ref, z_scr):
    x = x_ref[0]
    ms = jnp.mean(x * x, axis=-1, keepdims=True)
    z_scr[...] = (x * lax.rsqrt(ms + NORM_EPS) * g_ref[...]).astype(BF16)
    n_out = w_ref.shape[1]
    for c in range(n_out // PROJ_COLS):
        cols = slice(c * PROJ_COLS, (c + 1) * PROJ_COLS)
        o_ref[0, :, cols] = jnp.dot(z_scr[...], w_ref[:, cols], preferred_element_type=F32).astype(o_ref.dtype)


def _norm_proj(x, g, w):
    B, L, D = x.shape
    N = w.shape[1]
    tm = min(PROJ_ROWS, L)
    return pl.pallas_call(
        _norm_proj_kernel,
        grid=(B, L // tm),
        in_specs=[
            pl.BlockSpec((1, tm, D), lambda b, i: (b, i, 0)),
            pl.BlockSpec((1, D), lambda b, i: (0, 0)),
            pl.BlockSpec((D, N), lambda b, i: (0, 0)),
        ],
        out_specs=pl.BlockSpec((1, tm, N), lambda b, i: (b, i, 0)),
        out_shape=jax.ShapeDtypeStruct((B, L, N), BF16),
        scratch_shapes=[pltpu.VMEM((tm, D), BF16)],
        compiler_params=_params("parallel", "parallel"),
        name="norm_proj_even",
    )(x, g.reshape(1, D), w)


def _swap_halves(x, first_half):
    return jnp.where(first_half, pltpu.roll(x, LANES - ATTN_HEAD_DIM // 2, 1), pltpu.roll(x, ATTN_HEAD_DIM // 2, 1))


ODD_CHUNKS = (("q", 0), ("k", 4), ("v", 0), ("z", 8), ("q", 12), ("k", 16), ("v", 4), ("z", 20))


def _out_norm_proj_rope_kernel(a_ref, b_ref, wa_ref, wb_ref, gpost_ref, h_ref, g_ref, w_ref, cos_ref, sin_ref,
                               hout_ref, o_ref, vt_ref, z_scr):
    y = jnp.dot(a_ref[0], wa_ref[...], preferred_element_type=F32)
    y = y + jnp.dot(b_ref[0], wb_ref[...], preferred_element_type=F32)
    ms = jnp.mean(y * y, axis=-1, keepdims=True)
    x = h_ref[0] + y * lax.rsqrt(ms + NORM_EPS) * gpost_ref[...]
    hout_ref[0] = x
    ms = jnp.mean(x * x, axis=-1, keepdims=True)
    z_scr[...] = (x * lax.rsqrt(ms + NORM_EPS) * g_ref[...]).astype(BF16)
    slabs = PROJ_COLS // LANES
    lane = lax.broadcasted_iota(jnp.int32, (1, LANES), 1)
    first_half = (lane % ATTN_HEAD_DIM) < (ATTN_HEAD_DIM // 2)
    for c, (kind, base) in enumerate(ODD_CHUNKS):
        y = jnp.dot(z_scr[...], w_ref[:, c * PROJ_COLS:(c + 1) * PROJ_COLS], preferred_element_type=F32)
        for s in range(slabs):
            t = y[:, s * LANES:(s + 1) * LANES]
            if kind in ("q", "k"):
                t = t * cos_ref[...] + _swap_halves(t, first_half) * sin_ref[...]
            if kind == "q":
                t = t * QUERY_SCALE
            if kind == "v":
                vt_ref[0, base + s] = jnp.transpose(t).astype(vt_ref.dtype)
            else:
                o_ref[0, base + s] = t.astype(o_ref.dtype)


def _out_norm_proj_rope(a, b, w_out, g_post, h, g, w, cos_t, sin_t):
    B, L, D = h.shape
    Wb = a.shape[2]
    N = w.shape[1]
    assert N == len(ODD_CHUNKS) * PROJ_COLS
    tm = min(PROJ_ROWS, L)
    n_v = sum(kind == "v" for kind, _ in ODD_CHUNKS) * (PROJ_COLS // LANES)
    n_o = N // LANES - n_v
    row = lambda b_, i: (b_, i, 0)
    const = lambda b_, i: (0, 0)
    return pl.pallas_call(
        _out_norm_proj_rope_kernel,
        grid=(B, L // tm),
        in_specs=[
            pl.BlockSpec((1, tm, Wb), row),
            pl.BlockSpec((1, tm, Wb), row),
            pl.BlockSpec((Wb, D), const),
            pl.BlockSpec((Wb, D), const),
            pl.BlockSpec((1, D), const),
            pl.BlockSpec((1, tm, D), row),
            pl.BlockSpec((1, D), const),
            pl.BlockSpec((D, N), const),
            pl.BlockSpec((tm, LANES), lambda b_, i: (i, 0)),
            pl.BlockSpec((tm, LANES), lambda b_, i: (i, 0)),
        ],
        out_specs=[
            pl.BlockSpec((1, tm, D), row),
            pl.BlockSpec((1, n_o, tm, LANES), lambda b_, i: (b_, 0, i, 0)),
            pl.BlockSpec((1, n_v, LANES, tm), lambda b_, i: (b_, 0, 0, i)),
        ],
        out_shape=[
            jax.ShapeDtypeStruct((B, L, D), F32),
            jax.ShapeDtypeStruct((B, n_o, L, LANES), BF16),
            jax.ShapeDtypeStruct((B, n_v, LANES, L), BF16),
        ],
        scratch_shapes=[pltpu.VMEM((tm, D), BF16)],
        compiler_params=_params("parallel", "parallel"),
        name="out_norm_proj_odd",
    )(a, b, w_out[:Wb].astype(BF16), w_out[Wb:].astype(BF16), g_post.reshape(1, D), h, g.reshape(1, D), w,
      cos_t, sin_t)


def _out_proj_kernel(a_ref, b_ref, wa_ref, wb_ref, g_ref, h_ref, o_ref):
    y = jnp.dot(a_ref[0], wa_ref[...], preferred_element_type=F32)
    y = y + jnp.dot(b_ref[0], wb_ref[...], preferred_element_type=F32)
    ms = jnp.mean(y * y, axis=-1, keepdims=True)
    o_ref[0] = h_ref[0] + y * lax.rsqrt(ms + NORM_EPS) * g_ref[...]


def _out_proj(a, b, w_out, g, h):
    B, L, W = a.shape
    D = w_out.shape[1]
    tm = min(OUT_PROJ_ROWS, L)
    return pl.pallas_call(
        _out_proj_kernel,
        grid=(B, L // tm),
        in_specs=[
            pl.BlockSpec((1, tm, W), lambda b_, i: (b_, i, 0)),
            pl.BlockSpec((1, tm, W), lambda b_, i: (b_, i, 0)),
            pl.BlockSpec((W, D), lambda b_, i: (0, 0)),
            pl.BlockSpec((W, D), lambda b_, i: (0, 0)),
            pl.BlockSpec((1, D), lambda b_, i: (0, 0)),
            pl.BlockSpec((1, tm, D), lambda b_, i: (b_, i, 0)),
        ],
        out_specs=pl.BlockSpec((1, tm, D), lambda b_, i: (b_, i, 0)),
        out_shape=jax.ShapeDtypeStruct((B, L, D), F32),
        compiler_params=_params("parallel", "parallel"),
        name="out_proj",
    )(a, b, w_out[:W].astype(BF16), w_out[W:].astype(BF16), g.reshape(1, D), h)


def _s5_kernel(*refs):
    r = S5_STRIDE
    u_refs, z_refs = refs[:r], refs[r:2 * r]
    (bin_ref, wx_ref, wu_ref, ar_ref, ai_ref, d_ref, gw_ref, gb_ref, o_ref,
     us_scr, zs_scr, xs_scr, ys_scr, os_scr, st_scr) = refs[2 * r:]
    nb, n_grp, W = u_refs[0].shape
    n_blk = W // LANES
    half = xs_scr.shape[2] // 2
    n_t = n_grp // S5_PIECE
    piece_rows = S5_PIECE * nb

    @pl.when(pl.program_id(1) == 0)
    def _():
        st_scr[...] = jnp.zeros_like(st_scr)

    def rows_of(i):
        return slice(i * piece_rows, (i + 1) * piece_rows)

    def load_piece(i):
        grp = slice(i * S5_PIECE, (i + 1) * S5_PIECE)
        for srcs, dst in ((u_refs, us_scr), (z_refs, zs_scr)):
            for j in range(r):
                for b in range(nb):
                    xb = srcs[j][b, grp, :].astype(F32)
                    for k in range(n_blk):
                        dst[k, j, pl.ds(i * piece_rows + b, S5_PIECE, stride=nb), :] = xb[:, k * LANES:(k + 1) * LANES]

    def store_piece(i):
        grp = slice(i * S5_PIECE, (i + 1) * S5_PIECE)
        for b in range(nb):
            for j in range(r):
                o_ref[b, grp, j * W:(j + 1) * W] = jnp.concatenate(
                    [os_scr[k, j, pl.ds(i * piece_rows + b, S5_PIECE, stride=nb), :] for k in range(n_blk)],
                    axis=1).astype(o_ref.dtype)

    def group_inputs(i, k):
        return jnp.concatenate([us_scr[k, j, rows_of(i), :] for j in range(r)], axis=1).astype(BF16)

    def input_matmul(i, k):
        xs_scr[k, rows_of(i), :] = jnp.dot(group_inputs(i, k), bin_ref[k], preferred_element_type=F32)

    def scan(i, k):
        ar, ai = ar_ref[k], ai_ref[k]
        xr, xi = st_scr[k, 0], st_scr[k, 1]
        for t in range(S5_PIECE):
            rs = slice(i * piece_rows + t * nb, i * piece_rows + (t + 1) * nb)
            cr, ci = xs_scr[k, rs, 0:half], xs_scr[k, rs, half:2 * half]
            xs_scr[k, rs, 0:half] = xr
            xs_scr[k, rs, half:2 * half] = xi
            xr, xi = ar * xr - ai * xi + cr, ar * xi + ai * xr + ci
        st_scr[k, 0] = xr
        st_scr[k, 1] = xi

    def output_matmul(i, k):
        y = jnp.dot(xs_scr[k, rows_of(i), :].astype(BF16), wx_ref[k], preferred_element_type=F32)
        y = y + jnp.dot(group_inputs(i, k), wu_ref[k], preferred_element_type=F32)
        for j in range(r):
            ys_scr[k, j, rows_of(i), :] = y[:, j * LANES:(j + 1) * LANES]

    def glu(i):
        def tokens(ref):
            return jnp.concatenate(
                [jnp.concatenate([ref[k, j, rows_of(i), :] for k in range(n_blk)], axis=1) for j in range(r)], axis=0)
        y = tokens(ys_scr) + d_ref[...] * tokens(us_scr)
        g = jax.nn.gelu(y)
        gate = jnp.dot(g.astype(BF16), gw_ref[...], preferred_element_type=F32) + gb_ref[...]
        out = g * jax.nn.sigmoid(gate) * _silu(tokens(zs_scr))
        for j in range(r):
            for k in range(n_blk):
                os_scr[k, j, rows_of(i), :] = out[j * piece_rows:(j + 1) * piece_rows, k * LANES:(k + 1) * LANES]

    def drain(p):
        i, k = pieces[p]
        output_matmul(i, k)
        if k == n_blk - 1:
            glu(i)
            store_piece(i)

    pieces = [(i, k) for i in range(n_t) for k in range(n_blk)]
    load_piece(0)
    input_matmul(*pieces[0])
    for p, (i, k) in enumerate(pieces):
        if k == 0 and i + 1 < n_t:
            load_piece(i + 1)
        if p + 1 < len(pieces):
            input_matmul(*pieces[p + 1])
        if p >= 1:
            drain(p - 1)
        scan(i, k)
    drain(len(pieces) - 1)


def _s5_weights(lam_re, lam_im, log_dt, b_re, b_im, c_re, c_im):
    G, P = lam_re.shape
    r = S5_STRIDE
    gpb = LANES // S5_GROUP
    n_blk = G // gpb
    lr, li = lam_re.astype(F32), lam_im.astype(F32)
    dt = jnp.exp(log_dt.astype(F32))[:, None]
    mag = jnp.exp(lr * dt)
    ab_r, ab_i = mag * jnp.cos(li * dt), mag * jnp.sin(li * dt)
    den = lr * lr + li * li
    f_r = ((ab_r - 1.0) * lr + ab_i * li) / den
    f_i = (ab_i * lr - (ab_r - 1.0) * li) / den
    br, bi = b_re.astype(F32), b_im.astype(F32)
    bb_r = f_r[..., None] * br - f_i[..., None] * bi
    bb_i = f_r[..., None] * bi + f_i[..., None] * br
    cr, ci = c_re.astype(F32), c_im.astype(F32)
    eye = jnp.eye(gpb, dtype=F32)

    pows = [(jnp.ones_like(ab_r), jnp.zeros_like(ab_i))]
    for _ in range(r):
        pr, pi = pows[-1]
        pows.append((pr * ab_r - pi * ab_i, pr * ab_i + pi * ab_r))

    def pack_in(xr, xi):
        x = jnp.stack([xr, xi]).reshape(2, n_blk, gpb, P, S5_GROUP)
        x = jnp.transpose(x, (1, 2, 4, 0, 3))
        x = x[:, :, :, :, None, :] * eye[None, :, None, None, :, None]
        return x.reshape(n_blk, LANES, 2 * gpb * P)

    def pack_out(er, ei):
        x = jnp.stack([er, ei]).reshape(2, n_blk, gpb, S5_GROUP, P)
        x = jnp.transpose(x, (1, 0, 2, 4, 3))
        x = x[:, :, :, :, None, :] * eye[None, None, :, None, :, None]
        return x.reshape(n_blk, 2 * gpb * P, LANES)

    def pack_direct(kq):
        x = jnp.transpose(kq.reshape(n_blk, gpb, S5_GROUP, S5_GROUP), (0, 1, 3, 2))
        x = x[:, :, :, None, :] * eye[None, :, None, :, None]
        return x.reshape(n_blk, LANES, LANES)

    w_in = []
    for j in range(r):
        pr, pi = pows[r - 1 - j]
        w_in.append(pack_in(pr[..., None] * bb_r - pi[..., None] * bb_i, pr[..., None] * bb_i + pi[..., None] * bb_r))
    w_in = jnp.concatenate(w_in, axis=1).astype(BF16)

    w_x = []
    for j in range(r):
        pr, pi = pows[j + 1]
        er = cr * pr[:, None, :] - ci * pi[:, None, :]
        ei = cr * pi[:, None, :] + ci * pr[:, None, :]
        w_x.append(pack_out(er, -ei))
    w_x = jnp.concatenate(w_x, axis=2).astype(BF16)

    direct = []
    for q in range(r):
        pr, pi = pows[q]
        er = cr * pr[:, None, :] - ci * pi[:, None, :]
        ei = cr * pi[:, None, :] + ci * pr[:, None, :]
        direct.append(pack_direct(jnp.einsum('gop,gpi->goi', er, bb_r) - jnp.einsum('gop,gpi->goi', ei, bb_i)))
    zero = jnp.zeros_like(direct[0])
    w_u = jnp.concatenate(
        [jnp.concatenate([direct[j - i] if i <= j else zero for j in range(r)], axis=2) for i in range(r)],
        axis=1).astype(BF16)

    ar_r, ai_r = pows[r]
    ar = jnp.broadcast_to(ar_r.reshape(n_blk, 1, gpb * P), (n_blk, S5_BATCH, gpb * P))
    ai = jnp.broadcast_to(ai_r.reshape(n_blk, 1, gpb * P), (n_blk, S5_BATCH, gpb * P))
    return w_in, w_x, w_u, ar, ai


def _s5_branch(proj, lam_re, lam_im, log_dt, b_re, b_im, c_re, c_im, d_skip, glu_w, glu_b):
    B, L, N = proj.shape
    W = N // 4
    nb = S5_BATCH
    T = min(S5_TIME, L)
    r = S5_STRIDE
    n_blk = W // LANES
    w_in, w_x, w_u, ar, ai = _s5_weights(lam_re, lam_im, log_dt, b_re, b_im, c_re, c_im)
    S2 = w_in.shape[2]
    rows = nb * T // r
    const3 = lambda g, c: (0, 0, 0)
    const2 = lambda g, c: (0, 0)
    proj_r = proj.reshape(B, L // r, r * N)
    col_blocks = N // W
    u_specs = [pl.BlockSpec((nb, T // r, W), functools.partial(lambda g, c, j: (g, c, j * col_blocks), j=j))
               for j in range(r)]
    z_specs = [pl.BlockSpec((nb, T // r, W), functools.partial(lambda g, c, j: (g, c, j * col_blocks + 1), j=j))
               for j in range(r)]
    out = pl.pallas_call(
        _s5_kernel,
        grid=(B // nb, L // T),
        in_specs=u_specs + z_specs + [
            pl.BlockSpec(w_in.shape, const3),
            pl.BlockSpec(w_x.shape, const3),
            pl.BlockSpec(w_u.shape, const3),
            pl.BlockSpec(ar.shape, const3),
            pl.BlockSpec(ai.shape, const3),
            pl.BlockSpec((1, W), const2),
            pl.BlockSpec((W, W), const2),
            pl.BlockSpec((1, W), const2),
        ],
        out_specs=pl.BlockSpec((nb, T // r, r * W), lambda g, c: (g, c, 0)),
        out_shape=jax.ShapeDtypeStruct((B, L // r, r * W), BF16),
        scratch_shapes=[
            pltpu.VMEM((n_blk, r, rows, LANES), F32),
            pltpu.VMEM((n_blk, r, rows, LANES), F32),
            pltpu.VMEM((n_blk, rows, S2), F32),
            pltpu.VMEM((n_blk, r, rows, LANES), F32),
            pltpu.VMEM((n_blk, r, rows, LANES), F32),
            pltpu.VMEM((n_blk, 2, nb, S2 // 2), F32),
        ],
        compiler_params=_params("parallel", "arbitrary"),
        name="s5_scan",
    )(*([proj_r] * (2 * r)), w_in, w_x, w_u, ar, ai, d_skip.reshape(1, W).astype(F32), glu_w.astype(BF16),
      glu_b.reshape(1, W).astype(F32))
    return out.reshape(B, L, W)


def _log_sigmoid(x):
    return jnp.minimum(x, 0.0) - jnp.log1p(jnp.exp(-jnp.abs(x)))


def _split3(x):
    hi = x.astype(BF16)
    r1 = x - hi.astype(F32)
    mid = r1.astype(BF16)
    lo = (r1 - mid.astype(F32)).astype(BF16)
    return hi, mid, lo


def _mlstm_kernel(x_ref, z_ref, cw_ref, cb_ref, wq_ref, wk_ref, wv_ref, wvt_ref, gw_ref, gwt_ref, gbc_ref, gbr_ref,
                  ng_ref, sk_ref, o_ref, xe_scr, qkv_scr, ks_scr, vta_scr, ca_scr, m_scr):
    n_sub, Tc = qkv_scr.shape[0], qkv_scr.shape[1]
    W = x_ref.shape[2]
    H = wq_ref.shape[0]
    dh = W // H
    pad = SUBLANES
    heads = range(H)
    nt = (((1,), (1,)), ((), ()))

    @pl.when(pl.program_id(1) == 0)
    def _():
        xe_scr[0:pad, :] = jnp.zeros((pad, W), F32)
        ca_scr[...] = jnp.zeros_like(ca_scr)
        m_scr[...] = jnp.zeros_like(m_scr)
        vta_scr[:, :, dh:, :] = jnp.ones((n_sub, H, pad, Tc), BF16)

    xe_scr[pad:, :] = x_ref[0].astype(F32)

    ti = lax.broadcasted_iota(jnp.int32, (Tc, Tc), 0)
    si = lax.broadcasted_iota(jnp.int32, (Tc, Tc), 1)
    lower = jnp.where(si <= ti, 1.0, 0.0).astype(BF16)
    allowed = ti <= si
    upper = jnp.where(allowed, 1.0, 0.0).astype(BF16)

    def front(s):
        r0 = s * Tc
        xb = x_ref[0, r0:r0 + Tc, :]
        xc = cb_ref[...] + sum(xe_scr[pl.ds(pad + r0 - (ML_CONV - 1) + j, Tc), :] * cw_ref[j:j + 1, :]
                               for j in range(ML_CONV))
        xc = _silu(xc)
        xcb = xc.astype(BF16)
        for h in heads:
            cols = slice(h * dh, (h + 1) * dh)
            q = jnp.dot(xcb[:, cols], wq_ref[h], preferred_element_type=F32)
            k = jnp.dot(xcb[:, cols], wk_ref[h], preferred_element_type=F32)
            v = jnp.dot(xb[:, cols], wv_ref[h], preferred_element_type=F32)
            qkv_scr[s, :, h * dh:(h + 1) * dh] = q.astype(BF16)
            qkv_scr[s, :, W + h * dh:W + (h + 1) * dh] = k.astype(BF16)
            qkv_scr[s, :, 2 * W + h * dh:2 * W + (h + 1) * dh] = v.astype(BF16)
            vta_scr[s, h, :dh, :] = lax.dot_general(wvt_ref[h], xb[:, cols], nt,
                                                    preferred_element_type=F32).astype(BF16)
            ks_scr[s, :, cols] = (k * (dh ** -0.5)).astype(BF16)
        qkv = qkv_scr[s]
        g_col = jnp.dot(qkv, gw_ref[...], preferred_element_type=F32) + gbc_ref[...]
        g_row = lax.dot_general(gwt_ref[...], qkv, nt, preferred_element_type=F32) + gbr_ref[...]
        b_col = sum(jnp.dot(lower, p, preferred_element_type=F32) for p in _split3(_log_sigmoid(g_col)))
        b_row = sum(jnp.dot(p, upper, preferred_element_type=F32) for p in _split3(_log_sigmoid(g_row)))
        return xc, g_col, g_row, b_col, b_row

    def rec_matmuls(s):
        q_h = [qkv_scr[s, :, h * dh:(h + 1) * dh] for h in heads]
        ks_h = [ks_scr[s, :, h * dh:(h + 1) * dh] for h in heads]
        s_qk = [lax.dot_general(ks_h[h], q_h[h], nt, preferred_element_type=F32) for h in heads]
        q_ca = [lax.dot_general(ca_scr[h].astype(BF16), q_h[h], nt, preferred_element_type=F32)
                for h in heads]
        return ks_h, s_qk, q_ca

    def rec_rest(s, mm, xc, g_col, g_row, b_col, b_row):
        ks_h, s_qk, q_ca = mm
        rows = slice(s * Tc, (s + 1) * Tc)
        m_prev = [m_scr[h][:, 0:1] for h in heads]
        br = [b_row[H + h:H + h + 1, :] for h in heads]
        li = [g_row[h:h + 1, :] for h in heads]
        b_last = [br[h][:, Tc - 1:Tc] for h in heads]
        dmat, sc, floor = [], [], []
        for h in heads:
            c_col = g_col[:, h:h + 1] - b_col[:, H + h:H + h + 1]
            log_d = jnp.where(allowed, br[h] + c_col, NEG_BIG)
            inter = br[h] + m_prev[h]
            m_t = jnp.maximum(inter, jnp.max(log_d, axis=0, keepdims=True))
            dmat.append(jnp.exp(log_d - m_t))
            sc.append(jnp.exp(inter - m_t))
            floor.append(jnp.exp(-m_t))
        pv = [jnp.dot(vta_scr[s, h], (s_qk[h] * dmat[h]).astype(BF16), preferred_element_type=F32) for h in heads]
        for h in heads:
            w_log = b_last[h] - br[h] + li[h]
            m_new = jnp.maximum(b_last[h] + m_prev[h], jnp.max(w_log, axis=1, keepdims=True))
            weighted = (vta_scr[s, h].astype(F32) * jnp.exp(w_log - m_new)).astype(BF16)
            decay = jnp.exp(b_last[h] + m_prev[h] - m_new)
            ca_scr[h] = decay * ca_scr[h] + jnp.dot(weighted, ks_h[h], preferred_element_type=F32)
            m_scr[h] = jnp.broadcast_to(m_new, (1, LANES))
        for h in heads:
            cols = slice(h * dh, (h + 1) * dh)
            tot = sc[h] * q_ca[h] + pv[h]
            hh = tot[:dh] / jnp.maximum(jnp.abs(tot[dh:dh + 1]), floor[h])
            mu = jnp.mean(hh, axis=0, keepdims=True)
            cen = hh - mu
            var = jnp.mean(cen * cen, axis=0, keepdims=True)
            hn = jnp.transpose(cen * lax.rsqrt(var + HEAD_NORM_EPS))
            out = hn * ng_ref[:, cols] + sk_ref[:, cols] * xc[:, cols]
            o_ref[0, rows, cols] = (out * _silu(z_ref[0, rows, cols].astype(F32))).astype(o_ref.dtype)

    ready = front(0)
    for s in range(n_sub):
        mm = rec_matmuls(s)
        following = front(s + 1) if s + 1 < n_sub else None
        rec_rest(s, mm, *ready)
        ready = following
    xe_scr[0:pad, :] = xe_scr[n_sub * Tc:n_sub * Tc + pad, :]


def _mlstm_branch(proj, conv_w, conv_b, wq, wk, wv, gate_w, gate_b, norm_g, skip):
    B, L, N = proj.shape
    W = N // 4
    H = wq.shape[0]
    dh = W // H
    Tc = min(ML_CHUNK, L)
    n_sub = min(ML_SUB, L // Tc)
    rows = n_sub * Tc
    gw = jnp.zeros((3 * W, LANES), F32).at[:, :2 * H].set(gate_w.astype(F32)).astype(BF16)
    gwt = jnp.transpose(gate_w.astype(F32)).astype(BF16)
    gbc = jnp.zeros((1, LANES), F32).at[0, :2 * H].set(gate_b.astype(F32))
    gbr = gate_b.astype(F32).reshape(2 * H, 1)
    c2 = lambda b, c: (0, 0)
    c3 = lambda b, c: (0, 0, 0)
    return pl.pallas_call(
        _mlstm_kernel,
        grid=(B, L // rows),
        in_specs=[
            pl.BlockSpec((1, rows, W), lambda b, c: (b, c, 2)),
            pl.BlockSpec((1, rows, W), lambda b, c: (b, c, 3)),
            pl.BlockSpec((ML_CONV, W), c2),
            pl.BlockSpec((1, W), c2),
            pl.BlockSpec((H, dh, dh), c3),
            pl.BlockSpec((H, dh, dh), c3),
            pl.BlockSpec((H, dh, dh), c3),
            pl.BlockSpec((H, dh, dh), c3),
            pl.BlockSpec((3 * W, LANES), c2),
            pl.BlockSpec((2 * H, 3 * W), c2),
            pl.BlockSpec((1, LANES), c2),
            pl.BlockSpec((2 * H, 1), c2),
            pl.BlockSpec((1, W), c2),
            pl.BlockSpec((1, W), c2),
        ],
        out_specs=pl.BlockSpec((1, rows, W), lambda b, c: (b, c, 0)),
        out_shape=jax.ShapeDtypeStruct((B, L, W), BF16),
        scratch_shapes=[
            pltpu.VMEM((SUBLANES + rows, W), F32),
            pltpu.VMEM((n_sub, Tc, 3 * W), BF16),
            pltpu.VMEM((n_sub, Tc, W), BF16),
            pltpu.VMEM((n_sub, H, dh + SUBLANES, Tc), BF16),
            pltpu.VMEM((H, dh + SUBLANES, dh), F32),
            pltpu.VMEM((H, 1, LANES), F32),
        ],
        compiler_params=_params("parallel", "arbitrary"),
        name="mlstm",
    )(proj, proj, conv_w.astype(F32), conv_b.reshape(1, W).astype(F32), wq.astype(BF16), wk.astype(BF16),
      wv.astype(BF16), jnp.transpose(wv, (0, 2, 1)).astype(BF16), gw, gwt, gbc, gbr, norm_g.reshape(1, W).astype(F32), skip.reshape(1, W).astype(F32))


def _attn_kernel(q_ref, k_ref, vt_ref, z_ref, bias_ref, lam_ref, ng_ref, o_ref, m_scr, acc_scr, *,
                 mode, out_scale, blk, bias_index):
    L = q_ref.shape[2]
    n = L // blk
    lane = lax.broadcasted_iota(jnp.int32, (1, LANES), 1)
    map0 = lane < ATTN_HEAD_DIM
    feat = lax.broadcasted_iota(jnp.int32, (LANES, 1), 0)
    ones_rows = jnp.ones((SUBLANES, blk), BF16)

    def stacked_q(qi):
        q = q_ref[0, 0, qi * blk:(qi + 1) * blk, :]
        zero = jnp.zeros_like(q)
        return jnp.concatenate([jnp.where(map0, q, zero), jnp.where(map0, zero, q)], axis=0)

    def scores(qi, j):
        s = lax.dot_general(k_ref[0, 0, j * blk:(j + 1) * blk, :], q2[qi], (((1,), (1,)), ((), ())),
                            preferred_element_type=F32)
        e = bias_index(qi - j)
        return s if e is None else s + bias_ref[e]

    def update(qi, j, s):
        m_cur = jnp.max(s, axis=0, keepdims=True)
        m_next = m_cur if j == 0 else jnp.maximum(m_scr[qi], m_cur)
        p = jnp.exp2(s - m_next).astype(BF16)
        pv = jnp.dot(jnp.concatenate([vt_ref[0, 0, :, j * blk:(j + 1) * blk], ones_rows], axis=0), p,
                     preferred_element_type=F32)
        if j == 0:
            acc_scr[qi] = pv
        else:
            acc_scr[qi] = jnp.exp2(m_scr[qi] - m_next) * acc_scr[qi] + pv
        m_scr[qi] = m_next

    def finish(qi):
        rows = slice(qi * blk, (qi + 1) * blk)
        acc = acc_scr[qi]
        o = acc[:LANES] / acc[LANES:LANES + 1]
        o0, o1 = o[:, :blk], o[:, blk:]
        if mode == "dilated":
            out = jnp.transpose(jnp.where(feat < ATTN_HEAD_DIM, o0, o1))
        else:
            out = jnp.transpose(o0 - lam_ref[:, 0:1] * o1)
            ms = jnp.mean(out * out, axis=-1, keepdims=True)
            out = out * lax.rsqrt(ms + HEAD_NORM_EPS) * ng_ref[...] * out_scale
        o_ref[0, rows, :] = (out * _silu(z_ref[0, 0, rows, :].astype(F32))).astype(o_ref.dtype)

    steps = []
    for a in range(n // 2):
        b = n - 1 - a
        ja = 0
        for jb in range(b + 1):
            steps.append((b, jb))
            if ja <= a and (jb + 1) * (a + 1) >= (ja + 1) * (b + 1):
                steps.append((a, ja))
                ja += 1
    q2 = {qi: stacked_q(qi) for qi in range(n)}
    pending = {}
    for t in range(min(ATTN_LOOKAHEAD, len(steps))):
        pending[t] = scores(*steps[t])
    for t, (qi, j) in enumerate(steps):
        if t + ATTN_LOOKAHEAD < len(steps):
            pending[t + ATTN_LOOKAHEAD] = scores(*steps[t + ATTN_LOOKAHEAD])
        update(qi, j, pending.pop(t))
        if j == qi:
            finish(qi)


def _dilated_multiplicity(delta):
    mult = np.zeros(delta.shape, np.int64)
    for window, dil in DIL_PAIRS:
        mult += (delta >= 0) & (delta % dil == 0) & (delta <= window)
    return mult


def _bias_tables(L, blk, mode):
    n = L // blk
    key = np.arange(blk)[:, None]
    qry = np.arange(blk)[None, :]
    tables, index = [], {}
    for d in range(n):
        delta = d * blk + qry - key
        mult = _dilated_multiplicity(delta) if mode == "dilated" else (delta >= 0).astype(np.int64)
        t = np.where(mult > 0, np.log2(np.maximum(mult, 1)), NEG_BIG).astype(np.float32)
        if not t.any():
            index[d] = None
            continue
        for e, other in enumerate(tables):
            if np.array_equal(t, other):
                index[d] = e
                break
        else:
            index[d] = len(tables)
            tables.append(t)
    stacked = np.stack(tables)
    return jnp.asarray(np.concatenate([stacked, stacked], axis=2)), index


def _attention(qkz, vt, q0, k0, v0, z0, n_blocks, blk, lam, norm_g, mode, out_scale):
    B, _, L, _ = qkz.shape
    bias, index = _bias_tables(L, blk, mode)
    n_q = L // blk
    assert n_q % 2 == 0
    kern = functools.partial(_attn_kernel, mode=mode, out_scale=out_scale, blk=blk, bias_index=index.get)
    c2 = lambda b, n: (0, 0)
    return pl.pallas_call(
        kern,
        grid=(B, n_blocks),
        in_specs=[
            pl.BlockSpec((1, 1, L, LANES), lambda b, n: (b, q0 + n, 0, 0)),
            pl.BlockSpec((1, 1, L, LANES), lambda b, n: (b, k0 + n, 0, 0)),
            pl.BlockSpec((1, 1, LANES, L), lambda b, n: (b, v0 + n, 0, 0)),
            pl.BlockSpec((1, 1, L, LANES), lambda b, n: (b, z0 + n, 0, 0)),
            pl.BlockSpec(bias.shape, lambda b, n: (0, 0, 0)),
            pl.BlockSpec((1, LANES), c2),
            pl.BlockSpec((1, LANES), c2),
        ],
        out_specs=pl.BlockSpec((1, L, LANES), lambda b, n: (b, 0, n)),
        out_shape=jax.ShapeDtypeStruct((B, L, n_blocks * LANES), BF16),
        scratch_shapes=[
            pltpu.VMEM((n_q, 1, 2 * blk), F32),
            pltpu.VMEM((n_q, LANES + SUBLANES, 2 * blk), F32),
        ],
        compiler_params=_params("parallel", "parallel"),
        name="attn_" + mode,
    )(qkz, qkz, vt, qkz, bias, lam, norm_g)


def _rope_tables(L):
    dh = ATTN_HEAD_DIM
    inv = ROPE_THETA ** (-jnp.arange(0, dh, 2, dtype=F32) / dh)
    ang = jnp.arange(L, dtype=F32)[:, None] * inv[None, :]
    cos, sin = jnp.cos(ang), jnp.sin(ang)
    reps = LANES // dh
    cos_t = jnp.tile(jnp.concatenate([cos, cos], axis=-1), (1, reps))
    sin_t = jnp.tile(jnp.concatenate([-sin, sin], axis=-1), (1, reps))
    return cos_t, sin_t


def _even_layer(h, pre_g, w_in, s5_params, ml_params):
    proj = _norm_proj(h, pre_g, w_in.astype(BF16))
    return _s5_branch(proj, *s5_params), _mlstm_branch(proj, *ml_params)


def _odd_layer(prev, h, pre_g, w_in, lq1, lk1, lq2, lk2, diff_norm, layer_idx):
    B, L, D = h.shape
    nblk = PROJ_COLS // LANES
    cos_t, sin_t = _rope_tables(L)
    h, qkz, vt = _out_norm_proj_rope(*prev, h, pre_g, w_in.astype(BF16), cos_t, sin_t)
    blk = min(ATTN_BLOCK, L)
    lam_init = 0.8 - 0.6 * math.exp(-0.3 * layer_idx)
    lam = (jnp.exp(jnp.sum(lq1.astype(F32) * lk1.astype(F32)))
           - jnp.exp(jnp.sum(lq2.astype(F32) * lk2.astype(F32))) + lam_init)
    lam_row = jnp.full((1, LANES), lam, F32)
    ones_row = jnp.ones((1, LANES), F32)
    base = [b for _, b in ODD_CHUNKS]
    c_out = _attention(qkz, vt, base[0], base[1], base[2], base[3], nblk, blk, ones_row, ones_row, "dilated", 1.0)
    d_out = _attention(qkz, vt, base[4], base[5], base[6], base[7], nblk, blk, lam_row,
                       diff_norm.reshape(1, LANES).astype(F32), "diff", 1.0 - lam_init)
    return h, c_out, d_out


def kernel(x, pre_norm, post_norm, w_in_ab, s5_lambda_re, s5_lambda_im, s5_log_dt, s5_b_re, s5_b_im,
           s5_c_re, s5_c_im, s5_d, s5_glu_w, s5_glu_b, ml_conv_w, ml_conv_b, ml_wq, ml_wk, ml_wv,
           ml_gate_w, ml_gate_b, ml_norm, ml_skip, w_out_ab, w_in_cd, diff_lq1, diff_lk1, diff_lq2,
           diff_lk2, diff_norm, w_out_cd):
    depth = pre_norm.shape[0]
    h = x
    prev = None
    for l in range(depth):
        i = l // 2
        if l % 2 == 0:
            if prev is not None:
                h = _out_proj(*prev, h)
            s5_params = (s5_lambda_re[i], s5_lambda_im[i], s5_log_dt[i], s5_b_re[i], s5_b_im[i],
                         s5_c_re[i], s5_c_im[i], s5_d[i], s5_glu_w[i], s5_glu_b[i])
            ml_params = (ml_conv_w[i], ml_conv_b[i], ml_wq[i], ml_wk[i], ml_wv[i], ml_gate_w[i],
                         ml_gate_b[i], ml_norm[i], ml_skip[i])
            a, b = _even_layer(h, pre_norm[l], w_in_ab[i], s5_params, ml_params)
            prev = (a, b, w_out_ab[i], post_norm[l])
        else:
            h, c, d = _odd_layer(prev, h, pre_norm[l], w_in_cd[i], diff_lq1[i], diff_lk1[i], diff_lq2[i],
                                 diff_lk2[i], diff_norm[i], l)
            prev = (c, d, w_out_cd[i], post_norm[l])
    return _out_proj(*prev, h)
```

```python
import functools
import math

import numpy as np
import jax
import jax.numpy as jnp
from jax import lax
from jax.experimental import pallas as pl
from jax.experimental.pallas import tpu as pltpu

F32 = jnp.float32
BF16 = jnp.bfloat16

S5_GROUP = 16
S5_STATE = 64
ML_HEADS = 4
ML_CONV = 4
DIL_PAIRS = ((128, 1), (512, 4), (2048, 16))
ROPE_THETA = 10000.0
NORM_EPS = 1e-6
HEAD_NORM_EPS = 1e-5
ATTN_HEAD_DIM = 64
QUERY_SCALE = ATTN_HEAD_DIM ** -0.5 * math.log2(math.e)

LANES = 128
SUBLANES = 8
VMEM_LIMIT_BYTES = 56 * 1024 * 1024

NEG_BIG = -1e30

PROJ_ROWS = 1024
OUT_PROJ_ROWS = 2048
PROJ_COLS = 512
S5_BATCH = SUBLANES
S5_TIME = 256
S5_STRIDE = 2
S5_PIECE = 32
ML_CHUNK = 256
ML_SUB = 4
ATTN_BLOCK = 256
ATTN_LOOKAHEAD = 3


def _params(*semantics):
    return pltpu.CompilerParams(dimension_semantics=semantics, vmem_limit_bytes=VMEM_LIMIT_BYTES)


def _silu(x):
    return x * jax.nn.sigmoid(x)


def _norm_proj_kernel(x_ref, g_ref, w_ref, o_ref, z_scr):
    x = x_ref[0]
    ms = jnp.mean(x * x, axis=-1, keepdims=True)
    z_scr[...] = (x * lax.rsqrt(ms + NORM_EPS) * g_ref[...]).astype(BF16)
    n_out = w_ref.shape[1]
    for c in range(n_out // PROJ_COLS):
        cols = slice(c * PROJ_COLS, (c + 1) * PROJ_COLS)
        o_ref[0, :, cols] = jnp.dot(z_scr[...], w_ref[:, cols], preferred_element_type=F32).astype(o_ref.dtype)


def _norm_proj(x, g, w):
    B, L, D = x.shape
    N = w.shape[1]
    tm = min(PROJ_ROWS, L)
    return pl.pallas_call(
        _norm_proj_kernel,
        grid=(B, L // tm),
        in_specs=[
            pl.BlockSpec((1, tm, D), lambda b, i: (b, i, 0)),
            pl.BlockSpec((1, D), lambda b, i: (0, 0)),
            pl.BlockSpec((D, N), lambda b, i: (0, 0)),
        ],
        out_specs=pl.BlockSpec((1, tm, N), lambda b, i: (b, i, 0)),
        out_shape=jax.ShapeDtypeStruct((B, L, N), BF16),
        scratch_shapes=[pltpu.VMEM((tm, D), BF16)],
        compiler_params=_params("parallel", "parallel"),
        name="norm_proj_even",
    )(x, g.reshape(1, D), w)


def _swap_halves(x, first_half):
    return jnp.where(first_half, pltpu.roll(x, LANES - ATTN_HEAD_DIM // 2, 1), pltpu.roll(x, ATTN_HEAD_DIM // 2, 1))


ODD_CHUNKS = (("q", 0), ("k", 4), ("v", 0), ("z", 8), ("q", 12), ("k", 16), ("v", 4), ("z", 20))


def _out_norm_proj_rope_kernel(a_ref, b_ref, wa_ref, wb_ref, gpost_ref, h_ref, g_ref, w_ref, cos_ref, sin_ref,
                               hout_ref, o_ref, vt_ref, z_scr):
    y = jnp.dot(a_ref[0], wa_ref[...], preferred_element_type=F32)
    y = y + jnp.dot(b_ref[0], wb_ref[...], preferred_element_type=F32)
    ms = jnp.mean(y * y, axis=-1, keepdims=True)
    x = h_ref[0] + y * lax.rsqrt(ms + NORM_EPS) * gpost_ref[...]
    hout_ref[0] = x
    ms = jnp.mean(x * x, axis=-1, keepdims=True)
    z_scr[...] = (x * lax.rsqrt(ms + NORM_EPS) * g_ref[...]).astype(BF16)
    slabs = PROJ_COLS // LANES
    lane = lax.broadcasted_iota(jnp.int32, (1, LANES), 1)
    first_half = (lane % ATTN_HEAD_DIM) < (ATTN_HEAD_DIM // 2)
    for c, (kind, base) in enumerate(ODD_CHUNKS):
        y = jnp.dot(z_scr[...], w_ref[:, c * PROJ_COLS:(c + 1) * PROJ_COLS], preferred_element_type=F32)
        for s in range(slabs):
            t = y[:, s * LANES:(s + 1) * LANES]
            if kind in ("q", "k"):
                t = t * cos_ref[...] + _swap_halves(t, first_half) * sin_ref[...]
            if kind == "q":
                t = t * QUERY_SCALE
            if kind == "v":
                vt_ref[0, base + s] = jnp.transpose(t).astype(vt_ref.dtype)
            else:
                o_ref[0, base + s] = t.astype(o_ref.dtype)


def _out_norm_proj_rope(a, b, w_out, g_post, h, g, w, cos_t, sin_t):
    B, L, D = h.shape
    Wb = a.shape[2]
    N = w.shape[1]
    assert N == len(ODD_CHUNKS) * PROJ_COLS
    tm = min(PROJ_ROWS, L)
    n_v = sum(kind == "v" for kind, _ in ODD_CHUNKS) * (PROJ_COLS // LANES)
    n_o = N // LANES - n_v
    row = lambda b_, i: (b_, i, 0)
    const = lambda b_, i: (0, 0)
    return pl.pallas_call(
        _out_norm_proj_rope_kernel,
        grid=(B, L // tm),
        in_specs=[
            pl.BlockSpec((1, tm, Wb), row),
            pl.BlockSpec((1, tm, Wb), row),
            pl.BlockSpec((Wb, D), const),
            pl.BlockSpec((Wb, D), const),
            pl.BlockSpec((1, D), const),
            pl.BlockSpec((1, tm, D), row),
            pl.BlockSpec((1, D), const),
            pl.BlockSpec((D, N), const),
            pl.BlockSpec((tm, LANES), lambda b_, i: (i, 0)),
            pl.BlockSpec((tm, LANES), lambda b_, i: (i, 0)),
        ],
        out_specs=[
            pl.BlockSpec((1, tm, D), row),
            pl.BlockSpec((1, n_o, tm, LANES), lambda b_, i: (b_, 0, i, 0)),
            pl.BlockSpec((1, n_v, LANES, tm), lambda b_, i: (b_, 0, 0, i)),
        ],
        out_shape=[
            jax.ShapeDtypeStruct((B, L, D), F32),
            jax.ShapeDtypeStruct((B, n_o, L, LANES), BF16),
            jax.ShapeDtypeStruct((B, n_v, LANES, L), BF16),
        ],
        scratch_shapes=[pltpu.VMEM((tm, D), BF16)],
        compiler_params=_params("parallel", "parallel"),
        name="out_norm_proj_odd",
    )(a, b, w_out[:Wb].astype(BF16), w_out[Wb:].astype(BF16), g_post.reshape(1, D), h, g.reshape(1, D), w,
      cos_t, sin_t)


def _out_proj_kernel(a_ref, b_ref, wa_ref, wb_ref, g_ref, h_ref, o_ref):
    y = jnp.dot(a_ref[0], wa_ref[...], preferred_element_type=F32)
    y = y + jnp.dot(b_ref[0], wb_ref[...], preferred_element_type=F32)
    ms = jnp.mean(y * y, axis=-1, keepdims=True)
    o_ref[0] = h_ref[0] + y * lax.rsqrt(ms + NORM_EPS) * g_ref[...]


def _out_proj(a, b, w_out, g, h):
    B, L, W = a.shape
    D = w_out.shape[1]
    tm = min(OUT_PROJ_ROWS, L)
    return pl.pallas_call(
        _out_proj_kernel,
        grid=(B, L // tm),
        in_specs=[
            pl.BlockSpec((1, tm, W), lambda b_, i: (b_, i, 0)),
            pl.BlockSpec((1, tm, W), lambda b_, i: (b_, i, 0)),
            pl.BlockSpec((W, D), lambda b_, i: (0, 0)),
            pl.BlockSpec((W, D), lambda b_, i: (0, 0)),
            pl.BlockSpec((1, D), lambda b_, i: (0, 0)),
            pl.BlockSpec((1, tm, D), lambda b_, i: (b_, i, 0)),
        ],
        out_specs=pl.BlockSpec((1, tm, D), lambda b_, i: (b_, i, 0)),
        out_shape=jax.ShapeDtypeStruct((B, L, D), F32),
        compiler_params=_params("parallel", "parallel"),
        name="out_proj",
    )(a, b, w_out[:W].astype(BF16), w_out[W:].astype(BF16), g.reshape(1, D), h)


def _s5_kernel(u_ref, z_ref, bin_ref, wx_ref, wu_ref, ar_ref, ai_ref, d_ref, gw_ref, gb_ref, o_ref,
               tmp_scr, us_scr, zs_scr, xs_scr, ys_scr, os_scr, st_scr):
    nb, T, W = u_ref.shape
    n_blk = W // LANES
    r = S5_STRIDE
    n_grp = T // r
    half = xs_scr.shape[2] // 2
    n_t = n_grp // S5_PIECE
    piece_rows = S5_PIECE * nb

    @pl.when(pl.program_id(1) == 0)
    def _():
        st_scr[...] = jnp.zeros_like(st_scr)

    def rows_of(i):
        return slice(i * piece_rows, (i + 1) * piece_rows)

    piece_time = S5_PIECE * r

    def load_piece(i):
        steps = slice(i * piece_time, (i + 1) * piece_time)
        for s_idx, (src, dst) in enumerate(((u_ref, us_scr), (z_ref, zs_scr))):
            for b in range(nb):
                xb = src[b, steps, :].astype(F32)
                for k in range(n_blk):
                    tmp_scr[s_idx, b, k] = xb[:, k * LANES:(k + 1) * LANES]
                for k in range(n_blk):
                    for j in range(r):
                        dst[k, j, pl.ds(i * piece_rows + b, S5_PIECE, stride=nb), :] = (
                            tmp_scr[s_idx, b, k, pl.ds(j, S5_PIECE, stride=r), :])

    def store_piece(i):
        steps = slice(i * piece_time, (i + 1) * piece_time)
        for b in range(nb):
            for k in range(n_blk):
                for j in range(r):
                    tmp_scr[2, b, k, pl.ds(j, S5_PIECE, stride=r), :] = (
                        os_scr[k, j, pl.ds(i * piece_rows + b, S5_PIECE, stride=nb), :])
            o_ref[b, steps, :] = jnp.concatenate([tmp_scr[2, b, k] for k in range(n_blk)],
                                                 axis=1).astype(o_ref.dtype)

    def group_inputs(i, k):
        return jnp.concatenate([us_scr[k, j, rows_of(i), :] for j in range(r)], axis=1).astype(BF16)

    def input_matmul(i, k):
        xs_scr[k, rows_of(i), :] = jnp.dot(group_inputs(i, k), bin_ref[k], preferred_element_type=F32)

    def scan(i, k):
        ar, ai = ar_ref[k], ai_ref[k]
        xr, xi = st_scr[k, 0], st_scr[k, 1]
        for t in range(S5_PIECE):
            rs = slice(i * piece_rows + t * nb, i * piece_rows + (t + 1) * nb)
            cr, ci = xs_scr[k, rs, 0:half], xs_scr[k, rs, half:2 * half]
            xs_scr[k, rs, 0:half] = xr
            xs_scr[k, rs, half:2 * half] = xi
            xr, xi = ar * xr - ai * xi + cr, ar * xi + ai * xr + ci
        st_scr[k, 0] = xr
        st_scr[k, 1] = xi

    def output_matmul(i, k):
        y = jnp.dot(xs_scr[k, rows_of(i), :].astype(BF16), wx_ref[k], preferred_element_type=F32)
        y = y + jnp.dot(group_inputs(i, k), wu_ref[k], preferred_element_type=F32)
        for j in range(r):
            ys_scr[k, j, rows_of(i), :] = y[:, j * LANES:(j + 1) * LANES]

    def glu(i):
        def tokens(ref):
            return jnp.concatenate(
                [jnp.concatenate([ref[k, j, rows_of(i), :] for k in range(n_blk)], axis=1) for j in range(r)], axis=0)
        y = tokens(ys_scr) + d_ref[...] * tokens(us_scr)
        g = jax.nn.gelu(y)
        gate = jnp.dot(g.astype(BF16), gw_ref[...], preferred_element_type=F32) + gb_ref[...]
        out = g * jax.nn.sigmoid(gate) * _silu(tokens(zs_scr))
        for j in range(r):
            for k in range(n_blk):
                os_scr[k, j, rows_of(i), :] = out[j * piece_rows:(j + 1) * piece_rows, k * LANES:(k + 1) * LANES]

    def drain(p):
        i, k = pieces[p]
        output_matmul(i, k)
        if k == n_blk - 1:
            glu(i)
            store_piece(i)

    pieces = [(i, k) for i in range(n_t) for k in range(n_blk)]
    load_piece(0)
    input_matmul(*pieces[0])
    for p, (i, k) in enumerate(pieces):
        if k == 0 and i + 1 < n_t:
            load_piece(i + 1)
        if p + 1 < len(pieces):
            input_matmul(*pieces[p + 1])
        if p >= 1:
            drain(p - 1)
        scan(i, k)
    drain(len(pieces) - 1)


def _s5_weights(lam_re, lam_im, log_dt, b_re, b_im, c_re, c_im):
    G, P = lam_re.shape
    r = S5_STRIDE
    gpb = LANES // S5_GROUP
    n_blk = G // gpb
    lr, li = lam_re.astype(F32), lam_im.astype(F32)
    dt = jnp.exp(log_dt.astype(F32))[:, None]
    mag = jnp.exp(lr * dt)
    ab_r, ab_i = mag * jnp.cos(li * dt), mag * jnp.sin(li * dt)
    den = lr * lr + li * li
    f_r = ((ab_r - 1.0) * lr + ab_i * li) / den
    f_i = (ab_i * lr - (ab_r - 1.0) * li) / den
    br, bi = b_re.astype(F32), b_im.astype(F32)
    bb_r = f_r[..., None] * br - f_i[..., None] * bi
    bb_i = f_r[..., None] * bi + f_i[..., None] * br
    cr, ci = c_re.astype(F32), c_im.astype(F32)
    eye = jnp.eye(gpb, dtype=F32)

    pows = [(jnp.ones_like(ab_r), jnp.zeros_like(ab_i))]
    for _ in range(r):
        pr, pi = pows[-1]
        pows.append((pr * ab_r - pi * ab_i, pr * ab_i + pi * ab_r))

    def pack_in(xr, xi):
        x = jnp.stack([xr, xi]).reshape(2, n_blk, gpb, P, S5_GROUP)
        x = jnp.transpose(x, (1, 2, 4, 0, 3))
        x = x[:, :, :, :, None, :] * eye[None, :, None, None, :, None]
        return x.reshape(n_blk, LANES, 2 * gpb * P)

    def pack_out(er, ei):
        x = jnp.stack([er, ei]).reshape(2, n_blk, gpb, S5_GROUP, P)
        x = jnp.transpose(x, (1, 0, 2, 4, 3))
        x = x[:, :, :, :, None, :] * eye[None, None, :, None, :, None]
        return x.reshape(n_blk, 2 * gpb * P, LANES)

    def pack_direct(kq):
        x = jnp.transpose(kq.reshape(n_blk, gpb, S5_GROUP, S5_GROUP), (0, 1, 3, 2))
        x = x[:, :, :, None, :] * eye[None, :, None, :, None]
        return x.reshape(n_blk, LANES, LANES)

    w_in = []
    for j in range(r):
        pr, pi = pows[r - 1 - j]
        w_in.append(pack_in(pr[..., None] * bb_r - pi[..., None] * bb_i, pr[..., None] * bb_i + pi[..., None] * bb_r))
    w_in = jnp.concatenate(w_in, axis=1).astype(BF16)

    w_x = []
    for j in range(r):
        pr, pi = pows[j + 1]
        er = cr * pr[:, None, :] - ci * pi[:, None, :]
        ei = cr * pi[:, None, :] + ci * pr[:, None, :]
        w_x.append(pack_out(er, -ei))
    w_x = jnp.concatenate(w_x, axis=2).astype(BF16)

    direct = []
    for q in range(r):
        pr, pi = pows[q]
        er = cr * pr[:, None, :] - ci * pi[:, None, :]
        ei = cr * pi[:, None, :] + ci * pr[:, None, :]
        direct.append(pack_direct(jnp.einsum('gop,gpi->goi', er, bb_r) - jnp.einsum('gop,gpi->goi', ei, bb_i)))
    zero = jnp.zeros_like(direct[0])
    w_u = jnp.concatenate(
        [jnp.concatenate([direct[j - i] if i <= j else zero for j in range(r)], axis=2) for i in range(r)],
        axis=1).astype(BF16)

    ar_r, ai_r = pows[r]
    ar = jnp.broadcast_to(ar_r.reshape(n_blk, 1, gpb * P), (n_blk, S5_BATCH, gpb * P))
    ai = jnp.broadcast_to(ai_r.reshape(n_blk, 1, gpb * P), (n_blk, S5_BATCH, gpb * P))
    return w_in, w_x, w_u, ar, ai


def _s5_branch(proj, lam_re, lam_im, log_dt, b_re, b_im, c_re, c_im, d_skip, glu_w, glu_b):
    B, L, N = proj.shape
    W = N // 4
    nb = S5_BATCH
    T = min(S5_TIME, L)
    r = S5_STRIDE
    n_blk = W // LANES
    w_in, w_x, w_u, ar, ai = _s5_weights(lam_re, lam_im, log_dt, b_re, b_im, c_re, c_im)
    S2 = w_in.shape[2]
    rows = nb * T // r
    const3 = lambda g, c: (0, 0, 0)
    const2 = lambda g, c: (0, 0)
    return pl.pallas_call(
        _s5_kernel,
        grid=(B // nb, L // T),
        in_specs=[
            pl.BlockSpec((nb, T, W), lambda g, c: (g, c, 0)),
            pl.BlockSpec((nb, T, W), lambda g, c: (g, c, 1)),
            pl.BlockSpec(w_in.shape, const3),
            pl.BlockSpec(w_x.shape, const3),
            pl.BlockSpec(w_u.shape, const3),
            pl.BlockSpec(ar.shape, const3),
            pl.BlockSpec(ai.shape, const3),
            pl.BlockSpec((1, W), const2),
            pl.BlockSpec((W, W), const2),
            pl.BlockSpec((1, W), const2),
        ],
        out_specs=pl.BlockSpec((nb, T, W), lambda g, c: (g, c, 0)),
        out_shape=jax.ShapeDtypeStruct((B, L, W), BF16),
        scratch_shapes=[
            pltpu.VMEM((3, nb, n_blk, S5_PIECE * r, LANES), F32),
            pltpu.VMEM((n_blk, r, rows, LANES), F32),
            pltpu.VMEM((n_blk, r, rows, LANES), F32),
            pltpu.VMEM((n_blk, rows, S2), F32),
            pltpu.VMEM((n_blk, r, rows, LANES), F32),
            pltpu.VMEM((n_blk, r, rows, LANES), F32),
            pltpu.VMEM((n_blk, 2, nb, S2 // 2), F32),
        ],
        compiler_params=_params("parallel", "arbitrary"),
        name="s5_scan",
    )(proj, proj, w_in, w_x, w_u, ar, ai, d_skip.reshape(1, W).astype(F32), glu_w.astype(BF16),
      glu_b.reshape(1, W).astype(F32))


def _log_sigmoid(x):
    return jnp.minimum(x, 0.0) - jnp.log1p(jnp.exp(-jnp.abs(x)))


def _split3(x):
    hi = x.astype(BF16)
    r1 = x - hi.astype(F32)
    mid = r1.astype(BF16)
    lo = (r1 - mid.astype(F32)).astype(BF16)
    return hi, mid, lo


def _mlstm_kernel(x_ref, z_ref, cw_ref, cb_ref, wq_ref, wk_ref, wv_ref, wvt_ref, gw_ref, gwt_ref, gbc_ref, gbr_ref,
                  ng_ref, sk_ref, o_ref, xe_scr, qkv_scr, ks_scr, vta_scr, ca_scr, m_scr):
    n_sub, Tc = qkv_scr.shape[0], qkv_scr.shape[1]
    W = x_ref.shape[2]
    H = wq_ref.shape[0]
    dh = W // H
    pad = SUBLANES
    heads = range(H)
    nt = (((1,), (1,)), ((), ()))

    @pl.when(pl.program_id(1) == 0)
    def _():
        xe_scr[0:pad, :] = jnp.zeros((pad, W), F32)
        ca_scr[...] = jnp.zeros_like(ca_scr)
        m_scr[...] = jnp.zeros_like(m_scr)
        vta_scr[:, :, dh:, :] = jnp.ones((n_sub, H, pad, Tc), BF16)

    xe_scr[pad:, :] = x_ref[0].astype(F32)

    ti = lax.broadcasted_iota(jnp.int32, (Tc, Tc), 0)
    si = lax.broadcasted_iota(jnp.int32, (Tc, Tc), 1)
    lower = jnp.where(si <= ti, 1.0, 0.0).astype(BF16)
    allowed = ti <= si
    upper = jnp.where(allowed, 1.0, 0.0).astype(BF16)

    def front(s):
        r0 = s * Tc
        xb = x_ref[0, r0:r0 + Tc, :]
        xc = cb_ref[...] + sum(xe_scr[pl.ds(pad + r0 - (ML_CONV - 1) + j, Tc), :] * cw_ref[j:j + 1, :]
                               for j in range(ML_CONV))
        xc = _silu(xc)
        xcb = xc.astype(BF16)
        for h in heads:
            cols = slice(h * dh, (h + 1) * dh)
            q = jnp.dot(xcb[:, cols], wq_ref[h], preferred_element_type=F32)
            k = jnp.dot(xcb[:, cols], wk_ref[h], preferred_element_type=F32)
            v = jnp.dot(xb[:, cols], wv_ref[h], preferred_element_type=F32)
            qkv_scr[s, :, h * dh:(h + 1) * dh] = q.astype(BF16)
            qkv_scr[s, :, W + h * dh:W + (h + 1) * dh] = k.astype(BF16)
            qkv_scr[s, :, 2 * W + h * dh:2 * W + (h + 1) * dh] = v.astype(BF16)
            vta_scr[s, h, :dh, :] = lax.dot_general(wvt_ref[h], xb[:, cols], nt,
                                                    preferred_element_type=F32).astype(BF16)
            ks_scr[s, :, cols] = (k * (dh ** -0.5)).astype(BF16)
        qkv = qkv_scr[s]
        g_col = jnp.dot(qkv, gw_ref[...], preferred_element_type=F32) + gbc_ref[...]
        g_row = lax.dot_general(gwt_ref[...], qkv, nt, preferred_element_type=F32) + gbr_ref[...]
        b_col = sum(jnp.dot(lower, p, preferred_element_type=F32) for p in _split3(_log_sigmoid(g_col)))
        b_row = sum(jnp.dot(p, upper, preferred_element_type=F32) for p in _split3(_log_sigmoid(g_row)))
        return xc, g_col, g_row, b_col, b_row

    def rec_matmuls(s):
        q_h = [qkv_scr[s, :, h * dh:(h + 1) * dh] for h in heads]
        ks_h = [ks_scr[s, :, h * dh:(h + 1) * dh] for h in heads]
        s_qk = [lax.dot_general(ks_h[h], q_h[h], nt, preferred_element_type=F32) for h in heads]
        q_ca = [lax.dot_general(ca_scr[h].astype(BF16), q_h[h], nt, preferred_element_type=F32)
                for h in heads]
        return ks_h, s_qk, q_ca

    def rec_rest(s, mm, xc, g_col, g_row, b_col, b_row):
        ks_h, s_qk, q_ca = mm
        rows = slice(s * Tc, (s + 1) * Tc)
        m_prev = [m_scr[h][:, 0:1] for h in heads]
        br = [b_row[H + h:H + h + 1, :] for h in heads]
        li = [g_row[h:h + 1, :] for h in heads]
        b_last = [br[h][:, Tc - 1:Tc] for h in heads]
        dmat, sc, floor = [], [], []
        for h in heads:
            c_col = g_col[:, h:h + 1] - b_col[:, H + h:H + h + 1]
            log_d = jnp.where(allowed, br[h] + c_col, NEG_BIG)
            inter = br[h] + m_prev[h]
            m_t = jnp.maximum(inter, jnp.max(log_d, axis=0, keepdims=True))
            dmat.append(jnp.exp(log_d - m_t))
            sc.append(jnp.exp(inter - m_t))
            floor.append(jnp.exp(-m_t))
        pv = [jnp.dot(vta_scr[s, h], (s_qk[h] * dmat[h]).astype(BF16), preferred_element_type=F32) for h in heads]
        for h in heads:
            w_log = b_last[h] - br[h] + li[h]
            m_new = jnp.maximum(b_last[h] + m_prev[h], jnp.max(w_log, axis=1, keepdims=True))
            weighted = (vta_scr[s, h].astype(F32) * jnp.exp(w_log - m_new)).astype(BF16)
            decay = jnp.exp(b_last[h] + m_prev[h] - m_new)
            ca_scr[h] = decay * ca_scr[h] + jnp.dot(weighted, ks_h[h], preferred_element_type=F32)
            m_scr[h] = jnp.broadcast_to(m_new, (1, LANES))
        for h in heads:
            cols = slice(h * dh, (h + 1) * dh)
            tot = sc[h] * q_ca[h] + pv[h]
            hh = tot[:dh] / jnp.maximum(jnp.abs(tot[dh:dh + 1]), floor[h])
            mu = jnp.mean(hh, axis=0, keepdims=True)
            cen = hh - mu
            var = jnp.mean(cen * cen, axis=0, keepdims=True)
            hn = jnp.transpose(cen * lax.rsqrt(var + HEAD_NORM_EPS))
            out = hn * ng_ref[:, cols] + sk_ref[:, cols] * xc[:, cols]
            o_ref[0, rows, cols] = (out * _silu(z_ref[0, rows, cols].astype(F32))).astype(o_ref.dtype)

    ready = front(0)
    for s in range(n_sub):
        mm = rec_matmuls(s)
        following = front(s + 1) if s + 1 < n_sub else None
        rec_rest(s, mm, *ready)
        ready = following
    xe_scr[0:pad, :] = xe_scr[n_sub * Tc:n_sub * Tc + pad, :]


def _mlstm_branch(proj, conv_w, conv_b, wq, wk, wv, gate_w, gate_b, norm_g, skip):
    B, L, N = proj.shape
    W = N // 4
    H = wq.shape[0]
    dh = W // H
    Tc = min(ML_CHUNK, L)
    n_sub = min(ML_SUB, L // Tc)
    rows = n_sub * Tc
    gw = jnp.zeros((3 * W, LANES), F32).at[:, :2 * H].set(gate_w.astype(F32)).astype(BF16)
    gwt = jnp.transpose(gate_w.astype(F32)).astype(BF16)
    gbc = jnp.zeros((1, LANES), F32).at[0, :2 * H].set(gate_b.astype(F32))
    gbr = gate_b.astype(F32).reshape(2 * H, 1)
    c2 = lambda b, c: (0, 0)
    c3 = lambda b, c: (0, 0, 0)
    return pl.pallas_call(
        _mlstm_kernel,
        grid=(B, L // rows),
        in_specs=[
            pl.BlockSpec((1, rows, W), lambda b, c: (b, c, 2)),
            pl.BlockSpec((1, rows, W), lambda b, c: (b, c, 3)),
            pl.BlockSpec((ML_CONV, W), c2),
            pl.BlockSpec((1, W), c2),
            pl.BlockSpec((H, dh, dh), c3),
            pl.BlockSpec((H, dh, dh), c3),
            pl.BlockSpec((H, dh, dh), c3),
            pl.BlockSpec((H, dh, dh), c3),
            pl.BlockSpec((3 * W, LANES), c2),
            pl.BlockSpec((2 * H, 3 * W), c2),
            pl.BlockSpec((1, LANES), c2),
            pl.BlockSpec((2 * H, 1), c2),
            pl.BlockSpec((1, W), c2),
            pl.BlockSpec((1, W), c2),
        ],
        out_specs=pl.BlockSpec((1, rows, W), lambda b, c: (b, c, 0)),
        out_shape=jax.ShapeDtypeStruct((B, L, W), BF16),
        scratch_shapes=[
            pltpu.VMEM((SUBLANES + rows, W), F32),
            pltpu.VMEM((n_sub, Tc, 3 * W), BF16),
            pltpu.VMEM((n_sub, Tc, W), BF16),
            pltpu.VMEM((n_sub, H, dh + SUBLANES, Tc), BF16),
            pltpu.VMEM((H, dh + SUBLANES, dh), F32),
            pltpu.VMEM((H, 1, LANES), F32),
        ],
        compiler_params=_params("parallel", "arbitrary"),
        name="mlstm",
    )(proj, proj, conv_w.astype(F32), conv_b.reshape(1, W).astype(F32), wq.astype(BF16), wk.astype(BF16),
      wv.astype(BF16), jnp.transpose(wv, (0, 2, 1)).astype(BF16), gw, gwt, gbc, gbr, norm_g.reshape(1, W).astype(F32), skip.reshape(1, W).astype(F32))


def _attn_kernel(q_ref, k_ref, vt_ref, z_ref, bias_ref, lam_ref, ng_ref, o_ref, m_scr, acc_scr, *,
                 mode, out_scale, blk, bias_index):
    L = q_ref.shape[2]
    n = L // blk
    lane = lax.broadcasted_iota(jnp.int32, (1, LANES), 1)
    map0 = lane < ATTN_HEAD_DIM
    feat = lax.broadcasted_iota(jnp.int32, (LANES, 1), 0)
    ones_rows = jnp.ones((SUBLANES, blk), BF16)

    def stacked_q(qi):
        q = q_ref[0, 0, qi * blk:(qi + 1) * blk, :]
        zero = jnp.zeros_like(q)
        return jnp.concatenate([jnp.where(map0, q, zero), jnp.where(map0, zero, q)], axis=0)

    def scores(qi, j):
        s = lax.dot_general(k_ref[0, 0, j * blk:(j + 1) * blk, :], q2[qi], (((1,), (1,)), ((), ())),
                            preferred_element_type=F32)
        e = bias_index(qi - j)
        return s if e is None else s + bias_ref[e]

    def update(qi, j, s):
        m_cur = jnp.max(s, axis=0, keepdims=True)
        m_next = m_cur if j == 0 else jnp.maximum(m_scr[qi], m_cur)
        p = jnp.exp2(s - m_next).astype(BF16)
        pv = jnp.dot(jnp.concatenate([vt_ref[0, 0, :, j * blk:(j + 1) * blk], ones_rows], axis=0), p,
                     preferred_element_type=F32)
        if j == 0:
            acc_scr[qi] = pv
        else:
            acc_scr[qi] = jnp.exp2(m_scr[qi] - m_next) * acc_scr[qi] + pv
        m_scr[qi] = m_next

    def finish(qi):
        rows = slice(qi * blk, (qi + 1) * blk)
        acc = acc_scr[qi]
        o = acc[:LANES] / acc[LANES:LANES + 1]
        o0, o1 = o[:, :blk], o[:, blk:]
        if mode == "dilated":
            out = jnp.transpose(jnp.where(feat < ATTN_HEAD_DIM, o0, o1))
        else:
            out = jnp.transpose(o0 - lam_ref[:, 0:1] * o1)
            ms = jnp.mean(out * out, axis=-1, keepdims=True)
            out = out * lax.rsqrt(ms + HEAD_NORM_EPS) * ng_ref[...] * out_scale
        o_ref[0, rows, :] = (out * _silu(z_ref[0, 0, rows, :].astype(F32))).astype(o_ref.dtype)

    steps = []
    for a in range(n // 2):
        b = n - 1 - a
        ja = 0
        for jb in range(b + 1):
            steps.append((b, jb))
            if ja <= a and (jb + 1) * (a + 1) >= (ja + 1) * (b + 1):
                steps.append((a, ja))
                ja += 1
    q2 = {qi: stacked_q(qi) for qi in range(n)}
    pending = {}
    for t in range(min(ATTN_LOOKAHEAD, len(steps))):
        pending[t] = scores(*steps[t])
    for t, (qi, j) in enumerate(steps):
        if t + ATTN_LOOKAHEAD < len(steps):
            pending[t + ATTN_LOOKAHEAD] = scores(*steps[t + ATTN_LOOKAHEAD])
        update(qi, j, pending.pop(t))
        if j == qi:
            finish(qi)


def _dilated_multiplicity(delta):
    mult = np.zeros(delta.shape, np.int64)
    for window, dil in DIL_PAIRS:
        mult += (delta >= 0) & (delta % dil == 0) & (delta <= window)
    return mult


def _bias_tables(L, blk, mode):
    n = L // blk
    key = np.arange(blk)[:, None]
    qry = np.arange(blk)[None, :]
    tables, index = [], {}
    for d in range(n):
        delta = d * blk + qry - key
        mult = _dilated_multiplicity(delta) if mode == "dilated" else (delta >= 0).astype(np.int64)
        t = np.where(mult > 0, np.log2(np.maximum(mult, 1)), NEG_BIG).astype(np.float32)
        if not t.any():
            index[d] = None
            continue
        for e, other in enumerate(tables):
            if np.array_equal(t, other):
                index[d] = e
                break
        else:
            index[d] = len(tables)
            tables.append(t)
    stacked = np.stack(tables)
    return jnp.asarray(np.concatenate([stacked, stacked], axis=2)), index


def _attention(qkz, vt, q0, k0, v0, z0, n_blocks, blk, lam, norm_g, mode, out_scale):
    B, _, L, _ = qkz.shape
    bias, index = _bias_tables(L, blk, mode)
    n_q = L // blk
    assert n_q % 2 == 0
    kern = functools.partial(_attn_kernel, mode=mode, out_scale=out_scale, blk=blk, bias_index=index.get)
    c2 = lambda b, n: (0, 0)
    return pl.pallas_call(
        kern,
        grid=(B, n_blocks),
        in_specs=[
            pl.BlockSpec((1, 1, L, LANES), lambda b, n: (b, q0 + n, 0, 0)),
            pl.BlockSpec((1, 1, L, LANES), lambda b, n: (b, k0 + n, 0, 0)),
            pl.BlockSpec((1, 1, LANES, L), lambda b, n: (b, v0 + n, 0, 0)),
            pl.BlockSpec((1, 1, L, LANES), lambda b, n: (b, z0 + n, 0, 0)),
            pl.BlockSpec(bias.shape, lambda b, n: (0, 0, 0)),
            pl.BlockSpec((1, LANES), c2),
            pl.BlockSpec((1, LANES), c2),
        ],
        out_specs=pl.BlockSpec((1, L, LANES), lambda b, n: (b, 0, n)),
        out_shape=jax.ShapeDtypeStruct((B, L, n_blocks * LANES), BF16),
        scratch_shapes=[
            pltpu.VMEM((n_q, 1, 2 * blk), F32),
            pltpu.VMEM((n_q, LANES + SUBLANES, 2 * blk), F32),
        ],
        compiler_params=_params("parallel", "parallel"),
        name="attn_" + mode,
    )(qkz, qkz, vt, qkz, bias, lam, norm_g)


def _rope_tables(L):
    dh = ATTN_HEAD_DIM
    inv = ROPE_THETA ** (-jnp.arange(0, dh, 2, dtype=F32) / dh)
    ang = jnp.arange(L, dtype=F32)[:, None] * inv[None, :]
    cos, sin = jnp.cos(ang), jnp.sin(ang)
    reps = LANES // dh
    cos_t = jnp.tile(jnp.concatenate([cos, cos], axis=-1), (1, reps))
    sin_t = jnp.tile(jnp.concatenate([-sin, sin], axis=-1), (1, reps))
    return cos_t, sin_t


def _even_layer(h, pre_g, w_in, s5_params, ml_params):
    proj = _norm_proj(h, pre_g, w_in.astype(BF16))
    return _s5_branch(proj, *s5_params), _mlstm_branch(proj, *ml_params)


def _odd_layer(prev, h, pre_g, w_in, lq1, lk1, lq2, lk2, diff_norm, layer_idx):
    B, L, D = h.shape
    nblk = PROJ_COLS // LANES
    cos_t, sin_t = _rope_tables(L)
    h, qkz, vt = _out_norm_proj_rope(*prev, h, pre_g, w_in.astype(BF16), cos_t, sin_t)
    blk = min(ATTN_BLOCK, L)
    lam_init = 0.8 - 0.6 * math.exp(-0.3 * layer_idx)
    lam = (jnp.exp(jnp.sum(lq1.astype(F32) * lk1.astype(F32)))
           - jnp.exp(jnp.sum(lq2.astype(F32) * lk2.astype(F32))) + lam_init)
    lam_row = jnp.full((1, LANES), lam, F32)
    ones_row = jnp.ones((1, LANES), F32)
    base = [b for _, b in ODD_CHUNKS]
    c_out = _attention(qkz, vt, base[0], base[1], base[2], base[3], nblk, blk, ones_row, ones_row, "dilated", 1.0)
    d_out = _attention(qkz, vt, base[4], base[5], base[6], base[7], nblk, blk, lam_row,
                       diff_norm.reshape(1, LANES).astype(F32), "diff", 1.0 - lam_init)
    return h, c_out, d_out


def kernel(x, pre_norm, post_norm, w_in_ab, s5_lambda_re, s5_lambda_im, s5_log_dt, s5_b_re, s5_b_im,
           s5_c_re, s5_c_im, s5_d, s5_glu_w, s5_glu_b, ml_conv_w, ml_conv_b, ml_wq, ml_wk, ml_wv,
           ml_gate_w, ml_gate_b, ml_norm, ml_skip, w_out_ab, w_in_cd, diff_lq1, diff_lk1, diff_lq2,
           diff_lk2, diff_norm, w_out_cd):
    depth = pre_norm.shape[0]
    h = x
    prev = None
    for l in range(depth):
        i = l // 2
        if l % 2 == 0:
            if prev is not None:
                h = _out_proj(*prev, h)
            s5_params = (s5_lambda_re[i], s5_lambda_im[i], s5_log_dt[i], s5_b_re[i], s5_b_im[i],
                         s5_c_re[i], s5_c_im[i], s5_d[i], s5_glu_w[i], s5_glu_b[i])
            ml_params = (ml_conv_w[i], ml_conv_b[i], ml_wq[i], ml_wk[i], ml_wv[i], ml_gate_w[i],
                         ml_gate_b[i], ml_norm[i], ml_skip[i])
            a, b = _even_layer(h, pre_norm[l], w_in_ab[i], s5_params, ml_params)
            prev = (a, b, w_out_ab[i], post_norm[l])
        else:
            h, c, d = _odd_layer(prev, h, pre_norm[l], w_in_cd[i], diff_lq1[i], diff_lk1[i], diff_lq2[i],
                                 diff_lk2[i], diff_norm[i], l)
            prev = (c, d, w_out_cd[i], post_norm[l])
    return _out_proj(*prev, h)
```

```python
import functools
import math

import numpy as np
import jax
import jax.numpy as jnp
from jax import lax
from jax.experimental import pallas as pl
from jax.experimental.pallas import tpu as pltpu

F32 = jnp.float32
BF16 = jnp.bfloat16

S5_GROUP = 16
ML_CONV = 4
DIL_PAIRS = ((128, 1), (512, 4), (2048, 16))
ROPE_THETA = 10000.0
NORM_EPS = 1e-6
HEAD_NORM_EPS = 1e-5
ATTN_HEAD_DIM = 64
QUERY_SCALE = ATTN_HEAD_DIM ** -0.5 * math.log2(math.e)

LANES = 128
SUBLANES = 8
VMEM_LIMIT_BYTES = 56 * 1024 * 1024

NEG_BIG = -1e30

PROJ_ROWS = 1024
OUT_PROJ_ROWS = 2048
PROJ_COLS = 512
S5_BATCH = SUBLANES
S5_TIME = 256
S5_STRIDE = 2
S5_PIECE = 32
ML_CHUNK = 256
ML_SUB = 8
ATTN_BLOCK = 256
ATTN_LOOKAHEAD = 3


def _params(*semantics):
    return pltpu.CompilerParams(dimension_semantics=semantics, vmem_limit_bytes=VMEM_LIMIT_BYTES)


def _silu(x):
    return x * jax.nn.sigmoid(x)


def _norm_proj_kernel(x_ref, g_ref, w_ref, o_ref, z_scr):
    x = x_ref[0]
    ms = jnp.mean(x * x, axis=-1, keepdims=True)
    z_scr[...] = (x * lax.rsqrt(ms + NORM_EPS) * g_ref[...]).astype(BF16)
    n_out = w_ref.shape[1]
    for c in range(n_out // PROJ_COLS):
        cols = slice(c * PROJ_COLS, (c + 1) * PROJ_COLS)
        o_ref[0, :, cols] = jnp.dot(z_scr[...], w_ref[:, cols], preferred_element_type=F32).astype(o_ref.dtype)


def _norm_proj(x, g, w):
    B, L, D = x.shape
    N = w.shape[1]
    tm = min(PROJ_ROWS, L)
    return pl.pallas_call(
        _norm_proj_kernel,
        grid=(B, L // tm),
        in_specs=[
            pl.BlockSpec((1, tm, D), lambda b, i: (b, i, 0)),
            pl.BlockSpec((1, D), lambda b, i: (0, 0)),
            pl.BlockSpec((D, N), lambda b, i: (0, 0)),
        ],
        out_specs=pl.BlockSpec((1, tm, N), lambda b, i: (b, i, 0)),
        out_shape=jax.ShapeDtypeStruct((B, L, N), BF16),
        scratch_shapes=[pltpu.VMEM((tm, D), BF16)],
        compiler_params=_params("parallel", "parallel"),
        name="norm_proj_even",
    )(x, g.reshape(1, D), w)


def _swap_halves(x, first_half):
    return jnp.where(first_half, pltpu.roll(x, LANES - ATTN_HEAD_DIM // 2, 1), pltpu.roll(x, ATTN_HEAD_DIM // 2, 1))


ODD_CHUNKS = (("q", 0), ("k", 4), ("v", 0), ("z", 8), ("q", 12), ("k", 16), ("v", 4), ("z", 20))


def _out_norm_proj_rope_kernel(a_ref, b_ref, wa_ref, wb_ref, gpost_ref, h_ref, g_ref, w_ref, cos_ref, sin_ref,
                               hout_ref, o_ref, vt_ref, z_scr):
    y = jnp.dot(a_ref[0], wa_ref[...], preferred_element_type=F32)
    y = y + jnp.dot(b_ref[0], wb_ref[...], preferred_element_type=F32)
    ms = jnp.mean(y * y, axis=-1, keepdims=True)
    x = h_ref[0] + y * lax.rsqrt(ms + NORM_EPS) * gpost_ref[...]
    hout_ref[0] = x
    ms = jnp.mean(x * x, axis=-1, keepdims=True)
    z_scr[...] = (x * lax.rsqrt(ms + NORM_EPS) * g_ref[...]).astype(BF16)
    slabs = PROJ_COLS // LANES
    lane = lax.broadcasted_iota(jnp.int32, (1, LANES), 1)
    first_half = (lane % ATTN_HEAD_DIM) < (ATTN_HEAD_DIM // 2)
    for c, (kind, base) in enumerate(ODD_CHUNKS):
        y = jnp.dot(z_scr[...], w_ref[:, c * PROJ_COLS:(c + 1) * PROJ_COLS], preferred_element_type=F32)
        for s in range(slabs):
            t = y[:, s * LANES:(s + 1) * LANES]
            if kind in ("q", "k"):
                t = t * cos_ref[...] + _swap_halves(t, first_half) * sin_ref[...]
            if kind == "q":
                t = t * QUERY_SCALE
            if kind == "v":
                vt_ref[0, base + s] = jnp.transpose(t).astype(vt_ref.dtype)
            else:
                o_ref[0, base + s] = t.astype(o_ref.dtype)


def _out_norm_proj_rope(a, b, w_out, g_post, h, g, w, cos_t, sin_t):
    B, L, D = h.shape
    Wb = a.shape[2]
    N = w.shape[1]
    assert N == len(ODD_CHUNKS) * PROJ_COLS
    tm = min(PROJ_ROWS, L)
    n_v = sum(kind == "v" for kind, _ in ODD_CHUNKS) * (PROJ_COLS // LANES)
    n_o = N // LANES - n_v
    row = lambda b_, i: (b_, i, 0)
    const = lambda b_, i: (0, 0)
    return pl.pallas_call(
        _out_norm_proj_rope_kernel,
        grid=(B, L // tm),
        in_specs=[
            pl.BlockSpec((1, tm, Wb), row),
            pl.BlockSpec((1, tm, Wb), row),
            pl.BlockSpec((Wb, D), const),
            pl.BlockSpec((Wb, D), const),
            pl.BlockSpec((1, D), const),
            pl.BlockSpec((1, tm, D), row),
            pl.BlockSpec((1, D), const),
            pl.BlockSpec((D, N), const),
            pl.BlockSpec((tm, LANES), lambda b_, i: (i, 0)),
            pl.BlockSpec((tm, LANES), lambda b_, i: (i, 0)),
        ],
        out_specs=[
            pl.BlockSpec((1, tm, D), row),
            pl.BlockSpec((1, n_o, tm, LANES), lambda b_, i: (b_, 0, i, 0)),
            pl.BlockSpec((1, n_v, LANES, tm), lambda b_, i: (b_, 0, 0, i)),
        ],
        out_shape=[
            jax.ShapeDtypeStruct((B, L, D), F32),
            jax.ShapeDtypeStruct((B, n_o, L, LANES), BF16),
            jax.ShapeDtypeStruct((B, n_v, LANES, L), BF16),
        ],
        scratch_shapes=[pltpu.VMEM((tm, D), BF16)],
        compiler_params=_params("parallel", "parallel"),
        name="out_norm_proj_odd",
    )(a, b, w_out[:Wb].astype(BF16), w_out[Wb:].astype(BF16), g_post.reshape(1, D), h, g.reshape(1, D), w,
      cos_t, sin_t)


def _out_proj_kernel(a_ref, b_ref, wa_ref, wb_ref, g_ref, h_ref, o_ref):
    y = jnp.dot(a_ref[0], wa_ref[...], preferred_element_type=F32)
    y = y + jnp.dot(b_ref[0], wb_ref[...], preferred_element_type=F32)
    ms = jnp.mean(y * y, axis=-1, keepdims=True)
    o_ref[0] = h_ref[0] + y * lax.rsqrt(ms + NORM_EPS) * g_ref[...]


def _out_proj(a, b, w_out, g, h):
    B, L, W = a.shape
    D = w_out.shape[1]
    tm = min(OUT_PROJ_ROWS, L)
    return pl.pallas_call(
        _out_proj_kernel,
        grid=(B, L // tm),
        in_specs=[
            pl.BlockSpec((1, tm, W), lambda b_, i: (b_, i, 0)),
            pl.BlockSpec((1, tm, W), lambda b_, i: (b_, i, 0)),
            pl.BlockSpec((W, D), lambda b_, i: (0, 0)),
            pl.BlockSpec((W, D), lambda b_, i: (0, 0)),
            pl.BlockSpec((1, D), lambda b_, i: (0, 0)),
            pl.BlockSpec((1, tm, D), lambda b_, i: (b_, i, 0)),
        ],
        out_specs=pl.BlockSpec((1, tm, D), lambda b_, i: (b_, i, 0)),
        out_shape=jax.ShapeDtypeStruct((B, L, D), F32),
        compiler_params=_params("parallel", "parallel"),
        name="out_proj",
    )(a, b, w_out[:W].astype(BF16), w_out[W:].astype(BF16), g.reshape(1, D), h)


def _s5_kernel(u_ref, z_ref, bin_ref, wx_ref, wu_ref, ar_ref, ai_ref, d_ref, gw_ref, gb_ref, o_ref,
               tmp_scr, us_scr, zs_scr, xs_scr, ys_scr, os_scr, st_scr):
    nb, T, W = u_ref.shape
    n_blk = W // LANES
    r = S5_STRIDE
    n_grp = T // r
    half = xs_scr.shape[2] // 2
    n_t = n_grp // S5_PIECE
    piece_rows = S5_PIECE * nb

    @pl.when(pl.program_id(1) == 0)
    def _():
        st_scr[...] = jnp.zeros_like(st_scr)

    def rows_of(i):
        return slice(i * piece_rows, (i + 1) * piece_rows)

    piece_time = S5_PIECE * r

    def load_piece(i):
        steps = slice(i * piece_time, (i + 1) * piece_time)
        for s_idx, (src, dst) in enumerate(((u_ref, us_scr), (z_ref, zs_scr))):
            for b in range(nb):
                xb = src[b, steps, :].astype(F32)
                for k in range(n_blk):
                    tmp_scr[s_idx, b, k] = xb[:, k * LANES:(k + 1) * LANES]
                for k in range(n_blk):
                    for j in range(r):
                        dst[k, j, pl.ds(i * piece_rows + b, S5_PIECE, stride=nb), :] = (
                            tmp_scr[s_idx, b, k, pl.ds(j, S5_PIECE, stride=r), :])

    def store_piece(i):
        steps = slice(i * piece_time, (i + 1) * piece_time)
        for b in range(nb):
            for k in range(n_blk):
                for j in range(r):
                    tmp_scr[2, b, k, pl.ds(j, S5_PIECE, stride=r), :] = (
                        os_scr[k, j, pl.ds(i * piece_rows + b, S5_PIECE, stride=nb), :])
            o_ref[b, steps, :] = jnp.concatenate([tmp_scr[2, b, k] for k in range(n_blk)],
                                                 axis=1).astype(o_ref.dtype)

    def group_inputs(i, k):
        return jnp.concatenate([us_scr[k, j, rows_of(i), :] for j in range(r)], axis=1).astype(BF16)

    def input_matmul(i, k):
        xs_scr[k, rows_of(i), :] = jnp.dot(group_inputs(i, k), bin_ref[k], preferred_element_type=F32)

    def scan(i, k):
        ar, ai = ar_ref[k], ai_ref[k]
        xr, xi = st_scr[k, 0], st_scr[k, 1]
        for t in range(S5_PIECE):
            rs = slice(i * piece_rows + t * nb, i * piece_rows + (t + 1) * nb)
            cr, ci = xs_scr[k, rs, 0:half], xs_scr[k, rs, half:2 * half]
            xs_scr[k, rs, 0:half] = xr
            xs_scr[k, rs, half:2 * half] = xi
            xr, xi = ar * xr - ai * xi + cr, ar * xi + ai * xr + ci
        st_scr[k, 0] = xr
        st_scr[k, 1] = xi

    def output_matmul(i, k):
        y = jnp.dot(xs_scr[k, rows_of(i), :].astype(BF16), wx_ref[k], preferred_element_type=F32)
        y = y + jnp.dot(group_inputs(i, k), wu_ref[k], preferred_element_type=F32)
        for j in range(r):
            ys_scr[k, j, rows_of(i), :] = y[:, j * LANES:(j + 1) * LANES]

    def glu(i):
        def tokens(ref):
            return jnp.concatenate(
                [jnp.concatenate([ref[k, j, rows_of(i), :] for k in range(n_blk)], axis=1) for j in range(r)], axis=0)
        y = tokens(ys_scr) + d_ref[...] * tokens(us_scr)
        g = jax.nn.gelu(y)
        gate = jnp.dot(g.astype(BF16), gw_ref[...], preferred_element_type=F32) + gb_ref[...]
        out = g * jax.nn.sigmoid(gate) * _silu(tokens(zs_scr))
        for j in range(r):
            for k in range(n_blk):
                os_scr[k, j, rows_of(i), :] = out[j * piece_rows:(j + 1) * piece_rows, k * LANES:(k + 1) * LANES]

    def drain(p):
        i, k = pieces[p]
        output_matmul(i, k)
        if k == n_blk - 1:
            glu(i)
            store_piece(i)

    pieces = [(i, k) for i in range(n_t) for k in range(n_blk)]
    load_piece(0)
    input_matmul(*pieces[0])
    for p, (i, k) in enumerate(pieces):
        if k == 0 and i + 1 < n_t:
            load_piece(i + 1)
        if p + 1 < len(pieces):
            input_matmul(*pieces[p + 1])
        if p >= 1:
            drain(p - 1)
        scan(i, k)
    drain(len(pieces) - 1)


def _s5_weights(lam_re, lam_im, log_dt, b_re, b_im, c_re, c_im):
    G, P = lam_re.shape
    r = S5_STRIDE
    gpb = LANES // S5_GROUP
    n_blk = G // gpb
    lr, li = lam_re.astype(F32), lam_im.astype(F32)
    dt = jnp.exp(log_dt.astype(F32))[:, None]
    mag = jnp.exp(lr * dt)
    ab_r, ab_i = mag * jnp.cos(li * dt), mag * jnp.sin(li * dt)
    den = lr * lr + li * li
    f_r = ((ab_r - 1.0) * lr + ab_i * li) / den
    f_i = (ab_i * lr - (ab_r - 1.0) * li) / den
    br, bi = b_re.astype(F32), b_im.astype(F32)
    bb_r = f_r[..., None] * br - f_i[..., None] * bi
    bb_i = f_r[..., None] * bi + f_i[..., None] * br
    cr, ci = c_re.astype(F32), c_im.astype(F32)
    eye = jnp.eye(gpb, dtype=F32)

    pows = [(jnp.ones_like(ab_r), jnp.zeros_like(ab_i))]
    for _ in range(r):
        pr, pi = pows[-1]
        pows.append((pr * ab_r - pi * ab_i, pr * ab_i + pi * ab_r))

    def pack_in(xr, xi):
        x = jnp.stack([xr, xi]).reshape(2, n_blk, gpb, P, S5_GROUP)
        x = jnp.transpose(x, (1, 2, 4, 0, 3))
        x = x[:, :, :, :, None, :] * eye[None, :, None, None, :, None]
        return x.reshape(n_blk, LANES, 2 * gpb * P)

    def pack_out(er, ei):
        x = jnp.stack([er, ei]).reshape(2, n_blk, gpb, S5_GROUP, P)
        x = jnp.transpose(x, (1, 0, 2, 4, 3))
        x = x[:, :, :, :, None, :] * eye[None, None, :, None, :, None]
        return x.reshape(n_blk, 2 * gpb * P, LANES)

    def pack_direct(kq):
        x = jnp.transpose(kq.reshape(n_blk, gpb, S5_GROUP, S5_GROUP), (0, 1, 3, 2))
        x = x[:, :, :, None, :] * eye[None, :, None, :, None]
        return x.reshape(n_blk, LANES, LANES)

    w_in = []
    for j in range(r):
        pr, pi = pows[r - 1 - j]
        w_in.append(pack_in(pr[..., None] * bb_r - pi[..., None] * bb_i, pr[..., None] * bb_i + pi[..., None] * bb_r))
    w_in = jnp.concatenate(w_in, axis=1).astype(BF16)

    w_x = []
    for j in range(r):
        pr, pi = pows[j + 1]
        er = cr * pr[:, None, :] - ci * pi[:, None, :]
        ei = cr * pi[:, None, :] + ci * pr[:, None, :]
        w_x.append(pack_out(er, -ei))
    w_x = jnp.concatenate(w_x, axis=2).astype(BF16)

    direct = []
    for q in range(r):
        pr, pi = pows[q]
        er = cr * pr[:, None, :] - ci * pi[:, None, :]
        ei = cr * pi[:, None, :] + ci * pr[:, None, :]
        direct.append(pack_direct(jnp.einsum('gop,gpi->goi', er, bb_r) - jnp.einsum('gop,gpi->goi', ei, bb_i)))
    zero = jnp.zeros_like(direct[0])
    w_u = jnp.concatenate(
        [jnp.concatenate([direct[j - i] if i <= j else zero for j in range(r)], axis=2) for i in range(r)],
        axis=1).astype(BF16)

    ar_r, ai_r = pows[r]
    ar = jnp.broadcast_to(ar_r.reshape(n_blk, 1, gpb * P), (n_blk, S5_BATCH, gpb * P))
    ai = jnp.broadcast_to(ai_r.reshape(n_blk, 1, gpb * P), (n_blk, S5_BATCH, gpb * P))
    return w_in, w_x, w_u, ar, ai


def _s5_branch(proj, lam_re, lam_im, log_dt, b_re, b_im, c_re, c_im, d_skip, glu_w, glu_b):
    B, L, N = proj.shape
    W = N // 4
    nb = S5_BATCH
    T = min(S5_TIME, L)
    r = S5_STRIDE
    n_blk = W // LANES
    w_in, w_x, w_u, ar, ai = _s5_weights(lam_re, lam_im, log_dt, b_re, b_im, c_re, c_im)
    S2 = w_in.shape[2]
    rows = nb * T // r
    const3 = lambda g, c: (0, 0, 0)
    const2 = lambda g, c: (0, 0)
    return pl.pallas_call(
        _s5_kernel,
        grid=(B // nb, L // T),
        in_specs=[
            pl.BlockSpec((nb, T, W), lambda g, c: (g, c, 0)),
            pl.BlockSpec((nb, T, W), lambda g, c: (g, c, 1)),
            pl.BlockSpec(w_in.shape, const3),
            pl.BlockSpec(w_x.shape, const3),
            pl.BlockSpec(w_u.shape, const3),
            pl.BlockSpec(ar.shape, const3),
            pl.BlockSpec(ai.shape, const3),
            pl.BlockSpec((1, W), const2),
            pl.BlockSpec((W, W), const2),
            pl.BlockSpec((1, W), const2),
        ],
        out_specs=pl.BlockSpec((nb, T, W), lambda g, c: (g, c, 0)),
        out_shape=jax.ShapeDtypeStruct((B, L, W), BF16),
        scratch_shapes=[
            pltpu.VMEM((3, nb, n_blk, S5_PIECE * r, LANES), F32),
            pltpu.VMEM((n_blk, r, rows, LANES), F32),
            pltpu.VMEM((n_blk, r, rows, LANES), F32),
            pltpu.VMEM((n_blk, rows, S2), F32),
            pltpu.VMEM((n_blk, r, rows, LANES), F32),
            pltpu.VMEM((n_blk, r, rows, LANES), F32),
            pltpu.VMEM((n_blk, 2, nb, S2 // 2), F32),
        ],
        compiler_params=_params("parallel", "arbitrary"),
        name="s5_scan",
    )(proj, proj, w_in, w_x, w_u, ar, ai, d_skip.reshape(1, W).astype(F32), glu_w.astype(BF16),
      glu_b.reshape(1, W).astype(F32))


def _log_sigmoid(x):
    return jnp.minimum(x, 0.0) - jnp.log1p(jnp.exp(-jnp.abs(x)))


def _split3(x):
    hi = x.astype(BF16)
    r1 = x - hi.astype(F32)
    mid = r1.astype(BF16)
    lo = (r1 - mid.astype(F32)).astype(BF16)
    return hi, mid, lo


def _mlstm_kernel(x_ref, z_ref, cw_ref, cb_ref, wq_ref, wk_ref, wv_ref, wvt_ref, gw_ref, gwt_ref, gbc_ref, gbr_ref,
                  ng_ref, sk_ref, o_ref, xe_scr, qkv_scr, ks_scr, vta_scr, ca_scr, m_scr):
    n_sub, Tc = qkv_scr.shape[0], qkv_scr.shape[1]
    W = x_ref.shape[2]
    H = wq_ref.shape[0]
    dh = W // H
    pad = SUBLANES
    heads = range(H)
    nt = (((1,), (1,)), ((), ()))

    @pl.when(pl.program_id(1) == 0)
    def _():
        xe_scr[0:pad, :] = jnp.zeros((pad, W), F32)
        ca_scr[...] = jnp.zeros_like(ca_scr)
        m_scr[...] = jnp.zeros_like(m_scr)
        vta_scr[:, :, dh:, :] = jnp.ones((n_sub, H, pad, Tc), BF16)

    xe_scr[pad:, :] = x_ref[0].astype(F32)

    ti = lax.broadcasted_iota(jnp.int32, (Tc, Tc), 0)
    si = lax.broadcasted_iota(jnp.int32, (Tc, Tc), 1)
    lower = jnp.where(si <= ti, 1.0, 0.0).astype(BF16)
    allowed = ti <= si
    upper = jnp.where(allowed, 1.0, 0.0).astype(BF16)

    def front(s):
        r0 = s * Tc
        xb = x_ref[0, r0:r0 + Tc, :]
        xc = cb_ref[...] + sum(xe_scr[pl.ds(pad + r0 - (ML_CONV - 1) + j, Tc), :] * cw_ref[j:j + 1, :]
                               for j in range(ML_CONV))
        xc = _silu(xc)
        xcb = xc.astype(BF16)
        for h in heads:
            cols = slice(h * dh, (h + 1) * dh)
            q = jnp.dot(xcb[:, cols], wq_ref[h], preferred_element_type=F32)
            k = jnp.dot(xcb[:, cols], wk_ref[h], preferred_element_type=F32)
            v = jnp.dot(xb[:, cols], wv_ref[h], preferred_element_type=F32)
            qkv_scr[s, :, h * dh:(h + 1) * dh] = q.astype(BF16)
            qkv_scr[s, :, W + h * dh:W + (h + 1) * dh] = k.astype(BF16)
            qkv_scr[s, :, 2 * W + h * dh:2 * W + (h + 1) * dh] = v.astype(BF16)
            vta_scr[s, h, :dh, :] = lax.dot_general(wvt_ref[h], xb[:, cols], nt,
                                                    preferred_element_type=F32).astype(BF16)
            ks_scr[s, :, cols] = (k * (dh ** -0.5)).astype(BF16)
        qkv = qkv_scr[s]
        g_col = jnp.dot(qkv, gw_ref[...], preferred_element_type=F32) + gbc_ref[...]
        g_row = lax.dot_general(gwt_ref[...], qkv, nt, preferred_element_type=F32) + gbr_ref[...]
        b_col = sum(jnp.dot(lower, p, preferred_element_type=F32) for p in _split3(_log_sigmoid(g_col)))
        b_row = sum(jnp.dot(p, upper, preferred_element_type=F32) for p in _split3(_log_sigmoid(g_row)))
        return xc, g_col, g_row, b_col, b_row

    def rec_matmuls(s):
        q_h = [qkv_scr[s, :, h * dh:(h + 1) * dh] for h in heads]
        ks_h = [ks_scr[s, :, h * dh:(h + 1) * dh] for h in heads]
        s_qk = [lax.dot_general(ks_h[h], q_h[h], nt, preferred_element_type=F32) for h in heads]
        q_ca = [lax.dot_general(ca_scr[h].astype(BF16), q_h[h], nt, preferred_element_type=F32)
                for h in heads]
        return ks_h, s_qk, q_ca

    def rec_rest(s, mm, xc, g_col, g_row, b_col, b_row):
        ks_h, s_qk, q_ca = mm
        rows = slice(s * Tc, (s + 1) * Tc)
        m_prev = [m_scr[h][:, 0:1] for h in heads]
        br = [b_row[H + h:H + h + 1, :] for h in heads]
        li = [g_row[h:h + 1, :] for h in heads]
        b_last = [br[h][:, Tc - 1:Tc] for h in heads]
        dmat, sc, floor = [], [], []
        for h in heads:
            c_col = g_col[:, h:h + 1] - b_col[:, H + h:H + h + 1]
            log_d = jnp.where(allowed, br[h] + c_col, NEG_BIG)
            inter = br[h] + m_prev[h]
            m_t = jnp.maximum(inter, jnp.max(log_d, axis=0, keepdims=True))
            dmat.append(jnp.exp(log_d - m_t))
            sc.append(jnp.exp(inter - m_t))
            floor.append(jnp.exp(-m_t))
        pv = [jnp.dot(vta_scr[s, h], (s_qk[h] * dmat[h]).astype(BF16), preferred_element_type=F32) for h in heads]
        for h in heads:
            w_log = b_last[h] - br[h] + li[h]
            m_new = jnp.maximum(b_last[h] + m_prev[h], jnp.max(w_log, axis=1, keepdims=True))
            weighted = (vta_scr[s, h].astype(F32) * jnp.exp(w_log - m_new)).astype(BF16)
            decay = jnp.exp(b_last[h] + m_prev[h] - m_new)
            ca_scr[h] = decay * ca_scr[h] + jnp.dot(weighted, ks_h[h], preferred_element_type=F32)
            m_scr[h] = jnp.broadcast_to(m_new, (1, LANES))
        for h in heads:
            cols = slice(h * dh, (h + 1) * dh)
            tot = sc[h] * q_ca[h] + pv[h]
            hh = tot[:dh] / jnp.maximum(jnp.abs(tot[dh:dh + 1]), floor[h])
            mu = jnp.mean(hh, axis=0, keepdims=True)
            cen = hh - mu
            var = jnp.mean(cen * cen, axis=0, keepdims=True)
            hn = jnp.transpose(cen * lax.rsqrt(var + HEAD_NORM_EPS))
            out = hn * ng_ref[:, cols] + sk_ref[:, cols] * xc[:, cols]
            o_ref[0, rows, cols] = (out * _silu(z_ref[0, rows, cols].astype(F32))).astype(o_ref.dtype)

    ready = front(0)
    for s in range(n_sub):
        mm = rec_matmuls(s)
        following = front(s + 1) if s + 1 < n_sub else None
        rec_rest(s, mm, *ready)
        ready = following
    xe_scr[0:pad, :] = xe_scr[n_sub * Tc:n_sub * Tc + pad, :]


def _mlstm_branch(proj, conv_w, conv_b, wq, wk, wv, gate_w, gate_b, norm_g, skip):
    B, L, N = proj.shape
    W = N // 4
    H = wq.shape[0]
    dh = W // H
    Tc = min(ML_CHUNK, L)
    n_sub = min(ML_SUB, L // Tc)
    rows = n_sub * Tc
    gw = jnp.zeros((3 * W, LANES), F32).at[:, :2 * H].set(gate_w.astype(F32)).astype(BF16)
    gwt = jnp.transpose(gate_w.astype(F32)).astype(BF16)
    gbc = jnp.zeros((1, LANES), F32).at[0, :2 * H].set(gate_b.astype(F32))
    gbr = gate_b.astype(F32).reshape(2 * H, 1)
    c2 = lambda b, c: (0, 0)
    c3 = lambda b, c: (0, 0, 0)
    return pl.pallas_call(
        _mlstm_kernel,
        grid=(B, L // rows),
        in_specs=[
            pl.BlockSpec((1, rows, W), lambda b, c: (b, c, 2)),
            pl.BlockSpec((1, rows, W), lambda b, c: (b, c, 3)),
            pl.BlockSpec((ML_CONV, W), c2),
            pl.BlockSpec((1, W), c2),
            pl.BlockSpec((H, dh, dh), c3),
            pl.BlockSpec((H, dh, dh), c3),
            pl.BlockSpec((H, dh, dh), c3),
            pl.BlockSpec((H, dh, dh), c3),
            pl.BlockSpec((3 * W, LANES), c2),
            pl.BlockSpec((2 * H, 3 * W), c2),
            pl.BlockSpec((1, LANES), c2),
            pl.BlockSpec((2 * H, 1), c2),
            pl.BlockSpec((1, W), c2),
            pl.BlockSpec((1, W), c2),
        ],
        out_specs=pl.BlockSpec((1, rows, W), lambda b, c: (b, c, 0)),
        out_shape=jax.ShapeDtypeStruct((B, L, W), BF16),
        scratch_shapes=[
            pltpu.VMEM((SUBLANES + rows, W), F32),
            pltpu.VMEM((n_sub, Tc, 3 * W), BF16),
            pltpu.VMEM((n_sub, Tc, W), BF16),
            pltpu.VMEM((n_sub, H, dh + SUBLANES, Tc), BF16),
            pltpu.VMEM((H, dh + SUBLANES, dh), F32),
            pltpu.VMEM((H, 1, LANES), F32),
        ],
        compiler_params=_params("parallel", "arbitrary"),
        name="mlstm",
    )(proj, proj, conv_w.astype(F32), conv_b.reshape(1, W).astype(F32), wq.astype(BF16), wk.astype(BF16),
      wv.astype(BF16), jnp.transpose(wv, (0, 2, 1)).astype(BF16), gw, gwt, gbc, gbr, norm_g.reshape(1, W).astype(F32), skip.reshape(1, W).astype(F32))


def _attn_kernel(q_ref, k_ref, vt_ref, z_ref, bias_ref, lam_ref, ng_ref, o_ref, m_scr, acc_scr, *,
                 mode, out_scale, blk, bias_index):
    L = q_ref.shape[2]
    n = L // blk
    lane = lax.broadcasted_iota(jnp.int32, (1, LANES), 1)
    map0 = lane < ATTN_HEAD_DIM
    feat = lax.broadcasted_iota(jnp.int32, (LANES, 1), 0)
    ones_rows = jnp.ones((SUBLANES, blk), BF16)

    def stacked_q(qi):
        q = q_ref[0, 0, qi * blk:(qi + 1) * blk, :]
        zero = jnp.zeros_like(q)
        return jnp.concatenate([jnp.where(map0, q, zero), jnp.where(map0, zero, q)], axis=0)

    def scores(qi, j):
        s = lax.dot_general(k_ref[0, 0, j * blk:(j + 1) * blk, :], q2[qi], (((1,), (1,)), ((), ())),
                            preferred_element_type=F32)
        e = bias_index(qi - j)
        return s if e is None else s + bias_ref[e]

    def update(qi, j, s):
        m_cur = jnp.max(s, axis=0, keepdims=True)
        m_next = m_cur if j == 0 else jnp.maximum(m_scr[qi], m_cur)
        p = jnp.exp2(s - m_next).astype(BF16)
        pv = jnp.dot(jnp.concatenate([vt_ref[0, 0, :, j * blk:(j + 1) * blk], ones_rows], axis=0), p,
                     preferred_element_type=F32)
        if j == 0:
            acc_scr[qi] = pv
        else:
            acc_scr[qi] = jnp.exp2(m_scr[qi] - m_next) * acc_scr[qi] + pv
        m_scr[qi] = m_next

    def finish(qi):
        rows = slice(qi * blk, (qi + 1) * blk)
        acc = acc_scr[qi]
        o = acc[:LANES] / acc[LANES:LANES + 1]
        o0, o1 = o[:, :blk], o[:, blk:]
        if mode == "dilated":
            out = jnp.transpose(jnp.where(feat < ATTN_HEAD_DIM, o0, o1))
        else:
            out = jnp.transpose(o0 - lam_ref[:, 0:1] * o1)
            ms = jnp.mean(out * out, axis=-1, keepdims=True)
            out = out * lax.rsqrt(ms + HEAD_NORM_EPS) * ng_ref[...] * out_scale
        o_ref[0, rows, :] = (out * _silu(z_ref[0, 0, rows, :].astype(F32))).astype(o_ref.dtype)

    steps = []
    for a in range(n // 2):
        b = n - 1 - a
        ja = 0
        for jb in range(b + 1):
            steps.append((b, jb))
            if ja <= a and (jb + 1) * (a + 1) >= (ja + 1) * (b + 1):
                steps.append((a, ja))
                ja += 1
    q2 = {qi: stacked_q(qi) for qi in range(n)}
    pending = {}
    for t in range(min(ATTN_LOOKAHEAD, len(steps))):
        pending[t] = scores(*steps[t])
    for t, (qi, j) in enumerate(steps):
        if t + ATTN_LOOKAHEAD < len(steps):
            pending[t + ATTN_LOOKAHEAD] = scores(*steps[t + ATTN_LOOKAHEAD])
        update(qi, j, pending.pop(t))
        if j == qi:
            finish(qi)


def _dilated_multiplicity(delta):
    mult = np.zeros(delta.shape, np.int64)
    for window, dil in DIL_PAIRS:
        mult += (delta >= 0) & (delta % dil == 0) & (delta <= window)
    return mult


def _bias_tables(L, blk, mode):
    n = L // blk
    key = np.arange(blk)[:, None]
    qry = np.arange(blk)[None, :]
    tables, index = [], {}
    for d in range(n):
        delta = d * blk + qry - key
        mult = _dilated_multiplicity(delta) if mode == "dilated" else (delta >= 0).astype(np.int64)
        t = np.where(mult > 0, np.log2(np.maximum(mult, 1)), NEG_BIG).astype(np.float32)
        if not t.any():
            index[d] = None
            continue
        for e, other in enumerate(tables):
            if np.array_equal(t, other):
                index[d] = e
                break
        else:
            index[d] = len(tables)
            tables.append(t)
    stacked = np.stack(tables)
    return jnp.asarray(np.concatenate([stacked, stacked], axis=2)), index


def _attention(qkz, vt, q0, k0, v0, z0, n_blocks, blk, lam, norm_g, mode, out_scale):
    B, _, L, _ = qkz.shape
    bias, index = _bias_tables(L, blk, mode)
    n_q = L // blk
    assert n_q % 2 == 0
    kern = functools.partial(_attn_kernel, mode=mode, out_scale=out_scale, blk=blk, bias_index=index.get)
    c2 = lambda b, n: (0, 0)
    return pl.pallas_call(
        kern,
        grid=(B, n_blocks),
        in_specs=[
            pl.BlockSpec((1, 1, L, LANES), lambda b, n: (b, q0 + n, 0, 0)),
            pl.BlockSpec((1, 1, L, LANES), lambda b, n: (b, k0 + n, 0, 0)),
            pl.BlockSpec((1, 1, LANES, L), lambda b, n: (b, v0 + n, 0, 0)),
            pl.BlockSpec((1, 1, L, LANES), lambda b, n: (b, z0 + n, 0, 0)),
            pl.BlockSpec(bias.shape, lambda b, n: (0, 0, 0)),
            pl.BlockSpec((1, LANES), c2),
            pl.BlockSpec((1, LANES), c2),
        ],
        out_specs=pl.BlockSpec((1, L, LANES), lambda b, n: (b, 0, n)),
        out_shape=jax.ShapeDtypeStruct((B, L, n_blocks * LANES), BF16),
        scratch_shapes=[
            pltpu.VMEM((n_q, 1, 2 * blk), F32),
            pltpu.VMEM((n_q, LANES + SUBLANES, 2 * blk), F32),
        ],
        compiler_params=_params("parallel", "parallel"),
        name="attn_" + mode,
    )(qkz, qkz, vt, qkz, bias, lam, norm_g)


def _rope_tables(L):
    dh = ATTN_HEAD_DIM
    inv = ROPE_THETA ** (-jnp.arange(0, dh, 2, dtype=F32) / dh)
    ang = jnp.arange(L, dtype=F32)[:, None] * inv[None, :]
    cos, sin = jnp.cos(ang), jnp.sin(ang)
    reps = LANES // dh
    cos_t = jnp.tile(jnp.concatenate([cos, cos], axis=-1), (1, reps))
    sin_t = jnp.tile(jnp.concatenate([-sin, sin], axis=-1), (1, reps))
    return cos_t, sin_t


def _even_layer(h, pre_g, w_in, s5_params, ml_params):
    proj = _norm_proj(h, pre_g, w_in.astype(BF16))
    return _s5_branch(proj, *s5_params), _mlstm_branch(proj, *ml_params)


def _odd_layer(prev, h, pre_g, w_in, lq1, lk1, lq2, lk2, diff_norm, layer_idx):
    B, L, D = h.shape
    nblk = PROJ_COLS // LANES
    cos_t, sin_t = _rope_tables(L)
    h, qkz, vt = _out_norm_proj_rope(*prev, h, pre_g, w_in.astype(BF16), cos_t, sin_t)
    blk = min(ATTN_BLOCK, L)
    lam_init = 0.8 - 0.6 * math.exp(-0.3 * layer_idx)
    lam = (jnp.exp(jnp.sum(lq1.astype(F32) * lk1.astype(F32)))
           - jnp.exp(jnp.sum(lq2.astype(F32) * lk2.astype(F32))) + lam_init)
    lam_row = jnp.full((1, LANES), lam, F32)
    ones_row = jnp.ones((1, LANES), F32)
    base = [b for _, b in ODD_CHUNKS]
    c_out = _attention(qkz, vt, base[0], base[1], base[2], base[3], nblk, blk, ones_row, ones_row, "dilated", 1.0)
    d_out = _attention(qkz, vt, base[4], base[5], base[6], base[7], nblk, blk, lam_row,
                       diff_norm.reshape(1, LANES).astype(F32), "diff", 1.0 - lam_init)
    return h, c_out, d_out


def kernel(x, pre_norm, post_norm, w_in_ab, s5_lambda_re, s5_lambda_im, s5_log_dt, s5_b_re, s5_b_im,
           s5_c_re, s5_c_im, s5_d, s5_glu_w, s5_glu_b, ml_conv_w, ml_conv_b, ml_wq, ml_wk, ml_wv,
           ml_gate_w, ml_gate_b, ml_norm, ml_skip, w_out_ab, w_in_cd, diff_lq1, diff_lk1, diff_lq2,
           diff_lk2, diff_norm, w_out_cd):
    depth = pre_norm.shape[0]
    h = x
    prev = None
    for l in range(depth):
        i = l // 2
        if l % 2 == 0:
            if prev is not None:
                h = _out_proj(*prev, h)
            s5_params = (s5_lambda_re[i], s5_lambda_im[i], s5_log_dt[i], s5_b_re[i], s5_b_im[i],
                         s5_c_re[i], s5_c_im[i], s5_d[i], s5_glu_w[i], s5_glu_b[i])
            ml_params = (ml_conv_w[i], ml_conv_b[i], ml_wq[i], ml_wk[i], ml_wv[i], ml_gate_w[i],
                         ml_gate_b[i], ml_norm[i], ml_skip[i])
            a, b = _even_layer(h, pre_norm[l], w_in_ab[i], s5_params, ml_params)
            prev = (a, b, w_out_ab[i], post_norm[l])
        else:
            h, c, d = _odd_layer(prev, h, pre_norm[l], w_in_cd[i], diff_lq1[i], diff_lk1[i], diff_lq2[i],
                                 diff_lk2[i], diff_norm[i], l)
            prev = (c, d, w_out_cd[i], post_norm[l])
    return _out_proj(*prev, h)
```

```python
import functools
import math

import numpy as np
import jax
import jax.numpy as jnp
from jax import lax
from jax.experimental import pallas as pl
from jax.experimental.pallas import tpu as pltpu

F32 = jnp.float32
BF16 = jnp.bfloat16

S5_GROUP = 16
ML_CONV = 4
DIL_PAIRS = ((128, 1), (512, 4), (2048, 16))
ROPE_THETA = 10000.0
NORM_EPS = 1e-6
HEAD_NORM_EPS = 1e-5
ATTN_HEAD_DIM = 64
QUERY_SCALE = ATTN_HEAD_DIM ** -0.5 * math.log2(math.e)

LANES = 128
SUBLANES = 8
VMEM_LIMIT_BYTES = 56 * 1024 * 1024

NEG_BIG = -1e30

PROJ_ROWS = 1024
OUT_PROJ_ROWS = 2048
PROJ_COLS = 512
S5_BATCH = SUBLANES
S5_TIME = 256
S5_STRIDE = 2
S5_PIECE = 32
ML_CHUNK = 256
ML_SUB = 8
ATTN_BLOCK = 256
ATTN_LOOKAHEAD = 3


def _params(*semantics):
    return pltpu.CompilerParams(dimension_semantics=semantics, vmem_limit_bytes=VMEM_LIMIT_BYTES)


def _silu(x):
    return x * jax.nn.sigmoid(x)


def _norm_proj_kernel(x_ref, g_ref, w_ref, o_ref, z_scr):
    tm = x_ref.shape[1]
    halves = [slice(0, tm // 2), slice(tm // 2, tm)]
    n_out = w_ref.shape[1]
    first = slice(0, PROJ_COLS)
    for rs in halves:
        x = x_ref[0, rs, :]
        ms = jnp.mean(x * x, axis=-1, keepdims=True)
        z_scr[rs, :] = (x * lax.rsqrt(ms + NORM_EPS) * g_ref[...]).astype(BF16)
        o_ref[0, rs, first] = jnp.dot(z_scr[rs, :], w_ref[:, first], preferred_element_type=F32).astype(o_ref.dtype)
    for c in range(1, n_out // PROJ_COLS):
        cols = slice(c * PROJ_COLS, (c + 1) * PROJ_COLS)
        o_ref[0, :, cols] = jnp.dot(z_scr[...], w_ref[:, cols], preferred_element_type=F32).astype(o_ref.dtype)


def _norm_proj(x, g, w):
    B, L, D = x.shape
    N = w.shape[1]
    tm = min(PROJ_ROWS, L)
    return pl.pallas_call(
        _norm_proj_kernel,
        grid=(B, L // tm),
        in_specs=[
            pl.BlockSpec((1, tm, D), lambda b, i: (b, i, 0)),
            pl.BlockSpec((1, D), lambda b, i: (0, 0)),
            pl.BlockSpec((D, N), lambda b, i: (0, 0)),
        ],
        out_specs=pl.BlockSpec((1, tm, N), lambda b, i: (b, i, 0)),
        out_shape=jax.ShapeDtypeStruct((B, L, N), BF16),
        scratch_shapes=[pltpu.VMEM((tm, D), BF16)],
        compiler_params=_params("parallel", "parallel"),
        name="norm_proj_even",
    )(x, g.reshape(1, D), w)


def _swap_halves(x, first_half):
    return jnp.where(first_half, pltpu.roll(x, LANES - ATTN_HEAD_DIM // 2, 1), pltpu.roll(x, ATTN_HEAD_DIM // 2, 1))


ODD_CHUNKS = (("q", 0), ("k", 4), ("v", 0), ("z", 8), ("q", 12), ("k", 16), ("v", 4), ("z", 20))


def _out_norm_proj_rope_kernel(a_ref, b_ref, wa_ref, wb_ref, gpost_ref, h_ref, g_ref, w_ref, cos_ref, sin_ref,
                               hout_ref, o_ref, vt_ref, z_scr):
    tm = a_ref.shape[1]
    halves = [slice(0, tm // 2), slice(tm // 2, tm)]
    ys = [jnp.dot(a_ref[0, rs, :], wa_ref[...], preferred_element_type=F32)
          + jnp.dot(b_ref[0, rs, :], wb_ref[...], preferred_element_type=F32) for rs in halves]
    for rs, y in zip(halves, ys):
        ms = jnp.mean(y * y, axis=-1, keepdims=True)
        x = h_ref[0, rs, :] + y * lax.rsqrt(ms + NORM_EPS) * gpost_ref[...]
        hout_ref[0, rs, :] = x
        ms = jnp.mean(x * x, axis=-1, keepdims=True)
        z_scr[rs, :] = (x * lax.rsqrt(ms + NORM_EPS) * g_ref[...]).astype(BF16)
    slabs = PROJ_COLS // LANES
    lane = lax.broadcasted_iota(jnp.int32, (1, LANES), 1)
    first_half = (lane % ATTN_HEAD_DIM) < (ATTN_HEAD_DIM // 2)
    for c, (kind, base) in enumerate(ODD_CHUNKS):
        y = jnp.dot(z_scr[...], w_ref[:, c * PROJ_COLS:(c + 1) * PROJ_COLS], preferred_element_type=F32)
        for s in range(slabs):
            t = y[:, s * LANES:(s + 1) * LANES]
            if kind in ("q", "k"):
                t = t * cos_ref[...] + _swap_halves(t, first_half) * sin_ref[...]
            if kind == "q":
                t = t * QUERY_SCALE
            if kind == "v":
                vt_ref[0, base + s] = jnp.transpose(t).astype(vt_ref.dtype)
            else:
                o_ref[0, base + s] = t.astype(o_ref.dtype)


def _out_norm_proj_rope(a, b, w_out, g_post, h, g, w, cos_t, sin_t):
    B, L, D = h.shape
    Wb = a.shape[2]
    N = w.shape[1]
    assert N == len(ODD_CHUNKS) * PROJ_COLS
    tm = min(PROJ_ROWS, L)
    n_v = sum(kind == "v" for kind, _ in ODD_CHUNKS) * (PROJ_COLS // LANES)
    n_o = N // LANES - n_v
    row = lambda b_, i: (b_, i, 0)
    const = lambda b_, i: (0, 0)
    return pl.pallas_call(
        _out_norm_proj_rope_kernel,
        grid=(B, L // tm),
        in_specs=[
            pl.BlockSpec((1, tm, Wb), row),
            pl.BlockSpec((1, tm, Wb), row),
            pl.BlockSpec((Wb, D), const),
            pl.BlockSpec((Wb, D), const),
            pl.BlockSpec((1, D), const),
            pl.BlockSpec((1, tm, D), row),
            pl.BlockSpec((1, D), const),
            pl.BlockSpec((D, N), const),
            pl.BlockSpec((tm, LANES), lambda b_, i: (i, 0)),
            pl.BlockSpec((tm, LANES), lambda b_, i: (i, 0)),
        ],
        out_specs=[
            pl.BlockSpec((1, tm, D), row),
            pl.BlockSpec((1, n_o, tm, LANES), lambda b_, i: (b_, 0, i, 0)),
            pl.BlockSpec((1, n_v, LANES, tm), lambda b_, i: (b_, 0, 0, i)),
        ],
        out_shape=[
            jax.ShapeDtypeStruct((B, L, D), F32),
            jax.ShapeDtypeStruct((B, n_o, L, LANES), BF16),
            jax.ShapeDtypeStruct((B, n_v, LANES, L), BF16),
        ],
        scratch_shapes=[pltpu.VMEM((tm, D), BF16)],
        compiler_params=_params("parallel", "parallel"),
        name="out_norm_proj_odd",
    )(a, b, w_out[:Wb].astype(BF16), w_out[Wb:].astype(BF16), g_post.reshape(1, D), h, g.reshape(1, D), w,
      cos_t, sin_t)


def _out_proj_kernel(a_ref, b_ref, wa_ref, wb_ref, g_ref, h_ref, o_ref):
    y = jnp.dot(a_ref[0], wa_ref[...], preferred_element_type=F32)
    y = y + jnp.dot(b_ref[0], wb_ref[...], preferred_element_type=F32)
    ms = jnp.mean(y * y, axis=-1, keepdims=True)
    o_ref[0] = h_ref[0] + y * lax.rsqrt(ms + NORM_EPS) * g_ref[...]


def _out_proj(a, b, w_out, g, h):
    B, L, W = a.shape
    D = w_out.shape[1]
    tm = min(OUT_PROJ_ROWS, L)
    return pl.pallas_call(
        _out_proj_kernel,
        grid=(B, L // tm),
        in_specs=[
            pl.BlockSpec((1, tm, W), lambda b_, i: (b_, i, 0)),
            pl.BlockSpec((1, tm, W), lambda b_, i: (b_, i, 0)),
            pl.BlockSpec((W, D), lambda b_, i: (0, 0)),
            pl.BlockSpec((W, D), lambda b_, i: (0, 0)),
            pl.BlockSpec((1, D), lambda b_, i: (0, 0)),
            pl.BlockSpec((1, tm, D), lambda b_, i: (b_, i, 0)),
        ],
        out_specs=pl.BlockSpec((1, tm, D), lambda b_, i: (b_, i, 0)),
        out_shape=jax.ShapeDtypeStruct((B, L, D), F32),
        compiler_params=_params("parallel", "parallel"),
        name="out_proj",
    )(a, b, w_out[:W].astype(BF16), w_out[W:].astype(BF16), g.reshape(1, D), h)


def _s5_kernel(u_ref, z_ref, bin_ref, wx_ref, wu_ref, ar_ref, ai_ref, d_ref, gw_ref, gb_ref, o_ref,
               tmp_scr, us_scr, zs_scr, xs_scr, ys_scr, os_scr, st_scr):
    nb, T, W = u_ref.shape
    n_blk = W // LANES
    r = S5_STRIDE
    n_grp = T // r
    half = xs_scr.shape[2] // 2
    n_t = n_grp // S5_PIECE
    piece_rows = S5_PIECE * nb

    @pl.when(pl.program_id(1) == 0)
    def _():
        st_scr[...] = jnp.zeros_like(st_scr)

    def rows_of(i):
        return slice(i * piece_rows, (i + 1) * piece_rows)

    piece_time = S5_PIECE * r

    def load_piece(i):
        steps = slice(i * piece_time, (i + 1) * piece_time)
        for s_idx, (src, dst) in enumerate(((u_ref, us_scr), (z_ref, zs_scr))):
            for b in range(nb):
                xb = src[b, steps, :].astype(F32)
                for k in range(n_blk):
                    tmp_scr[s_idx, b, k] = xb[:, k * LANES:(k + 1) * LANES]
                for k in range(n_blk):
                    for j in range(r):
                        dst[k, j, pl.ds(i * piece_rows + b, S5_PIECE, stride=nb), :] = (
                            tmp_scr[s_idx, b, k, pl.ds(j, S5_PIECE, stride=r), :])

    def store_piece(i):
        steps = slice(i * piece_time, (i + 1) * piece_time)
        for b in range(nb):
            for k in range(n_blk):
                for j in range(r):
                    tmp_scr[2, b, k, pl.ds(j, S5_PIECE, stride=r), :] = (
                        os_scr[k, j, pl.ds(i * piece_rows + b, S5_PIECE, stride=nb), :])
            o_ref[b, steps, :] = jnp.concatenate([tmp_scr[2, b, k] for k in range(n_blk)],
                                                 axis=1).astype(o_ref.dtype)

    def group_inputs(i, k):
        return jnp.concatenate([us_scr[k, j, rows_of(i), :] for j in range(r)], axis=1).astype(BF16)

    def input_matmul(i, k):
        xs_scr[k, rows_of(i), :] = jnp.dot(group_inputs(i, k), bin_ref[k], preferred_element_type=F32)

    def scan(i, k):
        ar, ai = ar_ref[k], ai_ref[k]
        xr, xi = st_scr[k, 0], st_scr[k, 1]
        for t in range(S5_PIECE):
            rs = slice(i * piece_rows + t * nb, i * piece_rows + (t + 1) * nb)
            cr, ci = xs_scr[k, rs, 0:half], xs_scr[k, rs, half:2 * half]
            xs_scr[k, rs, 0:half] = xr
            xs_scr[k, rs, half:2 * half] = xi
            xr, xi = ar * xr - ai * xi + cr, ar * xi + ai * xr + ci
        st_scr[k, 0] = xr
        st_scr[k, 1] = xi

    def output_matmul(i, k):
        y = jnp.dot(xs_scr[k, rows_of(i), :].astype(BF16), wx_ref[k], preferred_element_type=F32)
        y = y + jnp.dot(group_inputs(i, k), wu_ref[k], preferred_element_type=F32)
        for j in range(r):
            ys_scr[k, j, rows_of(i), :] = y[:, j * LANES:(j + 1) * LANES]

    def glu(i):
        def tokens(ref):
            return jnp.concatenate(
                [jnp.concatenate([ref[k, j, rows_of(i), :] for k in range(n_blk)], axis=1) for j in range(r)], axis=0)
        y = tokens(ys_scr) + d_ref[...] * tokens(us_scr)
        g = jax.nn.gelu(y)
        gate = jnp.dot(g.astype(BF16), gw_ref[...], preferred_element_type=F32) + gb_ref[...]
        out = g * jax.nn.sigmoid(gate) * _silu(tokens(zs_scr))
        for j in range(r):
            for k in range(n_blk):
                os_scr[k, j, rows_of(i), :] = out[j * piece_rows:(j + 1) * piece_rows, k * LANES:(k + 1) * LANES]

    def drain(p):
        i, k = pieces[p]
        output_matmul(i, k)
        if k == n_blk - 1:
            glu(i)
            store_piece(i)

    pieces = [(i, k) for i in range(n_t) for k in range(n_blk)]
    load_piece(0)
    input_matmul(*pieces[0])
    for p, (i, k) in enumerate(pieces):
        if k == 0 and i + 1 < n_t:
            load_piece(i + 1)
        if p + 1 < len(pieces):
            input_matmul(*pieces[p + 1])
        if p >= 1:
            drain(p - 1)
        scan(i, k)
    drain(len(pieces) - 1)


def _s5_weights(lam_re, lam_im, log_dt, b_re, b_im, c_re, c_im):
    G, P = lam_re.shape
    r = S5_STRIDE
    gpb = LANES // S5_GROUP
    n_blk = G // gpb
    lr, li = lam_re.astype(F32), lam_im.astype(F32)
    dt = jnp.exp(log_dt.astype(F32))[:, None]
    mag = jnp.exp(lr * dt)
    ab_r, ab_i = mag * jnp.cos(li * dt), mag * jnp.sin(li * dt)
    den = lr * lr + li * li
    f_r = ((ab_r - 1.0) * lr + ab_i * li) / den
    f_i = (ab_i * lr - (ab_r - 1.0) * li) / den
    br, bi = b_re.astype(F32), b_im.astype(F32)
    bb_r = f_r[..., None] * br - f_i[..., None] * bi
    bb_i = f_r[..., None] * bi + f_i[..., None] * br
    cr, ci = c_re.astype(F32), c_im.astype(F32)
    eye = jnp.eye(gpb, dtype=F32)

    pows = [(jnp.ones_like(ab_r), jnp.zeros_like(ab_i))]
    for _ in range(r):
        pr, pi = pows[-1]
        pows.append((pr * ab_r - pi * ab_i, pr * ab_i + pi * ab_r))

    def pack_in(xr, xi):
        x = jnp.stack([xr, xi]).reshape(2, n_blk, gpb, P, S5_GROUP)
        x = jnp.transpose(x, (1, 2, 4, 0, 3))
        x = x[:, :, :, :, None, :] * eye[None, :, None, None, :, None]
        return x.reshape(n_blk, LANES, 2 * gpb * P)

    def pack_out(er, ei):
        x = jnp.stack([er, ei]).reshape(2, n_blk, gpb, S5_GROUP, P)
        x = jnp.transpose(x, (1, 0, 2, 4, 3))
        x = x[:, :, :, :, None, :] * eye[None, None, :, None, :, None]
        return x.reshape(n_blk, 2 * gpb * P, LANES)

    def pack_direct(kq):
        x = jnp.transpose(kq.reshape(n_blk, gpb, S5_GROUP, S5_GROUP), (0, 1, 3, 2))
        x = x[:, :, :, None, :] * eye[None, :, None, :, None]
        return x.reshape(n_blk, LANES, LANES)

    w_in = []
    for j in range(r):
        pr, pi = pows[r - 1 - j]
        w_in.append(pack_in(pr[..., None] * bb_r - pi[..., None] * bb_i, pr[..., None] * bb_i + pi[..., None] * bb_r))
    w_in = jnp.concatenate(w_in, axis=1).astype(BF16)

    w_x = []
    for j in range(r):
        pr, pi = pows[j + 1]
        er = cr * pr[:, None, :] - ci * pi[:, None, :]
        ei = cr * pi[:, None, :] + ci * pr[:, None, :]
        w_x.append(pack_out(er, -ei))
    w_x = jnp.concatenate(w_x, axis=2).astype(BF16)

    direct = []
    for q in range(r):
        pr, pi = pows[q]
        er = cr * pr[:, None, :] - ci * pi[:, None, :]
        ei = cr * pi[:, None, :] + ci * pr[:, None, :]
        direct.append(pack_direct(jnp.einsum('gop,gpi->goi', er, bb_r) - jnp.einsum('gop,gpi->goi', ei, bb_i)))
    zero = jnp.zeros_like(direct[0])
    w_u = jnp.concatenate(
        [jnp.concatenate([direct[j - i] if i <= j else zero for j in range(r)], axis=2) for i in range(r)],
        axis=1).astype(BF16)

    ar_r, ai_r = pows[r]
    ar = jnp.broadcast_to(ar_r.reshape(n_blk, 1, gpb * P), (n_blk, S5_BATCH, gpb * P))
    ai = jnp.broadcast_to(ai_r.reshape(n_blk, 1, gpb * P), (n_blk, S5_BATCH, gpb * P))
    return w_in, w_x, w_u, ar, ai


def _s5_branch(proj, lam_re, lam_im, log_dt, b_re, b_im, c_re, c_im, d_skip, glu_w, glu_b):
    B, L, N = proj.shape
    W = N // 4
    nb = S5_BATCH
    T = min(S5_TIME, L)
    r = S5_STRIDE
    n_blk = W // LANES
    w_in, w_x, w_u, ar, ai = _s5_weights(lam_re, lam_im, log_dt, b_re, b_im, c_re, c_im)
    S2 = w_in.shape[2]
    rows = nb * T // r
    const3 = lambda g, c: (0, 0, 0)
    const2 = lambda g, c: (0, 0)
    return pl.pallas_call(
        _s5_kernel,
        grid=(B // nb, L // T),
        in_specs=[
            pl.BlockSpec((nb, T, W), lambda g, c: (g, c, 0)),
            pl.BlockSpec((nb, T, W), lambda g, c: (g, c, 1)),
            pl.BlockSpec(w_in.shape, const3),
            pl.BlockSpec(w_x.shape, const3),
            pl.BlockSpec(w_u.shape, const3),
            pl.BlockSpec(ar.shape, const3),
            pl.BlockSpec(ai.shape, const3),
            pl.BlockSpec((1, W), const2),
            pl.BlockSpec((W, W), const2),
            pl.BlockSpec((1, W), const2),
        ],
        out_specs=pl.BlockSpec((nb, T, W), lambda g, c: (g, c, 0)),
        out_shape=jax.ShapeDtypeStruct((B, L, W), BF16),
        scratch_shapes=[
            pltpu.VMEM((3, nb, n_blk, S5_PIECE * r, LANES), F32),
            pltpu.VMEM((n_blk, r, rows, LANES), F32),
            pltpu.VMEM((n_blk, r, rows, LANES), F32),
            pltpu.VMEM((n_blk, rows, S2), F32),
            pltpu.VMEM((n_blk, r, rows, LANES), F32),
            pltpu.VMEM((n_blk, r, rows, LANES), F32),
            pltpu.VMEM((n_blk, 2, nb, S2 // 2), F32),
        ],
        compiler_params=_params("parallel", "arbitrary"),
        name="s5_scan",
    )(proj, proj, w_in, w_x, w_u, ar, ai, d_skip.reshape(1, W).astype(F32), glu_w.astype(BF16),
      glu_b.reshape(1, W).astype(F32))


def _log_sigmoid(x):
    return jnp.minimum(x, 0.0) - jnp.log1p(jnp.exp(-jnp.abs(x)))


def _split3(x):
    hi = x.astype(BF16)
    r1 = x - hi.astype(F32)
    mid = r1.astype(BF16)
    lo = (r1 - mid.astype(F32)).astype(BF16)
    return hi, mid, lo


def _mlstm_kernel(x_ref, z_ref, cw_ref, cb_ref, wq_ref, wk_ref, wv_ref, wvt_ref, gw_ref, gwt_ref, gbc_ref, gbr_ref,
                  ng_ref, sk_ref, o_ref, xe_scr, qkv_scr, ks_scr, vta_scr, ca_scr, m_scr):
    n_sub, Tc = qkv_scr.shape[0], qkv_scr.shape[1]
    W = x_ref.shape[2]
    H = wq_ref.shape[0]
    dh = W // H
    pad = SUBLANES
    heads = range(H)
    nt = (((1,), (1,)), ((), ()))

    @pl.when(pl.program_id(1) == 0)
    def _():
        xe_scr[0:pad, :] = jnp.zeros((pad, W), F32)
        ca_scr[...] = jnp.zeros_like(ca_scr)
        m_scr[...] = jnp.zeros_like(m_scr)
        vta_scr[:, :, dh:, :] = jnp.ones((n_sub, H, pad, Tc), BF16)

    xe_scr[pad:, :] = x_ref[0].astype(F32)

    ti = lax.broadcasted_iota(jnp.int32, (Tc, Tc), 0)
    si = lax.broadcasted_iota(jnp.int32, (Tc, Tc), 1)
    lower = jnp.where(si <= ti, 1.0, 0.0).astype(BF16)
    allowed = ti <= si
    upper = jnp.where(allowed, 1.0, 0.0).astype(BF16)

    def front(s):
        r0 = s * Tc
        xb = x_ref[0, r0:r0 + Tc, :]
        xc = cb_ref[...] + sum(xe_scr[pl.ds(pad + r0 - (ML_CONV - 1) + j, Tc), :] * cw_ref[j:j + 1, :]
                               for j in range(ML_CONV))
        xc = _silu(xc)
        xcb = xc.astype(BF16)
        for h in heads:
            cols = slice(h * dh, (h + 1) * dh)
            q = jnp.dot(xcb[:, cols], wq_ref[h], preferred_element_type=F32)
            k = jnp.dot(xcb[:, cols], wk_ref[h], preferred_element_type=F32)
            v = jnp.dot(xb[:, cols], wv_ref[h], preferred_element_type=F32)
            qkv_scr[s, :, h * dh:(h + 1) * dh] = q.astype(BF16)
            qkv_scr[s, :, W + h * dh:W + (h + 1) * dh] = k.astype(BF16)
            qkv_scr[s, :, 2 * W + h * dh:2 * W + (h + 1) * dh] = v.astype(BF16)
            vta_scr[s, h, :dh, :] = lax.dot_general(wvt_ref[h], xb[:, cols], nt,
                                                    preferred_element_type=F32).astype(BF16)
            ks_scr[s, :, cols] = (k * (dh ** -0.5)).astype(BF16)
        qkv = qkv_scr[s]
        g_col = jnp.dot(qkv, gw_ref[...], preferred_element_type=F32) + gbc_ref[...]
        g_row = lax.dot_general(gwt_ref[...], qkv, nt, preferred_element_type=F32) + gbr_ref[...]
        b_col = sum(jnp.dot(lower, p, preferred_element_type=F32) for p in _split3(_log_sigmoid(g_col)))
        b_row = sum(jnp.dot(p, upper, preferred_element_type=F32) for p in _split3(_log_sigmoid(g_row)))
        return xc, g_col, g_row, b_col, b_row

    def rec_matmuls(s):
        q_h = [qkv_scr[s, :, h * dh:(h + 1) * dh] for h in heads]
        ks_h = [ks_scr[s, :, h * dh:(h + 1) * dh] for h in heads]
        s_qk = [lax.dot_general(ks_h[h], q_h[h], nt, preferred_element_type=F32) for h in heads]
        q_ca = [lax.dot_general(ca_scr[h].astype(BF16), q_h[h], nt, preferred_element_type=F32)
                for h in heads]
        return ks_h, s_qk, q_ca

    def rec_rest(s, mm, xc, g_col, g_row, b_col, b_row):
        ks_h, s_qk, q_ca = mm
        rows = slice(s * Tc, (s + 1) * Tc)
        m_prev = [m_scr[h][:, 0:1] for h in heads]
        br = [b_row[H + h:H + h + 1, :] for h in heads]
        li = [g_row[h:h + 1, :] for h in heads]
        b_last = [br[h][:, Tc - 1:Tc] for h in heads]
        dmat, sc, floor = [], [], []
        for h in heads:
            c_col = g_col[:, h:h + 1] - b_col[:, H + h:H + h + 1]
            log_d = jnp.where(allowed, br[h] + c_col, NEG_BIG)
            inter = br[h] + m_prev[h]
            m_t = jnp.maximum(inter, jnp.max(log_d, axis=0, keepdims=True))
            dmat.append(jnp.exp(log_d - m_t))
            sc.append(jnp.exp(inter - m_t))
            floor.append(jnp.exp(-m_t))
        pv = [jnp.dot(vta_scr[s, h], (s_qk[h] * dmat[h]).astype(BF16), preferred_element_type=F32) for h in heads]
        for h in heads:
            w_log = b_last[h] - br[h] + li[h]
            m_new = jnp.maximum(b_last[h] + m_prev[h], jnp.max(w_log, axis=1, keepdims=True))
            weighted = (vta_scr[s, h].astype(F32) * jnp.exp(w_log - m_new)).astype(BF16)
            decay = jnp.exp(b_last[h] + m_prev[h] - m_new)
            ca_scr[h] = decay * ca_scr[h] + jnp.dot(weighted, ks_h[h], preferred_element_type=F32)
            m_scr[h] = jnp.broadcast_to(m_new, (1, LANES))
        for h in heads:
            cols = slice(h * dh, (h + 1) * dh)
            tot = sc[h] * q_ca[h] + pv[h]
            hh = tot[:dh] / jnp.maximum(jnp.abs(tot[dh:dh + 1]), floor[h])
            mu = jnp.mean(hh, axis=0, keepdims=True)
            cen = hh - mu
            var = jnp.mean(cen * cen, axis=0, keepdims=True)
            hn = jnp.transpose(cen * lax.rsqrt(var + HEAD_NORM_EPS))
            out = hn * ng_ref[:, cols] + sk_ref[:, cols] * xc[:, cols]
            o_ref[0, rows, cols] = (out * _silu(z_ref[0, rows, cols].astype(F32))).astype(o_ref.dtype)

    ready = front(0)
    for s in range(n_sub):
        mm = rec_matmuls(s)
        following = front(s + 1) if s + 1 < n_sub else None
        rec_rest(s, mm, *ready)
        ready = following
    xe_scr[0:pad, :] = xe_scr[n_sub * Tc:n_sub * Tc + pad, :]


def _mlstm_branch(proj, conv_w, conv_b, wq, wk, wv, gate_w, gate_b, norm_g, skip):
    B, L, N = proj.shape
    W = N // 4
    H = wq.shape[0]
    dh = W // H
    Tc = min(ML_CHUNK, L)
    n_sub = min(ML_SUB, L // Tc)
    rows = n_sub * Tc
    gw = jnp.zeros((3 * W, LANES), F32).at[:, :2 * H].set(gate_w.astype(F32)).astype(BF16)
    gwt = jnp.transpose(gate_w.astype(F32)).astype(BF16)
    gbc = jnp.zeros((1, LANES), F32).at[0, :2 * H].set(gate_b.astype(F32))
    gbr = gate_b.astype(F32).reshape(2 * H, 1)
    c2 = lambda b, c: (0, 0)
    c3 = lambda b, c: (0, 0, 0)
    return pl.pallas_call(
        _mlstm_kernel,
        grid=(B, L // rows),
        in_specs=[
            pl.BlockSpec((1, rows, W), lambda b, c: (b, c, 2)),
            pl.BlockSpec((1, rows, W), lambda b, c: (b, c, 3)),
            pl.BlockSpec((ML_CONV, W), c2),
            pl.BlockSpec((1, W), c2),
            pl.BlockSpec((H, dh, dh), c3),
            pl.BlockSpec((H, dh, dh), c3),
            pl.BlockSpec((H, dh, dh), c3),
            pl.BlockSpec((H, dh, dh), c3),
            pl.BlockSpec((3 * W, LANES), c2),
            pl.BlockSpec((2 * H, 3 * W), c2),
            pl.BlockSpec((1, LANES), c2),
            pl.BlockSpec((2 * H, 1), c2),
            pl.BlockSpec((1, W), c2),
            pl.BlockSpec((1, W), c2),
        ],
        out_specs=pl.BlockSpec((1, rows, W), lambda b, c: (b, c, 0)),
        out_shape=jax.ShapeDtypeStruct((B, L, W), BF16),
        scratch_shapes=[
            pltpu.VMEM((SUBLANES + rows, W), F32),
            pltpu.VMEM((n_sub, Tc, 3 * W), BF16),
            pltpu.VMEM((n_sub, Tc, W), BF16),
            pltpu.VMEM((n_sub, H, dh + SUBLANES, Tc), BF16),
            pltpu.VMEM((H, dh + SUBLANES, dh), F32),
            pltpu.VMEM((H, 1, LANES), F32),
        ],
        compiler_params=_params("parallel", "arbitrary"),
        name="mlstm",
    )(proj, proj, conv_w.astype(F32), conv_b.reshape(1, W).astype(F32), wq.astype(BF16), wk.astype(BF16),
      wv.astype(BF16), jnp.transpose(wv, (0, 2, 1)).astype(BF16), gw, gwt, gbc, gbr, norm_g.reshape(1, W).astype(F32), skip.reshape(1, W).astype(F32))


def _attn_kernel(q_ref, k_ref, vt_ref, z_ref, bias_ref, lam_ref, ng_ref, o_ref, m_scr, acc_scr, *,
                 mode, out_scale, blk, bias_index):
    L = q_ref.shape[2]
    n = L // blk
    lane = lax.broadcasted_iota(jnp.int32, (1, LANES), 1)
    map0 = lane < ATTN_HEAD_DIM
    feat = lax.broadcasted_iota(jnp.int32, (LANES, 1), 0)
    ones_rows = jnp.ones((SUBLANES, blk), BF16)

    def stacked_q(qi):
        q = q_ref[0, 0, qi * blk:(qi + 1) * blk, :]
        zero = jnp.zeros_like(q)
        return jnp.concatenate([jnp.where(map0, q, zero), jnp.where(map0, zero, q)], axis=0)

    def scores(qi, j):
        s = lax.dot_general(k_ref[0, 0, j * blk:(j + 1) * blk, :], q2[qi], (((1,), (1,)), ((), ())),
                            preferred_element_type=F32)
        e = bias_index(qi - j)
        return s if e is None else s + bias_ref[e]

    def update(qi, j, s):
        m_cur = jnp.max(s, axis=0, keepdims=True)
        m_next = m_cur if j == 0 else jnp.maximum(m_scr[qi], m_cur)
        p = jnp.exp2(s - m_next).astype(BF16)
        pv = jnp.dot(jnp.concatenate([vt_ref[0, 0, :, j * blk:(j + 1) * blk], ones_rows], axis=0), p,
                     preferred_element_type=F32)
        if j == 0:
            acc_scr[qi] = pv
        else:
            acc_scr[qi] = jnp.exp2(m_scr[qi] - m_next) * acc_scr[qi] + pv
        m_scr[qi] = m_next

    def finish(qi):
        rows = slice(qi * blk, (qi + 1) * blk)
        acc = acc_scr[qi]
        o = acc[:LANES] / acc[LANES:LANES + 1]
        o0, o1 = o[:, :blk], o[:, blk:]
        if mode == "dilated":
            out = jnp.transpose(jnp.where(feat < ATTN_HEAD_DIM, o0, o1))
        else:
            out = jnp.transpose(o0 - lam_ref[:, 0:1] * o1)
            ms = jnp.mean(out * out, axis=-1, keepdims=True)
            out = out * lax.rsqrt(ms + HEAD_NORM_EPS) * ng_ref[...] * out_scale
        o_ref[0, rows, :] = (out * _silu(z_ref[0, 0, rows, :].astype(F32))).astype(o_ref.dtype)

    steps = []
    for a in range(n // 2):
        b = n - 1 - a
        ja = 0
        for jb in range(b + 1):
            steps.append((b, jb))
            if ja <= a and (jb + 1) * (a + 1) >= (ja + 1) * (b + 1):
                steps.append((a, ja))
                ja += 1
    q2 = {qi: stacked_q(qi) for qi in range(n)}
    pending = {}
    for t in range(min(ATTN_LOOKAHEAD, len(steps))):
        pending[t] = scores(*steps[t])
    for t, (qi, j) in enumerate(steps):
        if t + ATTN_LOOKAHEAD < len(steps):
            pending[t + ATTN_LOOKAHEAD] = scores(*steps[t + ATTN_LOOKAHEAD])
        update(qi, j, pending.pop(t))
        if j == qi:
            finish(qi)


def _dilated_multiplicity(delta):
    mult = np.zeros(delta.shape, np.int64)
    for window, dil in DIL_PAIRS:
        mult += (delta >= 0) & (delta % dil == 0) & (delta <= window)
    return mult


def _bias_tables(L, blk, mode):
    n = L // blk
    key = np.arange(blk)[:, None]
    qry = np.arange(blk)[None, :]
    tables, index = [], {}
    for d in range(n):
        delta = d * blk + qry - key
        mult = _dilated_multiplicity(delta) if mode == "dilated" else (delta >= 0).astype(np.int64)
        t = np.where(mult > 0, np.log2(np.maximum(mult, 1)), NEG_BIG).astype(np.float32)
        if not t.any():
            index[d] = None
            continue
        for e, other in enumerate(tables):
            if np.array_equal(t, other):
                index[d] = e
                break
        else:
            index[d] = len(tables)
            tables.append(t)
    stacked = np.stack(tables)
    return jnp.asarray(np.concatenate([stacked, stacked], axis=2)), index


def _attention(qkz, vt, q0, k0, v0, z0, n_blocks, blk, lam, norm_g, mode, out_scale):
    B, _, L, _ = qkz.shape
    bias, index = _bias_tables(L, blk, mode)
    n_q = L // blk
    assert n_q % 2 == 0
    kern = functools.partial(_attn_kernel, mode=mode, out_scale=out_scale, blk=blk, bias_index=index.get)
    c2 = lambda b, n: (0, 0)
    return pl.pallas_call(
        kern,
        grid=(B, n_blocks),
        in_specs=[
            pl.BlockSpec((1, 1, L, LANES), lambda b, n: (b, q0 + n, 0, 0)),
            pl.BlockSpec((1, 1, L, LANES), lambda b, n: (b, k0 + n, 0, 0)),
            pl.BlockSpec((1, 1, LANES, L), lambda b, n: (b, v0 + n, 0, 0)),
            pl.BlockSpec((1, 1, L, LANES), lambda b, n: (b, z0 + n, 0, 0)),
            pl.BlockSpec(bias.shape, lambda b, n: (0, 0, 0)),
            pl.BlockSpec((1, LANES), c2),
            pl.BlockSpec((1, LANES), c2),
        ],
        out_specs=pl.BlockSpec((1, L, LANES), lambda b, n: (b, 0, n)),
        out_shape=jax.ShapeDtypeStruct((B, L, n_blocks * LANES), BF16),
        scratch_shapes=[
            pltpu.VMEM((n_q, 1, 2 * blk), F32),
            pltpu.VMEM((n_q, LANES + SUBLANES, 2 * blk), F32),
        ],
        compiler_params=_params("parallel", "parallel"),
        name="attn_" + mode,
    )(qkz, qkz, vt, qkz, bias, lam, norm_g)


def _rope_tables(L):
    dh = ATTN_HEAD_DIM
    inv = ROPE_THETA ** (-jnp.arange(0, dh, 2, dtype=F32) / dh)
    ang = jnp.arange(L, dtype=F32)[:, None] * inv[None, :]
    cos, sin = jnp.cos(ang), jnp.sin(ang)
    reps = LANES // dh
    cos_t = jnp.tile(jnp.concatenate([cos, cos], axis=-1), (1, reps))
    sin_t = jnp.tile(jnp.concatenate([-sin, sin], axis=-1), (1, reps))
    return cos_t, sin_t


def _even_layer(h, pre_g, w_in, s5_params, ml_params):
    proj = _norm_proj(h, pre_g, w_in.astype(BF16))
    return _s5_branch(proj, *s5_params), _mlstm_branch(proj, *ml_params)


def _odd_layer(prev, h, pre_g, w_in, lq1, lk1, lq2, lk2, diff_norm, layer_idx):
    B, L, D = h.shape
    nblk = PROJ_COLS // LANES
    cos_t, sin_t = _rope_tables(L)
    h, qkz, vt = _out_norm_proj_rope(*prev, h, pre_g, w_in.astype(BF16), cos_t, sin_t)
    blk = min(ATTN_BLOCK, L)
    lam_init = 0.8 - 0.6 * math.exp(-0.3 * layer_idx)
    lam = (jnp.exp(jnp.sum(lq1.astype(F32) * lk1.astype(F32)))
           - jnp.exp(jnp.sum(lq2.astype(F32) * lk2.astype(F32))) + lam_init)
    lam_row = jnp.full((1, LANES), lam, F32)
    ones_row = jnp.ones((1, LANES), F32)
    base = [b for _, b in ODD_CHUNKS]
    c_out = _attention(qkz, vt, base[0], base[1], base[2], base[3], nblk, blk, ones_row, ones_row, "dilated", 1.0)
    d_out = _attention(qkz, vt, base[4], base[5], base[6], base[7], nblk, blk, lam_row,
                       diff_norm.reshape(1, LANES).astype(F32), "diff", 1.0 - lam_init)
    return h, c_out, d_out


def kernel(x, pre_norm, post_norm, w_in_ab, s5_lambda_re, s5_lambda_im, s5_log_dt, s5_b_re, s5_b_im,
           s5_c_re, s5_c_im, s5_d, s5_glu_w, s5_glu_b, ml_conv_w, ml_conv_b, ml_wq, ml_wk, ml_wv,
           ml_gate_w, ml_gate_b, ml_norm, ml_skip, w_out_ab, w_in_cd, diff_lq1, diff_lk1, diff_lq2,
           diff_lk2, diff_norm, w_out_cd):
    depth = pre_norm.shape[0]
    h = x
    prev = None
    for l in range(depth):
        i = l // 2
        if l % 2 == 0:
            if prev is not None:
                h = _out_proj(*prev, h)
            s5_params = (s5_lambda_re[i], s5_lambda_im[i], s5_log_dt[i], s5_b_re[i], s5_b_im[i],
                         s5_c_re[i], s5_c_im[i], s5_d[i], s5_glu_w[i], s5_glu_b[i])
            ml_params = (ml_conv_w[i], ml_conv_b[i], ml_wq[i], ml_wk[i], ml_wv[i], ml_gate_w[i],
                         ml_gate_b[i], ml_norm[i], ml_skip[i])
            a, b = _even_layer(h, pre_norm[l], w_in_ab[i], s5_params, ml_params)
            prev = (a, b, w_out_ab[i], post_norm[l])
        else:
            h, c, d = _odd_layer(prev, h, pre_norm[l], w_in_cd[i], diff_lq1[i], diff_lk1[i], diff_lq2[i],
                                 diff_lk2[i], diff_norm[i], l)
            prev = (c, d, w_out_cd[i], post_norm[l])
    return _out_proj(*prev, h)
```

```python
import functools
import math

import numpy as np
import jax
import jax.numpy as jnp
from jax import lax
from jax.experimental import pallas as pl
from jax.experimental.pallas import tpu as pltpu

F32 = jnp.float32
BF16 = jnp.bfloat16

S5_GROUP = 16
ML_CONV = 4
DIL_PAIRS = ((128, 1), (512, 4), (2048, 16))
ROPE_THETA = 10000.0
NORM_EPS = 1e-6
HEAD_NORM_EPS = 1e-5
ATTN_HEAD_DIM = 64
QUERY_SCALE = ATTN_HEAD_DIM ** -0.5 * math.log2(math.e)

LANES = 128
SUBLANES = 8
VMEM_LIMIT_BYTES = 56 * 1024 * 1024

NEG_BIG = -1e30

PROJ_ROWS = 1024
OUT_PROJ_ROWS = 2048
PROJ_COLS = 512
S5_BATCH = SUBLANES
S5_TIME = 256
S5_STRIDE = 2
S5_PIECE = 32
ML_CHUNK = 256
ML_SUB = 8
ATTN_BLOCK = 256
ATTN_LOOKAHEAD = 3


def _params(*semantics):
    return pltpu.CompilerParams(dimension_semantics=semantics, vmem_limit_bytes=VMEM_LIMIT_BYTES)


def _silu(x):
    return x * jax.nn.sigmoid(x)


def _norm_proj_kernel(x_ref, g_ref, w_ref, o_ref, z_scr):
    tm = x_ref.shape[1]
    halves = [slice(0, tm // 2), slice(tm // 2, tm)]
    n_out = w_ref.shape[1]
    first = slice(0, PROJ_COLS)
    for rs in halves:
        x = x_ref[0, rs, :]
        ms = jnp.mean(x * x, axis=-1, keepdims=True)
        z_scr[rs, :] = (x * lax.rsqrt(ms + NORM_EPS) * g_ref[...]).astype(BF16)
        o_ref[0, rs, first] = jnp.dot(z_scr[rs, :], w_ref[:, first], preferred_element_type=F32).astype(o_ref.dtype)
    for c in range(1, n_out // PROJ_COLS):
        cols = slice(c * PROJ_COLS, (c + 1) * PROJ_COLS)
        o_ref[0, :, cols] = jnp.dot(z_scr[...], w_ref[:, cols], preferred_element_type=F32).astype(o_ref.dtype)


def _norm_proj(x, g, w):
    B, L, D = x.shape
    N = w.shape[1]
    tm = min(PROJ_ROWS, L)
    return pl.pallas_call(
        _norm_proj_kernel,
        grid=(B, L // tm),
        in_specs=[
            pl.BlockSpec((1, tm, D), lambda b, i: (b, i, 0)),
            pl.BlockSpec((1, D), lambda b, i: (0, 0)),
            pl.BlockSpec((D, N), lambda b, i: (0, 0)),
        ],
        out_specs=pl.BlockSpec((1, tm, N), lambda b, i: (b, i, 0)),
        out_shape=jax.ShapeDtypeStruct((B, L, N), BF16),
        scratch_shapes=[pltpu.VMEM((tm, D), BF16)],
        compiler_params=_params("parallel", "parallel"),
        name="norm_proj_even",
    )(x, g.reshape(1, D), w)


def _swap_halves(x, first_half):
    return jnp.where(first_half, pltpu.roll(x, LANES - ATTN_HEAD_DIM // 2, 1), pltpu.roll(x, ATTN_HEAD_DIM // 2, 1))


ODD_CHUNKS = (("q", 0), ("k", 4), ("v", 0), ("z", 8), ("q", 12), ("k", 16), ("v", 4), ("z", 20))


def _out_norm_proj_rope_kernel(a_ref, b_ref, wa_ref, wb_ref, gpost_ref, h_ref, g_ref, w_ref, cos_ref, sin_ref,
                               hout_ref, o_ref, vt_ref, z_scr):
    tm = a_ref.shape[1]
    halves = [slice(i * tm // 4, (i + 1) * tm // 4) for i in range(4)]
    ys = [jnp.dot(a_ref[0, rs, :], wa_ref[...], preferred_element_type=F32)
          + jnp.dot(b_ref[0, rs, :], wb_ref[...], preferred_element_type=F32) for rs in halves]
    for rs, y in zip(halves, ys):
        ms = jnp.mean(y * y, axis=-1, keepdims=True)
        x = h_ref[0, rs, :] + y * lax.rsqrt(ms + NORM_EPS) * gpost_ref[...]
        hout_ref[0, rs, :] = x
        ms = jnp.mean(x * x, axis=-1, keepdims=True)
        z_scr[rs, :] = (x * lax.rsqrt(ms + NORM_EPS) * g_ref[...]).astype(BF16)
    slabs = PROJ_COLS // LANES
    lane = lax.broadcasted_iota(jnp.int32, (1, LANES), 1)
    first_half = (lane % ATTN_HEAD_DIM) < (ATTN_HEAD_DIM // 2)
    for c, (kind, base) in enumerate(ODD_CHUNKS):
        y = jnp.dot(z_scr[...], w_ref[:, c * PROJ_COLS:(c + 1) * PROJ_COLS], preferred_element_type=F32)
        for s in range(slabs):
            t = y[:, s * LANES:(s + 1) * LANES]
            if kind in ("q", "k"):
                t = t * cos_ref[...] + _swap_halves(t, first_half) * sin_ref[...]
            if kind == "q":
                t = t * QUERY_SCALE
            if kind == "v":
                vt_ref[0, base + s] = jnp.transpose(t).astype(vt_ref.dtype)
            else:
                o_ref[0, base + s] = t.astype(o_ref.dtype)


def _out_norm_proj_rope(a, b, w_out, g_post, h, g, w, cos_t, sin_t):
    B, L, D = h.shape
    Wb = a.shape[2]
    N = w.shape[1]
    assert N == len(ODD_CHUNKS) * PROJ_COLS
    tm = min(PROJ_ROWS, L)
    n_v = sum(kind == "v" for kind, _ in ODD_CHUNKS) * (PROJ_COLS // LANES)
    n_o = N // LANES - n_v
    row = lambda b_, i: (b_, i, 0)
    const = lambda b_, i: (0, 0)
    return pl.pallas_call(
        _out_norm_proj_rope_kernel,
        grid=(B, L // tm),
        in_specs=[
            pl.BlockSpec((1, tm, Wb), row),
            pl.BlockSpec((1, tm, Wb), row),
            pl.BlockSpec((Wb, D), const),
            pl.BlockSpec((Wb, D), const),
            pl.BlockSpec((1, D), const),
            pl.BlockSpec((1, tm, D), row),
            pl.BlockSpec((1, D), const),
            pl.BlockSpec((D, N), const),
            pl.BlockSpec((tm, LANES), lambda b_, i: (i, 0)),
            pl.BlockSpec((tm, LANES), lambda b_, i: (i, 0)),
        ],
        out_specs=[
            pl.BlockSpec((1, tm, D), row),
            pl.BlockSpec((1, n_o, tm, LANES), lambda b_, i: (b_, 0, i, 0)),
            pl.BlockSpec((1, n_v, LANES, tm), lambda b_, i: (b_, 0, 0, i)),
        ],
        out_shape=[
            jax.ShapeDtypeStruct((B, L, D), F32),
            jax.ShapeDtypeStruct((B, n_o, L, LANES), BF16),
            jax.ShapeDtypeStruct((B, n_v, LANES, L), BF16),
        ],
        scratch_shapes=[pltpu.VMEM((tm, D), BF16)],
        compiler_params=_params("parallel", "parallel"),
        name="out_norm_proj_odd",
    )(a, b, w_out[:Wb].astype(BF16), w_out[Wb:].astype(BF16), g_post.reshape(1, D), h, g.reshape(1, D), w,
      cos_t, sin_t)


def _out_proj_kernel(a_ref, b_ref, wa_ref, wb_ref, g_ref, h_ref, o_ref):
    tm = a_ref.shape[1]
    parts = [slice(i * tm // 4, (i + 1) * tm // 4) for i in range(4)]
    ys = [jnp.dot(a_ref[0, rs, :], wa_ref[...], preferred_element_type=F32)
          + jnp.dot(b_ref[0, rs, :], wb_ref[...], preferred_element_type=F32) for rs in parts]
    for rs, y in zip(parts, ys):
        ms = jnp.mean(y * y, axis=-1, keepdims=True)
        o_ref[0, rs, :] = h_ref[0, rs, :] + y * lax.rsqrt(ms + NORM_EPS) * g_ref[...]


def _out_proj(a, b, w_out, g, h):
    B, L, W = a.shape
    D = w_out.shape[1]
    tm = min(OUT_PROJ_ROWS, L)
    return pl.pallas_call(
        _out_proj_kernel,
        grid=(B, L // tm),
        in_specs=[
            pl.BlockSpec((1, tm, W), lambda b_, i: (b_, i, 0)),
            pl.BlockSpec((1, tm, W), lambda b_, i: (b_, i, 0)),
            pl.BlockSpec((W, D), lambda b_, i: (0, 0)),
            pl.BlockSpec((W, D), lambda b_, i: (0, 0)),
            pl.BlockSpec((1, D), lambda b_, i: (0, 0)),
            pl.BlockSpec((1, tm, D), lambda b_, i: (b_, i, 0)),
        ],
        out_specs=pl.BlockSpec((1, tm, D), lambda b_, i: (b_, i, 0)),
        out_shape=jax.ShapeDtypeStruct((B, L, D), F32),
        compiler_params=_params("parallel", "parallel"),
        name="out_proj",
    )(a, b, w_out[:W].astype(BF16), w_out[W:].astype(BF16), g.reshape(1, D), h)


def _s5_kernel(u_ref, z_ref, bin_ref, wx_ref, wu_ref, ar_ref, ai_ref, d_ref, gw_ref, gb_ref, o_ref,
               tmp_scr, us_scr, zs_scr, xs_scr, ys_scr, os_scr, st_scr):
    nb, T, W = u_ref.shape
    n_blk = W // LANES
    r = S5_STRIDE
    n_grp = T // r
    half = xs_scr.shape[2] // 2
    n_t = n_grp // S5_PIECE
    piece_rows = S5_PIECE * nb

    @pl.when(pl.program_id(1) == 0)
    def _():
        st_scr[...] = jnp.zeros_like(st_scr)

    def rows_of(i):
        return slice(i * piece_rows, (i + 1) * piece_rows)

    piece_time = S5_PIECE * r

    def load_piece(i):
        steps = slice(i * piece_time, (i + 1) * piece_time)
        for s_idx, (src, dst) in enumerate(((u_ref, us_scr), (z_ref, zs_scr))):
            for b in range(nb):
                xb = src[b, steps, :].astype(F32)
                for k in range(n_blk):
                    tmp_scr[s_idx, b, k] = xb[:, k * LANES:(k + 1) * LANES]
                for k in range(n_blk):
                    for j in range(r):
                        dst[k, j, pl.ds(i * piece_rows + b, S5_PIECE, stride=nb), :] = (
                            tmp_scr[s_idx, b, k, pl.ds(j, S5_PIECE, stride=r), :])

    def store_piece(i):
        steps = slice(i * piece_time, (i + 1) * piece_time)
        for b in range(nb):
            for k in range(n_blk):
                for j in range(r):
                    tmp_scr[2, b, k, pl.ds(j, S5_PIECE, stride=r), :] = (
                        os_scr[k, j, pl.ds(i * piece_rows + b, S5_PIECE, stride=nb), :])
            o_ref[b, steps, :] = jnp.concatenate([tmp_scr[2, b, k] for k in range(n_blk)],
                                                 axis=1).astype(o_ref.dtype)

    def group_inputs(i, k):
        return jnp.concatenate([us_scr[k, j, rows_of(i), :] for j in range(r)], axis=1).astype(BF16)

    def input_matmul(i, k):
        xs_scr[k, rows_of(i), :] = jnp.dot(group_inputs(i, k), bin_ref[k], preferred_element_type=F32)

    def scan(i, k):
        ar, ai = ar_ref[k], ai_ref[k]
        xr, xi = st_scr[k, 0], st_scr[k, 1]
        for t in range(S5_PIECE):
            rs = slice(i * piece_rows + t * nb, i * piece_rows + (t + 1) * nb)
            cr, ci = xs_scr[k, rs, 0:half], xs_scr[k, rs, half:2 * half]
            xs_scr[k, rs, 0:half] = xr
            xs_scr[k, rs, half:2 * half] = xi
            xr, xi = ar * xr - ai * xi + cr, ar * xi + ai * xr + ci
        st_scr[k, 0] = xr
        st_scr[k, 1] = xi

    def output_matmul(i, k):
        y = jnp.dot(xs_scr[k, rows_of(i), :].astype(BF16), wx_ref[k], preferred_element_type=F32)
        y = y + jnp.dot(group_inputs(i, k), wu_ref[k], preferred_element_type=F32)
        for j in range(r):
            ys_scr[k, j, rows_of(i), :] = y[:, j * LANES:(j + 1) * LANES]

    def glu(i):
        def tokens(ref):
            return jnp.concatenate(
                [jnp.concatenate([ref[k, j, rows_of(i), :] for k in range(n_blk)], axis=1) for j in range(r)], axis=0)
        y = tokens(ys_scr) + d_ref[...] * tokens(us_scr)
        g = jax.nn.gelu(y)
        gate = jnp.dot(g.astype(BF16), gw_ref[...], preferred_element_type=F32) + gb_ref[...]
        out = g * jax.nn.sigmoid(gate) * _silu(tokens(zs_scr))
        for j in range(r):
            for k in range(n_blk):
                os_scr[k, j, rows_of(i), :] = out[j * piece_rows:(j + 1) * piece_rows, k * LANES:(k + 1) * LANES]

    def drain(p):
        i, k = pieces[p]
        output_matmul(i, k)
        if k == n_blk - 1:
            glu(i)
            store_piece(i)

    pieces = [(i, k) for i in range(n_t) for k in range(n_blk)]
    load_piece(0)
    input_matmul(*pieces[0])
    for p, (i, k) in enumerate(pieces):
        if k == 0 and i + 1 < n_t:
            load_piece(i + 1)
        if p + 1 < len(pieces):
            input_matmul(*pieces[p + 1])
        if p >= 1:
            drain(p - 1)
        scan(i, k)
    drain(len(pieces) - 1)


def _s5_weights(lam_re, lam_im, log_dt, b_re, b_im, c_re, c_im):
    G, P = lam_re.shape
    r = S5_STRIDE
    gpb = LANES // S5_GROUP
    n_blk = G // gpb
    lr, li = lam_re.astype(F32), lam_im.astype(F32)
    dt = jnp.exp(log_dt.astype(F32))[:, None]
    mag = jnp.exp(lr * dt)
    ab_r, ab_i = mag * jnp.cos(li * dt), mag * jnp.sin(li * dt)
    den = lr * lr + li * li
    f_r = ((ab_r - 1.0) * lr + ab_i * li) / den
    f_i = (ab_i * lr - (ab_r - 1.0) * li) / den
    br, bi = b_re.astype(F32), b_im.astype(F32)
    bb_r = f_r[..., None] * br - f_i[..., None] * bi
    bb_i = f_r[..., None] * bi + f_i[..., None] * br
    cr, ci = c_re.astype(F32), c_im.astype(F32)
    eye = jnp.eye(gpb, dtype=F32)

    pows = [(jnp.ones_like(ab_r), jnp.zeros_like(ab_i))]
    for _ in range(r):
        pr, pi = pows[-1]
        pows.append((pr * ab_r - pi * ab_i, pr * ab_i + pi * ab_r))

    def pack_in(xr, xi):
        x = jnp.stack([xr, xi]).reshape(2, n_blk, gpb, P, S5_GROUP)
        x = jnp.transpose(x, (1, 2, 4, 0, 3))
        x = x[:, :, :, :, None, :] * eye[None, :, None, None, :, None]
        return x.reshape(n_blk, LANES, 2 * gpb * P)

    def pack_out(er, ei):
        x = jnp.stack([er, ei]).reshape(2, n_blk, gpb, S5_GROUP, P)
        x = jnp.transpose(x, (1, 0, 2, 4, 3))
        x = x[:, :, :, :, None, :] * eye[None, None, :, None, :, None]
        return x.reshape(n_blk, 2 * gpb * P, LANES)

    def pack_direct(kq):
        x = jnp.transpose(kq.reshape(n_blk, gpb, S5_GROUP, S5_GROUP), (0, 1, 3, 2))
        x = x[:, :, :, None, :] * eye[None, :, None, :, None]
        return x.reshape(n_blk, LANES, LANES)

    w_in = []
    for j in range(r):
        pr, pi = pows[r - 1 - j]
        w_in.append(pack_in(pr[..., None] * bb_r - pi[..., None] * bb_i, pr[..., None] * bb_i + pi[..., None] * bb_r))
    w_in = jnp.concatenate(w_in, axis=1).astype(BF16)

    w_x = []
    for j in range(r):
        pr, pi = pows[j + 1]
        er = cr * pr[:, None, :] - ci * pi[:, None, :]
        ei = cr * pi[:, None, :] + ci * pr[:, None, :]
        w_x.append(pack_out(er, -ei))
    w_x = jnp.concatenate(w_x, axis=2).astype(BF16)

    direct = []
    for q in range(r):
        pr, pi = pows[q]
        er = cr * pr[:, None, :] - ci * pi[:, None, :]
        ei = cr * pi[:, None, :] + ci * pr[:, None, :]
        direct.append(pack_direct(jnp.einsum('gop,gpi->goi', er, bb_r) - jnp.einsum('gop,gpi->goi', ei, bb_i)))
    zero = jnp.zeros_like(direct[0])
    w_u = jnp.concatenate(
        [jnp.concatenate([direct[j - i] if i <= j else zero for j in range(r)], axis=2) for i in range(r)],
        axis=1).astype(BF16)

    ar_r, ai_r = pows[r]
    ar = jnp.broadcast_to(ar_r.reshape(n_blk, 1, gpb * P), (n_blk, S5_BATCH, gpb * P))
    ai = jnp.broadcast_to(ai_r.reshape(n_blk, 1, gpb * P), (n_blk, S5_BATCH, gpb * P))
    return w_in, w_x, w_u, ar, ai


def _s5_branch(proj, lam_re, lam_im, log_dt, b_re, b_im, c_re, c_im, d_skip, glu_w, glu_b):
    B, L, N = proj.shape
    W = N // 4
    nb = S5_BATCH
    T = min(S5_TIME, L)
    r = S5_STRIDE
    n_blk = W // LANES
    w_in, w_x, w_u, ar, ai = _s5_weights(lam_re, lam_im, log_dt, b_re, b_im, c_re, c_im)
    S2 = w_in.shape[2]
    rows = nb * T // r
    const3 = lambda g, c: (0, 0, 0)
    const2 = lambda g, c: (0, 0)
    return pl.pallas_call(
        _s5_kernel,
        grid=(B // nb, L // T),
        in_specs=[
            pl.BlockSpec((nb, T, W), lambda g, c: (g, c, 0)),
            pl.BlockSpec((nb, T, W), lambda g, c: (g, c, 1)),
            pl.BlockSpec(w_in.shape, const3),
            pl.BlockSpec(w_x.shape, const3),
            pl.BlockSpec(w_u.shape, const3),
            pl.BlockSpec(ar.shape, const3),
            pl.BlockSpec(ai.shape, const3),
            pl.BlockSpec((1, W), const2),
            pl.BlockSpec((W, W), const2),
            pl.BlockSpec((1, W), const2),
        ],
        out_specs=pl.BlockSpec((nb, T, W), lambda g, c: (g, c, 0)),
        out_shape=jax.ShapeDtypeStruct((B, L, W), BF16),
        scratch_shapes=[
            pltpu.VMEM((3, nb, n_blk, S5_PIECE * r, LANES), F32),
            pltpu.VMEM((n_blk, r, rows, LANES), F32),
            pltpu.VMEM((n_blk, r, rows, LANES), F32),
            pltpu.VMEM((n_blk, rows, S2), F32),
            pltpu.VMEM((n_blk, r, rows, LANES), F32),
            pltpu.VMEM((n_blk, r, rows, LANES), F32),
            pltpu.VMEM((n_blk, 2, nb, S2 // 2), F32),
        ],
        compiler_params=_params("parallel", "arbitrary"),
        name="s5_scan",
    )(proj, proj, w_in, w_x, w_u, ar, ai, d_skip.reshape(1, W).astype(F32), glu_w.astype(BF16),
      glu_b.reshape(1, W).astype(F32))


def _log_sigmoid(x):
    return jnp.minimum(x, 0.0) - jnp.log1p(jnp.exp(-jnp.abs(x)))


def _split3(x):
    hi = x.astype(BF16)
    r1 = x - hi.astype(F32)
    mid = r1.astype(BF16)
    lo = (r1 - mid.astype(F32)).astype(BF16)
    return hi, mid, lo


def _mlstm_kernel(x_ref, z_ref, cw_ref, cb_ref, wq_ref, wk_ref, wv_ref, wvt_ref, gw_ref, gwt_ref, gbc_ref, gbr_ref,
                  ng_ref, sk_ref, o_ref, xe_scr, qkv_scr, ks_scr, vta_scr, ca_scr, m_scr):
    n_sub, Tc = qkv_scr.shape[0], qkv_scr.shape[1]
    W = x_ref.shape[2]
    H = wq_ref.shape[0]
    dh = W // H
    pad = SUBLANES
    heads = range(H)
    nt = (((1,), (1,)), ((), ()))

    @pl.when(pl.program_id(1) == 0)
    def _():
        xe_scr[0:pad, :] = jnp.zeros((pad, W), F32)
        ca_scr[...] = jnp.zeros_like(ca_scr)
        m_scr[...] = jnp.zeros_like(m_scr)
        vta_scr[:, :, dh:, :] = jnp.ones((n_sub, H, pad, Tc), BF16)

    xe_scr[pad:, :] = x_ref[0].astype(F32)

    ti = lax.broadcasted_iota(jnp.int32, (Tc, Tc), 0)
    si = lax.broadcasted_iota(jnp.int32, (Tc, Tc), 1)
    lower = jnp.where(si <= ti, 1.0, 0.0).astype(BF16)
    allowed = ti <= si
    upper = jnp.where(allowed, 1.0, 0.0).astype(BF16)

    def front(s):
        r0 = s * Tc
        xb = x_ref[0, r0:r0 + Tc, :]
        xc = cb_ref[...] + sum(xe_scr[pl.ds(pad + r0 - (ML_CONV - 1) + j, Tc), :] * cw_ref[j:j + 1, :]
                               for j in range(ML_CONV))
        xc = _silu(xc)
        xcb = xc.astype(BF16)
        for h in heads:
            cols = slice(h * dh, (h + 1) * dh)
            q = jnp.dot(xcb[:, cols], wq_ref[h], preferred_element_type=F32)
            k = jnp.dot(xcb[:, cols], wk_ref[h], preferred_element_type=F32)
            v = jnp.dot(xb[:, cols], wv_ref[h], preferred_element_type=F32)
            qkv_scr[s, :, h * dh:(h + 1) * dh] = q.astype(BF16)
            qkv_scr[s, :, W + h * dh:W + (h + 1) * dh] = k.astype(BF16)
            qkv_scr[s, :, 2 * W + h * dh:2 * W + (h + 1) * dh] = v.astype(BF16)
            vta_scr[s, h, :dh, :] = lax.dot_general(wvt_ref[h], xb[:, cols], nt,
                                                    preferred_element_type=F32).astype(BF16)
            ks_scr[s, :, cols] = (k * (dh ** -0.5)).astype(BF16)
        qkv = qkv_scr[s]
        g_col = jnp.dot(qkv, gw_ref[...], preferred_element_type=F32) + gbc_ref[...]
        g_row = lax.dot_general(gwt_ref[...], qkv, nt, preferred_element_type=F32) + gbr_ref[...]
        b_col = sum(jnp.dot(lower, p, preferred_element_type=F32) for p in _split3(_log_sigmoid(g_col)))
        b_row = sum(jnp.dot(p, upper, preferred_element_type=F32) for p in _split3(_log_sigmoid(g_row)))
        return xc, g_col, g_row, b_col, b_row

    def rec_matmuls(s):
        q_h = [qkv_scr[s, :, h * dh:(h + 1) * dh] for h in heads]
        ks_h = [ks_scr[s, :, h * dh:(h + 1) * dh] for h in heads]
        s_qk = [lax.dot_general(ks_h[h], q_h[h], nt, preferred_element_type=F32) for h in heads]
        q_ca = [lax.dot_general(ca_scr[h].astype(BF16), q_h[h], nt, preferred_element_type=F32)
                for h in heads]
        return ks_h, s_qk, q_ca

    def rec_rest(s, mm, xc, g_col, g_row, b_col, b_row):
        ks_h, s_qk, q_ca = mm
        rows = slice(s * Tc, (s + 1) * Tc)
        m_prev = [m_scr[h][:, 0:1] for h in heads]
        br = [b_row[H + h:H + h + 1, :] for h in heads]
        li = [g_row[h:h + 1, :] for h in heads]
        b_last = [br[h][:, Tc - 1:Tc] for h in heads]
        dmat, sc, floor = [], [], []
        for h in heads:
            c_col = g_col[:, h:h + 1] - b_col[:, H + h:H + h + 1]
            log_d = jnp.where(allowed, br[h] + c_col, NEG_BIG)
            inter = br[h] + m_prev[h]
            m_t = jnp.maximum(inter, jnp.max(log_d, axis=0, keepdims=True))
            dmat.append(jnp.exp(log_d - m_t))
            sc.append(jnp.exp(inter - m_t))
            floor.append(jnp.exp(-m_t))
        pv = [jnp.dot(vta_scr[s, h], (s_qk[h] * dmat[h]).astype(BF16), preferred_element_type=F32) for h in heads]
        for h in heads:
            w_log = b_last[h] - br[h] + li[h]
            m_new = jnp.maximum(b_last[h] + m_prev[h], jnp.max(w_log, axis=1, keepdims=True))
            weighted = (vta_scr[s, h].astype(F32) * jnp.exp(w_log - m_new)).astype(BF16)
            decay = jnp.exp(b_last[h] + m_prev[h] - m_new)
            ca_scr[h] = decay * ca_scr[h] + jnp.dot(weighted, ks_h[h], preferred_element_type=F32)
            m_scr[h] = jnp.broadcast_to(m_new, (1, LANES))
        for h in heads:
            cols = slice(h * dh, (h + 1) * dh)
            tot = sc[h] * q_ca[h] + pv[h]
            hh = tot[:dh] / jnp.maximum(jnp.abs(tot[dh:dh + 1]), floor[h])
            mu = jnp.mean(hh, axis=0, keepdims=True)
            cen = hh - mu
            var = jnp.mean(cen * cen, axis=0, keepdims=True)
            hn = jnp.transpose(cen * lax.rsqrt(var + HEAD_NORM_EPS))
            out = hn * ng_ref[:, cols] + sk_ref[:, cols] * xc[:, cols]
            o_ref[0, rows, cols] = (out * _silu(z_ref[0, rows, cols].astype(F32))).astype(o_ref.dtype)

    ready = front(0)
    for s in range(n_sub):
        mm = rec_matmuls(s)
        following = front(s + 1) if s + 1 < n_sub else None
        rec_rest(s, mm, *ready)
        ready = following
    xe_scr[0:pad, :] = xe_scr[n_sub * Tc:n_sub * Tc + pad, :]


def _mlstm_branch(proj, conv_w, conv_b, wq, wk, wv, gate_w, gate_b, norm_g, skip):
    B, L, N = proj.shape
    W = N // 4
    H = wq.shape[0]
    dh = W // H
    Tc = min(ML_CHUNK, L)
    n_sub = min(ML_SUB, L // Tc)
    rows = n_sub * Tc
    gw = jnp.zeros((3 * W, LANES), F32).at[:, :2 * H].set(gate_w.astype(F32)).astype(BF16)
    gwt = jnp.transpose(gate_w.astype(F32)).astype(BF16)
    gbc = jnp.zeros((1, LANES), F32).at[0, :2 * H].set(gate_b.astype(F32))
    gbr = gate_b.astype(F32).reshape(2 * H, 1)
    c2 = lambda b, c: (0, 0)
    c3 = lambda b, c: (0, 0, 0)
    return pl.pallas_call(
        _mlstm_kernel,
        grid=(B, L // rows),
        in_specs=[
            pl.BlockSpec((1, rows, W), lambda b, c: (b, c, 2)),
            pl.BlockSpec((1, rows, W), lambda b, c: (b, c, 3)),
            pl.BlockSpec((ML_CONV, W), c2),
            pl.BlockSpec((1, W), c2),
            pl.BlockSpec((H, dh, dh), c3),
            pl.BlockSpec((H, dh, dh), c3),
            pl.BlockSpec((H, dh, dh), c3),
            pl.BlockSpec((H, dh, dh), c3),
            pl.BlockSpec((3 * W, LANES), c2),
            pl.BlockSpec((2 * H, 3 * W), c2),
            pl.BlockSpec((1, LANES), c2),
            pl.BlockSpec((2 * H, 1), c2),
            pl.BlockSpec((1, W), c2),
            pl.BlockSpec((1, W), c2),
        ],
        out_specs=pl.BlockSpec((1, rows, W), lambda b, c: (b, c, 0)),
        out_shape=jax.ShapeDtypeStruct((B, L, W), BF16),
        scratch_shapes=[
            pltpu.VMEM((SUBLANES + rows, W), F32),
            pltpu.VMEM((n_sub, Tc, 3 * W), BF16),
            pltpu.VMEM((n_sub, Tc, W), BF16),
            pltpu.VMEM((n_sub, H, dh + SUBLANES, Tc), BF16),
            pltpu.VMEM((H, dh + SUBLANES, dh), F32),
            pltpu.VMEM((H, 1, LANES), F32),
        ],
        compiler_params=_params("parallel", "arbitrary"),
        name="mlstm",
    )(proj, proj, conv_w.astype(F32), conv_b.reshape(1, W).astype(F32), wq.astype(BF16), wk.astype(BF16),
      wv.astype(BF16), jnp.transpose(wv, (0, 2, 1)).astype(BF16), gw, gwt, gbc, gbr, norm_g.reshape(1, W).astype(F32), skip.reshape(1, W).astype(F32))


def _attn_kernel(q_ref, k_ref, vt_ref, z_ref, bias_ref, lam_ref, ng_ref, o_ref, m_scr, acc_scr, *,
                 mode, out_scale, blk, bias_index):
    L = q_ref.shape[2]
    n = L // blk
    lane = lax.broadcasted_iota(jnp.int32, (1, LANES), 1)
    map0 = lane < ATTN_HEAD_DIM
    feat = lax.broadcasted_iota(jnp.int32, (LANES, 1), 0)
    ones_rows = jnp.ones((SUBLANES, blk), BF16)

    def stacked_q(qi):
        q = q_ref[0, 0, qi * blk:(qi + 1) * blk, :]
        zero = jnp.zeros_like(q)
        return jnp.concatenate([jnp.where(map0, q, zero), jnp.where(map0, zero, q)], axis=0)

    def scores(qi, j):
        s = lax.dot_general(k_ref[0, 0, j * blk:(j + 1) * blk, :], q2[qi], (((1,), (1,)), ((), ())),
                            preferred_element_type=F32)
        e = bias_index(qi - j)
        return s if e is None else s + bias_ref[e]

    def update(qi, j, s):
        m_cur = jnp.max(s, axis=0, keepdims=True)
        m_next = m_cur if j == 0 else jnp.maximum(m_scr[qi], m_cur)
        p = jnp.exp2(s - m_next).astype(BF16)
        pv = jnp.dot(jnp.concatenate([vt_ref[0, 0, :, j * blk:(j + 1) * blk], ones_rows], axis=0), p,
                     preferred_element_type=F32)
        if j == 0:
            acc_scr[qi] = pv
        else:
            acc_scr[qi] = jnp.exp2(m_scr[qi] - m_next) * acc_scr[qi] + pv
        m_scr[qi] = m_next

    def finish(qi):
        rows = slice(qi * blk, (qi + 1) * blk)
        acc = acc_scr[qi]
        o = acc[:LANES] / acc[LANES:LANES + 1]
        o0, o1 = o[:, :blk], o[:, blk:]
        if mode == "dilated":
            out = jnp.transpose(jnp.where(feat < ATTN_HEAD_DIM, o0, o1))
        else:
            out = jnp.transpose(o0 - lam_ref[:, 0:1] * o1)
            ms = jnp.mean(out * out, axis=-1, keepdims=True)
            out = out * lax.rsqrt(ms + HEAD_NORM_EPS) * ng_ref[...] * out_scale
        o_ref[0, rows, :] = (out * _silu(z_ref[0, 0, rows, :].astype(F32))).astype(o_ref.dtype)

    steps = []
    for a in range(n // 2):
        b = n - 1 - a
        ja = 0
        for jb in range(b + 1):
            steps.append((b, jb))
            if ja <= a and (jb + 1) * (a + 1) >= (ja + 1) * (b + 1):
                steps.append((a, ja))
                ja += 1
    q2 = {qi: stacked_q(qi) for qi in range(n)}
    pending = {}
    for t in range(min(ATTN_LOOKAHEAD, len(steps))):
        pending[t] = scores(*steps[t])
    for t, (qi, j) in enumerate(steps):
        if t + ATTN_LOOKAHEAD < len(steps):
            pending[t + ATTN_LOOKAHEAD] = scores(*steps[t + ATTN_LOOKAHEAD])
        update(qi, j, pending.pop(t))
        if j == qi:
            finish(qi)


def _dilated_multiplicity(delta):
    mult = np.zeros(delta.shape, np.int64)
    for window, dil in DIL_PAIRS:
        mult += (delta >= 0) & (delta % dil == 0) & (delta <= window)
    return mult


def _bias_tables(L, blk, mode):
    n = L // blk
    key = np.arange(blk)[:, None]
    qry = np.arange(blk)[None, :]
    tables, index = [], {}
    for d in range(n):
        delta = d * blk + qry - key
        mult = _dilated_multiplicity(delta) if mode == "dilated" else (delta >= 0).astype(np.int64)
        t = np.where(mult > 0, np.log2(np.maximum(mult, 1)), NEG_BIG).astype(np.float32)
        if not t.any():
            index[d] = None
            continue
        for e, other in enumerate(tables):
            if np.array_equal(t, other):
                index[d] = e
                break
        else:
            index[d] = len(tables)
            tables.append(t)
    stacked = np.stack(tables)
    return jnp.asarray(np.concatenate([stacked, stacked], axis=2)), index


def _attention(qkz, vt, q0, k0, v0, z0, n_blocks, blk, lam, norm_g, mode, out_scale):
    B, _, L, _ = qkz.shape
    bias, index = _bias_tables(L, blk, mode)
    n_q = L // blk
    assert n_q % 2 == 0
    kern = functools.partial(_attn_kernel, mode=mode, out_scale=out_scale, blk=blk, bias_index=index.get)
    c2 = lambda b, n: (0, 0)
    return pl.pallas_call(
        kern,
        grid=(B, n_blocks),
        in_specs=[
            pl.BlockSpec((1, 1, L, LANES), lambda b, n: (b, q0 + n, 0, 0)),
            pl.BlockSpec((1, 1, L, LANES), lambda b, n: (b, k0 + n, 0, 0)),
            pl.BlockSpec((1, 1, LANES, L), lambda b, n: (b, v0 + n, 0, 0)),
            pl.BlockSpec((1, 1, L, LANES), lambda b, n: (b, z0 + n, 0, 0)),
            pl.BlockSpec(bias.shape, lambda b, n: (0, 0, 0)),
            pl.BlockSpec((1, LANES), c2),
            pl.BlockSpec((1, LANES), c2),
        ],
        out_specs=pl.BlockSpec((1, L, LANES), lambda b, n: (b, 0, n)),
        out_shape=jax.ShapeDtypeStruct((B, L, n_blocks * LANES), BF16),
        scratch_shapes=[
            pltpu.VMEM((n_q, 1, 2 * blk), F32),
            pltpu.VMEM((n_q, LANES + SUBLANES, 2 * blk), F32),
        ],
        compiler_params=_params("parallel", "parallel"),
        name="attn_" + mode,
    )(qkz, qkz, vt, qkz, bias, lam, norm_g)


def _rope_tables(L):
    dh = ATTN_HEAD_DIM
    inv = ROPE_THETA ** (-jnp.arange(0, dh, 2, dtype=F32) / dh)
    ang = jnp.arange(L, dtype=F32)[:, None] * inv[None, :]
    cos, sin = jnp.cos(ang), jnp.sin(ang)
    reps = LANES // dh
    cos_t = jnp.tile(jnp.concatenate([cos, cos], axis=-1), (1, reps))
    sin_t = jnp.tile(jnp.concatenate([-sin, sin], axis=-1), (1, reps))
    return cos_t, sin_t


def _even_layer(h, pre_g, w_in, s5_params, ml_params):
    proj = _norm_proj(h, pre_g, w_in.astype(BF16))
    return _s5_branch(proj, *s5_params), _mlstm_branch(proj, *ml_params)


def _odd_layer(prev, h, pre_g, w_in, lq1, lk1, lq2, lk2, diff_norm, layer_idx):
    B, L, D = h.shape
    nblk = PROJ_COLS // LANES
    cos_t, sin_t = _rope_tables(L)
    h, qkz, vt = _out_norm_proj_rope(*prev, h, pre_g, w_in.astype(BF16), cos_t, sin_t)
    blk = min(ATTN_BLOCK, L)
    lam_init = 0.8 - 0.6 * math.exp(-0.3 * layer_idx)
    lam = (jnp.exp(jnp.sum(lq1.astype(F32) * lk1.astype(F32)))
           - jnp.exp(jnp.sum(lq2.astype(F32) * lk2.astype(F32))) + lam_init)
    lam_row = jnp.full((1, LANES), lam, F32)
    ones_row = jnp.ones((1, LANES), F32)
    base = [b for _, b in ODD_CHUNKS]
    c_out = _attention(qkz, vt, base[0], base[1], base[2], base[3], nblk, blk, ones_row, ones_row, "dilated", 1.0)
    d_out = _attention(qkz, vt, base[4], base[5], base[6], base[7], nblk, blk, lam_row,
                       diff_norm.reshape(1, LANES).astype(F32), "diff", 1.0 - lam_init)
    return h, c_out, d_out


def kernel(x, pre_norm, post_norm, w_in_ab, s5_lambda_re, s5_lambda_im, s5_log_dt, s5_b_re, s5_b_im,
           s5_c_re, s5_c_im, s5_d, s5_glu_w, s5_glu_b, ml_conv_w, ml_conv_b, ml_wq, ml_wk, ml_wv,
           ml_gate_w, ml_gate_b, ml_norm, ml_skip, w_out_ab, w_in_cd, diff_lq1, diff_lk1, diff_lq2,
           diff_lk2, diff_norm, w_out_cd):
    depth = pre_norm.shape[0]
    h = x
    prev = None
    for l in range(depth):
        i = l // 2
        if l % 2 == 0:
            if prev is not None:
                h = _out_proj(*prev, h)
            s5_params = (s5_lambda_re[i], s5_lambda_im[i], s5_log_dt[i], s5_b_re[i], s5_b_im[i],
                         s5_c_re[i], s5_c_im[i], s5_d[i], s5_glu_w[i], s5_glu_b[i])
            ml_params = (ml_conv_w[i], ml_conv_b[i], ml_wq[i], ml_wk[i], ml_wv[i], ml_gate_w[i],
                         ml_gate_b[i], ml_norm[i], ml_skip[i])
            a, b = _even_layer(h, pre_norm[l], w_in_ab[i], s5_params, ml_params)
            prev = (a, b, w_out_ab[i], post_norm[l])
        else:
            h, c, d = _odd_layer(prev, h, pre_norm[l], w_in_cd[i], diff_lq1[i], diff_lk1[i], diff_lq2[i],
                                 diff_lk2[i], diff_norm[i], l)
            prev = (c, d, w_out_cd[i], post_norm[l])
    return _out_proj(*prev, h)
```

```python
import functools
import math

import numpy as np
import jax
import jax.numpy as jnp
from jax import lax
from jax.experimental import pallas as pl
from jax.experimental.pallas import tpu as pltpu

F32 = jnp.float32
BF16 = jnp.bfloat16

S5_GROUP = 16
ML_CONV = 4
DIL_PAIRS = ((128, 1), (512, 4), (2048, 16))
ROPE_THETA = 10000.0
NORM_EPS = 1e-6
HEAD_NORM_EPS = 1e-5
ATTN_HEAD_DIM = 64
QUERY_SCALE = ATTN_HEAD_DIM ** -0.5 * math.log2(math.e)

LANES = 128
SUBLANES = 8
VMEM_LIMIT_BYTES = 56 * 1024 * 1024

NEG_BIG = -1e30

PROJ_ROWS = 1024
OUT_PROJ_ROWS = 2048
PROJ_COLS = 512
S5_BATCH = SUBLANES
S5_TIME = 256
S5_STRIDE = 2
S5_PIECE = 32
ML_CHUNK = 256
ML_SUB = 8
ATTN_BLOCK = 256
ATTN_LOOKAHEAD = 3


def _params(*semantics):
    return pltpu.CompilerParams(dimension_semantics=semantics, vmem_limit_bytes=VMEM_LIMIT_BYTES)


def _silu(x):
    return x * jax.nn.sigmoid(x)


def _norm_proj_kernel(x_ref, g_ref, w_ref, o_ref, z_scr):
    tm = x_ref.shape[1]
    halves = [slice(0, tm // 2), slice(tm // 2, tm)]
    n_out = w_ref.shape[1]
    first = slice(0, PROJ_COLS)
    for rs in halves:
        x = x_ref[0, rs, :]
        ms = jnp.mean(x * x, axis=-1, keepdims=True)
        z_scr[rs, :] = (x * lax.rsqrt(ms + NORM_EPS) * g_ref[...]).astype(BF16)
        o_ref[0, rs, first] = jnp.dot(z_scr[rs, :], w_ref[:, first], preferred_element_type=F32).astype(o_ref.dtype)
    for c in range(1, n_out // PROJ_COLS):
        cols = slice(c * PROJ_COLS, (c + 1) * PROJ_COLS)
        o_ref[0, :, cols] = jnp.dot(z_scr[...], w_ref[:, cols], preferred_element_type=F32).astype(o_ref.dtype)


def _norm_proj(x, g, w):
    B, L, D = x.shape
    N = w.shape[1]
    tm = min(PROJ_ROWS, L)
    return pl.pallas_call(
        _norm_proj_kernel,
        grid=(B, L // tm),
        in_specs=[
            pl.BlockSpec((1, tm, D), lambda b, i: (b, i, 0)),
            pl.BlockSpec((1, D), lambda b, i: (0, 0)),
            pl.BlockSpec((D, N), lambda b, i: (0, 0)),
        ],
        out_specs=pl.BlockSpec((1, tm, N), lambda b, i: (b, i, 0)),
        out_shape=jax.ShapeDtypeStruct((B, L, N), BF16),
        scratch_shapes=[pltpu.VMEM((tm, D), BF16)],
        compiler_params=_params("parallel", "parallel"),
        name="norm_proj_even",
    )(x, g.reshape(1, D), w)


def _swap_halves(x, first_half):
    return jnp.where(first_half, pltpu.roll(x, LANES - ATTN_HEAD_DIM // 2, 1), pltpu.roll(x, ATTN_HEAD_DIM // 2, 1))


ODD_CHUNKS = (("q", 0), ("k", 4), ("v", 0), ("z", 8), ("q", 12), ("k", 16), ("v", 4), ("z", 20))


def _out_norm_proj_rope_kernel(a_ref, b_ref, wa_ref, wb_ref, gpost_ref, h_ref, g_ref, w_ref, cos_ref, sin_ref,
                               hout_ref, o_ref, vt_ref, z_scr):
    tm = a_ref.shape[1]
    halves = [slice(i * tm // 4, (i + 1) * tm // 4) for i in range(4)]
    ys = [jnp.dot(a_ref[0, rs, :], wa_ref[...], preferred_element_type=F32)
          + jnp.dot(b_ref[0, rs, :], wb_ref[...], preferred_element_type=F32) for rs in halves]
    for rs, y in zip(halves, ys):
        ms = jnp.mean(y * y, axis=-1, keepdims=True)
        x = h_ref[0, rs, :] + y * lax.rsqrt(ms + NORM_EPS) * gpost_ref[...]
        hout_ref[0, rs, :] = x
        ms = jnp.mean(x * x, axis=-1, keepdims=True)
        z_scr[rs, :] = (x * lax.rsqrt(ms + NORM_EPS) * g_ref[...]).astype(BF16)
    slabs = PROJ_COLS // LANES
    lane = lax.broadcasted_iota(jnp.int32, (1, LANES), 1)
    first_half = (lane % ATTN_HEAD_DIM) < (ATTN_HEAD_DIM // 2)
    for c, (kind, base) in enumerate(ODD_CHUNKS):
        y = jnp.dot(z_scr[...], w_ref[:, c * PROJ_COLS:(c + 1) * PROJ_COLS], preferred_element_type=F32)
        for s in range(slabs):
            t = y[:, s * LANES:(s + 1) * LANES]
            if kind in ("q", "k"):
                t = t * cos_ref[...] + _swap_halves(t, first_half) * sin_ref[...]
            if kind == "q":
                t = t * QUERY_SCALE
            if kind == "v":
                vt_ref[0, base + s] = jnp.transpose(t).astype(vt_ref.dtype)
            else:
                o_ref[0, base + s] = t.astype(o_ref.dtype)


def _out_norm_proj_rope(a, b, w_out, g_post, h, g, w, cos_t, sin_t):
    B, L, D = h.shape
    Wb = a.shape[2]
    N = w.shape[1]
    assert N == len(ODD_CHUNKS) * PROJ_COLS
    tm = min(PROJ_ROWS, L)
    n_v = sum(kind == "v" for kind, _ in ODD_CHUNKS) * (PROJ_COLS // LANES)
    n_o = N // LANES - n_v
    row = lambda b_, i: (b_, i, 0)
    const = lambda b_, i: (0, 0)
    return pl.pallas_call(
        _out_norm_proj_rope_kernel,
        grid=(B, L // tm),
        in_specs=[
            pl.BlockSpec((1, tm, Wb), row),
            pl.BlockSpec((1, tm, Wb), row),
            pl.BlockSpec((Wb, D), const),
            pl.BlockSpec((Wb, D), const),
            pl.BlockSpec((1, D), const),
            pl.BlockSpec((1, tm, D), row),
            pl.BlockSpec((1, D), const),
            pl.BlockSpec((D, N), const),
            pl.BlockSpec((tm, LANES), lambda b_, i: (i, 0)),
            pl.BlockSpec((tm, LANES), lambda b_, i: (i, 0)),
        ],
        out_specs=[
            pl.BlockSpec((1, tm, D), row),
            pl.BlockSpec((1, n_o, tm, LANES), lambda b_, i: (b_, 0, i, 0)),
            pl.BlockSpec((1, n_v, LANES, tm), lambda b_, i: (b_, 0, 0, i)),
        ],
        out_shape=[
            jax.ShapeDtypeStruct((B, L, D), F32),
            jax.ShapeDtypeStruct((B, n_o, L, LANES), BF16),
            jax.ShapeDtypeStruct((B, n_v, LANES, L), BF16),
        ],
        scratch_shapes=[pltpu.VMEM((tm, D), BF16)],
        compiler_params=_params("parallel", "parallel"),
        name="out_norm_proj_odd",
    )(a, b, w_out[:Wb].astype(BF16), w_out[Wb:].astype(BF16), g_post.reshape(1, D), h, g.reshape(1, D), w,
      cos_t, sin_t)


def _out_proj_kernel(a_ref, b_ref, wa_ref, wb_ref, g_ref, h_ref, o_ref):
    y = jnp.dot(a_ref[0], wa_ref[...], preferred_element_type=F32)
    y = y + jnp.dot(b_ref[0], wb_ref[...], preferred_element_type=F32)
    ms = jnp.mean(y * y, axis=-1, keepdims=True)
    o_ref[0] = h_ref[0] + y * lax.rsqrt(ms + NORM_EPS) * g_ref[...]


def _out_proj(a, b, w_out, g, h):
    B, L, W = a.shape
    D = w_out.shape[1]
    tm = min(OUT_PROJ_ROWS, L)
    return pl.pallas_call(
        _out_proj_kernel,
        grid=(B, L // tm),
        in_specs=[
            pl.BlockSpec((1, tm, W), lambda b_, i: (b_, i, 0)),
            pl.BlockSpec((1, tm, W), lambda b_, i: (b_, i, 0)),
            pl.BlockSpec((W, D), lambda b_, i: (0, 0)),
            pl.BlockSpec((W, D), lambda b_, i: (0, 0)),
            pl.BlockSpec((1, D), lambda b_, i: (0, 0)),
            pl.BlockSpec((1, tm, D), lambda b_, i: (b_, i, 0)),
        ],
        out_specs=pl.BlockSpec((1, tm, D), lambda b_, i: (b_, i, 0)),
        out_shape=jax.ShapeDtypeStruct((B, L, D), F32),
        compiler_params=_params("parallel", "parallel"),
        name="out_proj",
    )(a, b, w_out[:W].astype(BF16), w_out[W:].astype(BF16), g.reshape(1, D), h)


def _s5_kernel(u_ref, z_ref, bin_ref, wx_ref, wu_ref, ar_ref, ai_ref, d_ref, gw_ref, gb_ref, o_ref,
               tmp_scr, us_scr, zs_scr, xs_scr, ys_scr, os_scr, st_scr):
    nb, T, W = u_ref.shape
    n_blk = W // LANES
    r = S5_STRIDE
    n_grp = T // r
    half = xs_scr.shape[2] // 2
    n_t = n_grp // S5_PIECE
    piece_rows = S5_PIECE * nb

    @pl.when(pl.program_id(1) == 0)
    def _():
        st_scr[...] = jnp.zeros_like(st_scr)

    def rows_of(i):
        return slice(i * piece_rows, (i + 1) * piece_rows)

    piece_time = S5_PIECE * r

    def load_piece(i):
        steps = slice(i * piece_time, (i + 1) * piece_time)
        for s_idx, (src, dst) in enumerate(((u_ref, us_scr), (z_ref, zs_scr))):
            for b in range(nb):
                xb = src[b, steps, :].astype(F32)
                for k in range(n_blk):
                    tmp_scr[s_idx, b, k] = xb[:, k * LANES:(k + 1) * LANES]
                for k in range(n_blk):
                    for j in range(r):
                        dst[k, j, pl.ds(i * piece_rows + b, S5_PIECE, stride=nb), :] = (
                            tmp_scr[s_idx, b, k, pl.ds(j, S5_PIECE, stride=r), :])

    def store_piece(i):
        steps = slice(i * piece_time, (i + 1) * piece_time)
        for b in range(nb):
            for k in range(n_blk):
                for j in range(r):
                    tmp_scr[2, b, k, pl.ds(j, S5_PIECE, stride=r), :] = (
                        os_scr[k, j, pl.ds(i * piece_rows + b, S5_PIECE, stride=nb), :])
            o_ref[b, steps, :] = jnp.concatenate([tmp_scr[2, b, k] for k in range(n_blk)],
                                                 axis=1).astype(o_ref.dtype)

    def group_inputs(i, k):
        return jnp.concatenate([us_scr[k, j, rows_of(i), :] for j in range(r)], axis=1).astype(BF16)

    def input_matmul(i, k):
        xs_scr[k, rows_of(i), :] = jnp.dot(group_inputs(i, k), bin_ref[k], preferred_element_type=F32)

    def scan(i, k):
        ar, ai = ar_ref[k], ai_ref[k]
        xr, xi = st_scr[k, 0], st_scr[k, 1]
        for t in range(S5_PIECE):
            rs = slice(i * piece_rows + t * nb, i * piece_rows + (t + 1) * nb)
            cr, ci = xs_scr[k, rs, 0:half], xs_scr[k, rs, half:2 * half]
            xs_scr[k, rs, 0:half] = xr
            xs_scr[k, rs, half:2 * half] = xi
            xr, xi = ar * xr - ai * xi + cr, ar * xi + ai * xr + ci
        st_scr[k, 0] = xr
        st_scr[k, 1] = xi

    def output_matmul(i, k):
        y = jnp.dot(xs_scr[k, rows_of(i), :].astype(BF16), wx_ref[k], preferred_element_type=F32)
        y = y + jnp.dot(group_inputs(i, k), wu_ref[k], preferred_element_type=F32)
        for j in range(r):
            ys_scr[k, j, rows_of(i), :] = y[:, j * LANES:(j + 1) * LANES]

    def glu(i):
        def tokens(ref):
            return jnp.concatenate(
                [jnp.concatenate([ref[k, j, rows_of(i), :] for k in range(n_blk)], axis=1) for j in range(r)], axis=0)
        y = tokens(ys_scr) + d_ref[...] * tokens(us_scr)
        g = jax.nn.gelu(y)
        gate = jnp.dot(g.astype(BF16), gw_ref[...], preferred_element_type=F32) + gb_ref[...]
        out = g * jax.nn.sigmoid(gate) * _silu(tokens(zs_scr))
        for j in range(r):
            for k in range(n_blk):
                os_scr[k, j, rows_of(i), :] = out[j * piece_rows:(j + 1) * piece_rows, k * LANES:(k + 1) * LANES]

    def drain(p):
        i, k = pieces[p]
        output_matmul(i, k)
        if k == n_blk - 1:
            glu(i)
            store_piece(i)

    pieces = [(i, k) for i in range(n_t) for k in range(n_blk)]
    load_piece(0)
    input_matmul(*pieces[0])
    for p, (i, k) in enumerate(pieces):
        if k == 0 and i + 1 < n_t:
            load_piece(i + 1)
        if p + 1 < len(pieces):
            input_matmul(*pieces[p + 1])
        if p >= 1:
            drain(p - 1)
        scan(i, k)
    drain(len(pieces) - 1)


def _s5_weights(lam_re, lam_im, log_dt, b_re, b_im, c_re, c_im):
    G, P = lam_re.shape
    r = S5_STRIDE
    gpb = LANES // S5_GROUP
    n_blk = G // gpb
    lr, li = lam_re.astype(F32), lam_im.astype(F32)
    dt = jnp.exp(log_dt.astype(F32))[:, None]
    mag = jnp.exp(lr * dt)
    ab_r, ab_i = mag * jnp.cos(li * dt), mag * jnp.sin(li * dt)
    den = lr * lr + li * li
    f_r = ((ab_r - 1.0) * lr + ab_i * li) / den
    f_i = (ab_i * lr - (ab_r - 1.0) * li) / den
    br, bi = b_re.astype(F32), b_im.astype(F32)
    bb_r = f_r[..., None] * br - f_i[..., None] * bi
    bb_i = f_r[..., None] * bi + f_i[..., None] * br
    cr, ci = c_re.astype(F32), c_im.astype(F32)
    eye = jnp.eye(gpb, dtype=F32)

    pows = [(jnp.ones_like(ab_r), jnp.zeros_like(ab_i))]
    for _ in range(r):
        pr, pi = pows[-1]
        pows.append((pr * ab_r - pi * ab_i, pr * ab_i + pi * ab_r))

    def pack_in(xr, xi):
        x = jnp.stack([xr, xi]).reshape(2, n_blk, gpb, P, S5_GROUP)
        x = jnp.transpose(x, (1, 2, 4, 0, 3))
        x = x[:, :, :, :, None, :] * eye[None, :, None, None, :, None]
        return x.reshape(n_blk, LANES, 2 * gpb * P)

    def pack_out(er, ei):
        x = jnp.stack([er, ei]).reshape(2, n_blk, gpb, S5_GROUP, P)
        x = jnp.transpose(x, (1, 0, 2, 4, 3))
        x = x[:, :, :, :, None, :] * eye[None, None, :, None, :, None]
        return x.reshape(n_blk, 2 * gpb * P, LANES)

    def pack_direct(kq):
        x = jnp.transpose(kq.reshape(n_blk, gpb, S5_GROUP, S5_GROUP), (0, 1, 3, 2))
        x = x[:, :, :, None, :] * eye[None, :, None, :, None]
        return x.reshape(n_blk, LANES, LANES)

    w_in = []
    for j in range(r):
        pr, pi = pows[r - 1 - j]
        w_in.append(pack_in(pr[..., None] * bb_r - pi[..., None] * bb_i, pr[..., None] * bb_i + pi[..., None] * bb_r))
    w_in = jnp.concatenate(w_in, axis=1).astype(BF16)

    w_x = []
    for j in range(r):
        pr, pi = pows[j + 1]
        er = cr * pr[:, None, :] - ci * pi[:, None, :]
        ei = cr * pi[:, None, :] + ci * pr[:, None, :]
        w_x.append(pack_out(er, -ei))
    w_x = jnp.concatenate(w_x, axis=2).astype(BF16)

    direct = []
    for q in range(r):
        pr, pi = pows[q]
        er = cr * pr[:, None, :] - ci * pi[:, None, :]
        ei = cr * pi[:, None, :] + ci * pr[:, None, :]
        direct.append(pack_direct(jnp.einsum('gop,gpi->goi', er, bb_r) - jnp.einsum('gop,gpi->goi', ei, bb_i)))
    zero = jnp.zeros_like(direct[0])
    w_u = jnp.concatenate(
        [jnp.concatenate([direct[j - i] if i <= j else zero for j in range(r)], axis=2) for i in range(r)],
        axis=1).astype(BF16)

    ar_r, ai_r = pows[r]
    ar = jnp.broadcast_to(ar_r.reshape(n_blk, 1, gpb * P), (n_blk, S5_BATCH, gpb * P))
    ai = jnp.broadcast_to(ai_r.reshape(n_blk, 1, gpb * P), (n_blk, S5_BATCH, gpb * P))
    return w_in, w_x, w_u, ar, ai


def _s5_branch(proj, lam_re, lam_im, log_dt, b_re, b_im, c_re, c_im, d_skip, glu_w, glu_b):
    B, L, N = proj.shape
    W = N // 4
    nb = S5_BATCH
    T = min(S5_TIME, L)
    r = S5_STRIDE
    n_blk = W // LANES
    w_in, w_x, w_u, ar, ai = _s5_weights(lam_re, lam_im, log_dt, b_re, b_im, c_re, c_im)
    S2 = w_in.shape[2]
    rows = nb * T // r
    const3 = lambda g, c: (0, 0, 0)
    const2 = lambda g, c: (0, 0)
    return pl.pallas_call(
        _s5_kernel,
        grid=(B // nb, L // T),
        in_specs=[
            pl.BlockSpec((nb, T, W), lambda g, c: (g, c, 0)),
            pl.BlockSpec((nb, T, W), lambda g, c: (g, c, 1)),
            pl.BlockSpec(w_in.shape, const3),
            pl.BlockSpec(w_x.shape, const3),
            pl.BlockSpec(w_u.shape, const3),
            pl.BlockSpec(ar.shape, const3),
            pl.BlockSpec(ai.shape, const3),
            pl.BlockSpec((1, W), const2),
            pl.BlockSpec((W, W), const2),
            pl.BlockSpec((1, W), const2),
        ],
        out_specs=pl.BlockSpec((nb, T, W), lambda g, c: (g, c, 0)),
        out_shape=jax.ShapeDtypeStruct((B, L, W), BF16),
        scratch_shapes=[
            pltpu.VMEM((3, nb, n_blk, S5_PIECE * r, LANES), F32),
            pltpu.VMEM((n_blk, r, rows, LANES), F32),
            pltpu.VMEM((n_blk, r, rows, LANES), F32),
            pltpu.VMEM((n_blk, rows, S2), F32),
            pltpu.VMEM((n_blk, r, rows, LANES), F32),
            pltpu.VMEM((n_blk, r, rows, LANES), F32),
            pltpu.VMEM((n_blk, 2, nb, S2 // 2), F32),
        ],
        compiler_params=_params("parallel", "arbitrary"),
        name="s5_scan",
    )(proj, proj, w_in, w_x, w_u, ar, ai, d_skip.reshape(1, W).astype(F32), glu_w.astype(BF16),
      glu_b.reshape(1, W).astype(F32))


def _log_sigmoid(x):
    return jnp.minimum(x, 0.0) - jnp.log1p(jnp.exp(-jnp.abs(x)))


def _split3(x):
    hi = x.astype(BF16)
    r1 = x - hi.astype(F32)
    mid = r1.astype(BF16)
    lo = (r1 - mid.astype(F32)).astype(BF16)
    return hi, mid, lo


def _mlstm_kernel(x_ref, z_ref, cw_ref, cb_ref, wq_ref, wk_ref, wv_ref, wvt_ref, gw_ref, gwt_ref, gbc_ref, gbr_ref,
                  ng_ref, sk_ref, o_ref, xe_scr, qkv_scr, ks_scr, vta_scr, ca_scr, m_scr):
    n_sub, Tc = qkv_scr.shape[0], qkv_scr.shape[1]
    W = x_ref.shape[2]
    H = wq_ref.shape[0]
    dh = W // H
    pad = SUBLANES
    heads = range(H)
    nt = (((1,), (1,)), ((), ()))

    @pl.when(pl.program_id(1) == 0)
    def _():
        xe_scr[0:pad, :] = jnp.zeros((pad, W), F32)
        ca_scr[...] = jnp.zeros_like(ca_scr)
        m_scr[...] = jnp.zeros_like(m_scr)
        vta_scr[:, :, dh:, :] = jnp.ones((n_sub, H, pad, Tc), BF16)

    xe_scr[pad:, :] = x_ref[0].astype(F32)

    ti = lax.broadcasted_iota(jnp.int32, (Tc, Tc), 0)
    si = lax.broadcasted_iota(jnp.int32, (Tc, Tc), 1)
    lower = jnp.where(si <= ti, 1.0, 0.0).astype(BF16)
    allowed = ti <= si
    upper = jnp.where(allowed, 1.0, 0.0).astype(BF16)

    def front(s):
        r0 = s * Tc
        xb = x_ref[0, r0:r0 + Tc, :]
        xc = cb_ref[...] + sum(xe_scr[pl.ds(pad + r0 - (ML_CONV - 1) + j, Tc), :] * cw_ref[j:j + 1, :]
                               for j in range(ML_CONV))
        xc = _silu(xc)
        xcb = xc.astype(BF16)
        for h in heads:
            cols = slice(h * dh, (h + 1) * dh)
            q = jnp.dot(xcb[:, cols], wq_ref[h], preferred_element_type=F32)
            k = jnp.dot(xcb[:, cols], wk_ref[h], preferred_element_type=F32)
            v = jnp.dot(xb[:, cols], wv_ref[h], preferred_element_type=F32)
            qkv_scr[s, :, h * dh:(h + 1) * dh] = q.astype(BF16)
            qkv_scr[s, :, W + h * dh:W + (h + 1) * dh] = k.astype(BF16)
            qkv_scr[s, :, 2 * W + h * dh:2 * W + (h + 1) * dh] = v.astype(BF16)
            vta_scr[s, h, :dh, :] = lax.dot_general(wvt_ref[h], xb[:, cols], nt,
                                                    preferred_element_type=F32).astype(BF16)
            ks_scr[s, :, cols] = (k * (dh ** -0.5)).astype(BF16)
        qkv = qkv_scr[s]
        g_col = jnp.dot(qkv, gw_ref[...], preferred_element_type=F32) + gbc_ref[...]
        g_row = lax.dot_general(gwt_ref[...], qkv, nt, preferred_element_type=F32) + gbr_ref[...]
        b_col = sum(jnp.dot(lower, p, preferred_element_type=F32) for p in _split3(_log_sigmoid(g_col)))
        b_row = sum(jnp.dot(p, upper, preferred_element_type=F32) for p in _split3(_log_sigmoid(g_row)))
        return xc, g_col, g_row, b_col, b_row

    def rec_matmuls(s):
        q_h = [qkv_scr[s, :, h * dh:(h + 1) * dh] for h in heads]
        ks_h = [ks_scr[s, :, h * dh:(h + 1) * dh] for h in heads]
        s_qk = [lax.dot_general(ks_h[h], q_h[h], nt, preferred_element_type=F32) for h in heads]
        q_ca = [lax.dot_general(ca_scr[h].astype(BF16), q_h[h], nt, preferred_element_type=F32)
                for h in heads]
        return ks_h, s_qk, q_ca

    def rec_rest(s, mm, xc, g_col, g_row, b_col, b_row):
        ks_h, s_qk, q_ca = mm
        rows = slice(s * Tc, (s + 1) * Tc)
        m_prev = [m_scr[h][:, 0:1] for h in heads]
        br = [b_row[H + h:H + h + 1, :] for h in heads]
        li = [g_row[h:h + 1, :] for h in heads]
        b_last = [br[h][:, Tc - 1:Tc] for h in heads]
        dmat, sc, floor = [], [], []
        for h in heads:
            c_col = g_col[:, h:h + 1] - b_col[:, H + h:H + h + 1]
            log_d = jnp.where(allowed, br[h] + c_col, NEG_BIG)
            inter = br[h] + m_prev[h]
            m_t = jnp.maximum(inter, jnp.max(log_d, axis=0, keepdims=True))
            dmat.append(jnp.exp(log_d - m_t))
            sc.append(jnp.exp(inter - m_t))
            floor.append(jnp.exp(-m_t))
        pv = [jnp.dot(vta_scr[s, h], (s_qk[h] * dmat[h]).astype(BF16), preferred_element_type=F32) for h in heads]
        for h in heads:
            w_log = b_last[h] - br[h] + li[h]
            m_new = jnp.maximum(b_last[h] + m_prev[h], jnp.max(w_log, axis=1, keepdims=True))
            weighted = (vta_scr[s, h].astype(F32) * jnp.exp(w_log - m_new)).astype(BF16)
            decay = jnp.exp(b_last[h] + m_prev[h] - m_new)
            ca_scr[h] = decay * ca_scr[h] + jnp.dot(weighted, ks_h[h], preferred_element_type=F32)
            m_scr[h] = jnp.broadcast_to(m_new, (1, LANES))
        for h in heads:
            cols = slice(h * dh, (h + 1) * dh)
            tot = sc[h] * q_ca[h] + pv[h]
            hh = tot[:dh] / jnp.maximum(jnp.abs(tot[dh:dh + 1]), floor[h])
            mu = jnp.mean(hh, axis=0, keepdims=True)
            cen = hh - mu
            var = jnp.mean(cen * cen, axis=0, keepdims=True)
            hn = jnp.transpose(cen * lax.rsqrt(var + HEAD_NORM_EPS))
            out = hn * ng_ref[:, cols] + sk_ref[:, cols] * xc[:, cols]
            o_ref[0, rows, cols] = (out * _silu(z_ref[0, rows, cols].astype(F32))).astype(o_ref.dtype)

    ready = front(0)
    for s in range(n_sub):
        mm = rec_matmuls(s)
        following = front(s + 1) if s + 1 < n_sub else None
        rec_rest(s, mm, *ready)
        ready = following
    xe_scr[0:pad, :] = xe_scr[n_sub * Tc:n_sub * Tc + pad, :]


def _mlstm_branch(proj, conv_w, conv_b, wq, wk, wv, gate_w, gate_b, norm_g, skip):
    B, L, N = proj.shape
    W = N // 4
    H = wq.shape[0]
    dh = W // H
    Tc = min(ML_CHUNK, L)
    n_sub = min(ML_SUB, L // Tc)
    rows = n_sub * Tc
    gw = jnp.zeros((3 * W, LANES), F32).at[:, :2 * H].set(gate_w.astype(F32)).astype(BF16)
    gwt = jnp.transpose(gate_w.astype(F32)).astype(BF16)
    gbc = jnp.zeros((1, LANES), F32).at[0, :2 * H].set(gate_b.astype(F32))
    gbr = gate_b.astype(F32).reshape(2 * H, 1)
    c2 = lambda b, c: (0, 0)
    c3 = lambda b, c: (0, 0, 0)
    return pl.pallas_call(
        _mlstm_kernel,
        grid=(B, L // rows),
        in_specs=[
            pl.BlockSpec((1, rows, W), lambda b, c: (b, c, 2)),
            pl.BlockSpec((1, rows, W), lambda b, c: (b, c, 3)),
            pl.BlockSpec((ML_CONV, W), c2),
            pl.BlockSpec((1, W), c2),
            pl.BlockSpec((H, dh, dh), c3),
            pl.BlockSpec((H, dh, dh), c3),
            pl.BlockSpec((H, dh, dh), c3),
            pl.BlockSpec((H, dh, dh), c3),
            pl.BlockSpec((3 * W, LANES), c2),
            pl.BlockSpec((2 * H, 3 * W), c2),
            pl.BlockSpec((1, LANES), c2),
            pl.BlockSpec((2 * H, 1), c2),
            pl.BlockSpec((1, W), c2),
            pl.BlockSpec((1, W), c2),
        ],
        out_specs=pl.BlockSpec((1, rows, W), lambda b, c: (b, c, 0)),
        out_shape=jax.ShapeDtypeStruct((B, L, W), BF16),
        scratch_shapes=[
            pltpu.VMEM((SUBLANES + rows, W), F32),
            pltpu.VMEM((n_sub, Tc, 3 * W), BF16),
            pltpu.VMEM((n_sub, Tc, W), BF16),
            pltpu.VMEM((n_sub, H, dh + SUBLANES, Tc), BF16),
            pltpu.VMEM((H, dh + SUBLANES, dh), F32),
            pltpu.VMEM((H, 1, LANES), F32),
        ],
        compiler_params=_params("parallel", "arbitrary"),
        name="mlstm",
    )(proj, proj, conv_w.astype(F32), conv_b.reshape(1, W).astype(F32), wq.astype(BF16), wk.astype(BF16),
      wv.astype(BF16), jnp.transpose(wv, (0, 2, 1)).astype(BF16), gw, gwt, gbc, gbr, norm_g.reshape(1, W).astype(F32), skip.reshape(1, W).astype(F32))


def _attn_kernel(q_ref, k_ref, vt_ref, z_ref, bias_ref, lam_ref, ng_ref, o_ref, m_scr, acc_scr, *,
                 mode, out_scale, blk, bias_index):
    L = q_ref.shape[2]
    n = L // blk
    lane = lax.broadcasted_iota(jnp.int32, (1, LANES), 1)
    map0 = lane < ATTN_HEAD_DIM
    feat = lax.broadcasted_iota(jnp.int32, (LANES, 1), 0)
    ones_rows = jnp.ones((SUBLANES, blk), BF16)

    def stacked_q(qi):
        q = q_ref[0, 0, qi * blk:(qi + 1) * blk, :]
        zero = jnp.zeros_like(q)
        return jnp.concatenate([jnp.where(map0, q, zero), jnp.where(map0, zero, q)], axis=0)

    def scores(qi, j):
        s = lax.dot_general(k_ref[0, 0, j * blk:(j + 1) * blk, :], q2[qi], (((1,), (1,)), ((), ())),
                            preferred_element_type=F32)
        e = bias_index(qi - j)
        return s if e is None else s + bias_ref[e]

    def update(qi, j, s):
        m_cur = jnp.max(s, axis=0, keepdims=True)
        m_next = m_cur if j == 0 else jnp.maximum(m_scr[qi], m_cur)
        p = jnp.exp2(s - m_next).astype(BF16)
        pv = jnp.dot(jnp.concatenate([vt_ref[0, 0, :, j * blk:(j + 1) * blk], ones_rows], axis=0), p,
                     preferred_element_type=F32)
        if j == 0:
            acc_scr[qi] = pv
        else:
            acc_scr[qi] = jnp.exp2(m_scr[qi] - m_next) * acc_scr[qi] + pv
        m_scr[qi] = m_next

    def finish(qi):
        rows = slice(qi * blk, (qi + 1) * blk)
        acc = acc_scr[qi]
        o = acc[:LANES] / acc[LANES:LANES + 1]
        o0, o1 = o[:, :blk], o[:, blk:]
        if mode == "dilated":
            out = jnp.transpose(jnp.where(feat < ATTN_HEAD_DIM, o0, o1))
        else:
            out = jnp.transpose(o0 - lam_ref[:, 0:1] * o1)
            ms = jnp.mean(out * out, axis=-1, keepdims=True)
            out = out * lax.rsqrt(ms + HEAD_NORM_EPS) * ng_ref[...] * out_scale
        o_ref[0, rows, :] = (out * _silu(z_ref[0, 0, rows, :].astype(F32))).astype(o_ref.dtype)

    steps = []
    for a in range(n // 2):
        b = n - 1 - a
        ja = 0
        for jb in range(b + 1):
            steps.append((b, jb))
            if ja <= a and (jb + 1) * (a + 1) >= (ja + 1) * (b + 1):
                steps.append((a, ja))
                ja += 1
    q2 = {qi: stacked_q(qi) for qi in range(n)}
    pending = {}
    for t in range(min(ATTN_LOOKAHEAD, len(steps))):
        pending[t] = scores(*steps[t])
    for t, (qi, j) in enumerate(steps):
        if t + ATTN_LOOKAHEAD < len(steps):
            pending[t + ATTN_LOOKAHEAD] = scores(*steps[t + ATTN_LOOKAHEAD])
        update(qi, j, pending.pop(t))
        if j == qi:
            finish(qi)


def _dilated_multiplicity(delta):
    mult = np.zeros(delta.shape, np.int64)
    for window, dil in DIL_PAIRS:
        mult += (delta >= 0) & (delta % dil == 0) & (delta <= window)
    return mult


def _bias_tables(L, blk, mode):
    n = L // blk
    key = np.arange(blk)[:, None]
    qry = np.arange(blk)[None, :]
    tables, index = [], {}
    for d in range(n):
        delta = d * blk + qry - key
        mult = _dilated_multiplicity(delta) if mode == "dilated" else (delta >= 0).astype(np.int64)
        t = np.where(mult > 0, np.log2(np.maximum(mult, 1)), NEG_BIG).astype(np.float32)
        if not t.any():
            index[d] = None
            continue
        for e, other in enumerate(tables):
            if np.array_equal(t, other):
                index[d] = e
                break
        else:
            index[d] = len(tables)
            tables.append(t)
    stacked = np.stack(tables)
    return jnp.asarray(np.concatenate([stacked, stacked], axis=2)), index


def _attention(qkz, vt, q0, k0, v0, z0, n_blocks, blk, lam, norm_g, mode, out_scale):
    B, _, L, _ = qkz.shape
    bias, index = _bias_tables(L, blk, mode)
    n_q = L // blk
    assert n_q % 2 == 0
    kern = functools.partial(_attn_kernel, mode=mode, out_scale=out_scale, blk=blk, bias_index=index.get)
    c2 = lambda b, n: (0, 0)
    return pl.pallas_call(
        kern,
        grid=(B, n_blocks),
        in_specs=[
            pl.BlockSpec((1, 1, L, LANES), lambda b, n: (b, q0 + n, 0, 0)),
            pl.BlockSpec((1, 1, L, LANES), lambda b, n: (b, k0 + n, 0, 0)),
            pl.BlockSpec((1, 1, LANES, L), lambda b, n: (b, v0 + n, 0, 0)),
            pl.BlockSpec((1, 1, L, LANES), lambda b, n: (b, z0 + n, 0, 0)),
            pl.BlockSpec(bias.shape, lambda b, n: (0, 0, 0)),
            pl.BlockSpec((1, LANES), c2),
            pl.BlockSpec((1, LANES), c2),
        ],
        out_specs=pl.BlockSpec((1, L, LANES), lambda b, n: (b, 0, n)),
        out_shape=jax.ShapeDtypeStruct((B, L, n_blocks * LANES), BF16),
        scratch_shapes=[
            pltpu.VMEM((n_q, 1, 2 * blk), F32),
            pltpu.VMEM((n_q, LANES + SUBLANES, 2 * blk), F32),
        ],
        compiler_params=_params("parallel", "parallel"),
        name="attn_" + mode,
    )(qkz, qkz, vt, qkz, bias, lam, norm_g)


def _rope_tables(L):
    dh = ATTN_HEAD_DIM
    inv = ROPE_THETA ** (-jnp.arange(0, dh, 2, dtype=F32) / dh)
    ang = jnp.arange(L, dtype=F32)[:, None] * inv[None, :]
    cos, sin = jnp.cos(ang), jnp.sin(ang)
    reps = LANES // dh
    cos_t = jnp.tile(jnp.concatenate([cos, cos], axis=-1), (1, reps))
    sin_t = jnp.tile(jnp.concatenate([-sin, sin], axis=-1), (1, reps))
    return cos_t, sin_t


def _even_layer(h, pre_g, w_in, s5_params, ml_params):
    proj = _norm_proj(h, pre_g, w_in.astype(BF16))
    return _s5_branch(proj, *s5_params), _mlstm_branch(proj, *ml_params)


def _odd_layer(prev, h, pre_g, w_in, lq1, lk1, lq2, lk2, diff_norm, layer_idx):
    B, L, D = h.shape
    nblk = PROJ_COLS // LANES
    cos_t, sin_t = _rope_tables(L)
    h, qkz, vt = _out_norm_proj_rope(*prev, h, pre_g, w_in.astype(BF16), cos_t, sin_t)
    blk = min(ATTN_BLOCK, L)
    lam_init = 0.8 - 0.6 * math.exp(-0.3 * layer_idx)
    lam = (jnp.exp(jnp.sum(lq1.astype(F32) * lk1.astype(F32)))
           - jnp.exp(jnp.sum(lq2.astype(F32) * lk2.astype(F32))) + lam_init)
    lam_row = jnp.full((1, LANES), lam, F32)
    ones_row = jnp.ones((1, LANES), F32)
    base = [b for _, b in ODD_CHUNKS]
    c_out = _attention(qkz, vt, base[0], base[1], base[2], base[3], nblk, blk, ones_row, ones_row, "dilated", 1.0)
    d_out = _attention(qkz, vt, base[4], base[5], base[6], base[7], nblk, blk, lam_row,
                       diff_norm.reshape(1, LANES).astype(F32), "diff", 1.0 - lam_init)
    return h, c_out, d_out


def kernel(x, pre_norm, post_norm, w_in_ab, s5_lambda_re, s5_lambda_im, s5_log_dt, s5_b_re, s5_b_im,
           s5_c_re, s5_c_im, s5_d, s5_glu_w, s5_glu_b, ml_conv_w, ml_conv_b, ml_wq, ml_wk, ml_wv,
           ml_gate_w, ml_gate_b, ml_norm, ml_skip, w_out_ab, w_in_cd, diff_lq1, diff_lk1, diff_lq2,
           diff_lk2, diff_norm, w_out_cd):
    depth = pre_norm.shape[0]
    h = x
    prev = None
    for l in range(depth):
        i = l // 2
        if l % 2 == 0:
            if prev is not None:
                h = _out_proj(*prev, h)
            s5_params = (s5_lambda_re[i], s5_lambda_im[i], s5_log_dt[i], s5_b_re[i], s5_b_im[i],
                         s5_c_re[i], s5_c_im[i], s5_d[i], s5_glu_w[i], s5_glu_b[i])
            ml_params = (ml_conv_w[i], ml_conv_b[i], ml_wq[i], ml_wk[i], ml_wv[i], ml_gate_w[i],
                         ml_gate_b[i], ml_norm[i], ml_skip[i])
            a, b = _even_layer(h, pre_norm[l], w_in_ab[i], s5_params, ml_params)
            prev = (a, b, w_out_ab[i], post_norm[l])
        else:
            h, c, d = _odd_layer(prev, h, pre_norm[l], w_in_cd[i], diff_lq1[i], diff_lk1[i], diff_lq2[i],
                                 diff_lk2[i], diff_norm[i], l)
            prev = (c, d, w_out_cd[i], post_norm[l])
    return _out_proj(*prev, h)
```
